```python
import math
import jax, jax.numpy as jnp
from jax import lax
import numpy as np

D_MODEL = 1024
BATCH = 8
SEQ = 2048
DEPTH = 1
DEC_BATCH = 128
DEC_SEQ = 4
PAST_LEN = 16384
PAGE_SIZE = 128

DN_HEADS = 4
DN_DK = 128
DN_DV = 128
DN_KEY = DN_HEADS * DN_DK
DN_VAL = DN_HEADS * DN_DV
DN_CONV = 4
DN_CONV_DIM = 2 * DN_KEY + DN_VAL
DN_CHUNK = 64
SSM_INNER = D_MODEL
SSM_HEADDIM = 64
SSM_HEADS = SSM_INNER // SSM_HEADDIM
SSM_GROUPS = 2
SSM_STATE = 128
SSM_CONV = 4
SSM_CONV_DIM = SSM_INNER + 2 * SSM_GROUPS * SSM_STATE
SSM_CHUNK = 64
MOE_GROUPS = 4
MOE_PER_GROUP = 8
MOE_EXPERTS = MOE_GROUPS * MOE_PER_GROUP
MOE_TOPK = 2
MOE_FF = D_MODEL // 4
EPS = 1e-6

IN_SIZES = (DN_CONV_DIM, DN_VAL, DN_HEADS, DN_HEADS,
            SSM_CONV_DIM, SSM_INNER, SSM_HEADS,
            D_MODEL, D_MODEL)
D_IN_PROJ = sum(IN_SIZES)

kernel_name = 'hybrid_gdn_ssd_hmoe_adaln_step'


def _rms_norm(x, w):
    xf = x.astype(jnp.float32)
    y = xf * lax.rsqrt(jnp.mean(xf * xf, axis=-1, keepdims=True) + EPS)
    return (y * w.astype(jnp.float32)).astype(x.dtype)


def _l2norm(t):
    t = t.astype(jnp.float32)
    return t * lax.rsqrt(jnp.sum(t * t, axis=-1, keepdims=True) + EPS)


def _causal_conv(u, buf, w):
    width, L = w.shape[0], u.shape[1]
    full = jnp.concatenate([buf.astype(u.dtype), u], axis=1)
    y = full[:, 0:L] * w[0]
    for i in range(1, width):
        y = y + full[:, i:i + L] * w[i]
    return y, full[:, L:]


def _chunk(t, C):
    Bsz, L = t.shape[:2]
    n = -(-L // C)
    t = jnp.pad(t, [(0, 0), (0, n * C - L)] + [(0, 0)] * (t.ndim - 2))
    t = t.reshape((Bsz, n, C) + t.shape[2:])
    return jnp.transpose(t, (1, 0, 3, 2) + tuple(range(4, t.ndim)))


def _unchunk(t, L):
    n, Bsz, H, C = t.shape[:4]
    t = jnp.transpose(t, (1, 0, 3, 2) + tuple(range(4, t.ndim)))
    return t.reshape((Bsz, n * C, H) + t.shape[4:])[:, :L]


def _gated_delta(q, k, v, g, beta, s0):
    L = q.shape[1]
    C = min(DN_CHUNK, L)
    tri_incl = jnp.tril(jnp.ones((C, C), bool))
    tri_strict = jnp.tril(jnp.ones((C, C), bool), -1)
    eye = jnp.eye(C, dtype=jnp.float32)

    def step(S, inp):
        qc, kc, vc, gc, bc = inp
        cs = jnp.cumsum(gc, axis=-1)
        decay = jnp.exp(jnp.where(tri_incl, cs[..., :, None] - cs[..., None, :], -jnp.inf))
        kk = jnp.einsum('bhid,bhjd->bhij', kc, kc)
        lmat = jnp.where(tri_strict, bc[..., :, None] * kk * decay, 0.0)
        rhs = jnp.concatenate([vc * bc[..., None], kc * (bc * jnp.exp(cs))[..., None]], axis=-1)
        sol = lax.linalg.triangular_solve(lmat + eye, rhs, left_side=True, lower=True)
        u, w = sol[..., :DN_DV], sol[..., DN_DV:]
        v_new = u - jnp.einsum('bhik,bhkv->bhiv', w, S)
        qk = jnp.einsum('bhid,bhjd->bhij', qc, kc) * decay
        o = (jnp.einsum('bhik,bhkv->bhiv', qc * jnp.exp(cs)[..., None], S)
             + jnp.einsum('bhij,bhjv->bhiv', qk, v_new))
        last = cs[..., -1:]
        S = (jnp.exp(last)[..., None] * S
             + jnp.einsum('bhjk,bhjv->bhkv', kc * jnp.exp(last - cs)[..., None], v_new))
        return S, o

    S, o = lax.scan(step, s0.astype(jnp.float32),
                    (_chunk(q, C), _chunk(k, C), _chunk(v, C), _chunk(g, C), _chunk(beta, C)))
    return _unchunk(o, L), S


def _ssd(x, dt, bm, cm, A, h0):
    Bsz, L = x.shape[:2]
    C = min(SSM_CHUNK, L)
    E = SSM_HEADS // SSM_GROUPS
    xc = _chunk(x, C)
    n = xc.shape[0]
    xc = xc.reshape(n, Bsz, SSM_GROUPS, E, C, SSM_HEADDIM)
    dtc = _chunk(dt, C).reshape(n, Bsz, SSM_GROUPS, E, C)
    bc, cc = _chunk(bm, C), _chunk(cm, C)
    a_ge = A.reshape(SSM_GROUPS, E)
    tri_incl = jnp.tril(jnp.ones((C, C), bool))

    def step(hs, inp):
        xk, dtk, bk, ck = inp
        cs = jnp.cumsum(dtk * a_ge[:, :, None], axis=-1)
        decay = jnp.exp(jnp.where(tri_incl, cs[..., :, None] - cs[..., None, :], -jnp.inf))
        xdt = xk * dtk[..., None]
        cb = jnp.einsum('bgin,bgjn->bgij', ck, bk)
        y = jnp.einsum('bgeij,bgejp->bgeip', cb[:, :, None] * decay, xdt)
        y = y + jnp.einsum('bgin,bgepn->bgeip', ck, hs) * jnp.exp(cs)[..., None]
        last = cs[..., -1:]
        hs = (jnp.exp(last)[..., None] * hs
              + jnp.einsum('bgjn,bgejp->bgepn', bk, xdt * jnp.exp(last - cs)[..., None]))
        return hs, y

    h0 = h0.astype(jnp.float32).reshape(Bsz, SSM_GROUPS, E, SSM_HEADDIM, SSM_STATE)
    hs, y = lax.scan(step, h0, (xc, dtc, bc, cc))
    y = y.reshape(n, Bsz, SSM_HEADS, C, SSM_HEADDIM)
    return _unchunk(y, L), hs.reshape(Bsz, SSM_HEADS, SSM_HEADDIM, SSM_STATE)


def _mixer(h, dn_conv_buf, dn_s, ssm_conv_buf, ssm_h, lp):
    Bsz, L, _ = h.shape
    f32 = jnp.float32
    split_at = np.cumsum(IN_SIZES)[:-1].tolist()
    (dn_qkv, dn_z, dn_a, dn_b, ssm_xbc, ssm_z, ssm_dt, gate_dn, gate_ssm) = jnp.split(
        h @ lp['w_in'], split_at, axis=-1)

    dn_qkv, dn_conv_new = _causal_conv(dn_qkv, dn_conv_buf, lp['dn_conv_w'])
    dn_qkv = jax.nn.silu(dn_qkv)
    q, k, v = jnp.split(dn_qkv, [DN_KEY, 2 * DN_KEY], axis=-1)
    q = _l2norm(q.reshape(Bsz, L, DN_HEADS, DN_DK)) * (DN_DK ** -0.5)
    k = _l2norm(k.reshape(Bsz, L, DN_HEADS, DN_DK))
    v = v.reshape(Bsz, L, DN_HEADS, DN_DV).astype(f32)
    g = -jnp.exp(lp['dn_A_log'].astype(f32)) * jax.nn.softplus(dn_a.astype(f32) + lp['dn_dt_bias'].astype(f32))
    beta = jax.nn.sigmoid(dn_b.astype(f32))
    o, dn_s_new = _gated_delta(q, k, v, g, beta, dn_s)
    o = _rms_norm(o, lp['dn_norm_w']) * jax.nn.silu(dn_z.reshape(Bsz, L, DN_HEADS, DN_DV).astype(f32))
    y_dn = o.reshape(Bsz, L, DN_VAL).astype(h.dtype) @ lp['w_dn_out']

    xbc, ssm_conv_new = _causal_conv(ssm_xbc, ssm_conv_buf, lp['ssm_conv_w'])
    xbc = jax.nn.silu(xbc + lp['ssm_conv_b'])
    xs, bm, cm = jnp.split(xbc, [SSM_INNER, SSM_INNER + SSM_GROUPS * SSM_STATE], axis=-1)
    xs = xs.reshape(Bsz, L, SSM_HEADS, SSM_HEADDIM).astype(f32)
    bm = bm.reshape(Bsz, L, SSM_GROUPS, SSM_STATE).astype(f32)
    cm = cm.reshape(Bsz, L, SSM_GROUPS, SSM_STATE).astype(f32)
    dt = jax.nn.softplus(ssm_dt.astype(f32) + lp['ssm_dt_bias'].astype(f32))
    A = -jnp.exp(lp['ssm_A_log'].astype(f32))
    y, ssm_h_new = _ssd(xs, dt, bm, cm, A, ssm_h)
    y = y + lp['ssm_D'].astype(f32)[:, None] * xs
    y = y.reshape(Bsz, L, SSM_INNER) * jax.nn.silu(ssm_z.astype(f32))
    y = _rms_norm(y.reshape(Bsz, L, SSM_GROUPS, SSM_INNER // SSM_GROUPS),
                  lp['ssm_norm_w'].reshape(SSM_GROUPS, SSM_INNER // SSM_GROUPS)).reshape(Bsz, L, SSM_INNER)
    y_ssm = y.astype(h.dtype) @ lp['w_ssm_out']

    merged = jax.nn.sigmoid(gate_dn) * y_dn + jax.nn.sigmoid(gate_ssm) * y_ssm
    return merged @ lp['w_out'], dn_conv_new, dn_s_new, ssm_conv_new, ssm_h_new


def _hier_moe(h, lp):
    Bsz, L, D = h.shape
    t = h.reshape(Bsz * L, D)
    g_logits = (t @ lp['w_group_router'] + lp['b_group_router']).astype(jnp.float32)
    g_sel = jnp.argmax(g_logits, axis=-1)
    p_group = jnp.take_along_axis(jax.nn.softmax(g_logits, axis=-1), g_sel[:, None], axis=-1)
    e_logits = (t @ lp['w_expert_router'] + lp['b_expert_router']).astype(jnp.float32)
    e_logits = jnp.take_along_axis(e_logits.reshape(-1, MOE_GROUPS, MOE_PER_GROUP),
                                   g_sel[:, None, None], axis=1)[:, 0]
    top_p, top_i = lax.top_k(jax.nn.softmax(e_logits, axis=-1), MOE_TOPK)
    comb = p_group * top_p / jnp.sum(top_p, axis=-1, keepdims=True)
    expert = g_sel[:, None] * MOE_PER_GROUP + top_i
    w_tok = jnp.sum(jax.nn.one_hot(expert, MOE_EXPERTS, dtype=jnp.float32) * comb[..., None], axis=1)
    gate = jnp.einsum('td,edf->tef', t, lp['w_exp_gate'])
    up = jnp.einsum('td,edf->tef', t, lp['w_exp_up'])
    act = jax.nn.silu(gate) * up * w_tok[..., None].astype(t.dtype)
    out = jnp.einsum('tef,efd->td', act, lp['w_exp_down'])
    return out.reshape(Bsz, L, D)


def _layer(x, c, dn_conv_buf, dn_s, ssm_conv_buf, ssm_h, lp):
    mod = (jax.nn.silu(c) @ lp['w_ada'] + lp['b_ada'])[:, None, :]
    sh1, sc1, gt1, sh2, sc2, gt2 = jnp.split(mod, 6, axis=-1)
    h = _rms_norm(x, lp['norm_mix_w']) * (1 + sc1) + sh1
    mix, dn_conv_new, dn_s_new, ssm_conv_new, ssm_h_new = _mixer(h, dn_conv_buf, dn_s, ssm_conv_buf, ssm_h, lp)
    x = x + gt1 * mix
    h = _rms_norm(x, lp['norm_ffn_w']) * (1 + sc2) + sh2
    x = x + gt2 * _hier_moe(h, lp)
    return x, dn_conv_new, dn_s_new, ssm_conv_new, ssm_h_new


def _trunk(x, c, dn_conv, dn_rec, ssm_conv, ssm_rec, layers, w_ada_final, b_ada_final, final_norm_w):
    n_dc, n_dr, n_sc, n_sr = [], [], [], []
    for l in range(DEPTH):
        x, a1, a2, a3, a4 = _layer(x, c, dn_conv[l], dn_rec[l], ssm_conv[l], ssm_rec[l], layers[l])
        n_dc.append(a1)
        n_dr.append(a2)
        n_sc.append(a3)
        n_sr.append(a4)
    shift, scale = jnp.split((jax.nn.silu(c) @ w_ada_final + b_ada_final)[:, None, :], 2, axis=-1)
    y = _rms_norm(x, final_norm_w) * (1 + scale) + shift
    return y, jnp.stack(n_dc), jnp.stack(n_dr), jnp.stack(n_sc), jnp.stack(n_sr)


def _dt_bias(key, n):
    dt = jnp.exp(jax.random.uniform(key, (DEPTH, n), jnp.float32, math.log(1e-3), math.log(1e-1)))
    return dt + jnp.log(-jnp.expm1(-dt))


def setup_inputs(seed: int = 0) -> dict:
    key = jax.random.key(seed)
    keys = jax.random.split(key, 64)
    counter = [0]
    f32 = jnp.float32

    def nk():
        counter[0] += 1
        return keys[counter[0] - 1]

    def nrm(shape, scale=1.0):
        return jax.random.normal(nk(), shape, f32) * scale

    def gain(shape):
        return 1.0 + nrm(shape, 0.02)

    D = D_MODEL
    sD = D ** -0.5
    return {
        'x_prompt': nrm((BATCH, SEQ, D)),
        'x_sample': nrm((DEC_BATCH, DEC_SEQ, D)),
        'c_prompt': nrm((BATCH, D)),
        'c_sample': nrm((DEC_BATCH, D)),
        'state_dn_conv': nrm((DEPTH, DEC_BATCH, DN_CONV - 1, DN_CONV_DIM)),
        'state_dn_rec': nrm((DEPTH, DEC_BATCH, DN_HEADS, DN_DK, DN_DV), 0.05),
        'state_ssm_conv': nrm((DEPTH, DEC_BATCH, SSM_CONV - 1, SSM_CONV_DIM)),
        'state_ssm_rec': nrm((DEPTH, DEC_BATCH, SSM_HEADS, SSM_HEADDIM, SSM_STATE), 0.05),
        'w_ada': nrm((DEPTH, D, 6 * D), sD),
        'b_ada': nrm((DEPTH, 6 * D), 0.02),
        'norm_mix_w': gain((DEPTH, D)),
        'w_in': nrm((DEPTH, D, D_IN_PROJ), sD),
        'dn_conv_w': nrm((DEPTH, DN_CONV, DN_CONV_DIM), DN_CONV ** -0.5),
        'dn_A_log': jnp.log(jax.random.uniform(nk(), (DEPTH, DN_HEADS), f32, 1.0, 16.0)),
        'dn_dt_bias': _dt_bias(nk(), DN_HEADS),
        'dn_norm_w': gain((DEPTH, DN_DV)),
        'w_dn_out': nrm((DEPTH, DN_VAL, D), DN_VAL ** -0.5),
        'ssm_conv_w': nrm((DEPTH, SSM_CONV, SSM_CONV_DIM), SSM_CONV ** -0.5),
        'ssm_conv_b': nrm((DEPTH, SSM_CONV_DIM), 0.02),
        'ssm_A_log': jnp.log(jax.random.uniform(nk(), (DEPTH, SSM_HEADS), f32, 1.0, 16.0)),
        'ssm_dt_bias': _dt_bias(nk(), SSM_HEADS),
        'ssm_D': 1.0 + nrm((DEPTH, SSM_HEADS), 0.1),
        'ssm_norm_w': gain((DEPTH, SSM_INNER)),
        'w_ssm_out': nrm((DEPTH, SSM_INNER, D), SSM_INNER ** -0.5),
        'w_out': nrm((DEPTH, D, D), sD),
        'norm_ffn_w': gain((DEPTH, D)),
        'w_group_router': nrm((DEPTH, D, MOE_GROUPS), sD),
        'b_group_router': nrm((DEPTH, MOE_GROUPS), 0.01),
        'w_expert_router': nrm((DEPTH, D, MOE_EXPERTS), sD),
        'b_expert_router': nrm((DEPTH, MOE_EXPERTS), 0.01),
        'w_exp_gate': nrm((DEPTH, MOE_EXPERTS, D, MOE_FF), sD),
        'w_exp_up': nrm((DEPTH, MOE_EXPERTS, D, MOE_FF), sD),
        'w_exp_down': nrm((DEPTH, MOE_EXPERTS, MOE_FF, D), MOE_FF ** -0.5),
        'w_ada_final': nrm((D, 2 * D), sD),
        'b_ada_final': nrm((2 * D,), 0.02),
        'final_norm_w': gain((D,)),
    }


def reference(x_prompt, x_sample, c_prompt, c_sample, state_dn_conv, state_dn_rec, state_ssm_conv,
              state_ssm_rec, w_ada, b_ada, norm_mix_w, w_in, dn_conv_w, dn_A_log, dn_dt_bias, dn_norm_w,
              w_dn_out, ssm_conv_w, ssm_conv_b, ssm_A_log, ssm_dt_bias, ssm_D, ssm_norm_w, w_ssm_out, w_out,
              norm_ffn_w, w_group_router, b_group_router, w_expert_router, b_expert_router, w_exp_gate,
              w_exp_up, w_exp_down, w_ada_final, b_ada_final, final_norm_w):
    layers = [dict(w_ada=w_ada[l], b_ada=b_ada[l], norm_mix_w=norm_mix_w[l], w_in=w_in[l],
                   dn_conv_w=dn_conv_w[l], dn_A_log=dn_A_log[l], dn_dt_bias=dn_dt_bias[l],
                   dn_norm_w=dn_norm_w[l], w_dn_out=w_dn_out[l], ssm_conv_w=ssm_conv_w[l],
                   ssm_conv_b=ssm_conv_b[l], ssm_A_log=ssm_A_log[l], ssm_dt_bias=ssm_dt_bias[l],
                   ssm_D=ssm_D[l], ssm_norm_w=ssm_norm_w[l], w_ssm_out=w_ssm_out[l], w_out=w_out[l],
                   norm_ffn_w=norm_ffn_w[l], w_group_router=w_group_router[l],
                   b_group_router=b_group_router[l], w_expert_router=w_expert_router[l],
                   b_expert_router=b_expert_router[l], w_exp_gate=w_exp_gate[l], w_exp_up=w_exp_up[l],
                   w_exp_down=w_exp_down[l])
              for l in range(DEPTH)]
    nb = x_prompt.shape[0]
    z_dn_conv = jnp.zeros((DEPTH, nb, DN_CONV - 1, DN_CONV_DIM), x_prompt.dtype)
    z_dn_rec = jnp.zeros((DEPTH, nb, DN_HEADS, DN_DK, DN_DV), jnp.float32)
    z_ssm_conv = jnp.zeros((DEPTH, nb, SSM_CONV - 1, SSM_CONV_DIM), x_prompt.dtype)
    z_ssm_rec = jnp.zeros((DEPTH, nb, SSM_HEADS, SSM_HEADDIM, SSM_STATE), jnp.float32)
    y_prompt, p_dn_conv, p_dn_rec, p_ssm_conv, p_ssm_rec = _trunk(
        x_prompt, c_prompt, z_dn_conv, z_dn_rec, z_ssm_conv, z_ssm_rec, layers,
        w_ada_final, b_ada_final, final_norm_w)
    y_sample, s_dn_conv, s_dn_rec, s_ssm_conv, s_ssm_rec = _trunk(
        x_sample, c_sample, state_dn_conv, state_dn_rec, state_ssm_conv, state_ssm_rec, layers,
        w_ada_final, b_ada_final, final_norm_w)
    return (y_prompt, y_sample, p_dn_conv, p_dn_rec, p_ssm_conv, p_ssm_rec,
            s_dn_conv, s_dn_rec, s_ssm_conv, s_ssm_rec)
```

```python
import functools
import math

import jax
import jax.numpy as jnp
from jax import lax
from jax.experimental import pallas as pl
from jax.experimental.pallas import tpu as pltpu

F32 = jnp.float32
BF16 = jnp.bfloat16

D_MODEL = 1024
DN_HEADS = 4
DN_DK = 128
DN_DV = 128
DN_KEY = DN_HEADS * DN_DK
DN_VAL = DN_HEADS * DN_DV
CONV_K = 4
DN_CONV_DIM = 2 * DN_KEY + DN_VAL
SSM_INNER = D_MODEL
SSM_HEADDIM = 64
SSM_HEADS = SSM_INNER // SSM_HEADDIM
SSM_GROUPS = 2
SSM_STATE = 128
SSM_CONV_DIM = SSM_INNER + 2 * SSM_GROUPS * SSM_STATE
MOE_GROUPS = 4
MOE_PER_GROUP = 8
MOE_EXPERTS = MOE_GROUPS * MOE_PER_GROUP
MOE_FF = D_MODEL // 4
EPS = 1e-6
SCAN_CHUNK = 64

LANES = 128
SUBLANES = 8
CARRY_ROW = SUBLANES - (CONV_K - 1)

COL_DN_QKV = 0
COL_SSM_XBC = COL_DN_QKV + DN_CONV_DIM
COL_SSM_Z = COL_SSM_XBC + SSM_CONV_DIM
COL_GATES = COL_SSM_Z + SSM_INNER
COL_DN_Z = COL_GATES + 2 * D_MODEL
COL_SMALL = COL_DN_Z + DN_VAL
PROJ_TN = 1024
PROJ_N = 7 * PROJ_TN
SM_A, SM_B, SM_DT = 0, DN_HEADS, 2 * DN_HEADS
RT_GROUP, RT_EXPERT = 0, MOE_GROUPS

VMEM_LIMIT = 56 * 1024 * 1024


def _cparams(*sem):
    return pltpu.CompilerParams(dimension_semantics=sem, vmem_limit_bytes=VMEM_LIMIT)


def _dot(a, b):
    return jnp.dot(a.astype(BF16), b.astype(BF16), preferred_element_type=F32)


def _dot_nt(a, b):
    return lax.dot_general(a.astype(BF16), b.astype(BF16), (((1,), (1,)), ((), ())),
                           preferred_element_type=F32)


def _dot_tn(a, b):
    return lax.dot_general(a.astype(BF16), b.astype(BF16), (((0,), (0,)), ((), ())),
                           preferred_element_type=F32)


def _split3(x):
    hi = x.astype(BF16)
    r = x - hi.astype(F32)
    mid = r.astype(BF16)
    lo = (r - mid.astype(F32)).astype(BF16)
    return hi, mid, lo


def _dot_exact_lhs(a_bf, b):
    hi, mid, lo = _split3(b)
    d = functools.partial(jnp.dot, preferred_element_type=F32)
    return d(a_bf, hi) + (d(a_bf, mid) + d(a_bf, lo))


def _dot_x3(a, b):
    a_hi = a.astype(BF16)
    a_lo = (a - a_hi.astype(F32)).astype(BF16)
    b_hi = b.astype(BF16)
    b_lo = (b - b_hi.astype(F32)).astype(BF16)
    d = functools.partial(jnp.dot, preferred_element_type=F32)
    return d(a_hi, b_hi) + (d(a_hi, b_lo) + d(a_lo, b_hi))


def _sigmoid(x):
    return 1.0 / (1.0 + jnp.exp(-x))


def _silu(x):
    return x * _sigmoid(x)


def _softplus(x):
    return jnp.maximum(x, 0.0) + jnp.log1p(jnp.exp(-jnp.abs(x)))


def _rms(x, w):
    return x * lax.rsqrt(jnp.mean(x * x, axis=-1, keepdims=True) + EPS) * w


def _iota(shape, dim):
    return lax.broadcasted_iota(jnp.int32, shape, dim)


def _ada_kernel(c_ref, w_ref, b_ref, o_ref):
    o_ref[...] = _dot(_silu(c_ref[...]), w_ref[...]) + b_ref[...]


def _ada(c, w, b, tn=512):
    m, d = c.shape
    n = w.shape[1]
    return pl.pallas_call(
        _ada_kernel,
        grid=(n // tn,),
        in_specs=[pl.BlockSpec((m, d), lambda j: (0, 0)),
                  pl.BlockSpec((d, tn), lambda j: (0, j)),
                  pl.BlockSpec((1, tn), lambda j: (0, j))],
        out_specs=pl.BlockSpec((m, tn), lambda j: (0, j)),
        out_shape=jax.ShapeDtypeStruct((m, n), F32),
        compiler_params=_cparams("arbitrary"),
        name="ada_mod",
    )(c, w, b.reshape(1, n))


def _mod_spec(mod, tm):
    if mod.shape[1] == 1:
        return pl.BlockSpec((None, 1, D_MODEL), lambda b, i, *_: (b, 0, 0))
    return pl.BlockSpec((None, tm, D_MODEL), lambda b, i, *_: (b, i, 0))


def _inproj_kernel(x_ref, sc_ref, sh_ref, nw_ref, w_ref, o_ref, h_ref, *, tm, sub):
    @pl.when(pl.program_id(2) == 0)
    def _():
        per_token = sc_ref.shape[0] != 1

        def body(r, carry):
            rows = pl.ds(pl.multiple_of(r * sub, sub), sub)
            sc = sc_ref[rows, :] if per_token else sc_ref[...]
            sh = sh_ref[rows, :] if per_token else sh_ref[...]
            h = _rms(x_ref[rows, :], nw_ref[...]) * (1.0 + sc) + sh
            h_ref[rows, :] = h.astype(BF16)
            return carry

        lax.fori_loop(0, tm // sub, body, 0)

    o_ref[...] = jnp.dot(h_ref[...], w_ref[...], preferred_element_type=F32)


def _inproj(x3, sc, sh, nw, w_cat, tm):
    bx, lx, d = x3.shape
    n = w_cat.shape[1]
    kern = functools.partial(_inproj_kernel, tm=tm, sub=min(tm, 256))
    return pl.pallas_call(
        kern,
        grid=(bx, lx // tm, n // PROJ_TN),
        in_specs=[pl.BlockSpec((None, tm, d), lambda b, i, j: (b, i, 0)),
                  _mod_spec(sc, tm), _mod_spec(sh, tm),
                  pl.BlockSpec((1, d), lambda b, i, j: (0, 0)),
                  pl.BlockSpec((d, PROJ_TN), lambda b, i, j: (0, j))],
        out_specs=pl.BlockSpec((None, tm, PROJ_TN), lambda b, i, j: (b, i, j)),
        out_shape=jax.ShapeDtypeStruct((bx, lx, n), F32),
        scratch_shapes=[pltpu.VMEM((tm, d), BF16)],
        compiler_params=_cparams("arbitrary", "arbitrary", "arbitrary"),
        name="norm_inproj",
    )(x3, sc, sh, nw.reshape(1, d), w_cat)


def _conv_silu(u, ext_ref, bb, cw_ref, c, lc, bias=None):
    ext_ref[bb, SUBLANES:SUBLANES + c, :] = u
    y = ext_ref[bb, CARRY_ROW:CARRY_ROW + c, :] * cw_ref[0:1, :]
    for i in range(1, CONV_K):
        y = y + ext_ref[bb, CARRY_ROW + i:CARRY_ROW + i + c, :] * cw_ref[i:i + 1, :]
    ext_ref[bb, CARRY_ROW:SUBLANES, :] = ext_ref[bb, CARRY_ROW + lc:SUBLANES + lc, :]
    if bias is not None:
        y = y + bias
    return _silu(y)


def _run_chunks(chunk, bt, r, c):
    g = r // c
    if g == 1:
        for bb in range(bt):
            chunk(bb, 0)
    else:
        def body(ci, carry):
            for bb in range(bt):
                chunk(bb, pl.multiple_of(ci * c, c))
            return carry
        lax.fori_loop(0, g, body, 0)


INV_BASE = SUBLANES


def _inverse_masks(row, col, c):
    sh = lambda x, s: jnp.right_shift(x, int(math.log2(s)))
    diag = sh(row, INV_BASE) == sh(col, INV_BASE)
    merges = []
    s = INV_BASE
    while s < c:
        merges.append((sh(row, 2 * s) == sh(col, 2 * s))
                      & (jnp.bitwise_and(sh(row, s), 1) == 1) & (jnp.bitwise_and(sh(col, s), 1) == 0))
        s *= 2
    return diag, merges


def _unit_lower_inverse(lmat, eye_f, masks):
    diag, merges = masks
    n0 = -jnp.where(diag, lmat, 0.0)
    p = eye_f + n0
    npow = n0
    for _ in range(int(math.log2(INV_BASE)) - 1):
        npow = _dot_x3(npow, npow)
        p = p + _dot_x3(p, npow)
    for m in merges:
        p = p - _dot_x3(_dot_x3(p, jnp.where(m, lmat, 0.0)), p)
    return p

def _dn_kernel(qkv_ref, z_ref, sm_ref, cw_ref, alog_ref, bias_ref, nw_ref, cin_ref, sin_ref,
               o_ref, cout_ref, sout_ref, ext_ref, *, bt, r, c, lc, ng):
    j = pl.program_id(1)

    @pl.when(j == 0)
    def _():
        ext_ref[:, CARRY_ROW:SUBLANES, :] = cin_ref[...]
        sout_ref[...] = sin_ref[...]

    row = _iota((c, c), 0)
    col = _iota((c, c), 1)
    tri_incl = row >= col
    tri_strict = row > col
    eye_f = (row == col).astype(F32)
    tril_bf = tri_incl.astype(BF16)
    valid = _iota((c, 1), 0) < lc
    nexp_a = -jnp.exp(alog_ref[...])
    bias = bias_ref[...]
    inv_masks = _inverse_masks(row, col, c)

    def chunk(bb, r0):
        rows = pl.ds(bb * r + r0, c)
        y = _conv_silu(qkv_ref[rows, :], ext_ref, bb, cw_ref, c, lc)
        sm = sm_ref[rows, :]
        g_all = nexp_a * _softplus(sm + bias)
        beta_all = _sigmoid(sm)
        if lc < c:
            g_all = jnp.where(valid, g_all, 0.0)
            beta_all = jnp.where(valid, beta_all, 0.0)
        cs_all = _dot_exact_lhs(tril_bf, g_all)
        cs_t = cs_all.T
        ecs_all = jnp.exp(cs_all)
        outs = []
        for h in range(DN_HEADS):
            q = y[:, h * DN_DK:(h + 1) * DN_DK]
            k = y[:, DN_KEY + h * DN_DK:DN_KEY + (h + 1) * DN_DK]
            v = y[:, 2 * DN_KEY + h * DN_DV:2 * DN_KEY + (h + 1) * DN_DV]
            q = q * lax.rsqrt(jnp.sum(q * q, axis=-1, keepdims=True) + EPS) * (DN_DK ** -0.5)
            k = k * lax.rsqrt(jnp.sum(k * k, axis=-1, keepdims=True) + EPS)
            if lc < c:
                k = jnp.where(valid, k, 0.0)
            cs = cs_all[:, SM_A + h:SM_A + h + 1]
            ecs = ecs_all[:, SM_A + h:SM_A + h + 1]
            beta = beta_all[:, SM_B + h:SM_B + h + 1]
            decay = jnp.exp(jnp.where(tri_incl, cs - cs_t[SM_A + h:SM_A + h + 1, :], -jnp.inf))
            k_bf = k.astype(BF16)
            kk = _dot_nt(k_bf, k_bf)
            lmat = jnp.where(tri_strict, beta * kk * decay, 0.0)
            p = _unit_lower_inverse(lmat, eye_f, inv_masks)
            rhs = jnp.concatenate([v * beta, k * (beta * ecs)], axis=1)
            sol = _dot_x3(p, rhs)
            u, w = sol[:, :DN_DV], sol[:, DN_DV:]
            s = sout_ref[bb, h]
            s_bf = s.astype(BF16)
            ws_qs = _dot(jnp.concatenate([w, q * ecs], axis=0), s_bf)
            v_new = u - ws_qs[:c]
            v_new_bf = v_new.astype(BF16)
            qk = _dot_nt(q, k_bf) * decay
            o = ws_qs[c:] + _dot(qk, v_new_bf)
            last = cs_all[c - 1:c, SM_A + h:SM_A + h + 1]
            sout_ref[bb, h] = jnp.exp(last) * s + _dot_tn(k * jnp.exp(last - cs), v_new_bf)
            zg = z_ref[rows, h * DN_DV:(h + 1) * DN_DV]
            outs.append(_rms(o, nw_ref[...]) * _silu(zg))
        o_ref[rows, :] = jnp.concatenate(outs, axis=1)

    _run_chunks(chunk, bt, r, c)

    @pl.when(j == ng - 1)
    def _():
        cout_ref[...] = ext_ref[:, CARRY_ROW:SUBLANES, :]


def _small_row(vals, offset, fill=0.0):
    row = jnp.full((1, LANES), fill, F32)
    return lax.dynamic_update_slice(row, vals.astype(F32).reshape(1, -1), (0, offset))


def _dn_branch(proj, conv_w, a_log, dt_bias, norm_w, conv_in, rec_in, *, bt, r, c, lc, ng):
    nb = conv_in.shape[0] // bt
    rows = bt * r
    t = proj.shape[0]
    kern = functools.partial(_dn_kernel, bt=bt, r=r, c=c, lc=lc, ng=ng)
    rowmap = lambda col: (lambda i, j: (i * ng + j, col))
    const = lambda i, j: (0, 0)
    return pl.pallas_call(
        kern,
        grid=(nb, ng),
        in_specs=[pl.BlockSpec((rows, DN_CONV_DIM), rowmap(COL_DN_QKV // DN_CONV_DIM)),
                  pl.BlockSpec((rows, DN_VAL), rowmap(COL_DN_Z // DN_VAL)),
                  pl.BlockSpec((rows, LANES), rowmap(COL_SMALL // LANES)),
                  pl.BlockSpec((CONV_K, DN_CONV_DIM), const),
                  pl.BlockSpec((1, LANES), const),
                  pl.BlockSpec((1, LANES), const),
                  pl.BlockSpec((1, DN_DV), const),
                  pl.BlockSpec((bt, CONV_K - 1, DN_CONV_DIM), lambda i, j: (i, 0, 0)),
                  pl.BlockSpec((bt, DN_HEADS, DN_DK, DN_DV), lambda i, j: (i, 0, 0, 0))],
        out_specs=[pl.BlockSpec((rows, DN_VAL), lambda i, j: (i * ng + j, 0)),
                   pl.BlockSpec((bt, CONV_K - 1, DN_CONV_DIM), lambda i, j: (i, 0, 0)),
                   pl.BlockSpec((bt, DN_HEADS, DN_DK, DN_DV), lambda i, j: (i, 0, 0, 0))],
        out_shape=[jax.ShapeDtypeStruct((t, DN_VAL), F32),
                   jax.ShapeDtypeStruct(conv_in.shape, F32),
                   jax.ShapeDtypeStruct(rec_in.shape, F32)],
        scratch_shapes=[pltpu.VMEM((bt, c + SUBLANES, DN_CONV_DIM), F32)],
        compiler_params=_cparams("arbitrary", "arbitrary"),
        name="gated_delta",
    )(proj, proj, proj, conv_w, _small_row(a_log, SM_A), _small_row(dt_bias, SM_A),
      norm_w.reshape(1, DN_DV), conv_in, rec_in)


def _ssd_kernel(xbc_ref, z_ref, sm_ref, cw_ref, cb_ref, alog_ref, bias_ref, dvec_ref, nw_ref, cin_ref,
                hin_ref, y_ref, cout_ref, hout_ref, ext_ref, *, bt, r, c, lc, ng):
    j = pl.program_id(1)

    @pl.when(j == 0)
    def _():
        ext_ref[:, CARRY_ROW:SUBLANES, :] = cin_ref[...]
        hout_ref[...] = hin_ref[...]

    tril_bf = (_iota((c, c), 0) >= _iota((c, c), 1)).astype(BF16)
    row2 = _iota((c, 2 * c), 0)
    lane2 = _iota((c, 2 * c), 1)
    left2 = lane2 < c
    tri2 = row2 >= jnp.where(left2, lane2, lane2 - c)
    left_x = _iota((c, LANES), 1) < SSM_HEADDIM
    top_h = _iota((2 * SSM_HEADDIM, 1), 0) < SSM_HEADDIM
    valid = _iota((c, 1), 0) < lc
    nexp_a = -jnp.exp(alog_ref[...])
    bias = bias_ref[...]
    gw = SSM_INNER // SSM_GROUPS
    pairs_per_group = SSM_HEADS // SSM_GROUPS // 2

    def chunk(bb, r0):
        rows = pl.ds(bb * r + r0, c)
        y = _conv_silu(xbc_ref[rows, :], ext_ref, bb, cw_ref, c, lc, bias=cb_ref[...])
        sm = sm_ref[rows, :]
        dt_all = _softplus(sm + bias)
        if lc < c:
            dt_all = jnp.where(valid, dt_all, 0.0)
        cs_all = _dot_exact_lhs(tril_bf, dt_all * nexp_a)
        cs_t2 = jnp.concatenate([cs_all, cs_all], axis=0).T
        ecs_all = jnp.exp(cs_all)
        last = cs_all[c - 1:c, :]
        w_all = jnp.exp(last - cs_all)
        elast = jnp.exp(last)
        outs = []
        for g in range(SSM_GROUPS):
            bm = y[:, SSM_INNER + g * SSM_STATE:SSM_INNER + (g + 1) * SSM_STATE]
            cm = y[:, SSM_INNER + (SSM_GROUPS + g) * SSM_STATE:SSM_INNER + (SSM_GROUPS + g + 1) * SSM_STATE]
            bm_bf = bm.astype(BF16)
            cm_bf = cm.astype(BF16)
            cb2 = _dot_nt(cm_bf, jnp.concatenate([bm_bf, bm_bf], axis=0))
            for pi in range(pairs_per_group):
                pair = g * pairs_per_group + pi
                la, lb = SM_DT + 2 * pair, SM_DT + 2 * pair + 1
                xs = y[:, pair * LANES:(pair + 1) * LANES]
                xdt = xs * jnp.where(left_x, dt_all[:, la:la + 1], dt_all[:, lb:lb + 1])
                diff = (jnp.where(left2, cs_all[:, la:la + 1], cs_all[:, lb:lb + 1])
                        - jnp.where(left2[0:1], cs_t2[la:la + 1, :], cs_t2[lb:lb + 1, :]))
                decay2 = jnp.exp(jnp.where(tri2, diff, -jnp.inf))
                rhs = jnp.concatenate([jnp.where(left_x, xdt, 0.0), jnp.where(left_x, 0.0, xdt)], axis=0)
                y_intra = _dot(cb2 * decay2, rhs)
                hs = hout_ref[bb, pair * LANES:(pair + 1) * LANES, :]
                y_inter = _dot_nt(cm_bf, hs) * jnp.where(left_x, ecs_all[:, la:la + 1], ecs_all[:, lb:lb + 1])
                xw = xdt * jnp.where(left_x, w_all[:, la:la + 1], w_all[:, lb:lb + 1])
                hout_ref[bb, pair * LANES:(pair + 1) * LANES, :] = (
                    jnp.where(top_h, elast[:, la:la + 1], elast[:, lb:lb + 1]) * hs + _dot_tn(xw, bm_bf))
                outs.append(y_intra + y_inter + dvec_ref[:, pair * LANES:(pair + 1) * LANES] * xs)
        yz = jnp.concatenate(outs, axis=1) * _silu(z_ref[rows, :])
        y_ref[rows, :] = jnp.concatenate(
            [_rms(yz[:, g * gw:(g + 1) * gw], nw_ref[:, g * gw:(g + 1) * gw]) for g in range(SSM_GROUPS)], axis=1)

    _run_chunks(chunk, bt, r, c)

    @pl.when(j == ng - 1)
    def _():
        cout_ref[...] = ext_ref[:, CARRY_ROW:SUBLANES, :]


def _ssd_branch(proj, conv_w, conv_b, a_log, dt_bias, d_skip, norm_w, conv_in, rec_in, *, bt, r, c, lc, ng):
    nb = conv_in.shape[0] // bt
    rows = bt * r
    t = proj.shape[0]
    hrows = SSM_HEADS * SSM_HEADDIM
    rec2 = rec_in.reshape(rec_in.shape[0], hrows, SSM_STATE)
    kern = functools.partial(_ssd_kernel, bt=bt, r=r, c=c, lc=lc, ng=ng)
    rowmap = lambda col: (lambda i, j: (i * ng + j, col))
    const = lambda i, j: (0, 0)
    y, conv_out, rec_out = pl.pallas_call(
        kern,
        grid=(nb, ng),
        in_specs=[pl.BlockSpec((rows, SSM_CONV_DIM), rowmap(COL_SSM_XBC // SSM_CONV_DIM)),
                  pl.BlockSpec((rows, SSM_INNER), rowmap(COL_SSM_Z // SSM_INNER)),
                  pl.BlockSpec((rows, LANES), rowmap(COL_SMALL // LANES)),
                  pl.BlockSpec((CONV_K, SSM_CONV_DIM), const),
                  pl.BlockSpec((1, SSM_CONV_DIM), const),
                  pl.BlockSpec((1, LANES), const),
                  pl.BlockSpec((1, LANES), const),
                  pl.BlockSpec((1, SSM_INNER), const),
                  pl.BlockSpec((1, SSM_INNER), const),
                  pl.BlockSpec((bt, CONV_K - 1, SSM_CONV_DIM), lambda i, j: (i, 0, 0)),
                  pl.BlockSpec((bt, hrows, SSM_STATE), lambda i, j: (i, 0, 0))],
        out_specs=[pl.BlockSpec((rows, SSM_INNER), lambda i, j: (i * ng + j, 0)),
                   pl.BlockSpec((bt, CONV_K - 1, SSM_CONV_DIM), lambda i, j: (i, 0, 0)),
                   pl.BlockSpec((bt, hrows, SSM_STATE), lambda i, j: (i, 0, 0))],
        out_shape=[jax.ShapeDtypeStruct((t, SSM_INNER), F32),
                   jax.ShapeDtypeStruct(conv_in.shape, F32),
                   jax.ShapeDtypeStruct(rec2.shape, F32)],
        scratch_shapes=[pltpu.VMEM((bt, c + SUBLANES, SSM_CONV_DIM), F32)],
        compiler_params=_cparams("arbitrary", "arbitrary"),
        name="ssd_scan",
    )(proj, proj, proj, conv_w, conv_b.reshape(1, -1), _small_row(a_log, SM_DT), _small_row(dt_bias, SM_DT),
      jnp.repeat(d_skip.astype(F32), SSM_HEADDIM).reshape(1, SSM_INNER), norm_w.reshape(1, SSM_INNER),
      conv_in, rec2)
    return y, conv_out, rec_out.reshape(rec_in.shape)


def _route(logits):
    lane = _iota(logits.shape, 1).astype(F32)
    big = float(LANES)
    is_group = (lane >= RT_GROUP) & (lane < RT_GROUP + MOE_GROUPS)
    gl = jnp.where(is_group, logits, -jnp.inf)
    gmax = jnp.max(gl, axis=-1, keepdims=True)
    g_sel = jnp.min(jnp.where(gl == gmax, lane, big), axis=-1, keepdims=True) - RT_GROUP
    p_group = 1.0 / jnp.sum(jnp.exp(gl - gmax), axis=-1, keepdims=True)
    e_lo = RT_EXPERT + MOE_PER_GROUP * g_sel
    in_grp = (lane >= e_lo) & (lane < e_lo + MOE_PER_GROUP)
    el = jnp.where(in_grp, logits, -jnp.inf)
    ee = jnp.exp(el - jnp.max(el, axis=-1, keepdims=True))
    pe = jnp.where(in_grp, ee / jnp.sum(ee, axis=-1, keepdims=True), -1.0)
    p1 = jnp.max(pe, axis=-1, keepdims=True)
    i1 = jnp.min(jnp.where(pe == p1, lane, big), axis=-1, keepdims=True)
    pe2 = jnp.where(lane == i1, -1.0, pe)
    p2 = jnp.max(pe2, axis=-1, keepdims=True)
    i2 = jnp.min(jnp.where(pe2 == p2, lane, big), axis=-1, keepdims=True)
    tot = p1 + p2
    return jnp.where(lane == i1, p_group * p1 / tot, 0.0) + jnp.where(lane == i2, p_group * p2 / tot, 0.0)


def _merge_kernel(x_ref, og_ref, ys_ref, gates_ref, gt1_ref, sc2_ref, sh2_ref, nw2_ref, wdn_ref, wssm_ref,
                  wout_ref, wr_ref, br_ref, x1_ref, h2_ref, wtok_ref):
    y_dn = _dot(og_ref[...], wdn_ref[...])
    y_ssm = _dot(ys_ref[...], wssm_ref[...])
    merged = (_sigmoid(gates_ref[:, :D_MODEL]) * y_dn + _sigmoid(gates_ref[:, D_MODEL:]) * y_ssm)
    x1 = x_ref[...] + gt1_ref[...] * _dot(merged, wout_ref[...])
    x1_ref[...] = x1
    h2 = _rms(x1, nw2_ref[...]) * (1.0 + sc2_ref[...]) + sh2_ref[...]
    h2_ref[...] = h2.astype(BF16)
    wtok_ref[...] = _route(_dot_x3(h2, wr_ref[...]) + br_ref[...])


def _merge(x3, og, ys, proj, gt1, sc2, sh2, nw2, wdn, wssm, wout, wr, br, tm):
    bx, lx, d = x3.shape
    nl = lx // tm
    rowmap = lambda col: (lambda b, i: (b * nl + i, col))
    const = lambda b, i: (0, 0)
    tok = lambda w, dt: jax.ShapeDtypeStruct((bx, lx, w), dt)
    return pl.pallas_call(
        _merge_kernel,
        grid=(bx, nl),
        in_specs=[pl.BlockSpec((None, tm, d), lambda b, i: (b, i, 0)),
                  pl.BlockSpec((tm, DN_VAL), rowmap(0)),
                  pl.BlockSpec((tm, SSM_INNER), rowmap(0)),
                  pl.BlockSpec((tm, 2 * d), rowmap(COL_GATES // (2 * d))),
                  _mod_spec(gt1, tm), _mod_spec(sc2, tm), _mod_spec(sh2, tm),
                  pl.BlockSpec((1, d), const),
                  pl.BlockSpec(wdn.shape, const), pl.BlockSpec(wssm.shape, const), pl.BlockSpec(wout.shape, const),
                  pl.BlockSpec(wr.shape, const), pl.BlockSpec((1, LANES), const)],
        out_specs=[pl.BlockSpec((None, tm, d), lambda b, i: (b, i, 0)),
                   pl.BlockSpec((None, tm, d), lambda b, i: (b, i, 0)),
                   pl.BlockSpec((None, tm, LANES), lambda b, i: (b, i, 0))],
        out_shape=[tok(d, F32), tok(d, BF16), tok(LANES, F32)],
        compiler_params=_cparams("arbitrary", "arbitrary"),
        name="merge_route",
    )(x3, og, ys, proj, gt1, sc2, sh2, nw2.reshape(1, d), wdn, wssm, wout, wr, br)


def _moe_kernel(h2_ref, wtok_ref, wg_ref, wu_ref, wd_ref, x1_ref, gt2_ref, fsc_ref, fsh_ref, fnw_ref,
                y_ref, acc_ref):
    e = pl.program_id(2)

    @pl.when(e == 0)
    def _():
        acc_ref[...] = jnp.zeros_like(acc_ref)

    h = h2_ref[...]
    gate = jnp.dot(h, wg_ref[...], preferred_element_type=F32)
    up = jnp.dot(h, wu_ref[...], preferred_element_type=F32)
    wtok = wtok_ref[...]
    w_e = jnp.sum(jnp.where(_iota(wtok.shape, 1) == e + RT_EXPERT, wtok, 0.0), axis=-1, keepdims=True)
    acc_ref[...] += _dot(_silu(gate) * up * w_e, wd_ref[...])

    @pl.when(e == MOE_EXPERTS - 1)
    def _():
        x2 = x1_ref[...] + gt2_ref[...] * acc_ref[...]
        y_ref[...] = _rms(x2, fnw_ref[...]) * (1.0 + fsc_ref[...]) + fsh_ref[...]


def _moe(h2, wtok, wg, wu, wd, x1, gt2, fsc, fsh, fnw, tm):
    bx, lx, d = x1.shape
    tokspec = lambda w: pl.BlockSpec((None, tm, w), lambda b, i, e: (b, i, 0))
    return pl.pallas_call(
        _moe_kernel,
        grid=(bx, lx // tm, MOE_EXPERTS),
        in_specs=[tokspec(d), tokspec(LANES),
                  pl.BlockSpec((None, d, MOE_FF), lambda b, i, e: (e, 0, 0)),
                  pl.BlockSpec((None, d, MOE_FF), lambda b, i, e: (e, 0, 0)),
                  pl.BlockSpec((None, MOE_FF, d), lambda b, i, e: (e, 0, 0)),
                  tokspec(d), _mod_spec(gt2, tm), _mod_spec(fsc, tm), _mod_spec(fsh, tm),
                  pl.BlockSpec((1, d), lambda b, i, e: (0, 0))],
        out_specs=tokspec(d),
        out_shape=jax.ShapeDtypeStruct((bx, lx, d), F32),
        scratch_shapes=[pltpu.VMEM((tm, d), F32)],
        compiler_params=_cparams("arbitrary", "arbitrary", "arbitrary"),
        name="moe_final",
    )(h2, wtok, wg, wu, wd, x1, gt2, fsc, fsh, fnw.reshape(1, d))


def _prep_layer(lp):
    w_in = lp["w_in"]
    offs = [0]
    for s in (DN_CONV_DIM, DN_VAL, DN_HEADS, DN_HEADS, SSM_CONV_DIM, SSM_INNER, SSM_HEADS, D_MODEL, D_MODEL):
        offs.append(offs[-1] + s)
    seg = lambda i: w_in[:, offs[i]:offs[i + 1]]
    small = jnp.concatenate([seg(2), seg(3), seg(6)], axis=1)
    pad = jnp.zeros((D_MODEL, PROJ_N - COL_SMALL - small.shape[1]), F32)
    w_cat = jnp.concatenate([seg(0), seg(4), seg(5), seg(7), seg(8), seg(1), small, pad], axis=1).astype(BF16)
    wr = jnp.concatenate([lp["w_group_router"], lp["w_expert_router"],
                          jnp.zeros((D_MODEL, LANES - MOE_GROUPS - MOE_EXPERTS), F32)], axis=1)
    br = jnp.concatenate([lp["b_group_router"], lp["b_expert_router"],
                          jnp.zeros((LANES - MOE_GROUPS - MOE_EXPERTS,), F32)]).reshape(1, LANES)
    return dict(lp, w_cat=w_cat, wr=wr, br=br,
                wdn=lp["w_dn_out"].astype(BF16), wssm=lp["w_ssm_out"].astype(BF16), wout=lp["w_out"].astype(BF16),
                wg=lp["w_exp_gate"].astype(BF16), wu=lp["w_exp_up"].astype(BF16), wd=lp["w_exp_down"].astype(BF16))


def _trunk(x3, mods, fins, states, layers, final_norm_w, cfg):
    bx, lx, d = x3.shape
    tm = cfg["tm"]
    scan = dict(bt=cfg["bt"], r=cfg["r"], c=cfg["c"], lc=cfg["lc"], ng=cfg["ng"])
    new_states = []
    n_layers = len(layers)
    for l, lp in enumerate(layers):
        sh1, sc1, gt1, sh2, sc2, gt2 = mods[l]
        dn_conv, dn_rec, ssm_conv, ssm_rec = states[l]
        proj = _inproj(x3, sc1, sh1, lp["norm_mix_w"], lp["w_cat"], tm).reshape(bx * lx, PROJ_N)
        og, dn_conv_new, dn_rec_new = _dn_branch(proj, lp["dn_conv_w"], lp["dn_A_log"], lp["dn_dt_bias"],
                                                 lp["dn_norm_w"], dn_conv, dn_rec, **scan)
        ys, ssm_conv_new, ssm_rec_new = _ssd_branch(proj, lp["ssm_conv_w"], lp["ssm_conv_b"], lp["ssm_A_log"],
                                                    lp["ssm_dt_bias"], lp["ssm_D"], lp["ssm_norm_w"],
                                                    ssm_conv, ssm_rec, **scan)
        x1, h2, wtok = _merge(x3, og, ys, proj, gt1, sc2, sh2, lp["norm_ffn_w"], lp["wdn"], lp["wssm"], lp["wout"],
                              lp["wr"], lp["br"], cfg["tm_merge"])
        if l == n_layers - 1:
            fsh, fsc, fnw = fins[0], fins[1], final_norm_w
            x3 = _moe(h2, wtok, lp["wg"], lp["wu"], lp["wd"], x1, gt2, fsc, fsh, fnw, cfg["tm_moe"])
        else:
            raise NotImplementedError("only the last layer fuses the final norm; depth is 1 here")
        new_states.append((dn_conv_new, dn_rec_new, ssm_conv_new, ssm_rec_new))
    return x3, new_states


def _per_seq(m):
    return m[:, None, :]


def kernel(x_prompt, x_sample, c_prompt, c_sample, state_dn_conv, state_dn_rec, state_ssm_conv, state_ssm_rec, w_ada, b_ada, norm_mix_w, w_in, dn_conv_w, dn_A_log, dn_dt_bias, dn_norm_w, w_dn_out, ssm_conv_w, ssm_conv_b, ssm_A_log, ssm_dt_bias, ssm_D, ssm_norm_w, w_ssm_out, w_out, norm_ffn_w, w_group_router, b_group_router, w_expert_router, b_expert_router, w_exp_gate, w_exp_up, w_exp_down, w_ada_final, b_ada_final, final_norm_w):
    depth = w_ada.shape[0]
    assert depth == 1
    per_layer = dict(w_ada=w_ada, b_ada=b_ada, norm_mix_w=norm_mix_w, w_in=w_in, dn_conv_w=dn_conv_w,
                     dn_A_log=dn_A_log, dn_dt_bias=dn_dt_bias, dn_norm_w=dn_norm_w, w_dn_out=w_dn_out,
                     ssm_conv_w=ssm_conv_w, ssm_conv_b=ssm_conv_b, ssm_A_log=ssm_A_log, ssm_dt_bias=ssm_dt_bias,
                     ssm_D=ssm_D, ssm_norm_w=ssm_norm_w, w_ssm_out=w_ssm_out, w_out=w_out, norm_ffn_w=norm_ffn_w,
                     w_group_router=w_group_router, b_group_router=b_group_router,
                     w_expert_router=w_expert_router, b_expert_router=b_expert_router,
                     w_exp_gate=w_exp_gate, w_exp_up=w_exp_up, w_exp_down=w_exp_down)
    layers = [_prep_layer({k: v[l] for k, v in per_layer.items()}) for l in range(depth)]

    nbp, lp_, d = x_prompt.shape
    nbs, ls, _ = x_sample.shape
    c_all = jnp.concatenate([c_prompt, c_sample], axis=0)
    mod_all = [_ada(c_all, lyr["w_ada"], lyr["b_ada"]) for lyr in layers]
    fin_all = _ada(c_all, w_ada_final, b_ada_final)

    mods_p = [[_per_seq(m) for m in jnp.split(ma[:nbp], 6, axis=-1)] for ma in mod_all]
    fins_p = [_per_seq(m) for m in jnp.split(fin_all[:nbp], 2, axis=-1)]
    zeros_p = [(jnp.zeros((nbp, CONV_K - 1, DN_CONV_DIM), F32), jnp.zeros((nbp, DN_HEADS, DN_DK, DN_DV), F32),
                jnp.zeros((nbp, CONV_K - 1, SSM_CONV_DIM), F32),
                jnp.zeros((nbp, SSM_HEADS, SSM_HEADDIM, SSM_STATE), F32)) for _ in range(depth)]
    c_p = min(SCAN_CHUNK, lp_)
    r_p = min(lp_, 8 * c_p)
    cfg_p = dict(tm=min(lp_, 1024), tm_merge=min(lp_, 512), tm_moe=min(lp_, 1024), bt=1, r=r_p, c=c_p, lc=c_p, ng=lp_ // r_p)
    y_p, st_p = _trunk(x_prompt, mods_p, fins_p, zeros_p, layers, final_norm_w, cfg_p)

    lpad = -(-ls // SUBLANES) * SUBLANES
    bt_s = 8
    xs = jnp.pad(x_sample, ((0, 0), (0, lpad - ls), (0, 0))).reshape(1, nbs * lpad, d)
    per_tok = lambda m: jnp.repeat(m, lpad, axis=0)[None]
    mods_s = [[per_tok(m) for m in jnp.split(ma[nbp:], 6, axis=-1)] for ma in mod_all]
    fins_s = [per_tok(m) for m in jnp.split(fin_all[nbp:], 2, axis=-1)]
    st_in = [(state_dn_conv[l], state_dn_rec[l], state_ssm_conv[l], state_ssm_rec[l]) for l in range(depth)]
    ts = nbs * lpad
    cfg_s = dict(tm=ts, tm_merge=min(ts, 512), tm_moe=min(ts, 512), bt=bt_s, r=lpad, c=lpad, lc=ls, ng=1)
    y_s, st_s = _trunk(xs, mods_s, fins_s, st_in, layers, final_norm_w, cfg_s)
    y_s = y_s.reshape(nbs, lpad, d)[:, :ls]

    stack = lambda sts, i: jnp.stack([s[i] for s in sts])
    return (y_p, y_s, stack(st_p, 0), stack(st_p, 1), stack(st_p, 2), stack(st_p, 3),
            stack(st_s, 0), stack(st_s, 1), stack(st_s, 2), stack(st_s, 3))
```

```python
import functools
import math

import jax
import jax.numpy as jnp
from jax import lax
from jax.experimental import pallas as pl
from jax.experimental.pallas import tpu as pltpu

F32 = jnp.float32
BF16 = jnp.bfloat16

D_MODEL = 1024
DN_HEADS = 4
DN_DK = 128
DN_DV = 128
DN_KEY = DN_HEADS * DN_DK
DN_VAL = DN_HEADS * DN_DV
CONV_K = 4
DN_CONV_DIM = 2 * DN_KEY + DN_VAL
SSM_INNER = D_MODEL
SSM_HEADDIM = 64
SSM_HEADS = SSM_INNER // SSM_HEADDIM
SSM_GROUPS = 2
SSM_STATE = 128
SSM_CONV_DIM = SSM_INNER + 2 * SSM_GROUPS * SSM_STATE
MOE_GROUPS = 4
MOE_PER_GROUP = 8
MOE_EXPERTS = MOE_GROUPS * MOE_PER_GROUP
MOE_FF = D_MODEL // 4
EPS = 1e-6
SCAN_CHUNK = 64

LANES = 128
SUBLANES = 8
CARRY_ROW = SUBLANES - (CONV_K - 1)

COL_DN_QKV = 0
COL_SSM_XBC = COL_DN_QKV + DN_CONV_DIM
COL_SSM_Z = COL_SSM_XBC + SSM_CONV_DIM
COL_GATES = COL_SSM_Z + SSM_INNER
COL_DN_Z = COL_GATES + 2 * D_MODEL
COL_SMALL = COL_DN_Z + DN_VAL
PROJ_TN = 1024
PROJ_N = 7 * PROJ_TN
SM_A, SM_B, SM_DT = 0, DN_HEADS, 2 * DN_HEADS
RT_GROUP, RT_EXPERT = 0, MOE_GROUPS

VMEM_LIMIT = 56 * 1024 * 1024


def _cparams(*sem):
    return pltpu.CompilerParams(dimension_semantics=sem, vmem_limit_bytes=VMEM_LIMIT)


def _dot(a, b):
    return jnp.dot(a.astype(BF16), b.astype(BF16), preferred_element_type=F32)


def _dot_nt(a, b):
    return lax.dot_general(a.astype(BF16), b.astype(BF16), (((1,), (1,)), ((), ())),
                           preferred_element_type=F32)


def _dot_tn(a, b):
    return lax.dot_general(a.astype(BF16), b.astype(BF16), (((0,), (0,)), ((), ())),
                           preferred_element_type=F32)


def _split3(x):
    hi = x.astype(BF16)
    r = x - hi.astype(F32)
    mid = r.astype(BF16)
    lo = (r - mid.astype(F32)).astype(BF16)
    return hi, mid, lo


def _dot_exact_lhs(a_bf, b):
    hi, mid, lo = _split3(b)
    d = functools.partial(jnp.dot, preferred_element_type=F32)
    return d(a_bf, hi) + (d(a_bf, mid) + d(a_bf, lo))


def _dot_x3(a, b):
    a_hi = a.astype(BF16)
    a_lo = (a - a_hi.astype(F32)).astype(BF16)
    b_hi = b.astype(BF16)
    b_lo = (b - b_hi.astype(F32)).astype(BF16)
    d = functools.partial(jnp.dot, preferred_element_type=F32)
    return d(a_hi, b_hi) + (d(a_hi, b_lo) + d(a_lo, b_hi))


def _sigmoid(x):
    return 1.0 / (1.0 + jnp.exp(-x))


def _silu(x):
    return x * _sigmoid(x)


def _softplus(x):
    return jnp.maximum(x, 0.0) + jnp.log1p(jnp.exp(-jnp.abs(x)))


def _rms(x, w):
    return x * lax.rsqrt(jnp.mean(x * x, axis=-1, keepdims=True) + EPS) * w


def _iota(shape, dim):
    return lax.broadcasted_iota(jnp.int32, shape, dim)


def _ada_kernel(c_ref, w_ref, b_ref, o_ref):
    o_ref[...] = _dot(_silu(c_ref[...]), w_ref[...]) + b_ref[...]


def _ada(c, w, b, tn=512):
    m, d = c.shape
    n = w.shape[1]
    return pl.pallas_call(
        _ada_kernel,
        grid=(n // tn,),
        in_specs=[pl.BlockSpec((m, d), lambda j: (0, 0)),
                  pl.BlockSpec((d, tn), lambda j: (0, j)),
                  pl.BlockSpec((1, tn), lambda j: (0, j))],
        out_specs=pl.BlockSpec((m, tn), lambda j: (0, j)),
        out_shape=jax.ShapeDtypeStruct((m, n), F32),
        compiler_params=_cparams("arbitrary"),
        name="ada_mod",
    )(c, w, b.reshape(1, n))


def _mod_spec(mod, tm):
    if mod.shape[1] == 1:
        return pl.BlockSpec((None, 1, D_MODEL), lambda b, i, *_: (b, 0, 0))
    return pl.BlockSpec((None, tm, D_MODEL), lambda b, i, *_: (b, i, 0))


def _inproj_kernel(x_ref, sc_ref, sh_ref, nw_ref, w_ref, o_ref, h_ref, *, tm, sub):
    @pl.when(pl.program_id(2) == 0)
    def _():
        per_token = sc_ref.shape[0] != 1

        def body(r, carry):
            rows = pl.ds(pl.multiple_of(r * sub, sub), sub)
            sc = sc_ref[rows, :] if per_token else sc_ref[...]
            sh = sh_ref[rows, :] if per_token else sh_ref[...]
            h = _rms(x_ref[rows, :], nw_ref[...]) * (1.0 + sc) + sh
            h_ref[rows, :] = h.astype(BF16)
            return carry

        lax.fori_loop(0, tm // sub, body, 0)

    o_ref[...] = jnp.dot(h_ref[...], w_ref[...], preferred_element_type=F32)


def _inproj(x3, sc, sh, nw, w_cat, tm):
    bx, lx, d = x3.shape
    n = w_cat.shape[1]
    kern = functools.partial(_inproj_kernel, tm=tm, sub=min(tm, 256))
    return pl.pallas_call(
        kern,
        grid=(bx, lx // tm, n // PROJ_TN),
        in_specs=[pl.BlockSpec((None, tm, d), lambda b, i, j: (b, i, 0)),
                  _mod_spec(sc, tm), _mod_spec(sh, tm),
                  pl.BlockSpec((1, d), lambda b, i, j: (0, 0)),
                  pl.BlockSpec((d, PROJ_TN), lambda b, i, j: (0, j))],
        out_specs=pl.BlockSpec((None, tm, PROJ_TN), lambda b, i, j: (b, i, j)),
        out_shape=jax.ShapeDtypeStruct((bx, lx, n), F32),
        scratch_shapes=[pltpu.VMEM((tm, d), BF16)],
        compiler_params=_cparams("arbitrary", "arbitrary", "arbitrary"),
        name="norm_inproj",
    )(x3, sc, sh, nw.reshape(1, d), w_cat)


def _conv_silu(u, ext_ref, bb, cw_ref, c, lc, bias=None):
    ext_ref[bb, SUBLANES:SUBLANES + c, :] = u
    y = ext_ref[bb, CARRY_ROW:CARRY_ROW + c, :] * cw_ref[0:1, :]
    for i in range(1, CONV_K):
        y = y + ext_ref[bb, CARRY_ROW + i:CARRY_ROW + i + c, :] * cw_ref[i:i + 1, :]
    ext_ref[bb, CARRY_ROW:SUBLANES, :] = ext_ref[bb, CARRY_ROW + lc:SUBLANES + lc, :]
    if bias is not None:
        y = y + bias
    return _silu(y)


def _run_rounds(process, bt, r, c, cpi):
    g = r // c
    if g == 1:
        process([[(bb, 0) for bb in range(bt)]])
    else:
        def body(ci, carry):
            process([[(bb, pl.multiple_of((ci * cpi + t) * c, c)) for bb in range(bt)] for t in range(cpi)])
            return carry
        lax.fori_loop(0, g // cpi, body, 0)


INV_BASE = SUBLANES


def _inverse_masks(row, col, c):
    sh = lambda x, s: jnp.right_shift(x, int(math.log2(s)))
    diag = sh(row, INV_BASE) == sh(col, INV_BASE)
    merges = []
    s = INV_BASE
    while s < c:
        merges.append((sh(row, 2 * s) == sh(col, 2 * s))
                      & (jnp.bitwise_and(sh(row, s), 1) == 1) & (jnp.bitwise_and(sh(col, s), 1) == 0))
        s *= 2
    return diag, merges


def _unit_lower_inverses(lmats, eye_f, masks):
    diag, merges = masks
    npows = [-jnp.where(diag, l, 0.0) for l in lmats]
    ps = [eye_f + n for n in npows]
    for _ in range(int(math.log2(INV_BASE)) - 1):
        npows = [_dot(n, n) for n in npows]
        ps = [p + _dot(p, n) for p, n in zip(ps, npows)]
    for m in merges:
        ts = [_dot(p, jnp.where(m, l, 0.0)) for p, l in zip(ps, lmats)]
        ps = [p - _dot(t, p) for t, p in zip(ts, ps)]
    return ps


def _dn_kernel(qkv_ref, z_ref, sm_ref, cw_ref, alog_ref, bias_ref, nw_ref, cin_ref, sin_ref,
               o_ref, cout_ref, sout_ref, ext_ref, *, bt, r, c, lc, ng, cpi):
    j = pl.program_id(1)

    @pl.when(j == 0)
    def _():
        ext_ref[:, CARRY_ROW:SUBLANES, :] = cin_ref[...]
        sout_ref[...] = sin_ref[...]

    row = _iota((c, c), 0)
    col = _iota((c, c), 1)
    tri_incl = row >= col
    tri_strict = row > col
    eye_f = (row == col).astype(F32)
    tril_bf = tri_incl.astype(BF16)
    valid = _iota((c, 1), 0) < lc
    nexp_a = -jnp.exp(alog_ref[...])
    bias = bias_ref[...]
    inv_masks = _inverse_masks(row, col, c)
    heads = range(DN_HEADS)

    def prep(bb, r0):
        rows = pl.ds(bb * r + r0, c)
        y = _conv_silu(qkv_ref[rows, :], ext_ref, bb, cw_ref, c, lc)
        sm = sm_ref[rows, :]
        g_all = nexp_a * _softplus(sm + bias)
        beta_all = _sigmoid(sm)
        if lc < c:
            g_all = jnp.where(valid, g_all, 0.0)
            beta_all = jnp.where(valid, beta_all, 0.0)
        cs_all = _dot_exact_lhs(tril_bf, g_all)
        cs_t = cs_all.T
        ecs_all = jnp.exp(cs_all)
        per_head = []
        for h in heads:
            q = y[:, h * DN_DK:(h + 1) * DN_DK]
            k = y[:, DN_KEY + h * DN_DK:DN_KEY + (h + 1) * DN_DK]
            v = y[:, 2 * DN_KEY + h * DN_DV:2 * DN_KEY + (h + 1) * DN_DV]
            q = q * lax.rsqrt(jnp.sum(q * q, axis=-1, keepdims=True) + EPS) * (DN_DK ** -0.5)
            k = k * lax.rsqrt(jnp.sum(k * k, axis=-1, keepdims=True) + EPS)
            if lc < c:
                k = jnp.where(valid, k, 0.0)
            cs = cs_all[:, SM_A + h:SM_A + h + 1]
            ecs = ecs_all[:, SM_A + h:SM_A + h + 1]
            beta = beta_all[:, SM_B + h:SM_B + h + 1]
            last = cs_all[c - 1:c, SM_A + h:SM_A + h + 1]
            decay = jnp.exp(jnp.where(tri_incl, cs - cs_t[SM_A + h:SM_A + h + 1, :], -jnp.inf))
            per_head.append(dict(q_ecs=(q * ecs).astype(BF16), q=q.astype(BF16), k=k.astype(BF16), beta=beta,
                                 decay=decay, elast=jnp.exp(last),
                                 k_tail=(k * jnp.exp(last - cs)).astype(BF16),
                                 rhs=jnp.concatenate([v * beta, k * (beta * ecs)], axis=1).astype(BF16)))
        return per_head

    def process(rounds):
        items = [it for rnd in rounds for it in rnd]
        pre = [prep(bb, r0) for bb, r0 in items]
        chains = [(i, h) for i in range(len(items)) for h in heads]
        a = {ch: pre[ch[0]][ch[1]] for ch in chains}
        kk = {ch: _dot_nt(a[ch]["k"], a[ch]["k"]) for ch in chains}
        qk = {ch: _dot_nt(a[ch]["q"], a[ch]["k"]) * a[ch]["decay"] for ch in chains}
        lmats = [jnp.where(tri_strict, a[ch]["beta"] * kk[ch] * a[ch]["decay"], 0.0) for ch in chains]
        pinv = _unit_lower_inverses(lmats, eye_f, inv_masks)
        sol = {ch: _dot(p, a[ch]["rhs"]) for ch, p in zip(chains, pinv)}
        first = 0
        for rnd in rounds:
            idx = range(first, first + len(rnd))
            first += len(rnd)
            rch = [(i, h) for i in idx for h in heads]
            s = {ch: sout_ref[items[ch[0]][0], ch[1]] for ch in rch}
            ws_qs = {ch: _dot(jnp.concatenate([sol[ch][:, DN_DV:].astype(BF16), a[ch]["q_ecs"]], axis=0), s[ch])
                     for ch in rch}
            v_new = {ch: (sol[ch][:, :DN_DV] - ws_qs[ch][:c]).astype(BF16) for ch in rch}
            o = {ch: ws_qs[ch][c:] + _dot(qk[ch], v_new[ch]) for ch in rch}
            for ch in rch:
                sout_ref[items[ch[0]][0], ch[1]] = a[ch]["elast"] * s[ch] + _dot_tn(a[ch]["k_tail"], v_new[ch])
            for i in idx:
                bb, r0 = items[i]
                rows = pl.ds(bb * r + r0, c)
                o_ref[rows, :] = jnp.concatenate(
                    [_rms(o[(i, h)], nw_ref[...]) * _silu(z_ref[rows, h * DN_DV:(h + 1) * DN_DV])
                     for h in heads], axis=1)

    _run_rounds(process, bt, r, c, cpi)

    @pl.when(j == ng - 1)
    def _():
        cout_ref[...] = ext_ref[:, CARRY_ROW:SUBLANES, :]


def _small_row(vals, offset, fill=0.0):
    row = jnp.full((1, LANES), fill, F32)
    return lax.dynamic_update_slice(row, vals.astype(F32).reshape(1, -1), (0, offset))


def _dn_branch(proj, conv_w, a_log, dt_bias, norm_w, conv_in, rec_in, *, bt, r, c, lc, ng, cpi):
    nb = conv_in.shape[0] // bt
    rows = bt * r
    t = proj.shape[0]
    kern = functools.partial(_dn_kernel, bt=bt, r=r, c=c, lc=lc, ng=ng, cpi=cpi)
    rowmap = lambda col: (lambda i, j: (i * ng + j, col))
    const = lambda i, j: (0, 0)
    return pl.pallas_call(
        kern,
        grid=(nb, ng),
        in_specs=[pl.BlockSpec((rows, DN_CONV_DIM), rowmap(COL_DN_QKV // DN_CONV_DIM)),
                  pl.BlockSpec((rows, DN_VAL), rowmap(COL_DN_Z // DN_VAL)),
                  pl.BlockSpec((rows, LANES), rowmap(COL_SMALL // LANES)),
                  pl.BlockSpec((CONV_K, DN_CONV_DIM), const),
                  pl.BlockSpec((1, LANES), const),
                  pl.BlockSpec((1, LANES), const),
                  pl.BlockSpec((1, DN_DV), const),
                  pl.BlockSpec((bt, CONV_K - 1, DN_CONV_DIM), lambda i, j: (i, 0, 0)),
                  pl.BlockSpec((bt, DN_HEADS, DN_DK, DN_DV), lambda i, j: (i, 0, 0, 0))],
        out_specs=[pl.BlockSpec((rows, DN_VAL), lambda i, j: (i * ng + j, 0)),
                   pl.BlockSpec((bt, CONV_K - 1, DN_CONV_DIM), lambda i, j: (i, 0, 0)),
                   pl.BlockSpec((bt, DN_HEADS, DN_DK, DN_DV), lambda i, j: (i, 0, 0, 0))],
        out_shape=[jax.ShapeDtypeStruct((t, DN_VAL), F32),
                   jax.ShapeDtypeStruct(conv_in.shape, F32),
                   jax.ShapeDtypeStruct(rec_in.shape, F32)],
        scratch_shapes=[pltpu.VMEM((bt, c + SUBLANES, DN_CONV_DIM), F32)],
        compiler_params=_cparams("arbitrary", "arbitrary"),
        name="gated_delta",
    )(proj, proj, proj, conv_w, _small_row(a_log, SM_A), _small_row(dt_bias, SM_A),
      norm_w.reshape(1, DN_DV), conv_in, rec_in)


def _ssd_kernel(xbc_ref, z_ref, sm_ref, cw_ref, cb_ref, alog_ref, bias_ref, dvec_ref, nw_ref, cin_ref,
                hin_ref, y_ref, cout_ref, hout_ref, ext_ref, *, bt, r, c, lc, ng, cpi):
    j = pl.program_id(1)

    @pl.when(j == 0)
    def _():
        ext_ref[:, CARRY_ROW:SUBLANES, :] = cin_ref[...]
        hout_ref[...] = hin_ref[...]

    tril_bf = (_iota((c, c), 0) >= _iota((c, c), 1)).astype(BF16)
    row2 = _iota((c, 2 * c), 0)
    lane2 = _iota((c, 2 * c), 1)
    left2 = lane2 < c
    tri2 = row2 >= jnp.where(left2, lane2, lane2 - c)
    left_x = _iota((c, LANES), 1) < SSM_HEADDIM
    top_h = _iota((2 * SSM_HEADDIM, 1), 0) < SSM_HEADDIM
    valid = _iota((c, 1), 0) < lc
    nexp_a = -jnp.exp(alog_ref[...])
    bias = bias_ref[...]
    gw = SSM_INNER // SSM_GROUPS
    pairs_per_group = SSM_HEADS // SSM_GROUPS // 2

    def chunk(bb, r0):
        rows = pl.ds(bb * r + r0, c)
        y = _conv_silu(xbc_ref[rows, :], ext_ref, bb, cw_ref, c, lc, bias=cb_ref[...])
        sm = sm_ref[rows, :]
        dt_all = _softplus(sm + bias)
        if lc < c:
            dt_all = jnp.where(valid, dt_all, 0.0)
        cs_all = _dot_exact_lhs(tril_bf, dt_all * nexp_a)
        cs_t2 = jnp.concatenate([cs_all, cs_all], axis=0).T
        ecs_all = jnp.exp(cs_all)
        last = cs_all[c - 1:c, :]
        w_all = jnp.exp(last - cs_all)
        elast = jnp.exp(last)
        outs = []
        for g in range(SSM_GROUPS):
            bm = y[:, SSM_INNER + g * SSM_STATE:SSM_INNER + (g + 1) * SSM_STATE]
            cm = y[:, SSM_INNER + (SSM_GROUPS + g) * SSM_STATE:SSM_INNER + (SSM_GROUPS + g + 1) * SSM_STATE]
            bm_bf = bm.astype(BF16)
            cm_bf = cm.astype(BF16)
            cb2 = _dot_nt(cm_bf, jnp.concatenate([bm_bf, bm_bf], axis=0))
            for pi in range(pairs_per_group):
                pair = g * pairs_per_group + pi
                la, lb = SM_DT + 2 * pair, SM_DT + 2 * pair + 1
                xs = y[:, pair * LANES:(pair + 1) * LANES]
                xdt = xs * jnp.where(left_x, dt_all[:, la:la + 1], dt_all[:, lb:lb + 1])
                diff = (jnp.where(left2, cs_all[:, la:la + 1], cs_all[:, lb:lb + 1])
                        - jnp.where(left2[0:1], cs_t2[la:la + 1, :], cs_t2[lb:lb + 1, :]))
                decay2 = jnp.exp(jnp.where(tri2, diff, -jnp.inf))
                rhs = jnp.concatenate([jnp.where(left_x, xdt, 0.0), jnp.where(left_x, 0.0, xdt)], axis=0)
                y_intra = _dot(cb2 * decay2, rhs)
                hs = hout_ref[bb, pair * LANES:(pair + 1) * LANES, :]
                y_inter = _dot_nt(cm_bf, hs) * jnp.where(left_x, ecs_all[:, la:la + 1], ecs_all[:, lb:lb + 1])
                xw = xdt * jnp.where(left_x, w_all[:, la:la + 1], w_all[:, lb:lb + 1])
                hout_ref[bb, pair * LANES:(pair + 1) * LANES, :] = (
                    jnp.where(top_h, elast[:, la:la + 1], elast[:, lb:lb + 1]) * hs + _dot_tn(xw, bm_bf))
                outs.append(y_intra + y_inter + dvec_ref[:, pair * LANES:(pair + 1) * LANES] * xs)
        yz = jnp.concatenate(outs, axis=1) * _silu(z_ref[rows, :])
        y_ref[rows, :] = jnp.concatenate(
            [_rms(yz[:, g * gw:(g + 1) * gw], nw_ref[:, g * gw:(g + 1) * gw]) for g in range(SSM_GROUPS)], axis=1)

    def process(rounds):
        for rnd in rounds:
            for bb, r0 in rnd:
                chunk(bb, r0)

    _run_rounds(process, bt, r, c, cpi)

    @pl.when(j == ng - 1)
    def _():
        cout_ref[...] = ext_ref[:, CARRY_ROW:SUBLANES, :]


def _ssd_branch(proj, conv_w, conv_b, a_log, dt_bias, d_skip, norm_w, conv_in, rec_in, *, bt, r, c, lc, ng, cpi):
    nb = conv_in.shape[0] // bt
    rows = bt * r
    t = proj.shape[0]
    hrows = SSM_HEADS * SSM_HEADDIM
    rec2 = rec_in.reshape(rec_in.shape[0], hrows, SSM_STATE)
    kern = functools.partial(_ssd_kernel, bt=bt, r=r, c=c, lc=lc, ng=ng, cpi=cpi)
    rowmap = lambda col: (lambda i, j: (i * ng + j, col))
    const = lambda i, j: (0, 0)
    y, conv_out, rec_out = pl.pallas_call(
        kern,
        grid=(nb, ng),
        in_specs=[pl.BlockSpec((rows, SSM_CONV_DIM), rowmap(COL_SSM_XBC // SSM_CONV_DIM)),
                  pl.BlockSpec((rows, SSM_INNER), rowmap(COL_SSM_Z // SSM_INNER)),
                  pl.BlockSpec((rows, LANES), rowmap(COL_SMALL // LANES)),
                  pl.BlockSpec((CONV_K, SSM_CONV_DIM), const),
                  pl.BlockSpec((1, SSM_CONV_DIM), const),
                  pl.BlockSpec((1, LANES), const),
                  pl.BlockSpec((1, LANES), const),
                  pl.BlockSpec((1, SSM_INNER), const),
                  pl.BlockSpec((1, SSM_INNER), const),
                  pl.BlockSpec((bt, CONV_K - 1, SSM_CONV_DIM), lambda i, j: (i, 0, 0)),
                  pl.BlockSpec((bt, hrows, SSM_STATE), lambda i, j: (i, 0, 0))],
        out_specs=[pl.BlockSpec((rows, SSM_INNER), lambda i, j: (i * ng + j, 0)),
                   pl.BlockSpec((bt, CONV_K - 1, SSM_CONV_DIM), lambda i, j: (i, 0, 0)),
                   pl.BlockSpec((bt, hrows, SSM_STATE), lambda i, j: (i, 0, 0))],
        out_shape=[jax.ShapeDtypeStruct((t, SSM_INNER), F32),
                   jax.ShapeDtypeStruct(conv_in.shape, F32),
                   jax.ShapeDtypeStruct(rec2.shape, F32)],
        scratch_shapes=[pltpu.VMEM((bt, c + SUBLANES, SSM_CONV_DIM), F32)],
        compiler_params=_cparams("arbitrary", "arbitrary"),
        name="ssd_scan",
    )(proj, proj, proj, conv_w, conv_b.reshape(1, -1), _small_row(a_log, SM_DT), _small_row(dt_bias, SM_DT),
      jnp.repeat(d_skip.astype(F32), SSM_HEADDIM).reshape(1, SSM_INNER), norm_w.reshape(1, SSM_INNER),
      conv_in, rec2)
    return y, conv_out, rec_out.reshape(rec_in.shape)


def _route(logits):
    lane = _iota(logits.shape, 1).astype(F32)
    big = float(LANES)
    is_group = (lane >= RT_GROUP) & (lane < RT_GROUP + MOE_GROUPS)
    gl = jnp.where(is_group, logits, -jnp.inf)
    gmax = jnp.max(gl, axis=-1, keepdims=True)
    g_sel = jnp.min(jnp.where(gl == gmax, lane, big), axis=-1, keepdims=True) - RT_GROUP
    p_group = 1.0 / jnp.sum(jnp.exp(gl - gmax), axis=-1, keepdims=True)
    e_lo = RT_EXPERT + MOE_PER_GROUP * g_sel
    in_grp = (lane >= e_lo) & (lane < e_lo + MOE_PER_GROUP)
    el = jnp.where(in_grp, logits, -jnp.inf)
    ee = jnp.exp(el - jnp.max(el, axis=-1, keepdims=True))
    pe = jnp.where(in_grp, ee / jnp.sum(ee, axis=-1, keepdims=True), -1.0)
    p1 = jnp.max(pe, axis=-1, keepdims=True)
    i1 = jnp.min(jnp.where(pe == p1, lane, big), axis=-1, keepdims=True)
    pe2 = jnp.where(lane == i1, -1.0, pe)
    p2 = jnp.max(pe2, axis=-1, keepdims=True)
    i2 = jnp.min(jnp.where(pe2 == p2, lane, big), axis=-1, keepdims=True)
    tot = p1 + p2
    return jnp.where(lane == i1, p_group * p1 / tot, 0.0) + jnp.where(lane == i2, p_group * p2 / tot, 0.0)


def _merge_kernel(x_ref, og_ref, ys_ref, gates_ref, gt1_ref, sc2_ref, sh2_ref, nw2_ref, wdn_ref, wssm_ref,
                  wout_ref, wr_ref, br_ref, x1_ref, h2_ref, wtok_ref):
    y_dn = _dot(og_ref[...], wdn_ref[...])
    y_ssm = _dot(ys_ref[...], wssm_ref[...])
    merged = (_sigmoid(gates_ref[:, :D_MODEL]) * y_dn + _sigmoid(gates_ref[:, D_MODEL:]) * y_ssm)
    x1 = x_ref[...] + gt1_ref[...] * _dot(merged, wout_ref[...])
    x1_ref[...] = x1
    h2 = _rms(x1, nw2_ref[...]) * (1.0 + sc2_ref[...]) + sh2_ref[...]
    h2_ref[...] = h2.astype(BF16)
    wtok_ref[...] = _route(_dot_x3(h2, wr_ref[...]) + br_ref[...])


def _merge(x3, og, ys, proj, gt1, sc2, sh2, nw2, wdn, wssm, wout, wr, br, tm):
    bx, lx, d = x3.shape
    nl = lx // tm
    rowmap = lambda col: (lambda b, i: (b * nl + i, col))
    const = lambda b, i: (0, 0)
    tok = lambda w, dt: jax.ShapeDtypeStruct((bx, lx, w), dt)
    return pl.pallas_call(
        _merge_kernel,
        grid=(bx, nl),
        in_specs=[pl.BlockSpec((None, tm, d), lambda b, i: (b, i, 0)),
                  pl.BlockSpec((tm, DN_VAL), rowmap(0)),
                  pl.BlockSpec((tm, SSM_INNER), rowmap(0)),
                  pl.BlockSpec((tm, 2 * d), rowmap(COL_GATES // (2 * d))),
                  _mod_spec(gt1, tm), _mod_spec(sc2, tm), _mod_spec(sh2, tm),
                  pl.BlockSpec((1, d), const),
                  pl.BlockSpec(wdn.shape, const), pl.BlockSpec(wssm.shape, const), pl.BlockSpec(wout.shape, const),
                  pl.BlockSpec(wr.shape, const), pl.BlockSpec((1, LANES), const)],
        out_specs=[pl.BlockSpec((None, tm, d), lambda b, i: (b, i, 0)),
                   pl.BlockSpec((None, tm, d), lambda b, i: (b, i, 0)),
                   pl.BlockSpec((None, tm, LANES), lambda b, i: (b, i, 0))],
        out_shape=[tok(d, F32), tok(d, BF16), tok(LANES, F32)],
        compiler_params=_cparams("arbitrary", "arbitrary"),
        name="merge_route",
    )(x3, og, ys, proj, gt1, sc2, sh2, nw2.reshape(1, d), wdn, wssm, wout, wr, br)


def _moe_kernel(h2_ref, wtok_ref, wg_ref, wu_ref, wd_ref, x1_ref, gt2_ref, fsc_ref, fsh_ref, fnw_ref,
                y_ref, acc_ref):
    e = pl.program_id(2)

    @pl.when(e == 0)
    def _():
        acc_ref[...] = jnp.zeros_like(acc_ref)

    h = h2_ref[...]
    gate = jnp.dot(h, wg_ref[...], preferred_element_type=F32)
    up = jnp.dot(h, wu_ref[...], preferred_element_type=F32)
    wtok = wtok_ref[...]
    w_e = jnp.sum(jnp.where(_iota(wtok.shape, 1) == e + RT_EXPERT, wtok, 0.0), axis=-1, keepdims=True)
    acc_ref[...] += _dot(_silu(gate) * up * w_e, wd_ref[...])

    @pl.when(e == MOE_EXPERTS - 1)
    def _():
        x2 = x1_ref[...] + gt2_ref[...] * acc_ref[...]
        y_ref[...] = _rms(x2, fnw_ref[...]) * (1.0 + fsc_ref[...]) + fsh_ref[...]


def _moe(h2, wtok, wg, wu, wd, x1, gt2, fsc, fsh, fnw, tm):
    bx, lx, d = x1.shape
    tokspec = lambda w: pl.BlockSpec((None, tm, w), lambda b, i, e: (b, i, 0))
    return pl.pallas_call(
        _moe_kernel,
        grid=(bx, lx // tm, MOE_EXPERTS),
        in_specs=[tokspec(d), tokspec(LANES),
                  pl.BlockSpec((None, d, MOE_FF), lambda b, i, e: (e, 0, 0)),
                  pl.BlockSpec((None, d, MOE_FF), lambda b, i, e: (e, 0, 0)),
                  pl.BlockSpec((None, MOE_FF, d), lambda b, i, e: (e, 0, 0)),
                  tokspec(d), _mod_spec(gt2, tm), _mod_spec(fsc, tm), _mod_spec(fsh, tm),
                  pl.BlockSpec((1, d), lambda b, i, e: (0, 0))],
        out_specs=tokspec(d),
        out_shape=jax.ShapeDtypeStruct((bx, lx, d), F32),
        scratch_shapes=[pltpu.VMEM((tm, d), F32)],
        compiler_params=_cparams("arbitrary", "arbitrary", "arbitrary"),
        name="moe_final",
    )(h2, wtok, wg, wu, wd, x1, gt2, fsc, fsh, fnw.reshape(1, d))


def _prep_layer(lp):
    w_in = lp["w_in"]
    offs = [0]
    for s in (DN_CONV_DIM, DN_VAL, DN_HEADS, DN_HEADS, SSM_CONV_DIM, SSM_INNER, SSM_HEADS, D_MODEL, D_MODEL):
        offs.append(offs[-1] + s)
    seg = lambda i: w_in[:, offs[i]:offs[i + 1]]
    small = jnp.concatenate([seg(2), seg(3), seg(6)], axis=1)
    pad = jnp.zeros((D_MODEL, PROJ_N - COL_SMALL - small.shape[1]), F32)
    w_cat = jnp.concatenate([seg(0), seg(4), seg(5), seg(7), seg(8), seg(1), small, pad], axis=1).astype(BF16)
    wr = jnp.concatenate([lp["w_group_router"], lp["w_expert_router"],
                          jnp.zeros((D_MODEL, LANES - MOE_GROUPS - MOE_EXPERTS), F32)], axis=1)
    br = jnp.concatenate([lp["b_group_router"], lp["b_expert_router"],
                          jnp.zeros((LANES - MOE_GROUPS - MOE_EXPERTS,), F32)]).reshape(1, LANES)
    return dict(lp, w_cat=w_cat, wr=wr, br=br,
                wdn=lp["w_dn_out"].astype(BF16), wssm=lp["w_ssm_out"].astype(BF16), wout=lp["w_out"].astype(BF16),
                wg=lp["w_exp_gate"].astype(BF16), wu=lp["w_exp_up"].astype(BF16), wd=lp["w_exp_down"].astype(BF16))


def _trunk(x3, mods, fins, states, layers, final_norm_w, cfg):
    bx, lx, d = x3.shape
    tm = cfg["tm"]
    scan = dict(bt=cfg["bt"], r=cfg["r"], c=cfg["c"], lc=cfg["lc"], ng=cfg["ng"], cpi=cfg["cpi"])
    new_states = []
    n_layers = len(layers)
    for l, lp in enumerate(layers):
        sh1, sc1, gt1, sh2, sc2, gt2 = mods[l]
        dn_conv, dn_rec, ssm_conv, ssm_rec = states[l]
        proj = _inproj(x3, sc1, sh1, lp["norm_mix_w"], lp["w_cat"], tm).reshape(bx * lx, PROJ_N)
        og, dn_conv_new, dn_rec_new = _dn_branch(proj, lp["dn_conv_w"], lp["dn_A_log"], lp["dn_dt_bias"],
                                                 lp["dn_norm_w"], dn_conv, dn_rec, **scan)
        ys, ssm_conv_new, ssm_rec_new = _ssd_branch(proj, lp["ssm_conv_w"], lp["ssm_conv_b"], lp["ssm_A_log"],
                                                    lp["ssm_dt_bias"], lp["ssm_D"], lp["ssm_norm_w"],
                                                    ssm_conv, ssm_rec, **scan)
        x1, h2, wtok = _merge(x3, og, ys, proj, gt1, sc2, sh2, lp["norm_ffn_w"], lp["wdn"], lp["wssm"], lp["wout"],
                              lp["wr"], lp["br"], cfg["tm_merge"])
        if l == n_layers - 1:
            fsh, fsc, fnw = fins[0], fins[1], final_norm_w
            x3 = _moe(h2, wtok, lp["wg"], lp["wu"], lp["wd"], x1, gt2, fsc, fsh, fnw, cfg["tm_moe"])
        else:
            raise NotImplementedError("only the last layer fuses the final norm; depth is 1 here")
        new_states.append((dn_conv_new, dn_rec_new, ssm_conv_new, ssm_rec_new))
    return x3, new_states


def _per_seq(m):
    return m[:, None, :]


def kernel(x_prompt, x_sample, c_prompt, c_sample, state_dn_conv, state_dn_rec, state_ssm_conv, state_ssm_rec, w_ada, b_ada, norm_mix_w, w_in, dn_conv_w, dn_A_log, dn_dt_bias, dn_norm_w, w_dn_out, ssm_conv_w, ssm_conv_b, ssm_A_log, ssm_dt_bias, ssm_D, ssm_norm_w, w_ssm_out, w_out, norm_ffn_w, w_group_router, b_group_router, w_expert_router, b_expert_router, w_exp_gate, w_exp_up, w_exp_down, w_ada_final, b_ada_final, final_norm_w):
    depth = w_ada.shape[0]
    assert depth == 1
    per_layer = dict(w_ada=w_ada, b_ada=b_ada, norm_mix_w=norm_mix_w, w_in=w_in, dn_conv_w=dn_conv_w,
                     dn_A_log=dn_A_log, dn_dt_bias=dn_dt_bias, dn_norm_w=dn_norm_w, w_dn_out=w_dn_out,
                     ssm_conv_w=ssm_conv_w, ssm_conv_b=ssm_conv_b, ssm_A_log=ssm_A_log, ssm_dt_bias=ssm_dt_bias,
                     ssm_D=ssm_D, ssm_norm_w=ssm_norm_w, w_ssm_out=w_ssm_out, w_out=w_out, norm_ffn_w=norm_ffn_w,
                     w_group_router=w_group_router, b_group_router=b_group_router,
                     w_expert_router=w_expert_router, b_expert_router=b_expert_router,
                     w_exp_gate=w_exp_gate, w_exp_up=w_exp_up, w_exp_down=w_exp_down)
    layers = [_prep_layer({k: v[l] for k, v in per_layer.items()}) for l in range(depth)]

    nbp, lp_, d = x_prompt.shape
    nbs, ls, _ = x_sample.shape
    c_all = jnp.concatenate([c_prompt, c_sample], axis=0)
    mod_all = [_ada(c_all, lyr["w_ada"], lyr["b_ada"]) for lyr in layers]
    fin_all = _ada(c_all, w_ada_final, b_ada_final)

    mods_p = [[_per_seq(m) for m in jnp.split(ma[:nbp], 6, axis=-1)] for ma in mod_all]
    fins_p = [_per_seq(m) for m in jnp.split(fin_all[:nbp], 2, axis=-1)]
    zeros_p = [(jnp.zeros((nbp, CONV_K - 1, DN_CONV_DIM), F32), jnp.zeros((nbp, DN_HEADS, DN_DK, DN_DV), F32),
                jnp.zeros((nbp, CONV_K - 1, SSM_CONV_DIM), F32),
                jnp.zeros((nbp, SSM_HEADS, SSM_HEADDIM, SSM_STATE), F32)) for _ in range(depth)]
    c_p = min(SCAN_CHUNK, lp_)
    r_p = min(lp_, 8 * c_p)
    cfg_p = dict(tm=min(lp_, 1024), tm_merge=min(lp_, 512), tm_moe=min(lp_, 1024), bt=1, r=r_p, c=c_p, lc=c_p, ng=lp_ // r_p,
                 cpi=4 if (r_p // c_p) % 4 == 0 else 1)
    y_p, st_p = _trunk(x_prompt, mods_p, fins_p, zeros_p, layers, final_norm_w, cfg_p)

    lpad = -(-ls // SUBLANES) * SUBLANES
    bt_s = 8
    xs = jnp.pad(x_sample, ((0, 0), (0, lpad - ls), (0, 0))).reshape(1, nbs * lpad, d)
    per_tok = lambda m: jnp.repeat(m, lpad, axis=0)[None]
    mods_s = [[per_tok(m) for m in jnp.split(ma[nbp:], 6, axis=-1)] for ma in mod_all]
    fins_s = [per_tok(m) for m in jnp.split(fin_all[nbp:], 2, axis=-1)]
    st_in = [(state_dn_conv[l], state_dn_rec[l], state_ssm_conv[l], state_ssm_rec[l]) for l in range(depth)]
    ts = nbs * lpad
    cfg_s = dict(tm=ts, tm_merge=min(ts, 512), tm_moe=min(ts, 512), bt=bt_s, r=lpad, c=lpad, lc=ls, ng=1, cpi=1)
    y_s, st_s = _trunk(xs, mods_s, fins_s, st_in, layers, final_norm_w, cfg_s)
    y_s = y_s.reshape(nbs, lpad, d)[:, :ls]

    stack = lambda sts, i: jnp.stack([s[i] for s in sts])
    return (y_p, y_s, stack(st_p, 0), stack(st_p, 1), stack(st_p, 2), stack(st_p, 3),
            stack(st_s, 0), stack(st_s, 1), stack(st_s, 2), stack(st_s, 3))
```

```python
import functools
import math

import jax
import jax.numpy as jnp
from jax import lax
from jax.experimental import pallas as pl
from jax.experimental.pallas import tpu as pltpu

F32 = jnp.float32
BF16 = jnp.bfloat16

D_MODEL = 1024
DN_HEADS = 4
DN_DK = 128
DN_DV = 128
DN_KEY = DN_HEADS * DN_DK
DN_VAL = DN_HEADS * DN_DV
CONV_K = 4
DN_CONV_DIM = 2 * DN_KEY + DN_VAL
SSM_INNER = D_MODEL
SSM_HEADDIM = 64
SSM_HEADS = SSM_INNER // SSM_HEADDIM
SSM_GROUPS = 2
SSM_STATE = 128
SSM_CONV_DIM = SSM_INNER + 2 * SSM_GROUPS * SSM_STATE
MOE_GROUPS = 4
MOE_PER_GROUP = 8
MOE_EXPERTS = MOE_GROUPS * MOE_PER_GROUP
MOE_FF = D_MODEL // 4
EPS = 1e-6
SCAN_CHUNK = 64

LANES = 128
SUBLANES = 8
CARRY_ROW = SUBLANES - (CONV_K - 1)

COL_DN_QKV = 0
COL_SSM_XBC = COL_DN_QKV + DN_CONV_DIM
COL_SSM_Z = COL_SSM_XBC + SSM_CONV_DIM
COL_GATES = COL_SSM_Z + SSM_INNER
COL_DN_Z = COL_GATES + 2 * D_MODEL
COL_SMALL = COL_DN_Z + DN_VAL
PROJ_TN = 1024
PROJ_N = 7 * PROJ_TN
SM_A, SM_B, SM_DT = 0, DN_HEADS, 2 * DN_HEADS
RT_GROUP, RT_EXPERT = 0, MOE_GROUPS

VMEM_LIMIT = 56 * 1024 * 1024


def _cparams(*sem):
    return pltpu.CompilerParams(dimension_semantics=sem, vmem_limit_bytes=VMEM_LIMIT)


def _dot(a, b):
    return jnp.dot(a.astype(BF16), b.astype(BF16), preferred_element_type=F32)


def _dot_nt(a, b):
    return lax.dot_general(a.astype(BF16), b.astype(BF16), (((1,), (1,)), ((), ())),
                           preferred_element_type=F32)


def _dot_tn(a, b):
    return lax.dot_general(a.astype(BF16), b.astype(BF16), (((0,), (0,)), ((), ())),
                           preferred_element_type=F32)


def _split3(x):
    hi = x.astype(BF16)
    r = x - hi.astype(F32)
    mid = r.astype(BF16)
    lo = (r - mid.astype(F32)).astype(BF16)
    return hi, mid, lo


def _dot_exact_lhs(a_bf, b):
    hi, mid, lo = _split3(b)
    d = functools.partial(jnp.dot, preferred_element_type=F32)
    return d(a_bf, hi) + (d(a_bf, mid) + d(a_bf, lo))


def _dot_x3(a, b):
    a_hi = a.astype(BF16)
    a_lo = (a - a_hi.astype(F32)).astype(BF16)
    b_hi = b.astype(BF16)
    b_lo = (b - b_hi.astype(F32)).astype(BF16)
    d = functools.partial(jnp.dot, preferred_element_type=F32)
    return d(a_hi, b_hi) + (d(a_hi, b_lo) + d(a_lo, b_hi))


def _sigmoid(x):
    return 1.0 / (1.0 + jnp.exp(-x))


def _silu(x):
    return x * _sigmoid(x)


def _softplus(x):
    return jnp.maximum(x, 0.0) + jnp.log1p(jnp.exp(-jnp.abs(x)))


def _rms(x, w):
    return x * lax.rsqrt(jnp.mean(x * x, axis=-1, keepdims=True) + EPS) * w


def _iota(shape, dim):
    return lax.broadcasted_iota(jnp.int32, shape, dim)


def _ada_kernel(c_ref, w_ref, b_ref, o_ref):
    o_ref[...] = _dot(_silu(c_ref[...]), w_ref[...]) + b_ref[...]


def _ada(c, w, b, tn=512):
    m, d = c.shape
    n = w.shape[1]
    return pl.pallas_call(
        _ada_kernel,
        grid=(n // tn,),
        in_specs=[pl.BlockSpec((m, d), lambda j: (0, 0)),
                  pl.BlockSpec((d, tn), lambda j: (0, j)),
                  pl.BlockSpec((1, tn), lambda j: (0, j))],
        out_specs=pl.BlockSpec((m, tn), lambda j: (0, j)),
        out_shape=jax.ShapeDtypeStruct((m, n), F32),
        compiler_params=_cparams("arbitrary"),
        name="ada_mod",
    )(c, w, b.reshape(1, n))


def _mod_spec(mod, tm):
    if mod.shape[1] == 1:
        return pl.BlockSpec((None, 1, D_MODEL), lambda b, i, *_: (b, 0, 0))
    return pl.BlockSpec((None, tm, D_MODEL), lambda b, i, *_: (b, i, 0))


def _inproj_kernel(x_ref, sc_ref, sh_ref, nw_ref, w_ref, o_ref, h_ref, *, tm, sub):
    @pl.when(pl.program_id(2) == 0)
    def _():
        per_token = sc_ref.shape[0] != 1

        def body(r, carry):
            rows = pl.ds(pl.multiple_of(r * sub, sub), sub)
            sc = sc_ref[rows, :] if per_token else sc_ref[...]
            sh = sh_ref[rows, :] if per_token else sh_ref[...]
            h = _rms(x_ref[rows, :], nw_ref[...]) * (1.0 + sc) + sh
            h_ref[rows, :] = h.astype(BF16)
            return carry

        lax.fori_loop(0, tm // sub, body, 0)

    o_ref[...] = jnp.dot(h_ref[...], w_ref[...], preferred_element_type=F32)


def _inproj(x3, sc, sh, nw, w_cat, tm):
    bx, lx, d = x3.shape
    n = w_cat.shape[1]
    kern = functools.partial(_inproj_kernel, tm=tm, sub=min(tm, 256))
    return pl.pallas_call(
        kern,
        grid=(bx, lx // tm, n // PROJ_TN),
        in_specs=[pl.BlockSpec((None, tm, d), lambda b, i, j: (b, i, 0)),
                  _mod_spec(sc, tm), _mod_spec(sh, tm),
                  pl.BlockSpec((1, d), lambda b, i, j: (0, 0)),
                  pl.BlockSpec((d, PROJ_TN), lambda b, i, j: (0, j))],
        out_specs=pl.BlockSpec((None, tm, PROJ_TN), lambda b, i, j: (b, i, j)),
        out_shape=jax.ShapeDtypeStruct((bx, lx, n), F32),
        scratch_shapes=[pltpu.VMEM((tm, d), BF16)],
        compiler_params=_cparams("arbitrary", "arbitrary", "arbitrary"),
        name="norm_inproj",
    )(x3, sc, sh, nw.reshape(1, d), w_cat)


def _conv_silu(u, ext_ref, bb, cw_ref, c, lc, bias=None):
    ext_ref[bb, SUBLANES:SUBLANES + c, :] = u
    y = ext_ref[bb, CARRY_ROW:CARRY_ROW + c, :] * cw_ref[0:1, :]
    for i in range(1, CONV_K):
        y = y + ext_ref[bb, CARRY_ROW + i:CARRY_ROW + i + c, :] * cw_ref[i:i + 1, :]
    ext_ref[bb, CARRY_ROW:SUBLANES, :] = ext_ref[bb, CARRY_ROW + lc:SUBLANES + lc, :]
    if bias is not None:
        y = y + bias
    return _silu(y)


def _run_rounds(process, bt, r, c, cpi):
    g = r // c
    if g == 1:
        process([[(bb, 0) for bb in range(bt)]])
    else:
        def body(ci, carry):
            process([[(bb, pl.multiple_of((ci * cpi + t) * c, c)) for bb in range(bt)] for t in range(cpi)])
            return carry
        lax.fori_loop(0, g // cpi, body, 0)


INV_BASE = SUBLANES


def _inverse_masks(row, col, c):
    sh = lambda x, s: jnp.right_shift(x, int(math.log2(s)))
    diag = sh(row, INV_BASE) == sh(col, INV_BASE)
    merges = []
    s = INV_BASE
    while s < c:
        merges.append((sh(row, 2 * s) == sh(col, 2 * s))
                      & (jnp.bitwise_and(sh(row, s), 1) == 1) & (jnp.bitwise_and(sh(col, s), 1) == 0))
        s *= 2
    return diag, merges


def _unit_lower_inverses(lmats, eye_f, masks):
    diag, merges = masks
    npows = [-jnp.where(diag, l, 0.0) for l in lmats]
    ps = [eye_f + n for n in npows]
    for _ in range(int(math.log2(INV_BASE)) - 1):
        npows = [_dot(n, n) for n in npows]
        ps = [p + _dot(p, n) for p, n in zip(ps, npows)]
    for m in merges:
        ts = [_dot(p, jnp.where(m, l, 0.0)) for p, l in zip(ps, lmats)]
        ps = [p - _dot(t, p) for t, p in zip(ts, ps)]
    return ps


def _dn_kernel(qkv_ref, z_ref, sm_ref, cw_ref, alog_ref, bias_ref, nw_ref, cin_ref, sin_ref,
               o_ref, cout_ref, sout_ref, ext_ref, *, bt, r, c, lc, ng, cpi):
    j = pl.program_id(1)

    @pl.when(j == 0)
    def _():
        ext_ref[:, CARRY_ROW:SUBLANES, :] = cin_ref[...]
        sout_ref[...] = sin_ref[...]

    row = _iota((c, c), 0)
    col = _iota((c, c), 1)
    tri_incl = row >= col
    tri_strict = row > col
    eye_f = (row == col).astype(F32)
    tril_bf = tri_incl.astype(BF16)
    valid = _iota((c, 1), 0) < lc
    nexp_a = -jnp.exp(alog_ref[...])
    bias = bias_ref[...]
    inv_masks = _inverse_masks(row, col, c)
    heads = range(DN_HEADS)

    def prep(bb, r0):
        rows = pl.ds(bb * r + r0, c)
        y = _conv_silu(qkv_ref[rows, :], ext_ref, bb, cw_ref, c, lc)
        sm = sm_ref[rows, :]
        g_all = nexp_a * _softplus(sm + bias)
        beta_all = _sigmoid(sm)
        if lc < c:
            g_all = jnp.where(valid, g_all, 0.0)
            beta_all = jnp.where(valid, beta_all, 0.0)
        cs_all = _dot_exact_lhs(tril_bf, g_all)
        cs_t = cs_all.T
        ecs_all = jnp.exp(cs_all)
        per_head = []
        for h in heads:
            q = y[:, h * DN_DK:(h + 1) * DN_DK]
            k = y[:, DN_KEY + h * DN_DK:DN_KEY + (h + 1) * DN_DK]
            v = y[:, 2 * DN_KEY + h * DN_DV:2 * DN_KEY + (h + 1) * DN_DV]
            q = q * lax.rsqrt(jnp.sum(q * q, axis=-1, keepdims=True) + EPS) * (DN_DK ** -0.5)
            k = k * lax.rsqrt(jnp.sum(k * k, axis=-1, keepdims=True) + EPS)
            if lc < c:
                k = jnp.where(valid, k, 0.0)
            cs = cs_all[:, SM_A + h:SM_A + h + 1]
            ecs = ecs_all[:, SM_A + h:SM_A + h + 1]
            beta = beta_all[:, SM_B + h:SM_B + h + 1]
            last = cs_all[c - 1:c, SM_A + h:SM_A + h + 1]
            decay = jnp.exp(jnp.where(tri_incl, cs - cs_t[SM_A + h:SM_A + h + 1, :], -jnp.inf))
            per_head.append(dict(q_ecs=(q * ecs).astype(BF16), q=q.astype(BF16), k=k.astype(BF16), beta=beta,
                                 decay=decay, elast=jnp.exp(last),
                                 k_tail=(k * jnp.exp(last - cs)).astype(BF16),
                                 rhs=jnp.concatenate([v * beta, k * (beta * ecs)], axis=1).astype(BF16)))
        return per_head

    def process(rounds):
        items = [it for rnd in rounds for it in rnd]
        pre = [prep(bb, r0) for bb, r0 in items]
        chains = [(i, h) for i in range(len(items)) for h in heads]
        a = {ch: pre[ch[0]][ch[1]] for ch in chains}
        kk = {ch: _dot_nt(a[ch]["k"], a[ch]["k"]) for ch in chains}
        qk = {ch: _dot_nt(a[ch]["q"], a[ch]["k"]) * a[ch]["decay"] for ch in chains}
        lmats = [jnp.where(tri_strict, a[ch]["beta"] * kk[ch] * a[ch]["decay"], 0.0) for ch in chains]
        pinv = _unit_lower_inverses(lmats, eye_f, inv_masks)
        sol = {ch: _dot(p, a[ch]["rhs"]) for ch, p in zip(chains, pinv)}
        first = 0
        for rnd in rounds:
            idx = range(first, first + len(rnd))
            first += len(rnd)
            rch = [(i, h) for i in idx for h in heads]
            s = {ch: sout_ref[items[ch[0]][0], ch[1]] for ch in rch}
            ws_qs = {ch: _dot(jnp.concatenate([sol[ch][:, DN_DV:].astype(BF16), a[ch]["q_ecs"]], axis=0), s[ch])
                     for ch in rch}
            v_new = {ch: (sol[ch][:, :DN_DV] - ws_qs[ch][:c]).astype(BF16) for ch in rch}
            o = {ch: ws_qs[ch][c:] + _dot(qk[ch], v_new[ch]) for ch in rch}
            for ch in rch:
                sout_ref[items[ch[0]][0], ch[1]] = a[ch]["elast"] * s[ch] + _dot_tn(a[ch]["k_tail"], v_new[ch])
            for i in idx:
                bb, r0 = items[i]
                rows = pl.ds(bb * r + r0, c)
                o_ref[rows, :] = jnp.concatenate(
                    [_rms(o[(i, h)], nw_ref[...]) * _silu(z_ref[rows, h * DN_DV:(h + 1) * DN_DV])
                     for h in heads], axis=1)

    _run_rounds(process, bt, r, c, cpi)

    @pl.when(j == ng - 1)
    def _():
        cout_ref[...] = ext_ref[:, CARRY_ROW:SUBLANES, :]


def _small_row(vals, offset, fill=0.0):
    row = jnp.full((1, LANES), fill, F32)
    return lax.dynamic_update_slice(row, vals.astype(F32).reshape(1, -1), (0, offset))


def _dn_branch(proj, conv_w, a_log, dt_bias, norm_w, conv_in, rec_in, *, bt, r, c, lc, ng, cpi):
    nb = conv_in.shape[0] // bt
    rows = bt * r
    t = proj.shape[0]
    kern = functools.partial(_dn_kernel, bt=bt, r=r, c=c, lc=lc, ng=ng, cpi=cpi)
    rowmap = lambda col: (lambda i, j: (i * ng + j, col))
    const = lambda i, j: (0, 0)
    return pl.pallas_call(
        kern,
        grid=(nb, ng),
        in_specs=[pl.BlockSpec((rows, DN_CONV_DIM), rowmap(COL_DN_QKV // DN_CONV_DIM)),
                  pl.BlockSpec((rows, DN_VAL), rowmap(COL_DN_Z // DN_VAL)),
                  pl.BlockSpec((rows, LANES), rowmap(COL_SMALL // LANES)),
                  pl.BlockSpec((CONV_K, DN_CONV_DIM), const),
                  pl.BlockSpec((1, LANES), const),
                  pl.BlockSpec((1, LANES), const),
                  pl.BlockSpec((1, DN_DV), const),
                  pl.BlockSpec((bt, CONV_K - 1, DN_CONV_DIM), lambda i, j: (i, 0, 0)),
                  pl.BlockSpec((bt, DN_HEADS, DN_DK, DN_DV), lambda i, j: (i, 0, 0, 0))],
        out_specs=[pl.BlockSpec((rows, DN_VAL), lambda i, j: (i * ng + j, 0)),
                   pl.BlockSpec((bt, CONV_K - 1, DN_CONV_DIM), lambda i, j: (i, 0, 0)),
                   pl.BlockSpec((bt, DN_HEADS, DN_DK, DN_DV), lambda i, j: (i, 0, 0, 0))],
        out_shape=[jax.ShapeDtypeStruct((t, DN_VAL), F32),
                   jax.ShapeDtypeStruct(conv_in.shape, F32),
                   jax.ShapeDtypeStruct(rec_in.shape, F32)],
        scratch_shapes=[pltpu.VMEM((bt, c + SUBLANES, DN_CONV_DIM), F32)],
        compiler_params=_cparams("arbitrary", "arbitrary"),
        name="gated_delta",
    )(proj, proj, proj, conv_w, _small_row(a_log, SM_A), _small_row(dt_bias, SM_A),
      norm_w.reshape(1, DN_DV), conv_in, rec_in)


def _ssd_kernel(xbc_ref, z_ref, sm_ref, cw_ref, cb_ref, alog_ref, bias_ref, dvec_ref, nw_ref, cin_ref,
                hin_ref, y_ref, cout_ref, hout_ref, ext_ref, *, bt, r, c, lc, ng, cpi):
    j = pl.program_id(1)

    @pl.when(j == 0)
    def _():
        ext_ref[:, CARRY_ROW:SUBLANES, :] = cin_ref[...]
        hout_ref[...] = hin_ref[...]

    tril_bf = (_iota((c, c), 0) >= _iota((c, c), 1)).astype(BF16)
    row2 = _iota((c, 2 * c), 0)
    lane2 = _iota((c, 2 * c), 1)
    left2 = lane2 < c
    tri2 = row2 >= jnp.where(left2, lane2, lane2 - c)
    left_x = _iota((c, LANES), 1) < SSM_HEADDIM
    top_h = _iota((2 * SSM_HEADDIM, 1), 0) < SSM_HEADDIM
    valid = _iota((c, 1), 0) < lc
    nexp_a = -jnp.exp(alog_ref[...])
    bias = bias_ref[...]
    gw = SSM_INNER // SSM_GROUPS
    pairs_per_group = SSM_HEADS // SSM_GROUPS // 2

    def chunk(bb, r0):
        rows = pl.ds(bb * r + r0, c)
        y = _conv_silu(xbc_ref[rows, :], ext_ref, bb, cw_ref, c, lc, bias=cb_ref[...])
        sm = sm_ref[rows, :]
        dt_all = _softplus(sm + bias)
        if lc < c:
            dt_all = jnp.where(valid, dt_all, 0.0)
        cs_all = _dot_exact_lhs(tril_bf, dt_all * nexp_a)
        cs_t2 = jnp.concatenate([cs_all, cs_all], axis=0).T
        ecs_all = jnp.exp(cs_all)
        last = cs_all[c - 1:c, :]
        w_all = jnp.exp(last - cs_all)
        elast = jnp.exp(last)
        outs = []
        for g in range(SSM_GROUPS):
            bm = y[:, SSM_INNER + g * SSM_STATE:SSM_INNER + (g + 1) * SSM_STATE]
            cm = y[:, SSM_INNER + (SSM_GROUPS + g) * SSM_STATE:SSM_INNER + (SSM_GROUPS + g + 1) * SSM_STATE]
            bm_bf = bm.astype(BF16)
            cm_bf = cm.astype(BF16)
            cb2 = _dot_nt(cm_bf, jnp.concatenate([bm_bf, bm_bf], axis=0))
            for pi in range(pairs_per_group):
                pair = g * pairs_per_group + pi
                la, lb = SM_DT + 2 * pair, SM_DT + 2 * pair + 1
                xs = y[:, pair * LANES:(pair + 1) * LANES]
                xdt = xs * jnp.where(left_x, dt_all[:, la:la + 1], dt_all[:, lb:lb + 1])
                diff = (jnp.where(left2, cs_all[:, la:la + 1], cs_all[:, lb:lb + 1])
                        - jnp.where(left2[0:1], cs_t2[la:la + 1, :], cs_t2[lb:lb + 1, :]))
                decay2 = jnp.exp(jnp.where(tri2, diff, -jnp.inf))
                rhs = jnp.concatenate([jnp.where(left_x, xdt, 0.0), jnp.where(left_x, 0.0, xdt)], axis=0)
                y_intra = _dot(cb2 * decay2, rhs)
                hs = hout_ref[bb, pair * LANES:(pair + 1) * LANES, :]
                y_inter = _dot_nt(cm_bf, hs) * jnp.where(left_x, ecs_all[:, la:la + 1], ecs_all[:, lb:lb + 1])
                xw = xdt * jnp.where(left_x, w_all[:, la:la + 1], w_all[:, lb:lb + 1])
                hout_ref[bb, pair * LANES:(pair + 1) * LANES, :] = (
                    jnp.where(top_h, elast[:, la:la + 1], elast[:, lb:lb + 1]) * hs + _dot_tn(xw, bm_bf))
                outs.append(y_intra + y_inter + dvec_ref[:, pair * LANES:(pair + 1) * LANES] * xs)
        yz = jnp.concatenate(outs, axis=1) * _silu(z_ref[rows, :])
        y_ref[rows, :] = jnp.concatenate(
            [_rms(yz[:, g * gw:(g + 1) * gw], nw_ref[:, g * gw:(g + 1) * gw]) for g in range(SSM_GROUPS)], axis=1)

    def process(rounds):
        for rnd in rounds:
            for bb, r0 in rnd:
                chunk(bb, r0)

    _run_rounds(process, bt, r, c, cpi)

    @pl.when(j == ng - 1)
    def _():
        cout_ref[...] = ext_ref[:, CARRY_ROW:SUBLANES, :]


def _ssd_branch(proj, conv_w, conv_b, a_log, dt_bias, d_skip, norm_w, conv_in, rec_in, *, bt, r, c, lc, ng, cpi):
    nb = conv_in.shape[0] // bt
    rows = bt * r
    t = proj.shape[0]
    hrows = SSM_HEADS * SSM_HEADDIM
    rec2 = rec_in.reshape(rec_in.shape[0], hrows, SSM_STATE)
    kern = functools.partial(_ssd_kernel, bt=bt, r=r, c=c, lc=lc, ng=ng, cpi=cpi)
    rowmap = lambda col: (lambda i, j: (i * ng + j, col))
    const = lambda i, j: (0, 0)
    y, conv_out, rec_out = pl.pallas_call(
        kern,
        grid=(nb, ng),
        in_specs=[pl.BlockSpec((rows, SSM_CONV_DIM), rowmap(COL_SSM_XBC // SSM_CONV_DIM)),
                  pl.BlockSpec((rows, SSM_INNER), rowmap(COL_SSM_Z // SSM_INNER)),
                  pl.BlockSpec((rows, LANES), rowmap(COL_SMALL // LANES)),
                  pl.BlockSpec((CONV_K, SSM_CONV_DIM), const),
                  pl.BlockSpec((1, SSM_CONV_DIM), const),
                  pl.BlockSpec((1, LANES), const),
                  pl.BlockSpec((1, LANES), const),
                  pl.BlockSpec((1, SSM_INNER), const),
                  pl.BlockSpec((1, SSM_INNER), const),
                  pl.BlockSpec((bt, CONV_K - 1, SSM_CONV_DIM), lambda i, j: (i, 0, 0)),
                  pl.BlockSpec((bt, hrows, SSM_STATE), lambda i, j: (i, 0, 0))],
        out_specs=[pl.BlockSpec((rows, SSM_INNER), lambda i, j: (i * ng + j, 0)),
                   pl.BlockSpec((bt, CONV_K - 1, SSM_CONV_DIM), lambda i, j: (i, 0, 0)),
                   pl.BlockSpec((bt, hrows, SSM_STATE), lambda i, j: (i, 0, 0))],
        out_shape=[jax.ShapeDtypeStruct((t, SSM_INNER), F32),
                   jax.ShapeDtypeStruct(conv_in.shape, F32),
                   jax.ShapeDtypeStruct(rec2.shape, F32)],
        scratch_shapes=[pltpu.VMEM((bt, c + SUBLANES, SSM_CONV_DIM), F32)],
        compiler_params=_cparams("arbitrary", "arbitrary"),
        name="ssd_scan",
    )(proj, proj, proj, conv_w, conv_b.reshape(1, -1), _small_row(a_log, SM_DT), _small_row(dt_bias, SM_DT),
      jnp.repeat(d_skip.astype(F32), SSM_HEADDIM).reshape(1, SSM_INNER), norm_w.reshape(1, SSM_INNER),
      conv_in, rec2)
    return y, conv_out, rec_out.reshape(rec_in.shape)


PLAN_E0, PLAN_E1, PLAN_C0, PLAN_C1 = 0, 1, 2, 3


def _route(logits):
    lane = _iota(logits.shape, 1).astype(F32)
    big = float(LANES)
    is_group = (lane >= RT_GROUP) & (lane < RT_GROUP + MOE_GROUPS)
    gl = jnp.where(is_group, logits, -jnp.inf)
    gmax = jnp.max(gl, axis=-1, keepdims=True)
    g_sel = jnp.min(jnp.where(gl == gmax, lane, big), axis=-1, keepdims=True) - RT_GROUP
    p_group = 1.0 / jnp.sum(jnp.exp(gl - gmax), axis=-1, keepdims=True)
    e_lo = RT_EXPERT + MOE_PER_GROUP * g_sel
    in_grp = (lane >= e_lo) & (lane < e_lo + MOE_PER_GROUP)
    el = jnp.where(in_grp, logits, -jnp.inf)
    ee = jnp.exp(el - jnp.max(el, axis=-1, keepdims=True))
    pe = jnp.where(in_grp, ee / jnp.sum(ee, axis=-1, keepdims=True), -1.0)
    p1 = jnp.max(pe, axis=-1, keepdims=True)
    i1 = jnp.min(jnp.where(pe == p1, lane, big), axis=-1, keepdims=True)
    pe2 = jnp.where(lane == i1, -1.0, pe)
    p2 = jnp.max(pe2, axis=-1, keepdims=True)
    i2 = jnp.min(jnp.where(pe2 == p2, lane, big), axis=-1, keepdims=True)
    tot = p1 + p2
    plan = jnp.where(lane == PLAN_E0, i1 - RT_EXPERT, 0.0) + jnp.where(lane == PLAN_E1, i2 - RT_EXPERT, 0.0)
    return plan + jnp.where(lane == PLAN_C0, p_group * p1 / tot, 0.0) + jnp.where(lane == PLAN_C1, p_group * p2 / tot, 0.0)


def _merge_kernel(x_ref, og_ref, ys_ref, gates_ref, gt1_ref, sc2_ref, sh2_ref, nw2_ref, wdn_ref, wssm_ref,
                  wout_ref, wr_ref, br_ref, x1_ref, h2_ref, plan_ref):
    y_dn = _dot(og_ref[...], wdn_ref[...])
    y_ssm = _dot(ys_ref[...], wssm_ref[...])
    merged = (_sigmoid(gates_ref[:, :D_MODEL]) * y_dn + _sigmoid(gates_ref[:, D_MODEL:]) * y_ssm)
    x1 = x_ref[...] + gt1_ref[...] * _dot(merged, wout_ref[...])
    x1_ref[...] = x1
    h2 = _rms(x1, nw2_ref[...]) * (1.0 + sc2_ref[...]) + sh2_ref[...]
    h2_ref[...] = h2.astype(BF16).astype(F32)
    plan_ref[...] = _route(_dot_x3(h2, wr_ref[...]) + br_ref[...])


def _merge(x3, og, ys, proj, gt1, sc2, sh2, nw2, wdn, wssm, wout, wr, br, tm):
    bx, lx, d = x3.shape
    nl = lx // tm
    rowmap = lambda col: (lambda b, i: (b * nl + i, col))
    const = lambda b, i: (0, 0)
    tok = lambda w, dt: jax.ShapeDtypeStruct((bx, lx, w), dt)
    return pl.pallas_call(
        _merge_kernel,
        grid=(bx, nl),
        in_specs=[pl.BlockSpec((None, tm, d), lambda b, i: (b, i, 0)),
                  pl.BlockSpec((tm, DN_VAL), rowmap(0)),
                  pl.BlockSpec((tm, SSM_INNER), rowmap(0)),
                  pl.BlockSpec((tm, 2 * d), rowmap(COL_GATES // (2 * d))),
                  _mod_spec(gt1, tm), _mod_spec(sc2, tm), _mod_spec(sh2, tm),
                  pl.BlockSpec((1, d), const),
                  pl.BlockSpec(wdn.shape, const), pl.BlockSpec(wssm.shape, const), pl.BlockSpec(wout.shape, const),
                  pl.BlockSpec(wr.shape, const), pl.BlockSpec((1, LANES), const)],
        out_specs=[pl.BlockSpec((None, tm, d), lambda b, i: (b, i, 0)),
                   pl.BlockSpec((None, tm, d), lambda b, i: (b, i, 0)),
                   pl.BlockSpec((None, tm, LANES), lambda b, i: (b, i, 0))],
        out_shape=[tok(d, F32), tok(d, F32), tok(LANES, F32)],
        compiler_params=_cparams("arbitrary", "arbitrary"),
        name="merge_route",
    )(x3, og, ys, proj, gt1, sc2, sh2, nw2.reshape(1, d), wdn, wssm, wout, wr, br)


FFN_ROWS = 256
TOK_TILE = 256


def _plan_kernel(plan_ref, rank_ref, cnt_ref, carry_ref):
    @pl.when(pl.program_id(0) == 0)
    def _():
        carry_ref[...] = jnp.zeros_like(carry_ref)

    plan = plan_ref[...]
    tp = plan.shape[0]
    lane = _iota(plan.shape, 1).astype(F32)
    sel0 = lane == plan[:, PLAN_E0:PLAN_E0 + 1]
    sel1 = lane == plan[:, PLAN_E1:PLAN_E1 + 1]
    sel = (sel0 | sel1).astype(BF16)
    before = (_iota((tp, tp), 0) > _iota((tp, tp), 1)).astype(BF16)
    excl = jnp.dot(before, sel, preferred_element_type=F32) + carry_ref[...]
    r0 = jnp.sum(jnp.where(sel0, excl, 0.0), axis=-1, keepdims=True)
    r1 = jnp.sum(jnp.where(sel1, excl, 0.0), axis=-1, keepdims=True)
    rank_ref[...] = jnp.where(lane == PLAN_E0, r0, 0.0) + jnp.where(lane == PLAN_E1, r1, 0.0)
    carry_ref[...] += jnp.sum(sel.astype(F32), axis=0, keepdims=True)
    cnt_ref[...] = carry_ref[...]


def _plan(plan2, tp):
    t = plan2.shape[0]
    return pl.pallas_call(
        _plan_kernel,
        grid=(t // tp,),
        in_specs=[pl.BlockSpec((tp, LANES), lambda i: (i, 0))],
        out_specs=[pl.BlockSpec((tp, LANES), lambda i: (i, 0)), pl.BlockSpec((1, LANES), lambda i: (0, 0))],
        out_shape=[jax.ShapeDtypeStruct((t, LANES), F32), jax.ShapeDtypeStruct((1, LANES), F32)],
        scratch_shapes=[pltpu.VMEM((1, LANES), F32)],
        compiler_params=_cparams("arbitrary"),
        name="moe_plan",
    )(plan2)


def _row_copy(src_ref, src_row, dst_ref, dst_row, sem):
    return pltpu.make_async_copy(src_ref.at[pl.ds(src_row, 1)], dst_ref.at[pl.ds(dst_row, 1)], sem)


def _dispatch_kernel(tv_ref, pos_ref, h2_ref, xs_ref, zero_ref, sem, zsem):
    tt = h2_ref.shape[0]

    @pl.when(pl.program_id(0) == 0)
    def _():
        zero_ref[...] = jnp.zeros_like(zero_ref)

        def zero_copy(i):
            rows = pl.ds(pl.multiple_of(i * FFN_ROWS, FFN_ROWS), FFN_ROWS)
            return pltpu.make_async_copy(zero_ref, xs_ref.at[rows], zsem)

        def zero_tiles(act):
            def body(i, carry):
                @pl.when(tv_ref[i] < FFN_ROWS)
                def _():
                    act(zero_copy(i))
                return carry
            lax.fori_loop(0, xs_ref.shape[0] // FFN_ROWS, body, 0)

        zero_tiles(lambda cp: cp.start())
        zero_tiles(lambda cp: cp.wait())

    def issue(r, carry):
        for k in range(2):
            _row_copy(h2_ref, r, xs_ref, pos_ref[0, 2 * r + k], sem).start()
        return carry

    lax.fori_loop(0, tt, issue, 0)

    def drain(r, carry):
        _row_copy(h2_ref, 0, xs_ref, 0, sem).wait()
        return carry

    lax.fori_loop(0, 2 * tt, drain, 0)


def _dispatch(tile_valid, pos3, h2, rows):
    t, d = h2.shape
    tt = pos3.shape[-1] // 2
    return pl.pallas_call(
        _dispatch_kernel,
        grid_spec=pltpu.PrefetchScalarGridSpec(
            num_scalar_prefetch=1,
            grid=(t // tt,),
            in_specs=[pl.BlockSpec((None, 1, 2 * tt), lambda i, tv: (i, 0, 0), memory_space=pltpu.SMEM),
                      pl.BlockSpec((tt, d), lambda i, tv: (i, 0))],
            out_specs=pl.BlockSpec(memory_space=pl.ANY),
            scratch_shapes=[pltpu.VMEM((FFN_ROWS, d), F32), pltpu.SemaphoreType.DMA(()),
                            pltpu.SemaphoreType.DMA(())]),
        out_shape=jax.ShapeDtypeStruct((rows, d), F32),
        compiler_params=_cparams("arbitrary"),
        name="moe_dispatch",
    )(tile_valid, pos3, h2)


def _ffn_kernel(te_ref, tv_ref, xs_ref, wg_ref, wu_ref, wd_ref, o_ref):
    nv = tv_ref[pl.program_id(0)]

    @pl.when(nv > 0)
    def _():
        x = jnp.where(_iota((FFN_ROWS, 1), 0) < nv, xs_ref[...], 0.0).astype(BF16)
        gate = jnp.dot(x, wg_ref[...], preferred_element_type=F32)
        up = jnp.dot(x, wu_ref[...], preferred_element_type=F32)
        o_ref[...] = _dot(_silu(gate) * up, wd_ref[...])

    @pl.when(nv <= 0)
    def _():
        o_ref[...] = jnp.zeros_like(o_ref)


def _ffn(tile_expert, tile_valid, xs, wg, wu, wd):
    rows, d = xs.shape
    wmap = lambda i, te, tv: (te[i], 0, 0)
    return pl.pallas_call(
        _ffn_kernel,
        grid_spec=pltpu.PrefetchScalarGridSpec(
            num_scalar_prefetch=2,
            grid=(rows // FFN_ROWS,),
            in_specs=[pl.BlockSpec((FFN_ROWS, d), lambda i, te, tv: (i, 0)),
                      pl.BlockSpec((None, d, MOE_FF), wmap),
                      pl.BlockSpec((None, d, MOE_FF), wmap),
                      pl.BlockSpec((None, MOE_FF, d), wmap)],
            out_specs=pl.BlockSpec((FFN_ROWS, d), lambda i, te, tv: (i, 0))),
        out_shape=jax.ShapeDtypeStruct((rows, d), F32),
        compiler_params=_cparams("arbitrary"),
        name="moe_ffn",
    )(tile_expert, tile_valid, xs, wg, wu, wd)


def _combine_kernel(pos_ref, ys_ref, plan_ref, x1_ref, gt2_ref, fsc_ref, fsh_ref, fnw_ref, y_ref,
                    buf_ref, sem):
    tt = x1_ref.shape[0]

    def issue(r, carry):
        for k in range(2):
            _row_copy(ys_ref, pos_ref[0, 2 * r + k], buf_ref.at[k], r, sem).start()
        return carry

    lax.fori_loop(0, tt, issue, 0)

    def drain(r, carry):
        _row_copy(ys_ref, 0, buf_ref.at[0], 0, sem).wait()
        return carry

    lax.fori_loop(0, 2 * tt, drain, 0)
    plan = plan_ref[...]
    moe = plan[:, PLAN_C0:PLAN_C0 + 1] * buf_ref[0] + plan[:, PLAN_C1:PLAN_C1 + 1] * buf_ref[1]
    x2 = x1_ref[...] + gt2_ref[...] * moe
    y_ref[...] = _rms(x2, fnw_ref[...]) * (1.0 + fsc_ref[...]) + fsh_ref[...]


def _combine(pos3, ys, plan, x1, gt2, fsc, fsh, fnw):
    bx, lx, d = x1.shape
    tt = pos3.shape[-1] // 2
    nl = lx // tt
    tokspec = lambda w: pl.BlockSpec((None, tt, w), lambda b, i: (b, i, 0))
    return pl.pallas_call(
        _combine_kernel,
        grid=(bx, nl),
        in_specs=[pl.BlockSpec((None, 1, 2 * tt), lambda b, i: (b * nl + i, 0, 0), memory_space=pltpu.SMEM),
                  pl.BlockSpec(memory_space=pl.ANY),
                  tokspec(LANES), tokspec(d), _mod_spec(gt2, tt), _mod_spec(fsc, tt), _mod_spec(fsh, tt),
                  pl.BlockSpec((1, d), lambda b, i: (0, 0))],
        out_specs=tokspec(d),
        out_shape=jax.ShapeDtypeStruct((bx, lx, d), F32),
        scratch_shapes=[pltpu.VMEM((2, tt, d), F32), pltpu.SemaphoreType.DMA(())],
        compiler_params=_cparams("arbitrary", "arbitrary"),
        name="moe_combine_final",
    )(pos3, ys, plan, x1, gt2, fsc, fsh, fnw.reshape(1, d))


def _moe(h2, plan, wg, wu, wd, x1, gt2, fsc, fsh, fnw):
    bx, lx, d = x1.shape
    t = bx * lx
    tt = min(TOK_TILE, lx)
    plan2 = plan.reshape(t, LANES)
    rank, cnt = _plan(plan2, min(t, 512))
    counts = cnt[0, :MOE_EXPERTS].astype(jnp.int32)
    padded = (counts + FFN_ROWS - 1) // FFN_ROWS * FFN_ROWS
    ends = jnp.cumsum(padded)
    starts = ends - padded
    e01 = plan2[:, PLAN_E0:PLAN_E1 + 1].astype(jnp.int32)
    pos = starts[e01] + rank[:, PLAN_E0:PLAN_E1 + 1].astype(jnp.int32)
    pos3 = pos.reshape(t // tt, 1, 2 * tt)
    rows = -(-(2 * t) // FFN_ROWS) * FFN_ROWS + MOE_EXPERTS * FFN_ROWS
    tile_start = jnp.arange(rows // FFN_ROWS, dtype=jnp.int32) * FFN_ROWS
    tile_expert = jnp.minimum(jnp.searchsorted(ends, tile_start, side="right"), MOE_EXPERTS - 1).astype(jnp.int32)
    tile_valid = jnp.clip(counts[tile_expert] - (tile_start - starts[tile_expert]), 0, FFN_ROWS).astype(jnp.int32)
    xs = _dispatch(tile_valid, pos3, h2.reshape(t, d), rows)
    ys = _ffn(tile_expert, tile_valid, xs, wg, wu, wd)
    return _combine(pos3, ys, plan, x1, gt2, fsc, fsh, fnw)


def _prep_layer(lp):
    w_in = lp["w_in"]
    offs = [0]
    for s in (DN_CONV_DIM, DN_VAL, DN_HEADS, DN_HEADS, SSM_CONV_DIM, SSM_INNER, SSM_HEADS, D_MODEL, D_MODEL):
        offs.append(offs[-1] + s)
    seg = lambda i: w_in[:, offs[i]:offs[i + 1]]
    small = jnp.concatenate([seg(2), seg(3), seg(6)], axis=1)
    pad = jnp.zeros((D_MODEL, PROJ_N - COL_SMALL - small.shape[1]), F32)
    w_cat = jnp.concatenate([seg(0), seg(4), seg(5), seg(7), seg(8), seg(1), small, pad], axis=1).astype(BF16)
    wr = jnp.concatenate([lp["w_group_router"], lp["w_expert_router"],
                          jnp.zeros((D_MODEL, LANES - MOE_GROUPS - MOE_EXPERTS), F32)], axis=1)
    br = jnp.concatenate([lp["b_group_router"], lp["b_expert_router"],
                          jnp.zeros((LANES - MOE_GROUPS - MOE_EXPERTS,), F32)]).reshape(1, LANES)
    return dict(lp, w_cat=w_cat, wr=wr, br=br,
                wdn=lp["w_dn_out"].astype(BF16), wssm=lp["w_ssm_out"].astype(BF16), wout=lp["w_out"].astype(BF16),
                wg=lp["w_exp_gate"].astype(BF16), wu=lp["w_exp_up"].astype(BF16), wd=lp["w_exp_down"].astype(BF16))


def _trunk(x3, mods, fins, states, layers, final_norm_w, cfg):
    bx, lx, d = x3.shape
    tm = cfg["tm"]
    scan = dict(bt=cfg["bt"], r=cfg["r"], c=cfg["c"], lc=cfg["lc"], ng=cfg["ng"], cpi=cfg["cpi"])
    new_states = []
    n_layers = len(layers)
    for l, lp in enumerate(layers):
        sh1, sc1, gt1, sh2, sc2, gt2 = mods[l]
        dn_conv, dn_rec, ssm_conv, ssm_rec = states[l]
        proj = _inproj(x3, sc1, sh1, lp["norm_mix_w"], lp["w_cat"], tm).reshape(bx * lx, PROJ_N)
        og, dn_conv_new, dn_rec_new = _dn_branch(proj, lp["dn_conv_w"], lp["dn_A_log"], lp["dn_dt_bias"],
                                                 lp["dn_norm_w"], dn_conv, dn_rec, **scan)
        ys, ssm_conv_new, ssm_rec_new = _ssd_branch(proj, lp["ssm_conv_w"], lp["ssm_conv_b"], lp["ssm_A_log"],
                                                    lp["ssm_dt_bias"], lp["ssm_D"], lp["ssm_norm_w"],
                                                    ssm_conv, ssm_rec, **scan)
        x1, h2, plan = _merge(x3, og, ys, proj, gt1, sc2, sh2, lp["norm_ffn_w"], lp["wdn"], lp["wssm"], lp["wout"],
                              lp["wr"], lp["br"], cfg["tm_merge"])
        if l == n_layers - 1:
            fsh, fsc, fnw = fins[0], fins[1], final_norm_w
            x3 = _moe(h2, plan, lp["wg"], lp["wu"], lp["wd"], x1, gt2, fsc, fsh, fnw)
        else:
            raise NotImplementedError("only the last layer fuses the final norm; depth is 1 here")
        new_states.append((dn_conv_new, dn_rec_new, ssm_conv_new, ssm_rec_new))
    return x3, new_states


def _per_seq(m):
    return m[:, None, :]


def kernel(x_prompt, x_sample, c_prompt, c_sample, state_dn_conv, state_dn_rec, state_ssm_conv, state_ssm_rec, w_ada, b_ada, norm_mix_w, w_in, dn_conv_w, dn_A_log, dn_dt_bias, dn_norm_w, w_dn_out, ssm_conv_w, ssm_conv_b, ssm_A_log, ssm_dt_bias, ssm_D, ssm_norm_w, w_ssm_out, w_out, norm_ffn_w, w_group_router, b_group_router, w_expert_router, b_expert_router, w_exp_gate, w_exp_up, w_exp_down, w_ada_final, b_ada_final, final_norm_w):
    depth = w_ada.shape[0]
    assert depth == 1
    per_layer = dict(w_ada=w_ada, b_ada=b_ada, norm_mix_w=norm_mix_w, w_in=w_in, dn_conv_w=dn_conv_w,
                     dn_A_log=dn_A_log, dn_dt_bias=dn_dt_bias, dn_norm_w=dn_norm_w, w_dn_out=w_dn_out,
                     ssm_conv_w=ssm_conv_w, ssm_conv_b=ssm_conv_b, ssm_A_log=ssm_A_log, ssm_dt_bias=ssm_dt_bias,
                     ssm_D=ssm_D, ssm_norm_w=ssm_norm_w, w_ssm_out=w_ssm_out, w_out=w_out, norm_ffn_w=norm_ffn_w,
                     w_group_router=w_group_router, b_group_router=b_group_router,
                     w_expert_router=w_expert_router, b_expert_router=b_expert_router,
                     w_exp_gate=w_exp_gate, w_exp_up=w_exp_up, w_exp_down=w_exp_down)
    layers = [_prep_layer({k: v[l] for k, v in per_layer.items()}) for l in range(depth)]

    nbp, lp_, d = x_prompt.shape
    nbs, ls, _ = x_sample.shape
    c_all = jnp.concatenate([c_prompt, c_sample], axis=0)
    mod_all = [_ada(c_all, lyr["w_ada"], lyr["b_ada"]) for lyr in layers]
    fin_all = _ada(c_all, w_ada_final, b_ada_final)

    mods_p = [[_per_seq(m) for m in jnp.split(ma[:nbp], 6, axis=-1)] for ma in mod_all]
    fins_p = [_per_seq(m) for m in jnp.split(fin_all[:nbp], 2, axis=-1)]
    zeros_p = [(jnp.zeros((nbp, CONV_K - 1, DN_CONV_DIM), F32), jnp.zeros((nbp, DN_HEADS, DN_DK, DN_DV), F32),
                jnp.zeros((nbp, CONV_K - 1, SSM_CONV_DIM), F32),
                jnp.zeros((nbp, SSM_HEADS, SSM_HEADDIM, SSM_STATE), F32)) for _ in range(depth)]
    c_p = min(SCAN_CHUNK, lp_)
    r_p = min(lp_, 8 * c_p)
    cfg_p = dict(tm=min(lp_, 1024), tm_merge=min(lp_, 512), tm_moe=min(lp_, 1024), bt=1, r=r_p, c=c_p, lc=c_p, ng=lp_ // r_p,
                 cpi=4 if (r_p // c_p) % 4 == 0 else 1)
    y_p, st_p = _trunk(x_prompt, mods_p, fins_p, zeros_p, layers, final_norm_w, cfg_p)

    lpad = -(-ls // SUBLANES) * SUBLANES
    bt_s = 8
    xs = jnp.pad(x_sample, ((0, 0), (0, lpad - ls), (0, 0))).reshape(1, nbs * lpad, d)
    per_tok = lambda m: jnp.repeat(m, lpad, axis=0)[None]
    mods_s = [[per_tok(m) for m in jnp.split(ma[nbp:], 6, axis=-1)] for ma in mod_all]
    fins_s = [per_tok(m) for m in jnp.split(fin_all[nbp:], 2, axis=-1)]
    st_in = [(state_dn_conv[l], state_dn_rec[l], state_ssm_conv[l], state_ssm_rec[l]) for l in range(depth)]
    ts = nbs * lpad
    cfg_s = dict(tm=ts, tm_merge=min(ts, 512), tm_moe=min(ts, 512), bt=bt_s, r=lpad, c=lpad, lc=ls, ng=1, cpi=1)
    y_s, st_s = _trunk(xs, mods_s, fins_s, st_in, layers, final_norm_w, cfg_s)
    y_s = y_s.reshape(nbs, lpad, d)[:, :ls]

    stack = lambda sts, i: jnp.stack([s[i] for s in sts])
    return (y_p, y_s, stack(st_p, 0), stack(st_p, 1), stack(st_p, 2), stack(st_p, 3),
            stack(st_s, 0), stack(st_s, 1), stack(st_s, 2), stack(st_s, 3))
```

```python
import functools
import math

import jax
import jax.numpy as jnp
from jax import lax
from jax.experimental import pallas as pl
from jax.experimental.pallas import tpu as pltpu

F32 = jnp.float32
BF16 = jnp.bfloat16

D_MODEL = 1024
DN_HEADS = 4
DN_DK = 128
DN_DV = 128
DN_KEY = DN_HEADS * DN_DK
DN_VAL = DN_HEADS * DN_DV
CONV_K = 4
DN_CONV_DIM = 2 * DN_KEY + DN_VAL
SSM_INNER = D_MODEL
SSM_HEADDIM = 64
SSM_HEADS = SSM_INNER // SSM_HEADDIM
SSM_GROUPS = 2
SSM_STATE = 128
SSM_CONV_DIM = SSM_INNER + 2 * SSM_GROUPS * SSM_STATE
MOE_GROUPS = 4
MOE_PER_GROUP = 8
MOE_EXPERTS = MOE_GROUPS * MOE_PER_GROUP
MOE_FF = D_MODEL // 4
EPS = 1e-6
SCAN_CHUNK = 64

LANES = 128
SUBLANES = 8
CARRY_ROW = SUBLANES - (CONV_K - 1)

COL_DN_QKV = 0
COL_SSM_XBC = COL_DN_QKV + DN_CONV_DIM
COL_SSM_Z = COL_SSM_XBC + SSM_CONV_DIM
COL_GATES = COL_SSM_Z + SSM_INNER
COL_DN_Z = COL_GATES + 2 * D_MODEL
COL_SMALL = COL_DN_Z + DN_VAL
PROJ_TN = 1024
PROJ_N = 7 * PROJ_TN
SM_A, SM_B, SM_DT = 0, DN_HEADS, 2 * DN_HEADS
RT_GROUP, RT_EXPERT = 0, MOE_GROUPS

VMEM_LIMIT = 56 * 1024 * 1024


def _cparams(*sem):
    return pltpu.CompilerParams(dimension_semantics=sem, vmem_limit_bytes=VMEM_LIMIT)


def _dot(a, b):
    return jnp.dot(a.astype(BF16), b.astype(BF16), preferred_element_type=F32)


def _dot_nt(a, b):
    return lax.dot_general(a.astype(BF16), b.astype(BF16), (((1,), (1,)), ((), ())),
                           preferred_element_type=F32)


def _dot_tn(a, b):
    return lax.dot_general(a.astype(BF16), b.astype(BF16), (((0,), (0,)), ((), ())),
                           preferred_element_type=F32)


def _split3(x):
    hi = x.astype(BF16)
    r = x - hi.astype(F32)
    mid = r.astype(BF16)
    lo = (r - mid.astype(F32)).astype(BF16)
    return hi, mid, lo


def _dot_exact_lhs(a_bf, b):
    hi, mid, lo = _split3(b)
    d = functools.partial(jnp.dot, preferred_element_type=F32)
    return d(a_bf, hi) + (d(a_bf, mid) + d(a_bf, lo))


def _dot_x3(a, b):
    a_hi = a.astype(BF16)
    a_lo = (a - a_hi.astype(F32)).astype(BF16)
    b_hi = b.astype(BF16)
    b_lo = (b - b_hi.astype(F32)).astype(BF16)
    d = functools.partial(jnp.dot, preferred_element_type=F32)
    return d(a_hi, b_hi) + (d(a_hi, b_lo) + d(a_lo, b_hi))


def _sigmoid(x):
    return 1.0 / (1.0 + jnp.exp(-x))


def _silu(x):
    return x * _sigmoid(x)


def _softplus(x):
    return jnp.maximum(x, 0.0) + jnp.log1p(jnp.exp(-jnp.abs(x)))


def _rms(x, w):
    return x * lax.rsqrt(jnp.mean(x * x, axis=-1, keepdims=True) + EPS) * w


def _iota(shape, dim):
    return lax.broadcasted_iota(jnp.int32, shape, dim)


def _ada_kernel(c_ref, w_ref, b_ref, o_ref):
    o_ref[...] = _dot(_silu(c_ref[...]), w_ref[...]) + b_ref[...]


def _ada(c, w, b, tn=512):
    m, d = c.shape
    n = w.shape[1]
    return pl.pallas_call(
        _ada_kernel,
        grid=(n // tn,),
        in_specs=[pl.BlockSpec((m, d), lambda j: (0, 0)),
                  pl.BlockSpec((d, tn), lambda j: (0, j)),
                  pl.BlockSpec((1, tn), lambda j: (0, j))],
        out_specs=pl.BlockSpec((m, tn), lambda j: (0, j)),
        out_shape=jax.ShapeDtypeStruct((m, n), F32),
        compiler_params=_cparams("arbitrary"),
        name="ada_mod",
    )(c, w, b.reshape(1, n))


def _mod_spec(mod, tm):
    if mod.shape[1] == 1:
        return pl.BlockSpec((None, 1, D_MODEL), lambda b, i, *_: (b, 0, 0))
    return pl.BlockSpec((None, tm, D_MODEL), lambda b, i, *_: (b, i, 0))


def _inproj_kernel(x_ref, sc_ref, sh_ref, nw_ref, w_ref, o_ref, h_ref, *, tm, sub):
    @pl.when(pl.program_id(2) == 0)
    def _():
        per_token = sc_ref.shape[0] != 1

        def body(r, carry):
            rows = pl.ds(pl.multiple_of(r * sub, sub), sub)
            sc = sc_ref[rows, :] if per_token else sc_ref[...]
            sh = sh_ref[rows, :] if per_token else sh_ref[...]
            h = _rms(x_ref[rows, :], nw_ref[...]) * (1.0 + sc) + sh
            h_ref[rows, :] = h.astype(BF16)
            return carry

        lax.fori_loop(0, tm // sub, body, 0)

    o_ref[...] = jnp.dot(h_ref[...], w_ref[...], preferred_element_type=F32)


def _inproj(x3, sc, sh, nw, w_cat, tm):
    bx, lx, d = x3.shape
    n = w_cat.shape[1]
    kern = functools.partial(_inproj_kernel, tm=tm, sub=min(tm, 256))
    return pl.pallas_call(
        kern,
        grid=(bx, lx // tm, n // PROJ_TN),
        in_specs=[pl.BlockSpec((None, tm, d), lambda b, i, j: (b, i, 0)),
                  _mod_spec(sc, tm), _mod_spec(sh, tm),
                  pl.BlockSpec((1, d), lambda b, i, j: (0, 0)),
                  pl.BlockSpec((d, PROJ_TN), lambda b, i, j: (0, j))],
        out_specs=pl.BlockSpec((None, tm, PROJ_TN), lambda b, i, j: (b, i, j)),
        out_shape=jax.ShapeDtypeStruct((bx, lx, n), F32),
        scratch_shapes=[pltpu.VMEM((tm, d), BF16)],
        compiler_params=_cparams("arbitrary", "arbitrary", "arbitrary"),
        name="norm_inproj",
    )(x3, sc, sh, nw.reshape(1, d), w_cat)


def _conv_silu(u, ext_ref, bb, cw_ref, c, lc, bias=None):
    ext_ref[bb, SUBLANES:SUBLANES + c, :] = u
    y = ext_ref[bb, CARRY_ROW:CARRY_ROW + c, :] * cw_ref[0:1, :]
    for i in range(1, CONV_K):
        y = y + ext_ref[bb, CARRY_ROW + i:CARRY_ROW + i + c, :] * cw_ref[i:i + 1, :]
    ext_ref[bb, CARRY_ROW:SUBLANES, :] = ext_ref[bb, CARRY_ROW + lc:SUBLANES + lc, :]
    if bias is not None:
        y = y + bias
    return _silu(y)


def _run_rounds(process, bt, r, c, cpi):
    g = r // c
    if g == 1:
        process([[(bb, 0) for bb in range(bt)]])
    else:
        def body(ci, carry):
            process([[(bb, pl.multiple_of((ci * cpi + t) * c, c)) for bb in range(bt)] for t in range(cpi)])
            return carry
        lax.fori_loop(0, g // cpi, body, 0)


INV_BASE = SUBLANES


def _inverse_masks(row, col, c):
    sh = lambda x, s: jnp.right_shift(x, int(math.log2(s)))
    diag = sh(row, INV_BASE) == sh(col, INV_BASE)
    merges = []
    s = INV_BASE
    while s < c:
        merges.append((sh(row, 2 * s) == sh(col, 2 * s))
                      & (jnp.bitwise_and(sh(row, s), 1) == 1) & (jnp.bitwise_and(sh(col, s), 1) == 0))
        s *= 2
    return diag, merges


def _unit_lower_inverses(lmats, eye_f, masks):
    diag, merges = masks
    npows = [-jnp.where(diag, l, 0.0) for l in lmats]
    ps = [eye_f + n for n in npows]
    for _ in range(int(math.log2(INV_BASE)) - 1):
        npows = [_dot(n, n) for n in npows]
        ps = [p + _dot(p, n) for p, n in zip(ps, npows)]
    for m in merges:
        ts = [_dot(p, jnp.where(m, l, 0.0)) for p, l in zip(ps, lmats)]
        ps = [p - _dot(t, p) for t, p in zip(ts, ps)]
    return ps


def _dn_kernel(qkv_ref, z_ref, sm_ref, cw_ref, alog_ref, bias_ref, nw_ref, cin_ref, sin_ref,
               o_ref, cout_ref, sout_ref, ext_ref, *, bt, r, c, lc, ng, cpi):
    j = pl.program_id(1)

    @pl.when(j == 0)
    def _():
        ext_ref[:, CARRY_ROW:SUBLANES, :] = cin_ref[...]
        sout_ref[...] = sin_ref[...]

    row = _iota((c, c), 0)
    col = _iota((c, c), 1)
    tri_incl = row >= col
    tri_strict = row > col
    eye_f = (row == col).astype(F32)
    tril_bf = tri_incl.astype(BF16)
    valid = _iota((c, 1), 0) < lc
    nexp_a = -jnp.exp(alog_ref[...])
    bias = bias_ref[...]
    inv_masks = _inverse_masks(row, col, c)
    heads = range(DN_HEADS)

    def prep(bb, r0):
        rows = pl.ds(bb * r + r0, c)
        y = _conv_silu(qkv_ref[rows, :], ext_ref, bb, cw_ref, c, lc)
        sm = sm_ref[rows, :]
        g_all = nexp_a * _softplus(sm + bias)
        beta_all = _sigmoid(sm)
        if lc < c:
            g_all = jnp.where(valid, g_all, 0.0)
            beta_all = jnp.where(valid, beta_all, 0.0)
        cs_all = _dot_exact_lhs(tril_bf, g_all)
        cs_t = cs_all.T
        ecs_all = jnp.exp(cs_all)
        per_head = []
        for h in heads:
            q = y[:, h * DN_DK:(h + 1) * DN_DK]
            k = y[:, DN_KEY + h * DN_DK:DN_KEY + (h + 1) * DN_DK]
            v = y[:, 2 * DN_KEY + h * DN_DV:2 * DN_KEY + (h + 1) * DN_DV]
            q = q * lax.rsqrt(jnp.sum(q * q, axis=-1, keepdims=True) + EPS) * (DN_DK ** -0.5)
            k = k * lax.rsqrt(jnp.sum(k * k, axis=-1, keepdims=True) + EPS)
            if lc < c:
                k = jnp.where(valid, k, 0.0)
            cs = cs_all[:, SM_A + h:SM_A + h + 1]
            ecs = ecs_all[:, SM_A + h:SM_A + h + 1]
            beta = beta_all[:, SM_B + h:SM_B + h + 1]
            last = cs_all[c - 1:c, SM_A + h:SM_A + h + 1]
            decay = jnp.exp(jnp.where(tri_incl, cs - cs_t[SM_A + h:SM_A + h + 1, :], -jnp.inf))
            per_head.append(dict(q_ecs=(q * ecs).astype(BF16), q=q.astype(BF16), k=k.astype(BF16), beta=beta,
                                 decay=decay, elast=jnp.exp(last),
                                 k_tail=(k * jnp.exp(last - cs)).astype(BF16),
                                 rhs=jnp.concatenate([v * beta, k * (beta * ecs)], axis=1).astype(BF16)))
        return per_head

    def process(rounds):
        items = [it for rnd in rounds for it in rnd]
        pre = [prep(bb, r0) for bb, r0 in items]
        chains = [(i, h) for i in range(len(items)) for h in heads]
        a = {ch: pre[ch[0]][ch[1]] for ch in chains}
        kk = {ch: _dot_nt(a[ch]["k"], a[ch]["k"]) for ch in chains}
        qk = {ch: _dot_nt(a[ch]["q"], a[ch]["k"]) * a[ch]["decay"] for ch in chains}
        lmats = [jnp.where(tri_strict, a[ch]["beta"] * kk[ch] * a[ch]["decay"], 0.0) for ch in chains]
        pinv = _unit_lower_inverses(lmats, eye_f, inv_masks)
        sol = {ch: _dot(p, a[ch]["rhs"]) for ch, p in zip(chains, pinv)}
        first = 0
        for rnd in rounds:
            idx = range(first, first + len(rnd))
            first += len(rnd)
            rch = [(i, h) for i in idx for h in heads]
            s = {ch: sout_ref[items[ch[0]][0], ch[1]] for ch in rch}
            ws_qs = {ch: _dot(jnp.concatenate([sol[ch][:, DN_DV:].astype(BF16), a[ch]["q_ecs"]], axis=0), s[ch])
                     for ch in rch}
            v_new = {ch: (sol[ch][:, :DN_DV] - ws_qs[ch][:c]).astype(BF16) for ch in rch}
            o = {ch: ws_qs[ch][c:] + _dot(qk[ch], v_new[ch]) for ch in rch}
            for ch in rch:
                sout_ref[items[ch[0]][0], ch[1]] = a[ch]["elast"] * s[ch] + _dot_tn(a[ch]["k_tail"], v_new[ch])
            for i in idx:
                bb, r0 = items[i]
                rows = pl.ds(bb * r + r0, c)
                o_ref[rows, :] = jnp.concatenate(
                    [_rms(o[(i, h)], nw_ref[...]) * _silu(z_ref[rows, h * DN_DV:(h + 1) * DN_DV])
                     for h in heads], axis=1)

    _run_rounds(process, bt, r, c, cpi)

    @pl.when(j == ng - 1)
    def _():
        cout_ref[...] = ext_ref[:, CARRY_ROW:SUBLANES, :]


def _small_row(vals, offset, fill=0.0):
    row = jnp.full((1, LANES), fill, F32)
    return lax.dynamic_update_slice(row, vals.astype(F32).reshape(1, -1), (0, offset))


def _dn_branch(proj, conv_w, a_log, dt_bias, norm_w, conv_in, rec_in, *, bt, r, c, lc, ng, cpi):
    nb = conv_in.shape[0] // bt
    rows = bt * r
    t = proj.shape[0]
    kern = functools.partial(_dn_kernel, bt=bt, r=r, c=c, lc=lc, ng=ng, cpi=cpi)
    rowmap = lambda col: (lambda i, j: (i * ng + j, col))
    const = lambda i, j: (0, 0)
    return pl.pallas_call(
        kern,
        grid=(nb, ng),
        in_specs=[pl.BlockSpec((rows, DN_CONV_DIM), rowmap(COL_DN_QKV // DN_CONV_DIM)),
                  pl.BlockSpec((rows, DN_VAL), rowmap(COL_DN_Z // DN_VAL)),
                  pl.BlockSpec((rows, LANES), rowmap(COL_SMALL // LANES)),
                  pl.BlockSpec((CONV_K, DN_CONV_DIM), const),
                  pl.BlockSpec((1, LANES), const),
                  pl.BlockSpec((1, LANES), const),
                  pl.BlockSpec((1, DN_DV), const),
                  pl.BlockSpec((bt, CONV_K - 1, DN_CONV_DIM), lambda i, j: (i, 0, 0)),
                  pl.BlockSpec((bt, DN_HEADS, DN_DK, DN_DV), lambda i, j: (i, 0, 0, 0))],
        out_specs=[pl.BlockSpec((rows, DN_VAL), lambda i, j: (i * ng + j, 0)),
                   pl.BlockSpec((bt, CONV_K - 1, DN_CONV_DIM), lambda i, j: (i, 0, 0)),
                   pl.BlockSpec((bt, DN_HEADS, DN_DK, DN_DV), lambda i, j: (i, 0, 0, 0))],
        out_shape=[jax.ShapeDtypeStruct((t, DN_VAL), F32),
                   jax.ShapeDtypeStruct(conv_in.shape, F32),
                   jax.ShapeDtypeStruct(rec_in.shape, F32)],
        scratch_shapes=[pltpu.VMEM((bt, c + SUBLANES, DN_CONV_DIM), F32)],
        compiler_params=_cparams("arbitrary", "arbitrary"),
        name="gated_delta",
    )(proj, proj, proj, conv_w, _small_row(a_log, SM_A), _small_row(dt_bias, SM_A),
      norm_w.reshape(1, DN_DV), conv_in, rec_in)


def _ssd_kernel(xbc_ref, z_ref, sm_ref, cw_ref, cb_ref, alog_ref, bias_ref, dvec_ref, nw_ref, cin_ref,
                hin_ref, y_ref, cout_ref, hout_ref, ext_ref, *, bt, r, c, lc, ng, cpi):
    j = pl.program_id(1)

    @pl.when(j == 0)
    def _():
        ext_ref[:, CARRY_ROW:SUBLANES, :] = cin_ref[...]
        hout_ref[...] = hin_ref[...]

    tril_bf = (_iota((c, c), 0) >= _iota((c, c), 1)).astype(BF16)
    row2 = _iota((c, 2 * c), 0)
    lane2 = _iota((c, 2 * c), 1)
    left2 = lane2 < c
    tri2 = row2 >= jnp.where(left2, lane2, lane2 - c)
    left_x = _iota((c, LANES), 1) < SSM_HEADDIM
    top_h = _iota((2 * SSM_HEADDIM, 1), 0) < SSM_HEADDIM
    valid = _iota((c, 1), 0) < lc
    nexp_a = -jnp.exp(alog_ref[...])
    bias = bias_ref[...]
    gw = SSM_INNER // SSM_GROUPS
    pairs_per_group = SSM_HEADS // SSM_GROUPS // 2

    def chunk(bb, r0):
        rows = pl.ds(bb * r + r0, c)
        y = _conv_silu(xbc_ref[rows, :], ext_ref, bb, cw_ref, c, lc, bias=cb_ref[...])
        sm = sm_ref[rows, :]
        dt_all = _softplus(sm + bias)
        if lc < c:
            dt_all = jnp.where(valid, dt_all, 0.0)
        cs_all = _dot_exact_lhs(tril_bf, dt_all * nexp_a)
        cs_t2 = jnp.concatenate([cs_all, cs_all], axis=0).T
        ecs_all = jnp.exp(cs_all)
        last = cs_all[c - 1:c, :]
        w_all = jnp.exp(last - cs_all)
        elast = jnp.exp(last)
        outs = []
        for g in range(SSM_GROUPS):
            bm = y[:, SSM_INNER + g * SSM_STATE:SSM_INNER + (g + 1) * SSM_STATE]
            cm = y[:, SSM_INNER + (SSM_GROUPS + g) * SSM_STATE:SSM_INNER + (SSM_GROUPS + g + 1) * SSM_STATE]
            bm_bf = bm.astype(BF16)
            cm_bf = cm.astype(BF16)
            cb2 = _dot_nt(cm_bf, jnp.concatenate([bm_bf, bm_bf], axis=0))
            for pi in range(pairs_per_group):
                pair = g * pairs_per_group + pi
                la, lb = SM_DT + 2 * pair, SM_DT + 2 * pair + 1
                xs = y[:, pair * LANES:(pair + 1) * LANES]
                xdt = xs * jnp.where(left_x, dt_all[:, la:la + 1], dt_all[:, lb:lb + 1])
                diff = (jnp.where(left2, cs_all[:, la:la + 1], cs_all[:, lb:lb + 1])
                        - jnp.where(left2[0:1], cs_t2[la:la + 1, :], cs_t2[lb:lb + 1, :]))
                decay2 = jnp.exp(jnp.where(tri2, diff, -jnp.inf))
                rhs = jnp.concatenate([jnp.where(left_x, xdt, 0.0), jnp.where(left_x, 0.0, xdt)], axis=0)
                y_intra = _dot(cb2 * decay2, rhs)
                hs = hout_ref[bb, pair * LANES:(pair + 1) * LANES, :]
                y_inter = _dot_nt(cm_bf, hs) * jnp.where(left_x, ecs_all[:, la:la + 1], ecs_all[:, lb:lb + 1])
                xw = xdt * jnp.where(left_x, w_all[:, la:la + 1], w_all[:, lb:lb + 1])
                hout_ref[bb, pair * LANES:(pair + 1) * LANES, :] = (
                    jnp.where(top_h, elast[:, la:la + 1], elast[:, lb:lb + 1]) * hs + _dot_tn(xw, bm_bf))
                outs.append(y_intra + y_inter + dvec_ref[:, pair * LANES:(pair + 1) * LANES] * xs)
        yz = jnp.concatenate(outs, axis=1) * _silu(z_ref[rows, :])
        y_ref[rows, :] = jnp.concatenate(
            [_rms(yz[:, g * gw:(g + 1) * gw], nw_ref[:, g * gw:(g + 1) * gw]) for g in range(SSM_GROUPS)], axis=1)

    def process(rounds):
        for rnd in rounds:
            for bb, r0 in rnd:
                chunk(bb, r0)

    _run_rounds(process, bt, r, c, cpi)

    @pl.when(j == ng - 1)
    def _():
        cout_ref[...] = ext_ref[:, CARRY_ROW:SUBLANES, :]


def _ssd_branch(proj, conv_w, conv_b, a_log, dt_bias, d_skip, norm_w, conv_in, rec_in, *, bt, r, c, lc, ng, cpi):
    nb = conv_in.shape[0] // bt
    rows = bt * r
    t = proj.shape[0]
    hrows = SSM_HEADS * SSM_HEADDIM
    rec2 = rec_in.reshape(rec_in.shape[0], hrows, SSM_STATE)
    kern = functools.partial(_ssd_kernel, bt=bt, r=r, c=c, lc=lc, ng=ng, cpi=cpi)
    rowmap = lambda col: (lambda i, j: (i * ng + j, col))
    const = lambda i, j: (0, 0)
    y, conv_out, rec_out = pl.pallas_call(
        kern,
        grid=(nb, ng),
        in_specs=[pl.BlockSpec((rows, SSM_CONV_DIM), rowmap(COL_SSM_XBC // SSM_CONV_DIM)),
                  pl.BlockSpec((rows, SSM_INNER), rowmap(COL_SSM_Z // SSM_INNER)),
                  pl.BlockSpec((rows, LANES), rowmap(COL_SMALL // LANES)),
                  pl.BlockSpec((CONV_K, SSM_CONV_DIM), const),
                  pl.BlockSpec((1, SSM_CONV_DIM), const),
                  pl.BlockSpec((1, LANES), const),
                  pl.BlockSpec((1, LANES), const),
                  pl.BlockSpec((1, SSM_INNER), const),
                  pl.BlockSpec((1, SSM_INNER), const),
                  pl.BlockSpec((bt, CONV_K - 1, SSM_CONV_DIM), lambda i, j: (i, 0, 0)),
                  pl.BlockSpec((bt, hrows, SSM_STATE), lambda i, j: (i, 0, 0))],
        out_specs=[pl.BlockSpec((rows, SSM_INNER), lambda i, j: (i * ng + j, 0)),
                   pl.BlockSpec((bt, CONV_K - 1, SSM_CONV_DIM), lambda i, j: (i, 0, 0)),
                   pl.BlockSpec((bt, hrows, SSM_STATE), lambda i, j: (i, 0, 0))],
        out_shape=[jax.ShapeDtypeStruct((t, SSM_INNER), F32),
                   jax.ShapeDtypeStruct(conv_in.shape, F32),
                   jax.ShapeDtypeStruct(rec2.shape, F32)],
        scratch_shapes=[pltpu.VMEM((bt, c + SUBLANES, SSM_CONV_DIM), F32)],
        compiler_params=_cparams("arbitrary", "arbitrary"),
        name="ssd_scan",
    )(proj, proj, proj, conv_w, conv_b.reshape(1, -1), _small_row(a_log, SM_DT), _small_row(dt_bias, SM_DT),
      jnp.repeat(d_skip.astype(F32), SSM_HEADDIM).reshape(1, SSM_INNER), norm_w.reshape(1, SSM_INNER),
      conv_in, rec2)
    return y, conv_out, rec_out.reshape(rec_in.shape)


PLAN_E0, PLAN_E1, PLAN_C0, PLAN_C1 = 0, 1, 2, 3


def _route(logits):
    lane = _iota(logits.shape, 1).astype(F32)
    big = float(LANES)
    is_group = (lane >= RT_GROUP) & (lane < RT_GROUP + MOE_GROUPS)
    gl = jnp.where(is_group, logits, -jnp.inf)
    gmax = jnp.max(gl, axis=-1, keepdims=True)
    g_sel = jnp.min(jnp.where(gl == gmax, lane, big), axis=-1, keepdims=True) - RT_GROUP
    p_group = 1.0 / jnp.sum(jnp.exp(gl - gmax), axis=-1, keepdims=True)
    e_lo = RT_EXPERT + MOE_PER_GROUP * g_sel
    in_grp = (lane >= e_lo) & (lane < e_lo + MOE_PER_GROUP)
    el = jnp.where(in_grp, logits, -jnp.inf)
    ee = jnp.exp(el - jnp.max(el, axis=-1, keepdims=True))
    pe = jnp.where(in_grp, ee / jnp.sum(ee, axis=-1, keepdims=True), -1.0)
    p1 = jnp.max(pe, axis=-1, keepdims=True)
    i1 = jnp.min(jnp.where(pe == p1, lane, big), axis=-1, keepdims=True)
    pe2 = jnp.where(lane == i1, -1.0, pe)
    p2 = jnp.max(pe2, axis=-1, keepdims=True)
    i2 = jnp.min(jnp.where(pe2 == p2, lane, big), axis=-1, keepdims=True)
    tot = p1 + p2
    plan = jnp.where(lane == PLAN_E0, i1 - RT_EXPERT, 0.0) + jnp.where(lane == PLAN_E1, i2 - RT_EXPERT, 0.0)
    return plan + jnp.where(lane == PLAN_C0, p_group * p1 / tot, 0.0) + jnp.where(lane == PLAN_C1, p_group * p2 / tot, 0.0)


def _merge_kernel(x_ref, og_ref, ys_ref, gates_ref, gt1_ref, sc2_ref, sh2_ref, nw2_ref, wdn_ref, wssm_ref,
                  wout_ref, wr_ref, br_ref, x1_ref, h2_ref, plan_ref):
    y_dn = _dot(og_ref[...], wdn_ref[...])
    y_ssm = _dot(ys_ref[...], wssm_ref[...])
    merged = (_sigmoid(gates_ref[:, :D_MODEL]) * y_dn + _sigmoid(gates_ref[:, D_MODEL:]) * y_ssm)
    x1 = x_ref[...] + gt1_ref[...] * _dot(merged, wout_ref[...])
    x1_ref[...] = x1
    h2 = _rms(x1, nw2_ref[...]) * (1.0 + sc2_ref[...]) + sh2_ref[...]
    h2_ref[...] = h2.astype(BF16)
    plan_ref[...] = _route(_dot_x3(h2, wr_ref[...]) + br_ref[...])


def _merge(x3, og, ys, proj, gt1, sc2, sh2, nw2, wdn, wssm, wout, wr, br, tm):
    bx, lx, d = x3.shape
    nl = lx // tm
    rowmap = lambda col: (lambda b, i: (b * nl + i, col))
    const = lambda b, i: (0, 0)
    tok = lambda w, dt: jax.ShapeDtypeStruct((bx, lx, w), dt)
    return pl.pallas_call(
        _merge_kernel,
        grid=(bx, nl),
        in_specs=[pl.BlockSpec((None, tm, d), lambda b, i: (b, i, 0)),
                  pl.BlockSpec((tm, DN_VAL), rowmap(0)),
                  pl.BlockSpec((tm, SSM_INNER), rowmap(0)),
                  pl.BlockSpec((tm, 2 * d), rowmap(COL_GATES // (2 * d))),
                  _mod_spec(gt1, tm), _mod_spec(sc2, tm), _mod_spec(sh2, tm),
                  pl.BlockSpec((1, d), const),
                  pl.BlockSpec(wdn.shape, const), pl.BlockSpec(wssm.shape, const), pl.BlockSpec(wout.shape, const),
                  pl.BlockSpec(wr.shape, const), pl.BlockSpec((1, LANES), const)],
        out_specs=[pl.BlockSpec((None, tm, d), lambda b, i: (b, i, 0)),
                   pl.BlockSpec((None, tm, d), lambda b, i: (b, i, 0)),
                   pl.BlockSpec((None, tm, LANES), lambda b, i: (b, i, 0))],
        out_shape=[tok(d, F32), tok(d, BF16), tok(LANES, F32)],
        compiler_params=_cparams("arbitrary", "arbitrary"),
        name="merge_route",
    )(x3, og, ys, proj, gt1, sc2, sh2, nw2.reshape(1, d), wdn, wssm, wout, wr, br)


FFN_ROWS = 256
TOK_TILE = 512
RUN_ALIGN = SUBLANES
RUN_SIZES = tuple(RUN_ALIGN << s for s in range(8, -1, -1))
LOCAL_ROWS = 2 * TOK_TILE + MOE_EXPERTS * RUN_ALIGN


def _plan_kernel(plan_ref, rank_ref, cnt_ref):
    plan = plan_ref[...]
    tp = plan.shape[0]
    lane = _iota(plan.shape, 1).astype(F32)
    sel0 = lane == plan[:, PLAN_E0:PLAN_E0 + 1]
    sel1 = lane == plan[:, PLAN_E1:PLAN_E1 + 1]
    sel = (sel0 | sel1).astype(BF16)
    before = (_iota((tp, tp), 0) > _iota((tp, tp), 1)).astype(BF16)
    excl = jnp.dot(before, sel, preferred_element_type=F32)
    r0 = jnp.sum(jnp.where(sel0, excl, 0.0), axis=-1, keepdims=True)
    r1 = jnp.sum(jnp.where(sel1, excl, 0.0), axis=-1, keepdims=True)
    rank_ref[...] = jnp.where(lane == PLAN_E0, r0, 0.0) + jnp.where(lane == PLAN_E1, r1, 0.0)
    cnt_ref[...] = jnp.sum(sel.astype(F32), axis=0, keepdims=True)


def _plan(plan2, tp):
    t = plan2.shape[0]
    return pl.pallas_call(
        _plan_kernel,
        grid=(t // tp,),
        in_specs=[pl.BlockSpec((tp, LANES), lambda i: (i, 0))],
        out_specs=[pl.BlockSpec((tp, LANES), lambda i: (i, 0)), pl.BlockSpec((None, 1, LANES), lambda i: (i, 0, 0))],
        out_shape=[jax.ShapeDtypeStruct((t, LANES), F32), jax.ShapeDtypeStruct((t // tp, 1, LANES), F32)],
        compiler_params=_cparams("arbitrary"),
        name="moe_plan",
    )(plan2)


def _local_slots(plan, rank, off_row):
    lane = _iota(plan.shape, 1).astype(F32)
    slots = []
    for k in (PLAN_E0, PLAN_E1):
        off = jnp.sum(jnp.where(lane == plan[:, k:k + 1], off_row, 0.0), axis=-1, keepdims=True)
        slots.append(off + rank[:, k:k + 1])
    return slots


def _run_blocks(p8_ref, loff_ref, base_ref, act):
    for e in range(MOE_EXPERTS):
        n = p8_ref[0, e]
        lo = loff_ref[0, e]
        go = base_ref[0, e]
        for size in RUN_SIZES:
            if size > LOCAL_ROWS:
                continue

            @pl.when(jnp.bitwise_and(n, size) != 0)
            def _(size=size):
                done = jnp.bitwise_and(n, ~(2 * size - 1))
                act(pl.multiple_of(lo + done, RUN_ALIGN), pl.multiple_of(go + done, RUN_ALIGN), size)


def _dispatch_kernel(tv_ref, p8_ref, loff_ref, base_ref, plan_ref, rank_ref, off_ref, h2_ref, xs_ref,
                     buf_ref, zero_ref, sem, zsem):
    @pl.when(pl.program_id(0) == 0)
    def _():
        zero_ref[...] = jnp.zeros_like(zero_ref)

        def zero_copy(i):
            rows = pl.ds(pl.multiple_of(i * FFN_ROWS, FFN_ROWS), FFN_ROWS)
            return pltpu.make_async_copy(zero_ref, xs_ref.at[rows], zsem)

        def zero_tiles(act):
            def body(i, carry):
                @pl.when(tv_ref[i] < FFN_ROWS)
                def _():
                    act(zero_copy(i))
                return carry
            lax.fori_loop(0, xs_ref.shape[0] // FFN_ROWS, body, 0)

        zero_tiles(lambda cp: cp.start())
        zero_tiles(lambda cp: cp.wait())

    s0, s1 = _local_slots(plan_ref[...], rank_ref[...], off_ref[...])
    row = _iota((1, LOCAL_ROWS), 1).astype(F32)
    onehot_t = ((row == s0) | (row == s1)).astype(BF16)
    buf_ref[...] = lax.dot_general(onehot_t, h2_ref[...], (((0,), (0,)), ((), ())), preferred_element_type=F32)

    def copy(lo, go, size):
        return pltpu.make_async_copy(buf_ref.at[pl.ds(lo, size)], xs_ref.at[pl.ds(go, size)], sem)

    _run_blocks(p8_ref, loff_ref, base_ref, lambda lo, go, size: copy(lo, go, size).start())
    _run_blocks(p8_ref, loff_ref, base_ref, lambda lo, go, size: copy(lo, go, size).wait())


def _tile_scalars():
    return pl.BlockSpec((None, 1, LANES), lambda i, *_: (i, 0, 0), memory_space=pltpu.SMEM)


def _dispatch(tile_valid, p8, loff, base, plan2, rank, off_f, h2, rows):
    t, d = h2.shape
    tt = TOK_TILE
    tok = lambda w: pl.BlockSpec((tt, w), lambda i, tv: (i, 0))
    return pl.pallas_call(
        _dispatch_kernel,
        grid_spec=pltpu.PrefetchScalarGridSpec(
            num_scalar_prefetch=1,
            grid=(t // tt,),
            in_specs=[_tile_scalars(), _tile_scalars(), _tile_scalars(),
                      tok(LANES), tok(LANES),
                      pl.BlockSpec((None, 1, LANES), lambda i, tv: (i, 0, 0)),
                      tok(d)],
            out_specs=pl.BlockSpec(memory_space=pl.ANY),
            scratch_shapes=[pltpu.VMEM((LOCAL_ROWS, d), F32), pltpu.VMEM((FFN_ROWS, d), F32),
                            pltpu.SemaphoreType.DMA(()), pltpu.SemaphoreType.DMA(())]),
        out_shape=jax.ShapeDtypeStruct((rows, d), F32),
        compiler_params=_cparams("arbitrary"),
        name="moe_dispatch",
    )(tile_valid, p8, loff, base, plan2, rank, off_f, h2)


def _ffn_kernel(te_ref, tv_ref, xs_ref, wg_ref, wu_ref, wd_ref, o_ref):
    nv = tv_ref[pl.program_id(0)]

    @pl.when(nv > 0)
    def _():
        x = xs_ref[...].astype(BF16)
        gate = jnp.dot(x, wg_ref[...], preferred_element_type=F32)
        up = jnp.dot(x, wu_ref[...], preferred_element_type=F32)
        o_ref[...] = _dot(_silu(gate) * up, wd_ref[...])

    @pl.when(nv <= 0)
    def _():
        o_ref[...] = jnp.zeros_like(o_ref)


def _ffn(tile_expert, tile_valid, xs, wg, wu, wd):
    rows, d = xs.shape
    wmap = lambda i, te, tv: (te[i], 0, 0)
    return pl.pallas_call(
        _ffn_kernel,
        grid_spec=pltpu.PrefetchScalarGridSpec(
            num_scalar_prefetch=2,
            grid=(rows // FFN_ROWS,),
            in_specs=[pl.BlockSpec((FFN_ROWS, d), lambda i, te, tv: (i, 0)),
                      pl.BlockSpec((None, d, MOE_FF), wmap),
                      pl.BlockSpec((None, d, MOE_FF), wmap),
                      pl.BlockSpec((None, MOE_FF, d), wmap)],
            out_specs=pl.BlockSpec((FFN_ROWS, d), lambda i, te, tv: (i, 0))),
        out_shape=jax.ShapeDtypeStruct((rows, d), F32),
        compiler_params=_cparams("arbitrary"),
        name="moe_ffn",
    )(tile_expert, tile_valid, xs, wg, wu, wd)


def _combine_kernel(p8_ref, loff_ref, base_ref, ys_ref, plan_ref, rank_ref, off_ref, x1_ref, gt2_ref, fsc_ref,
                    fsh_ref, fnw_ref, y_ref, buf_ref, sem):
    def copy(lo, go, size):
        return pltpu.make_async_copy(ys_ref.at[pl.ds(go, size)], buf_ref.at[pl.ds(lo, size)], sem)

    _run_blocks(p8_ref, loff_ref, base_ref, lambda lo, go, size: copy(lo, go, size).start())
    plan = plan_ref[...]
    s0, s1 = _local_slots(plan, rank_ref[...], off_ref[...])
    row = _iota((1, LOCAL_ROWS), 1).astype(F32)
    weights = (jnp.where(row == s0, plan[:, PLAN_C0:PLAN_C0 + 1], 0.0)
               + jnp.where(row == s1, plan[:, PLAN_C1:PLAN_C1 + 1], 0.0))
    _run_blocks(p8_ref, loff_ref, base_ref, lambda lo, go, size: copy(lo, go, size).wait())
    total = loff_ref[0, MOE_EXPERTS - 1] + p8_ref[0, MOE_EXPERTS - 1]
    filled = _iota((LOCAL_ROWS, 1), 0) < total
    moe = _dot(weights, jnp.where(filled, buf_ref[...], 0.0))
    x2 = x1_ref[...] + gt2_ref[...] * moe
    y_ref[...] = _rms(x2, fnw_ref[...]) * (1.0 + fsc_ref[...]) + fsh_ref[...]


def _combine(p8, loff, base, ys, plan, rank3, off_f, x1, gt2, fsc, fsh, fnw):
    bx, lx, d = x1.shape
    tt = TOK_TILE
    nl = lx // tt
    tile = lambda b, i: b * nl + i
    scal = pl.BlockSpec((None, 1, LANES), lambda b, i: (tile(b, i), 0, 0), memory_space=pltpu.SMEM)
    tokspec = lambda w: pl.BlockSpec((None, tt, w), lambda b, i: (b, i, 0))
    return pl.pallas_call(
        _combine_kernel,
        grid=(bx, nl),
        in_specs=[scal, scal, scal,
                  pl.BlockSpec(memory_space=pl.ANY),
                  tokspec(LANES), tokspec(LANES),
                  pl.BlockSpec((None, 1, LANES), lambda b, i: (tile(b, i), 0, 0)),
                  tokspec(d), _mod_spec(gt2, tt), _mod_spec(fsc, tt), _mod_spec(fsh, tt),
                  pl.BlockSpec((1, d), lambda b, i: (0, 0))],
        out_specs=tokspec(d),
        out_shape=jax.ShapeDtypeStruct((bx, lx, d), F32),
        scratch_shapes=[pltpu.VMEM((LOCAL_ROWS, d), F32), pltpu.SemaphoreType.DMA(())],
        compiler_params=_cparams("arbitrary", "arbitrary"),
        name="moe_combine_final",
    )(p8, loff, base, ys, plan, rank3, off_f, x1, gt2, fsc, fsh, fnw.reshape(1, d))


def _moe(h2, plan, wg, wu, wd, x1, gt2, fsc, fsh, fnw):
    bx, lx, d = x1.shape
    t = bx * lx
    assert lx % TOK_TILE == 0
    plan2 = plan.reshape(t, LANES)
    rank, cnt = _plan(plan2, TOK_TILE)
    n = cnt[:, 0, :MOE_EXPERTS].astype(jnp.int32)
    p8 = (n + RUN_ALIGN - 1) // RUN_ALIGN * RUN_ALIGN
    loff = jnp.cumsum(p8, axis=1) - p8
    erows = jnp.sum(p8, axis=0)
    epad = (erows + FFN_ROWS - 1) // FFN_ROWS * FFN_ROWS
    ends = jnp.cumsum(epad)
    starts = ends - epad
    base = starts[None, :] + jnp.cumsum(p8, axis=0) - p8
    lanes = lambda a: jnp.pad(a, ((0, 0), (0, LANES - MOE_EXPERTS)))[:, None, :]
    rows = (-(-(2 * t + (t // TOK_TILE) * MOE_EXPERTS * (RUN_ALIGN - 1)) // FFN_ROWS) + MOE_EXPERTS) * FFN_ROWS
    tile_start = jnp.arange(rows // FFN_ROWS, dtype=jnp.int32) * FFN_ROWS
    tile_expert = jnp.minimum(jnp.sum(tile_start[:, None] >= ends[None, :], axis=1), MOE_EXPERTS - 1).astype(jnp.int32)
    hot = tile_expert[:, None] == jnp.arange(MOE_EXPERTS)[None, :]
    tile_valid = jnp.clip(jnp.sum(jnp.where(hot, erows - (tile_start[:, None] - starts), 0), axis=1),
                          0, FFN_ROWS).astype(jnp.int32)
    p8l, loffl, basel = lanes(p8), lanes(loff), lanes(base)
    off_f = loffl.astype(F32)
    xs = _dispatch(tile_valid, p8l, loffl, basel, plan2, rank, off_f, h2.reshape(t, d), rows)
    ys = _ffn(tile_expert, tile_valid, xs, wg, wu, wd)
    return _combine(p8l, loffl, basel, ys, plan, rank.reshape(bx, lx, LANES), off_f, x1, gt2, fsc, fsh, fnw)


def _prep_layer(lp):
    w_in = lp["w_in"]
    offs = [0]
    for s in (DN_CONV_DIM, DN_VAL, DN_HEADS, DN_HEADS, SSM_CONV_DIM, SSM_INNER, SSM_HEADS, D_MODEL, D_MODEL):
        offs.append(offs[-1] + s)
    seg = lambda i: w_in[:, offs[i]:offs[i + 1]]
    small = jnp.concatenate([seg(2), seg(3), seg(6)], axis=1)
    pad = jnp.zeros((D_MODEL, PROJ_N - COL_SMALL - small.shape[1]), F32)
    w_cat = jnp.concatenate([seg(0), seg(4), seg(5), seg(7), seg(8), seg(1), small, pad], axis=1).astype(BF16)
    wr = jnp.concatenate([lp["w_group_router"], lp["w_expert_router"],
                          jnp.zeros((D_MODEL, LANES - MOE_GROUPS - MOE_EXPERTS), F32)], axis=1)
    br = jnp.concatenate([lp["b_group_router"], lp["b_expert_router"],
                          jnp.zeros((LANES - MOE_GROUPS - MOE_EXPERTS,), F32)]).reshape(1, LANES)
    return dict(lp, w_cat=w_cat, wr=wr, br=br,
                wdn=lp["w_dn_out"].astype(BF16), wssm=lp["w_ssm_out"].astype(BF16), wout=lp["w_out"].astype(BF16),
                wg=lp["w_exp_gate"].astype(BF16), wu=lp["w_exp_up"].astype(BF16), wd=lp["w_exp_down"].astype(BF16))


def _trunk(x3, mods, fins, states, layers, final_norm_w, cfg):
    bx, lx, d = x3.shape
    tm = cfg["tm"]
    scan = dict(bt=cfg["bt"], r=cfg["r"], c=cfg["c"], lc=cfg["lc"], ng=cfg["ng"], cpi=cfg["cpi"])
    new_states = []
    n_layers = len(layers)
    for l, lp in enumerate(layers):
        sh1, sc1, gt1, sh2, sc2, gt2 = mods[l]
        dn_conv, dn_rec, ssm_conv, ssm_rec = states[l]
        proj = _inproj(x3, sc1, sh1, lp["norm_mix_w"], lp["w_cat"], tm).reshape(bx * lx, PROJ_N)
        og, dn_conv_new, dn_rec_new = _dn_branch(proj, lp["dn_conv_w"], lp["dn_A_log"], lp["dn_dt_bias"],
                                                 lp["dn_norm_w"], dn_conv, dn_rec, **scan)
        ys, ssm_conv_new, ssm_rec_new = _ssd_branch(proj, lp["ssm_conv_w"], lp["ssm_conv_b"], lp["ssm_A_log"],
                                                    lp["ssm_dt_bias"], lp["ssm_D"], lp["ssm_norm_w"],
                                                    ssm_conv, ssm_rec, **scan)
        x1, h2, plan = _merge(x3, og, ys, proj, gt1, sc2, sh2, lp["norm_ffn_w"], lp["wdn"], lp["wssm"], lp["wout"],
                              lp["wr"], lp["br"], cfg["tm_merge"])
        if l == n_layers - 1:
            fsh, fsc, fnw = fins[0], fins[1], final_norm_w
            x3 = _moe(h2, plan, lp["wg"], lp["wu"], lp["wd"], x1, gt2, fsc, fsh, fnw)
        else:
            raise NotImplementedError("only the last layer fuses the final norm; depth is 1 here")
        new_states.append((dn_conv_new, dn_rec_new, ssm_conv_new, ssm_rec_new))
    return x3, new_states


def _per_seq(m):
    return m[:, None, :]


def kernel(x_prompt, x_sample, c_prompt, c_sample, state_dn_conv, state_dn_rec, state_ssm_conv, state_ssm_rec, w_ada, b_ada, norm_mix_w, w_in, dn_conv_w, dn_A_log, dn_dt_bias, dn_norm_w, w_dn_out, ssm_conv_w, ssm_conv_b, ssm_A_log, ssm_dt_bias, ssm_D, ssm_norm_w, w_ssm_out, w_out, norm_ffn_w, w_group_router, b_group_router, w_expert_router, b_expert_router, w_exp_gate, w_exp_up, w_exp_down, w_ada_final, b_ada_final, final_norm_w):
    depth = w_ada.shape[0]
    assert depth == 1
    per_layer = dict(w_ada=w_ada, b_ada=b_ada, norm_mix_w=norm_mix_w, w_in=w_in, dn_conv_w=dn_conv_w,
                     dn_A_log=dn_A_log, dn_dt_bias=dn_dt_bias, dn_norm_w=dn_norm_w, w_dn_out=w_dn_out,
                     ssm_conv_w=ssm_conv_w, ssm_conv_b=ssm_conv_b, ssm_A_log=ssm_A_log, ssm_dt_bias=ssm_dt_bias,
                     ssm_D=ssm_D, ssm_norm_w=ssm_norm_w, w_ssm_out=w_ssm_out, w_out=w_out, norm_ffn_w=norm_ffn_w,
                     w_group_router=w_group_router, b_group_router=b_group_router,
                     w_expert_router=w_expert_router, b_expert_router=b_expert_router,
                     w_exp_gate=w_exp_gate, w_exp_up=w_exp_up, w_exp_down=w_exp_down)
    layers = [_prep_layer({k: v[l] for k, v in per_layer.items()}) for l in range(depth)]

    nbp, lp_, d = x_prompt.shape
    nbs, ls, _ = x_sample.shape
    c_all = jnp.concatenate([c_prompt, c_sample], axis=0)
    mod_all = [_ada(c_all, lyr["w_ada"], lyr["b_ada"]) for lyr in layers]
    fin_all = _ada(c_all, w_ada_final, b_ada_final)

    mods_p = [[_per_seq(m) for m in jnp.split(ma[:nbp], 6, axis=-1)] for ma in mod_all]
    fins_p = [_per_seq(m) for m in jnp.split(fin_all[:nbp], 2, axis=-1)]
    zeros_p = [(jnp.zeros((nbp, CONV_K - 1, DN_CONV_DIM), F32), jnp.zeros((nbp, DN_HEADS, DN_DK, DN_DV), F32),
                jnp.zeros((nbp, CONV_K - 1, SSM_CONV_DIM), F32),
                jnp.zeros((nbp, SSM_HEADS, SSM_HEADDIM, SSM_STATE), F32)) for _ in range(depth)]
    c_p = min(SCAN_CHUNK, lp_)
    r_p = min(lp_, 8 * c_p)
    cfg_p = dict(tm=min(lp_, 1024), tm_merge=min(lp_, 512), tm_moe=min(lp_, 1024), bt=1, r=r_p, c=c_p, lc=c_p, ng=lp_ // r_p,
                 cpi=4 if (r_p // c_p) % 4 == 0 else 1)
    y_p, st_p = _trunk(x_prompt, mods_p, fins_p, zeros_p, layers, final_norm_w, cfg_p)

    lpad = -(-ls // SUBLANES) * SUBLANES
    bt_s = 8
    xs = jnp.pad(x_sample, ((0, 0), (0, lpad - ls), (0, 0))).reshape(1, nbs * lpad, d)
    per_tok = lambda m: jnp.repeat(m, lpad, axis=0)[None]
    mods_s = [[per_tok(m) for m in jnp.split(ma[nbp:], 6, axis=-1)] for ma in mod_all]
    fins_s = [per_tok(m) for m in jnp.split(fin_all[nbp:], 2, axis=-1)]
    st_in = [(state_dn_conv[l], state_dn_rec[l], state_ssm_conv[l], state_ssm_rec[l]) for l in range(depth)]
    ts = nbs * lpad
    cfg_s = dict(tm=ts, tm_merge=min(ts, 512), tm_moe=min(ts, 512), bt=bt_s, r=lpad, c=lpad, lc=ls, ng=1, cpi=1)
    y_s, st_s = _trunk(xs, mods_s, fins_s, st_in, layers, final_norm_w, cfg_s)
    y_s = y_s.reshape(nbs, lpad, d)[:, :ls]

    stack = lambda sts, i: jnp.stack([s[i] for s in sts])
    return (y_p, y_s, stack(st_p, 0), stack(st_p, 1), stack(st_p, 2), stack(st_p, 3),
            stack(st_s, 0), stack(st_s, 1), stack(st_s, 2), stack(st_s, 3))
```

```python
import functools
import math

import jax
import jax.numpy as jnp
from jax import lax
from jax.experimental import pallas as pl
from jax.experimental.pallas import tpu as pltpu

F32 = jnp.float32
BF16 = jnp.bfloat16

D_MODEL = 1024
DN_HEADS = 4
DN_DK = 128
DN_DV = 128
DN_KEY = DN_HEADS * DN_DK
DN_VAL = DN_HEADS * DN_DV
CONV_K = 4
DN_CONV_DIM = 2 * DN_KEY + DN_VAL
SSM_INNER = D_MODEL
SSM_HEADDIM = 64
SSM_HEADS = SSM_INNER // SSM_HEADDIM
SSM_GROUPS = 2
SSM_STATE = 128
SSM_CONV_DIM = SSM_INNER + 2 * SSM_GROUPS * SSM_STATE
MOE_GROUPS = 4
MOE_PER_GROUP = 8
MOE_EXPERTS = MOE_GROUPS * MOE_PER_GROUP
MOE_FF = D_MODEL // 4
EPS = 1e-6
SCAN_CHUNK = 64

LANES = 128
SUBLANES = 8
CARRY_ROW = SUBLANES - (CONV_K - 1)

COL_DN_QKV = 0
COL_SSM_XBC = COL_DN_QKV + DN_CONV_DIM
COL_SSM_Z = COL_SSM_XBC + SSM_CONV_DIM
COL_GATES = COL_SSM_Z + SSM_INNER
COL_DN_Z = COL_GATES + 2 * D_MODEL
COL_SMALL = COL_DN_Z + DN_VAL
PROJ_TN = 1024
PROJ_N = 7 * PROJ_TN
SM_A, SM_B, SM_DT = 0, DN_HEADS, 2 * DN_HEADS
RT_GROUP, RT_EXPERT = 0, MOE_GROUPS

VMEM_LIMIT = 56 * 1024 * 1024


def _cparams(*sem):
    return pltpu.CompilerParams(dimension_semantics=sem, vmem_limit_bytes=VMEM_LIMIT)


def _dot(a, b):
    return jnp.dot(a.astype(BF16), b.astype(BF16), preferred_element_type=F32)


def _dot_nt(a, b):
    return lax.dot_general(a.astype(BF16), b.astype(BF16), (((1,), (1,)), ((), ())),
                           preferred_element_type=F32)


def _dot_tn(a, b):
    return lax.dot_general(a.astype(BF16), b.astype(BF16), (((0,), (0,)), ((), ())),
                           preferred_element_type=F32)


def _split3(x):
    hi = x.astype(BF16)
    r = x - hi.astype(F32)
    mid = r.astype(BF16)
    lo = (r - mid.astype(F32)).astype(BF16)
    return hi, mid, lo


def _dot_exact_lhs(a_bf, b):
    hi, mid, lo = _split3(b)
    d = functools.partial(jnp.dot, preferred_element_type=F32)
    return d(a_bf, hi) + (d(a_bf, mid) + d(a_bf, lo))


def _dot_x3(a, b):
    a_hi = a.astype(BF16)
    a_lo = (a - a_hi.astype(F32)).astype(BF16)
    b_hi = b.astype(BF16)
    b_lo = (b - b_hi.astype(F32)).astype(BF16)
    d = functools.partial(jnp.dot, preferred_element_type=F32)
    return d(a_hi, b_hi) + (d(a_hi, b_lo) + d(a_lo, b_hi))


def _sigmoid(x):
    return 1.0 / (1.0 + jnp.exp(-x))


def _silu(x):
    return x * _sigmoid(x)


def _softplus(x):
    return jnp.maximum(x, 0.0) + jnp.log1p(jnp.exp(-jnp.abs(x)))


def _rms(x, w):
    return x * lax.rsqrt(jnp.mean(x * x, axis=-1, keepdims=True) + EPS) * w


def _iota(shape, dim):
    return lax.broadcasted_iota(jnp.int32, shape, dim)


def _ada_kernel(c_ref, w_ref, b_ref, o_ref):
    o_ref[...] = _dot(_silu(c_ref[...]), w_ref[...]) + b_ref[...]


def _ada(c, w, b, tn=512):
    m, d = c.shape
    n = w.shape[1]
    return pl.pallas_call(
        _ada_kernel,
        grid=(n // tn,),
        in_specs=[pl.BlockSpec((m, d), lambda j: (0, 0)),
                  pl.BlockSpec((d, tn), lambda j: (0, j)),
                  pl.BlockSpec((1, tn), lambda j: (0, j))],
        out_specs=pl.BlockSpec((m, tn), lambda j: (0, j)),
        out_shape=jax.ShapeDtypeStruct((m, n), F32),
        compiler_params=_cparams("arbitrary"),
        name="ada_mod",
    )(c, w, b.reshape(1, n))


def _mod_spec(mod, tm):
    if mod.shape[1] == 1:
        return pl.BlockSpec((None, 1, D_MODEL), lambda b, i, *_: (b, 0, 0))
    return pl.BlockSpec((None, tm, D_MODEL), lambda b, i, *_: (b, i, 0))


def _inproj_kernel(x_ref, sc_ref, sh_ref, nw_ref, w_ref, o_ref, h_ref, *, tm, sub):
    @pl.when(pl.program_id(2) == 0)
    def _():
        per_token = sc_ref.shape[0] != 1

        def body(r, carry):
            rows = pl.ds(pl.multiple_of(r * sub, sub), sub)
            sc = sc_ref[rows, :] if per_token else sc_ref[...]
            sh = sh_ref[rows, :] if per_token else sh_ref[...]
            h = _rms(x_ref[rows, :], nw_ref[...]) * (1.0 + sc) + sh
            h_ref[rows, :] = h.astype(BF16)
            return carry

        lax.fori_loop(0, tm // sub, body, 0)

    o_ref[...] = jnp.dot(h_ref[...], w_ref[...], preferred_element_type=F32)


def _inproj(x3, sc, sh, nw, w_cat, tm):
    bx, lx, d = x3.shape
    n = w_cat.shape[1]
    kern = functools.partial(_inproj_kernel, tm=tm, sub=min(tm, 256))
    return pl.pallas_call(
        kern,
        grid=(bx, lx // tm, n // PROJ_TN),
        in_specs=[pl.BlockSpec((None, tm, d), lambda b, i, j: (b, i, 0)),
                  _mod_spec(sc, tm), _mod_spec(sh, tm),
                  pl.BlockSpec((1, d), lambda b, i, j: (0, 0)),
                  pl.BlockSpec((d, PROJ_TN), lambda b, i, j: (0, j))],
        out_specs=pl.BlockSpec((None, tm, PROJ_TN), lambda b, i, j: (b, i, j)),
        out_shape=jax.ShapeDtypeStruct((bx, lx, n), F32),
        scratch_shapes=[pltpu.VMEM((tm, d), BF16)],
        compiler_params=_cparams("arbitrary", "arbitrary", "arbitrary"),
        name="norm_inproj",
    )(x3, sc, sh, nw.reshape(1, d), w_cat)


def _conv_silu(u, ext_ref, bb, cw_ref, c, lc, bias=None):
    ext_ref[bb, SUBLANES:SUBLANES + c, :] = u
    y = ext_ref[bb, CARRY_ROW:CARRY_ROW + c, :] * cw_ref[0:1, :]
    for i in range(1, CONV_K):
        y = y + ext_ref[bb, CARRY_ROW + i:CARRY_ROW + i + c, :] * cw_ref[i:i + 1, :]
    ext_ref[bb, CARRY_ROW:SUBLANES, :] = ext_ref[bb, CARRY_ROW + lc:SUBLANES + lc, :]
    if bias is not None:
        y = y + bias
    return _silu(y)


def _run_rounds(process, bt, r, c, cpi):
    g = r // c
    if g == 1:
        process([[(bb, 0) for bb in range(bt)]])
    else:
        def body(ci, carry):
            process([[(bb, pl.multiple_of((ci * cpi + t) * c, c)) for bb in range(bt)] for t in range(cpi)])
            return carry
        lax.fori_loop(0, g // cpi, body, 0)


INV_BASE = SUBLANES


def _inverse_masks(row, col, c):
    sh = lambda x, s: jnp.right_shift(x, int(math.log2(s)))
    diag = sh(row, INV_BASE) == sh(col, INV_BASE)
    merges = []
    s = INV_BASE
    while s < c:
        merges.append((sh(row, 2 * s) == sh(col, 2 * s))
                      & (jnp.bitwise_and(sh(row, s), 1) == 1) & (jnp.bitwise_and(sh(col, s), 1) == 0))
        s *= 2
    return diag, merges


def _unit_lower_inverses(lmats, eye_f, masks):
    diag, merges = masks
    npows = [-jnp.where(diag, l, 0.0) for l in lmats]
    ps = [eye_f + n for n in npows]
    for _ in range(int(math.log2(INV_BASE)) - 1):
        npows = [_dot(n, n) for n in npows]
        ps = [p + _dot(p, n) for p, n in zip(ps, npows)]
    for m in merges:
        ts = [_dot(p, jnp.where(m, l, 0.0)) for p, l in zip(ps, lmats)]
        ps = [p - _dot(t, p) for t, p in zip(ts, ps)]
    return ps


def _dn_kernel(qkv_ref, z_ref, sm_ref, cw_ref, alog_ref, bias_ref, nw_ref, cin_ref, sin_ref,
               o_ref, cout_ref, sout_ref, ext_ref, *, bt, r, c, lc, ng, cpi):
    j = pl.program_id(1)

    @pl.when(j == 0)
    def _():
        ext_ref[:, CARRY_ROW:SUBLANES, :] = cin_ref[...]
        sout_ref[...] = sin_ref[...]

    row = _iota((c, c), 0)
    col = _iota((c, c), 1)
    tri_incl = row >= col
    tri_strict = row > col
    eye_f = (row == col).astype(F32)
    tril_bf = tri_incl.astype(BF16)
    valid = _iota((c, 1), 0) < lc
    nexp_a = -jnp.exp(alog_ref[...])
    bias = bias_ref[...]
    inv_masks = _inverse_masks(row, col, c)
    heads = range(DN_HEADS)

    def prep(bb, r0):
        rows = pl.ds(bb * r + r0, c)
        y = _conv_silu(qkv_ref[rows, :], ext_ref, bb, cw_ref, c, lc)
        sm = sm_ref[rows, :]
        g_all = nexp_a * _softplus(sm + bias)
        beta_all = _sigmoid(sm)
        if lc < c:
            g_all = jnp.where(valid, g_all, 0.0)
            beta_all = jnp.where(valid, beta_all, 0.0)
        cs_all = _dot_exact_lhs(tril_bf, g_all)
        cs_t = cs_all.T
        ecs_all = jnp.exp(cs_all)
        per_head = []
        for h in heads:
            q = y[:, h * DN_DK:(h + 1) * DN_DK]
            k = y[:, DN_KEY + h * DN_DK:DN_KEY + (h + 1) * DN_DK]
            v = y[:, 2 * DN_KEY + h * DN_DV:2 * DN_KEY + (h + 1) * DN_DV]
            q = q * lax.rsqrt(jnp.sum(q * q, axis=-1, keepdims=True) + EPS) * (DN_DK ** -0.5)
            k = k * lax.rsqrt(jnp.sum(k * k, axis=-1, keepdims=True) + EPS)
            if lc < c:
                k = jnp.where(valid, k, 0.0)
            cs = cs_all[:, SM_A + h:SM_A + h + 1]
            ecs = ecs_all[:, SM_A + h:SM_A + h + 1]
            beta = beta_all[:, SM_B + h:SM_B + h + 1]
            last = cs_all[c - 1:c, SM_A + h:SM_A + h + 1]
            decay = jnp.exp(jnp.where(tri_incl, cs - cs_t[SM_A + h:SM_A + h + 1, :], -jnp.inf))
            per_head.append(dict(q_ecs=(q * ecs).astype(BF16), q=q.astype(BF16), k=k.astype(BF16), beta=beta,
                                 decay=decay, elast=jnp.exp(last),
                                 k_tail=(k * jnp.exp(last - cs)).astype(BF16),
                                 rhs=jnp.concatenate([v * beta, k * (beta * ecs)], axis=1).astype(BF16)))
        return per_head

    def process(rounds):
        items = [it for rnd in rounds for it in rnd]
        pre = [prep(bb, r0) for bb, r0 in items]
        chains = [(i, h) for i in range(len(items)) for h in heads]
        a = {ch: pre[ch[0]][ch[1]] for ch in chains}
        kk = {ch: _dot_nt(a[ch]["k"], a[ch]["k"]) for ch in chains}
        qk = {ch: _dot_nt(a[ch]["q"], a[ch]["k"]) * a[ch]["decay"] for ch in chains}
        lmats = [jnp.where(tri_strict, a[ch]["beta"] * kk[ch] * a[ch]["decay"], 0.0) for ch in chains]
        pinv = _unit_lower_inverses(lmats, eye_f, inv_masks)
        sol = {ch: _dot(p, a[ch]["rhs"]) for ch, p in zip(chains, pinv)}
        first = 0
        for rnd in rounds:
            idx = range(first, first + len(rnd))
            first += len(rnd)
            rch = [(i, h) for i in idx for h in heads]
            s = {ch: sout_ref[items[ch[0]][0], ch[1]] for ch in rch}
            ws_qs = {ch: _dot(jnp.concatenate([sol[ch][:, DN_DV:].astype(BF16), a[ch]["q_ecs"]], axis=0), s[ch])
                     for ch in rch}
            v_new = {ch: (sol[ch][:, :DN_DV] - ws_qs[ch][:c]).astype(BF16) for ch in rch}
            o = {ch: ws_qs[ch][c:] + _dot(qk[ch], v_new[ch]) for ch in rch}
            for ch in rch:
                sout_ref[items[ch[0]][0], ch[1]] = a[ch]["elast"] * s[ch] + _dot_tn(a[ch]["k_tail"], v_new[ch])
            for i in idx:
                bb, r0 = items[i]
                rows = pl.ds(bb * r + r0, c)
                o_ref[rows, :] = jnp.concatenate(
                    [_rms(o[(i, h)], nw_ref[...]) * _silu(z_ref[rows, h * DN_DV:(h + 1) * DN_DV])
                     for h in heads], axis=1)

    _run_rounds(process, bt, r, c, cpi)

    @pl.when(j == ng - 1)
    def _():
        cout_ref[...] = ext_ref[:, CARRY_ROW:SUBLANES, :]


def _small_row(vals, offset, fill=0.0):
    row = jnp.full((1, LANES), fill, F32)
    return lax.dynamic_update_slice(row, vals.astype(F32).reshape(1, -1), (0, offset))


def _dn_branch(proj, conv_w, a_log, dt_bias, norm_w, conv_in, rec_in, *, bt, r, c, lc, ng, cpi):
    nb = conv_in.shape[0] // bt
    rows = bt * r
    t = proj.shape[0]
    kern = functools.partial(_dn_kernel, bt=bt, r=r, c=c, lc=lc, ng=ng, cpi=cpi)
    rowmap = lambda col: (lambda i, j: (i * ng + j, col))
    const = lambda i, j: (0, 0)
    return pl.pallas_call(
        kern,
        grid=(nb, ng),
        in_specs=[pl.BlockSpec((rows, DN_CONV_DIM), rowmap(COL_DN_QKV // DN_CONV_DIM)),
                  pl.BlockSpec((rows, DN_VAL), rowmap(COL_DN_Z // DN_VAL)),
                  pl.BlockSpec((rows, LANES), rowmap(COL_SMALL // LANES)),
                  pl.BlockSpec((CONV_K, DN_CONV_DIM), const),
                  pl.BlockSpec((1, LANES), const),
                  pl.BlockSpec((1, LANES), const),
                  pl.BlockSpec((1, DN_DV), const),
                  pl.BlockSpec((bt, CONV_K - 1, DN_CONV_DIM), lambda i, j: (i, 0, 0)),
                  pl.BlockSpec((bt, DN_HEADS, DN_DK, DN_DV), lambda i, j: (i, 0, 0, 0))],
        out_specs=[pl.BlockSpec((rows, DN_VAL), lambda i, j: (i * ng + j, 0)),
                   pl.BlockSpec((bt, CONV_K - 1, DN_CONV_DIM), lambda i, j: (i, 0, 0)),
                   pl.BlockSpec((bt, DN_HEADS, DN_DK, DN_DV), lambda i, j: (i, 0, 0, 0))],
        out_shape=[jax.ShapeDtypeStruct((t, DN_VAL), F32),
                   jax.ShapeDtypeStruct(conv_in.shape, F32),
                   jax.ShapeDtypeStruct(rec_in.shape, F32)],
        scratch_shapes=[pltpu.VMEM((bt, c + SUBLANES, DN_CONV_DIM), F32)],
        compiler_params=_cparams("arbitrary", "arbitrary"),
        name="gated_delta",
    )(proj, proj, proj, conv_w, _small_row(a_log, SM_A), _small_row(dt_bias, SM_A),
      norm_w.reshape(1, DN_DV), conv_in, rec_in)


def _ssd_kernel(xbc_ref, z_ref, sm_ref, cw_ref, cb_ref, alog_ref, bias_ref, dvec_ref, nw_ref, cin_ref,
                hin_ref, y_ref, cout_ref, hout_ref, ext_ref, *, bt, r, c, lc, ng, cpi):
    j = pl.program_id(1)

    @pl.when(j == 0)
    def _():
        ext_ref[:, CARRY_ROW:SUBLANES, :] = cin_ref[...]
        hout_ref[...] = hin_ref[...]

    tril_bf = (_iota((c, c), 0) >= _iota((c, c), 1)).astype(BF16)
    row2 = _iota((c, 2 * c), 0)
    lane2 = _iota((c, 2 * c), 1)
    left2 = lane2 < c
    tri2 = row2 >= jnp.where(left2, lane2, lane2 - c)
    left_x = _iota((c, LANES), 1) < SSM_HEADDIM
    top_h = _iota((2 * SSM_HEADDIM, 1), 0) < SSM_HEADDIM
    valid = _iota((c, 1), 0) < lc
    nexp_a = -jnp.exp(alog_ref[...])
    bias = bias_ref[...]
    gw = SSM_INNER // SSM_GROUPS
    pairs_per_group = SSM_HEADS // SSM_GROUPS // 2

    pairs = range(SSM_HEADS // 2)
    pair_rows = lambda p: slice(p * LANES, (p + 1) * LANES)

    def prep(bb, r0):
        rows = pl.ds(bb * r + r0, c)
        y = _conv_silu(xbc_ref[rows, :], ext_ref, bb, cw_ref, c, lc, bias=cb_ref[...])
        dt_all = _softplus(sm_ref[rows, :] + bias)
        if lc < c:
            dt_all = jnp.where(valid, dt_all, 0.0)
        cs_all = _dot_exact_lhs(tril_bf, dt_all * nexp_a)
        cs_t2 = jnp.concatenate([cs_all, cs_all], axis=0).T
        last = cs_all[c - 1:c, :]
        bm = [y[:, SSM_INNER + g * SSM_STATE:SSM_INNER + (g + 1) * SSM_STATE].astype(BF16)
              for g in range(SSM_GROUPS)]
        cm = [y[:, SSM_INNER + (SSM_GROUPS + g) * SSM_STATE:SSM_INNER + (SSM_GROUPS + g + 1) * SSM_STATE].astype(BF16)
              for g in range(SSM_GROUPS)]
        return dict(rows=rows, y=y, dt=dt_all, cs=cs_all, cs_t2=cs_t2, ecs=jnp.exp(cs_all),
                    tail=jnp.exp(last - cs_all), elast=jnp.exp(last), bm=bm, cm=cm)

    def process(rounds):
        items = [it for rnd in rounds for it in rnd]
        pre = [prep(bb, r0) for bb, r0 in items]
        ids = range(len(items))
        cb2 = {(i, g): _dot_nt(pre[i]["cm"][g], jnp.concatenate([pre[i]["bm"][g]] * 2, axis=0))
               for i in ids for g in range(SSM_GROUPS)}
        y_intra, upd, xs_of = {}, {}, {}
        for i in ids:
            a = pre[i]
            for p in pairs:
                g = p // pairs_per_group
                la, lb = SM_DT + 2 * p, SM_DT + 2 * p + 1
                both = lambda v, mask=left_x: jnp.where(mask, v[:, la:la + 1], v[:, lb:lb + 1])
                xs = a["y"][:, pair_rows(p)]
                xdt = xs * both(a["dt"])
                diff = both(a["cs"], left2) - jnp.where(left2[0:1], a["cs_t2"][la:la + 1, :], a["cs_t2"][lb:lb + 1, :])
                decay2 = jnp.exp(jnp.where(tri2, diff, -jnp.inf))
                rhs = jnp.concatenate([jnp.where(left_x, xdt, 0.0), jnp.where(left_x, 0.0, xdt)], axis=0)
                y_intra[i, p] = _dot(cb2[i, g] * decay2, rhs)
                upd[i, p] = _dot_tn(xdt * both(a["tail"]), a["bm"][g])
                xs_of[i, p] = xs
        first = 0
        for rnd in rounds:
            idx = range(first, first + len(rnd))
            first += len(rnd)
            for i in idx:
                a = pre[i]
                bb = items[i][0]
                outs = []
                for p in pairs:
                    g = p // pairs_per_group
                    la, lb = SM_DT + 2 * p, SM_DT + 2 * p + 1
                    hs = hout_ref[bb, pair_rows(p), :]
                    y_inter = _dot_nt(a["cm"][g], hs) * jnp.where(left_x, a["ecs"][:, la:la + 1], a["ecs"][:, lb:lb + 1])
                    hout_ref[bb, pair_rows(p), :] = (
                        jnp.where(top_h, a["elast"][:, la:la + 1], a["elast"][:, lb:lb + 1]) * hs + upd[i, p])
                    outs.append(y_intra[i, p] + y_inter + dvec_ref[:, pair_rows(p)] * xs_of[i, p])
                yz = jnp.concatenate(outs, axis=1) * _silu(z_ref[a["rows"], :])
                y_ref[a["rows"], :] = jnp.concatenate(
                    [_rms(yz[:, g * gw:(g + 1) * gw], nw_ref[:, g * gw:(g + 1) * gw]) for g in range(SSM_GROUPS)],
                    axis=1)

    _run_rounds(process, bt, r, c, cpi)

    @pl.when(j == ng - 1)
    def _():
        cout_ref[...] = ext_ref[:, CARRY_ROW:SUBLANES, :]


def _ssd_branch(proj, conv_w, conv_b, a_log, dt_bias, d_skip, norm_w, conv_in, rec_in, *, bt, r, c, lc, ng, cpi):
    nb = conv_in.shape[0] // bt
    rows = bt * r
    t = proj.shape[0]
    hrows = SSM_HEADS * SSM_HEADDIM
    rec2 = rec_in.reshape(rec_in.shape[0], hrows, SSM_STATE)
    kern = functools.partial(_ssd_kernel, bt=bt, r=r, c=c, lc=lc, ng=ng, cpi=cpi)
    rowmap = lambda col: (lambda i, j: (i * ng + j, col))
    const = lambda i, j: (0, 0)
    y, conv_out, rec_out = pl.pallas_call(
        kern,
        grid=(nb, ng),
        in_specs=[pl.BlockSpec((rows, SSM_CONV_DIM), rowmap(COL_SSM_XBC // SSM_CONV_DIM)),
                  pl.BlockSpec((rows, SSM_INNER), rowmap(COL_SSM_Z // SSM_INNER)),
                  pl.BlockSpec((rows, LANES), rowmap(COL_SMALL // LANES)),
                  pl.BlockSpec((CONV_K, SSM_CONV_DIM), const),
                  pl.BlockSpec((1, SSM_CONV_DIM), const),
                  pl.BlockSpec((1, LANES), const),
                  pl.BlockSpec((1, LANES), const),
                  pl.BlockSpec((1, SSM_INNER), const),
                  pl.BlockSpec((1, SSM_INNER), const),
                  pl.BlockSpec((bt, CONV_K - 1, SSM_CONV_DIM), lambda i, j: (i, 0, 0)),
                  pl.BlockSpec((bt, hrows, SSM_STATE), lambda i, j: (i, 0, 0))],
        out_specs=[pl.BlockSpec((rows, SSM_INNER), lambda i, j: (i * ng + j, 0)),
                   pl.BlockSpec((bt, CONV_K - 1, SSM_CONV_DIM), lambda i, j: (i, 0, 0)),
                   pl.BlockSpec((bt, hrows, SSM_STATE), lambda i, j: (i, 0, 0))],
        out_shape=[jax.ShapeDtypeStruct((t, SSM_INNER), F32),
                   jax.ShapeDtypeStruct(conv_in.shape, F32),
                   jax.ShapeDtypeStruct(rec2.shape, F32)],
        scratch_shapes=[pltpu.VMEM((bt, c + SUBLANES, SSM_CONV_DIM), F32)],
        compiler_params=_cparams("arbitrary", "arbitrary"),
        name="ssd_scan",
    )(proj, proj, proj, conv_w, conv_b.reshape(1, -1), _small_row(a_log, SM_DT), _small_row(dt_bias, SM_DT),
      jnp.repeat(d_skip.astype(F32), SSM_HEADDIM).reshape(1, SSM_INNER), norm_w.reshape(1, SSM_INNER),
      conv_in, rec2)
    return y, conv_out, rec_out.reshape(rec_in.shape)


PLAN_E0, PLAN_E1, PLAN_C0, PLAN_C1 = 0, 1, 2, 3


def _route(logits):
    lane = _iota(logits.shape, 1).astype(F32)
    big = float(LANES)
    is_group = (lane >= RT_GROUP) & (lane < RT_GROUP + MOE_GROUPS)
    gl = jnp.where(is_group, logits, -jnp.inf)
    gmax = jnp.max(gl, axis=-1, keepdims=True)
    g_sel = jnp.min(jnp.where(gl == gmax, lane, big), axis=-1, keepdims=True) - RT_GROUP
    p_group = 1.0 / jnp.sum(jnp.exp(gl - gmax), axis=-1, keepdims=True)
    e_lo = RT_EXPERT + MOE_PER_GROUP * g_sel
    in_grp = (lane >= e_lo) & (lane < e_lo + MOE_PER_GROUP)
    el = jnp.where(in_grp, logits, -jnp.inf)
    ee = jnp.exp(el - jnp.max(el, axis=-1, keepdims=True))
    pe = jnp.where(in_grp, ee / jnp.sum(ee, axis=-1, keepdims=True), -1.0)
    p1 = jnp.max(pe, axis=-1, keepdims=True)
    i1 = jnp.min(jnp.where(pe == p1, lane, big), axis=-1, keepdims=True)
    pe2 = jnp.where(lane == i1, -1.0, pe)
    p2 = jnp.max(pe2, axis=-1, keepdims=True)
    i2 = jnp.min(jnp.where(pe2 == p2, lane, big), axis=-1, keepdims=True)
    tot = p1 + p2
    plan = jnp.where(lane == PLAN_E0, i1 - RT_EXPERT, 0.0) + jnp.where(lane == PLAN_E1, i2 - RT_EXPERT, 0.0)
    return plan + jnp.where(lane == PLAN_C0, p_group * p1 / tot, 0.0) + jnp.where(lane == PLAN_C1, p_group * p2 / tot, 0.0)


def _merge_kernel(x_ref, og_ref, ys_ref, gates_ref, gt1_ref, sc2_ref, sh2_ref, nw2_ref, wdn_ref, wssm_ref,
                  wout_ref, wr_ref, br_ref, x1_ref, h2_ref, plan_ref):
    y_dn = _dot(og_ref[...], wdn_ref[...])
    y_ssm = _dot(ys_ref[...], wssm_ref[...])
    merged = (_sigmoid(gates_ref[:, :D_MODEL]) * y_dn + _sigmoid(gates_ref[:, D_MODEL:]) * y_ssm)
    x1 = x_ref[...] + gt1_ref[...] * _dot(merged, wout_ref[...])
    x1_ref[...] = x1
    h2 = _rms(x1, nw2_ref[...]) * (1.0 + sc2_ref[...]) + sh2_ref[...]
    h2_ref[...] = h2.astype(BF16)
    plan_ref[...] = _route(_dot_x3(h2, wr_ref[...]) + br_ref[...])


def _merge(x3, og, ys, proj, gt1, sc2, sh2, nw2, wdn, wssm, wout, wr, br, tm):
    bx, lx, d = x3.shape
    nl = lx // tm
    rowmap = lambda col: (lambda b, i: (b * nl + i, col))
    const = lambda b, i: (0, 0)
    tok = lambda w, dt: jax.ShapeDtypeStruct((bx, lx, w), dt)
    return pl.pallas_call(
        _merge_kernel,
        grid=(bx, nl),
        in_specs=[pl.BlockSpec((None, tm, d), lambda b, i: (b, i, 0)),
                  pl.BlockSpec((tm, DN_VAL), rowmap(0)),
                  pl.BlockSpec((tm, SSM_INNER), rowmap(0)),
                  pl.BlockSpec((tm, 2 * d), rowmap(COL_GATES // (2 * d))),
                  _mod_spec(gt1, tm), _mod_spec(sc2, tm), _mod_spec(sh2, tm),
                  pl.BlockSpec((1, d), const),
                  pl.BlockSpec(wdn.shape, const), pl.BlockSpec(wssm.shape, const), pl.BlockSpec(wout.shape, const),
                  pl.BlockSpec(wr.shape, const), pl.BlockSpec((1, LANES), const)],
        out_specs=[pl.BlockSpec((None, tm, d), lambda b, i: (b, i, 0)),
                   pl.BlockSpec((None, tm, d), lambda b, i: (b, i, 0)),
                   pl.BlockSpec((None, tm, LANES), lambda b, i: (b, i, 0))],
        out_shape=[tok(d, F32), tok(d, BF16), tok(LANES, F32)],
        compiler_params=_cparams("arbitrary", "arbitrary"),
        name="merge_route",
    )(x3, og, ys, proj, gt1, sc2, sh2, nw2.reshape(1, d), wdn, wssm, wout, wr, br)


FFN_ROWS = 256
TOK_TILE = 512
RUN_ALIGN = 2 * SUBLANES
RUN_SIZES = tuple(RUN_ALIGN << s for s in range(8, -1, -1))
LOCAL_ROWS = 2 * TOK_TILE + MOE_EXPERTS * RUN_ALIGN


def _plan_kernel(plan_ref, rank_ref, cnt_ref):
    plan = plan_ref[...]
    tp = plan.shape[0]
    lane = _iota(plan.shape, 1).astype(F32)
    sel0 = lane == plan[:, PLAN_E0:PLAN_E0 + 1]
    sel1 = lane == plan[:, PLAN_E1:PLAN_E1 + 1]
    sel = (sel0 | sel1).astype(BF16)
    before = (_iota((tp, tp), 0) > _iota((tp, tp), 1)).astype(BF16)
    excl = jnp.dot(before, sel, preferred_element_type=F32)
    r0 = jnp.sum(jnp.where(sel0, excl, 0.0), axis=-1, keepdims=True)
    r1 = jnp.sum(jnp.where(sel1, excl, 0.0), axis=-1, keepdims=True)
    rank_ref[...] = jnp.where(lane == PLAN_E0, r0, 0.0) + jnp.where(lane == PLAN_E1, r1, 0.0)
    cnt_ref[...] = jnp.sum(sel.astype(F32), axis=0, keepdims=True)


def _plan(plan2, tp):
    t = plan2.shape[0]
    return pl.pallas_call(
        _plan_kernel,
        grid=(t // tp,),
        in_specs=[pl.BlockSpec((tp, LANES), lambda i: (i, 0))],
        out_specs=[pl.BlockSpec((tp, LANES), lambda i: (i, 0)), pl.BlockSpec((None, 1, LANES), lambda i: (i, 0, 0))],
        out_shape=[jax.ShapeDtypeStruct((t, LANES), F32), jax.ShapeDtypeStruct((t // tp, 1, LANES), F32)],
        compiler_params=_cparams("arbitrary"),
        name="moe_plan",
    )(plan2)


def _local_slots(plan, rank, off_row):
    lane = _iota(plan.shape, 1).astype(F32)
    slots = []
    for k in (PLAN_E0, PLAN_E1):
        off = jnp.sum(jnp.where(lane == plan[:, k:k + 1], off_row, 0.0), axis=-1, keepdims=True)
        slots.append(off + rank[:, k:k + 1])
    return slots


def _run_blocks(p8_ref, loff_ref, base_ref, act):
    for e in range(MOE_EXPERTS):
        n = p8_ref[0, e]
        lo = loff_ref[0, e]
        go = base_ref[0, e]
        for size in RUN_SIZES:
            if size > LOCAL_ROWS:
                continue

            @pl.when(jnp.bitwise_and(n, size) != 0)
            def _(size=size):
                done = jnp.bitwise_and(n, ~(2 * size - 1))
                act(pl.multiple_of(lo + done, RUN_ALIGN), pl.multiple_of(go + done, RUN_ALIGN), size)


def _dispatch_kernel(tv_ref, p8_ref, loff_ref, base_ref, plan_ref, rank_ref, off_ref, h2_ref, xs_ref,
                     buf_ref, zero_ref, sem, zsem):
    @pl.when(pl.program_id(0) == 0)
    def _():
        zero_ref[...] = jnp.zeros_like(zero_ref)

        def zero_copy(i):
            rows = pl.ds(pl.multiple_of(i * FFN_ROWS, FFN_ROWS), FFN_ROWS)
            return pltpu.make_async_copy(zero_ref, xs_ref.at[rows], zsem)

        def zero_tiles(act):
            def body(i, carry):
                @pl.when(tv_ref[i] < FFN_ROWS)
                def _():
                    act(zero_copy(i))
                return carry
            lax.fori_loop(0, xs_ref.shape[0] // FFN_ROWS, body, 0)

        zero_tiles(lambda cp: cp.start())
        zero_tiles(lambda cp: cp.wait())

    s0, s1 = _local_slots(plan_ref[...], rank_ref[...], off_ref[...])
    row = _iota((1, LOCAL_ROWS), 1).astype(F32)
    onehot_t = ((row == s0) | (row == s1)).astype(BF16)
    buf_ref[...] = lax.dot_general(onehot_t, h2_ref[...], (((0,), (0,)), ((), ())),
                                   preferred_element_type=F32).astype(BF16)

    def copy(lo, go, size):
        return pltpu.make_async_copy(buf_ref.at[pl.ds(lo, size)], xs_ref.at[pl.ds(go, size)], sem)

    _run_blocks(p8_ref, loff_ref, base_ref, lambda lo, go, size: copy(lo, go, size).start())
    _run_blocks(p8_ref, loff_ref, base_ref, lambda lo, go, size: copy(lo, go, size).wait())


def _tile_scalars():
    return pl.BlockSpec((None, 1, LANES), lambda i, *_: (i, 0, 0), memory_space=pltpu.SMEM)


def _dispatch(tile_valid, p8, loff, base, plan2, rank, off_f, h2, rows):
    t, d = h2.shape
    tt = TOK_TILE
    tok = lambda w: pl.BlockSpec((tt, w), lambda i, tv: (i, 0))
    return pl.pallas_call(
        _dispatch_kernel,
        grid_spec=pltpu.PrefetchScalarGridSpec(
            num_scalar_prefetch=1,
            grid=(t // tt,),
            in_specs=[_tile_scalars(), _tile_scalars(), _tile_scalars(),
                      tok(LANES), tok(LANES),
                      pl.BlockSpec((None, 1, LANES), lambda i, tv: (i, 0, 0)),
                      tok(d)],
            out_specs=pl.BlockSpec(memory_space=pl.ANY),
            scratch_shapes=[pltpu.VMEM((LOCAL_ROWS, d), BF16), pltpu.VMEM((FFN_ROWS, d), BF16),
                            pltpu.SemaphoreType.DMA(()), pltpu.SemaphoreType.DMA(())]),
        out_shape=jax.ShapeDtypeStruct((rows, d), BF16),
        compiler_params=_cparams("arbitrary"),
        name="moe_dispatch",
    )(tile_valid, p8, loff, base, plan2, rank, off_f, h2)


def _ffn_kernel(te_ref, tv_ref, xs_ref, wg_ref, wu_ref, wd_ref, o_ref):
    nv = tv_ref[pl.program_id(0)]

    @pl.when(nv > 0)
    def _():
        x = xs_ref[...]
        gate = jnp.dot(x, wg_ref[...], preferred_element_type=F32)
        up = jnp.dot(x, wu_ref[...], preferred_element_type=F32)
        o_ref[...] = _dot(_silu(gate) * up, wd_ref[...]).astype(BF16)

    @pl.when(nv <= 0)
    def _():
        o_ref[...] = jnp.zeros_like(o_ref)


def _ffn(tile_expert, tile_valid, xs, wg, wu, wd):
    rows, d = xs.shape
    wmap = lambda i, te, tv: (te[i], 0, 0)
    return pl.pallas_call(
        _ffn_kernel,
        grid_spec=pltpu.PrefetchScalarGridSpec(
            num_scalar_prefetch=2,
            grid=(rows // FFN_ROWS,),
            in_specs=[pl.BlockSpec((FFN_ROWS, d), lambda i, te, tv: (i, 0)),
                      pl.BlockSpec((None, d, MOE_FF), wmap),
                      pl.BlockSpec((None, d, MOE_FF), wmap),
                      pl.BlockSpec((None, MOE_FF, d), wmap)],
            out_specs=pl.BlockSpec((FFN_ROWS, d), lambda i, te, tv: (i, 0))),
        out_shape=jax.ShapeDtypeStruct((rows, d), BF16),
        compiler_params=_cparams("arbitrary"),
        name="moe_ffn",
    )(tile_expert, tile_valid, xs, wg, wu, wd)


def _combine_kernel(p8_ref, loff_ref, base_ref, ys_ref, plan_ref, rank_ref, off_ref, x1_ref, gt2_ref, fsc_ref,
                    fsh_ref, fnw_ref, y_ref, buf_ref, sem):
    def copy(lo, go, size):
        return pltpu.make_async_copy(ys_ref.at[pl.ds(go, size)], buf_ref.at[pl.ds(lo, size)], sem)

    _run_blocks(p8_ref, loff_ref, base_ref, lambda lo, go, size: copy(lo, go, size).start())
    plan = plan_ref[...]
    s0, s1 = _local_slots(plan, rank_ref[...], off_ref[...])
    row = _iota((1, LOCAL_ROWS), 1).astype(F32)
    weights = (jnp.where(row == s0, plan[:, PLAN_C0:PLAN_C0 + 1], 0.0)
               + jnp.where(row == s1, plan[:, PLAN_C1:PLAN_C1 + 1], 0.0))
    _run_blocks(p8_ref, loff_ref, base_ref, lambda lo, go, size: copy(lo, go, size).wait())
    total = loff_ref[0, MOE_EXPERTS - 1] + p8_ref[0, MOE_EXPERTS - 1]
    filled = _iota((LOCAL_ROWS, 1), 0) < total
    moe = _dot(weights, jnp.where(filled, buf_ref[...], jnp.zeros((), BF16)))
    x2 = x1_ref[...] + gt2_ref[...] * moe
    y_ref[...] = _rms(x2, fnw_ref[...]) * (1.0 + fsc_ref[...]) + fsh_ref[...]


def _combine(p8, loff, base, ys, plan, rank3, off_f, x1, gt2, fsc, fsh, fnw):
    bx, lx, d = x1.shape
    tt = TOK_TILE
    nl = lx // tt
    tile = lambda b, i: b * nl + i
    scal = pl.BlockSpec((None, 1, LANES), lambda b, i: (tile(b, i), 0, 0), memory_space=pltpu.SMEM)
    tokspec = lambda w: pl.BlockSpec((None, tt, w), lambda b, i: (b, i, 0))
    return pl.pallas_call(
        _combine_kernel,
        grid=(bx, nl),
        in_specs=[scal, scal, scal,
                  pl.BlockSpec(memory_space=pl.ANY),
                  tokspec(LANES), tokspec(LANES),
                  pl.BlockSpec((None, 1, LANES), lambda b, i: (tile(b, i), 0, 0)),
                  tokspec(d), _mod_spec(gt2, tt), _mod_spec(fsc, tt), _mod_spec(fsh, tt),
                  pl.BlockSpec((1, d), lambda b, i: (0, 0))],
        out_specs=tokspec(d),
        out_shape=jax.ShapeDtypeStruct((bx, lx, d), F32),
        scratch_shapes=[pltpu.VMEM((LOCAL_ROWS, d), BF16), pltpu.SemaphoreType.DMA(())],
        compiler_params=_cparams("arbitrary", "arbitrary"),
        name="moe_combine_final",
    )(p8, loff, base, ys, plan, rank3, off_f, x1, gt2, fsc, fsh, fnw.reshape(1, d))


def _moe(h2, plan, wg, wu, wd, x1, gt2, fsc, fsh, fnw):
    bx, lx, d = x1.shape
    t = bx * lx
    assert lx % TOK_TILE == 0
    plan2 = plan.reshape(t, LANES)
    rank, cnt = _plan(plan2, TOK_TILE)
    n = cnt[:, 0, :MOE_EXPERTS].astype(jnp.int32)
    p8 = (n + RUN_ALIGN - 1) // RUN_ALIGN * RUN_ALIGN
    loff = jnp.cumsum(p8, axis=1) - p8
    erows = jnp.sum(p8, axis=0)
    epad = (erows + FFN_ROWS - 1) // FFN_ROWS * FFN_ROWS
    ends = jnp.cumsum(epad)
    starts = ends - epad
    base = starts[None, :] + jnp.cumsum(p8, axis=0) - p8
    lanes = lambda a: jnp.pad(a, ((0, 0), (0, LANES - MOE_EXPERTS)))[:, None, :]
    rows = (-(-(2 * t + (t // TOK_TILE) * MOE_EXPERTS * (RUN_ALIGN - 1)) // FFN_ROWS) + MOE_EXPERTS) * FFN_ROWS
    tile_start = jnp.arange(rows // FFN_ROWS, dtype=jnp.int32) * FFN_ROWS
    tile_expert = jnp.minimum(jnp.sum(tile_start[:, None] >= ends[None, :], axis=1), MOE_EXPERTS - 1).astype(jnp.int32)
    hot = tile_expert[:, None] == jnp.arange(MOE_EXPERTS)[None, :]
    tile_valid = jnp.clip(jnp.sum(jnp.where(hot, erows - (tile_start[:, None] - starts), 0), axis=1),
                          0, FFN_ROWS).astype(jnp.int32)
    p8l, loffl, basel = lanes(p8), lanes(loff), lanes(base)
    off_f = loffl.astype(F32)
    xs = _dispatch(tile_valid, p8l, loffl, basel, plan2, rank, off_f, h2.reshape(t, d), rows)
    ys = _ffn(tile_expert, tile_valid, xs, wg, wu, wd)
    return _combine(p8l, loffl, basel, ys, plan, rank.reshape(bx, lx, LANES), off_f, x1, gt2, fsc, fsh, fnw)


def _prep_layer(lp):
    w_in = lp["w_in"]
    offs = [0]
    for s in (DN_CONV_DIM, DN_VAL, DN_HEADS, DN_HEADS, SSM_CONV_DIM, SSM_INNER, SSM_HEADS, D_MODEL, D_MODEL):
        offs.append(offs[-1] + s)
    seg = lambda i: w_in[:, offs[i]:offs[i + 1]]
    small = jnp.concatenate([seg(2), seg(3), seg(6)], axis=1)
    pad = jnp.zeros((D_MODEL, PROJ_N - COL_SMALL - small.shape[1]), F32)
    w_cat = jnp.concatenate([seg(0), seg(4), seg(5), seg(7), seg(8), seg(1), small, pad], axis=1).astype(BF16)
    wr = jnp.concatenate([lp["w_group_router"], lp["w_expert_router"],
                          jnp.zeros((D_MODEL, LANES - MOE_GROUPS - MOE_EXPERTS), F32)], axis=1)
    br = jnp.concatenate([lp["b_group_router"], lp["b_expert_router"],
                          jnp.zeros((LANES - MOE_GROUPS - MOE_EXPERTS,), F32)]).reshape(1, LANES)
    return dict(lp, w_cat=w_cat, wr=wr, br=br,
                wdn=lp["w_dn_out"].astype(BF16), wssm=lp["w_ssm_out"].astype(BF16), wout=lp["w_out"].astype(BF16),
                wg=lp["w_exp_gate"].astype(BF16), wu=lp["w_exp_up"].astype(BF16), wd=lp["w_exp_down"].astype(BF16))


def _trunk(x3, mods, fins, states, layers, final_norm_w, cfg):
    bx, lx, d = x3.shape
    tm = cfg["tm"]
    scan = dict(bt=cfg["bt"], r=cfg["r"], c=cfg["c"], lc=cfg["lc"], ng=cfg["ng"], cpi=cfg["cpi"])
    new_states = []
    n_layers = len(layers)
    for l, lp in enumerate(layers):
        sh1, sc1, gt1, sh2, sc2, gt2 = mods[l]
        dn_conv, dn_rec, ssm_conv, ssm_rec = states[l]
        proj = _inproj(x3, sc1, sh1, lp["norm_mix_w"], lp["w_cat"], tm).reshape(bx * lx, PROJ_N)
        og, dn_conv_new, dn_rec_new = _dn_branch(proj, lp["dn_conv_w"], lp["dn_A_log"], lp["dn_dt_bias"],
                                                 lp["dn_norm_w"], dn_conv, dn_rec, **scan)
        ys, ssm_conv_new, ssm_rec_new = _ssd_branch(proj, lp["ssm_conv_w"], lp["ssm_conv_b"], lp["ssm_A_log"],
                                                    lp["ssm_dt_bias"], lp["ssm_D"], lp["ssm_norm_w"],
                                                    ssm_conv, ssm_rec, **scan)
        x1, h2, plan = _merge(x3, og, ys, proj, gt1, sc2, sh2, lp["norm_ffn_w"], lp["wdn"], lp["wssm"], lp["wout"],
                              lp["wr"], lp["br"], cfg["tm_merge"])
        if l == n_layers - 1:
            fsh, fsc, fnw = fins[0], fins[1], final_norm_w
            x3 = _moe(h2, plan, lp["wg"], lp["wu"], lp["wd"], x1, gt2, fsc, fsh, fnw)
        else:
            raise NotImplementedError("only the last layer fuses the final norm; depth is 1 here")
        new_states.append((dn_conv_new, dn_rec_new, ssm_conv_new, ssm_rec_new))
    return x3, new_states


def _per_seq(m):
    return m[:, None, :]


def kernel(x_prompt, x_sample, c_prompt, c_sample, state_dn_conv, state_dn_rec, state_ssm_conv, state_ssm_rec, w_ada, b_ada, norm_mix_w, w_in, dn_conv_w, dn_A_log, dn_dt_bias, dn_norm_w, w_dn_out, ssm_conv_w, ssm_conv_b, ssm_A_log, ssm_dt_bias, ssm_D, ssm_norm_w, w_ssm_out, w_out, norm_ffn_w, w_group_router, b_group_router, w_expert_router, b_expert_router, w_exp_gate, w_exp_up, w_exp_down, w_ada_final, b_ada_final, final_norm_w):
    depth = w_ada.shape[0]
    assert depth == 1
    per_layer = dict(w_ada=w_ada, b_ada=b_ada, norm_mix_w=norm_mix_w, w_in=w_in, dn_conv_w=dn_conv_w,
                     dn_A_log=dn_A_log, dn_dt_bias=dn_dt_bias, dn_norm_w=dn_norm_w, w_dn_out=w_dn_out,
                     ssm_conv_w=ssm_conv_w, ssm_conv_b=ssm_conv_b, ssm_A_log=ssm_A_log, ssm_dt_bias=ssm_dt_bias,
                     ssm_D=ssm_D, ssm_norm_w=ssm_norm_w, w_ssm_out=w_ssm_out, w_out=w_out, norm_ffn_w=norm_ffn_w,
                     w_group_router=w_group_router, b_group_router=b_group_router,
                     w_expert_router=w_expert_router, b_expert_router=b_expert_router,
                     w_exp_gate=w_exp_gate, w_exp_up=w_exp_up, w_exp_down=w_exp_down)
    layers = [_prep_layer({k: v[l] for k, v in per_layer.items()}) for l in range(depth)]

    nbp, lp_, d = x_prompt.shape
    nbs, ls, _ = x_sample.shape
    c_all = jnp.concatenate([c_prompt, c_sample], axis=0)
    mod_all = [_ada(c_all, lyr["w_ada"], lyr["b_ada"]) for lyr in layers]
    fin_all = _ada(c_all, w_ada_final, b_ada_final)

    mods_p = [[_per_seq(m) for m in jnp.split(ma[:nbp], 6, axis=-1)] for ma in mod_all]
    fins_p = [_per_seq(m) for m in jnp.split(fin_all[:nbp], 2, axis=-1)]
    zeros_p = [(jnp.zeros((nbp, CONV_K - 1, DN_CONV_DIM), F32), jnp.zeros((nbp, DN_HEADS, DN_DK, DN_DV), F32),
                jnp.zeros((nbp, CONV_K - 1, SSM_CONV_DIM), F32),
                jnp.zeros((nbp, SSM_HEADS, SSM_HEADDIM, SSM_STATE), F32)) for _ in range(depth)]
    c_p = min(SCAN_CHUNK, lp_)
    r_p = min(lp_, 8 * c_p)
    cfg_p = dict(tm=min(lp_, 1024), tm_merge=min(lp_, 512), tm_moe=min(lp_, 1024), bt=1, r=r_p, c=c_p, lc=c_p, ng=lp_ // r_p,
                 cpi=4 if (r_p // c_p) % 4 == 0 else 1)
    y_p, st_p = _trunk(x_prompt, mods_p, fins_p, zeros_p, layers, final_norm_w, cfg_p)

    lpad = -(-ls // SUBLANES) * SUBLANES
    bt_s = 8
    xs = jnp.pad(x_sample, ((0, 0), (0, lpad - ls), (0, 0))).reshape(1, nbs * lpad, d)
    per_tok = lambda m: jnp.repeat(m, lpad, axis=0)[None]
    mods_s = [[per_tok(m) for m in jnp.split(ma[nbp:], 6, axis=-1)] for ma in mod_all]
    fins_s = [per_tok(m) for m in jnp.split(fin_all[nbp:], 2, axis=-1)]
    st_in = [(state_dn_conv[l], state_dn_rec[l], state_ssm_conv[l], state_ssm_rec[l]) for l in range(depth)]
    ts = nbs * lpad
    cfg_s = dict(tm=ts, tm_merge=min(ts, 512), tm_moe=min(ts, 512), bt=bt_s, r=lpad, c=lpad, lc=ls, ng=1, cpi=1)
    y_s, st_s = _trunk(xs, mods_s, fins_s, st_in, layers, final_norm_w, cfg_s)
    y_s = y_s.reshape(nbs, lpad, d)[:, :ls]

    stack = lambda sts, i: jnp.stack([s[i] for s in sts])
    return (y_p, y_s, stack(st_p, 0), stack(st_p, 1), stack(st_p, 2), stack(st_p, 3),
            stack(st_s, 0), stack(st_s, 1), stack(st_s, 2), stack(st_s, 3))
```

```python
import functools
import math

import jax
import jax.numpy as jnp
from jax import lax
from jax.experimental import pallas as pl
from jax.experimental.pallas import tpu as pltpu

F32 = jnp.float32
BF16 = jnp.bfloat16

D_MODEL = 1024
DN_HEADS = 4
DN_DK = 128
DN_DV = 128
DN_KEY = DN_HEADS * DN_DK
DN_VAL = DN_HEADS * DN_DV
CONV_K = 4
DN_CONV_DIM = 2 * DN_KEY + DN_VAL
SSM_INNER = D_MODEL
SSM_HEADDIM = 64
SSM_HEADS = SSM_INNER // SSM_HEADDIM
SSM_GROUPS = 2
SSM_STATE = 128
SSM_CONV_DIM = SSM_INNER + 2 * SSM_GROUPS * SSM_STATE
MOE_GROUPS = 4
MOE_PER_GROUP = 8
MOE_EXPERTS = MOE_GROUPS * MOE_PER_GROUP
MOE_FF = D_MODEL // 4
EPS = 1e-6
SCAN_CHUNK = 64

LANES = 128
SUBLANES = 8
CARRY_ROW = SUBLANES - (CONV_K - 1)

COL_DN_QKV = 0
COL_SSM_XBC = COL_DN_QKV + DN_CONV_DIM
COL_SSM_Z = COL_SSM_XBC + SSM_CONV_DIM
COL_GATES = COL_SSM_Z + SSM_INNER
COL_DN_Z = COL_GATES + 2 * D_MODEL
COL_SMALL = COL_DN_Z + DN_VAL
PROJ_TN = 1024
PROJ_N = 7 * PROJ_TN
SM_A, SM_B, SM_DT = 0, DN_HEADS, 2 * DN_HEADS
RT_GROUP, RT_EXPERT = 0, MOE_GROUPS

VMEM_LIMIT = 56 * 1024 * 1024


def _cparams(*sem):
    return pltpu.CompilerParams(dimension_semantics=sem, vmem_limit_bytes=VMEM_LIMIT)


def _dot(a, b):
    return jnp.dot(a.astype(BF16), b.astype(BF16), preferred_element_type=F32)


def _dot_nt(a, b):
    return lax.dot_general(a.astype(BF16), b.astype(BF16), (((1,), (1,)), ((), ())),
                           preferred_element_type=F32)


def _dot_tn(a, b):
    return lax.dot_general(a.astype(BF16), b.astype(BF16), (((0,), (0,)), ((), ())),
                           preferred_element_type=F32)


def _split3(x):
    hi = x.astype(BF16)
    r = x - hi.astype(F32)
    mid = r.astype(BF16)
    lo = (r - mid.astype(F32)).astype(BF16)
    return hi, mid, lo


def _dot_exact_lhs(a_bf, b):
    hi, mid, lo = _split3(b)
    d = functools.partial(jnp.dot, preferred_element_type=F32)
    return d(a_bf, hi) + (d(a_bf, mid) + d(a_bf, lo))


def _dot_x3(a, b):
    a_hi = a.astype(BF16)
    a_lo = (a - a_hi.astype(F32)).astype(BF16)
    b_hi = b.astype(BF16)
    b_lo = (b - b_hi.astype(F32)).astype(BF16)
    d = functools.partial(jnp.dot, preferred_element_type=F32)
    return d(a_hi, b_hi) + (d(a_hi, b_lo) + d(a_lo, b_hi))


def _sigmoid(x):
    return 1.0 / (1.0 + jnp.exp(-x))


def _silu(x):
    return x * _sigmoid(x)


def _softplus(x):
    return jnp.maximum(x, 0.0) + jnp.log1p(jnp.exp(-jnp.abs(x)))


def _rms(x, w):
    return x * lax.rsqrt(jnp.mean(x * x, axis=-1, keepdims=True) + EPS) * w


def _iota(shape, dim):
    return lax.broadcasted_iota(jnp.int32, shape, dim)


def _ada_kernel(c_ref, w_ref, b_ref, o_ref):
    o_ref[...] = _dot(_silu(c_ref[...]), w_ref[...]) + b_ref[...]


def _ada(c, w, b, tn=512):
    m, d = c.shape
    n = w.shape[1]
    return pl.pallas_call(
        _ada_kernel,
        grid=(n // tn,),
        in_specs=[pl.BlockSpec((m, d), lambda j: (0, 0)),
                  pl.BlockSpec((d, tn), lambda j: (0, j)),
                  pl.BlockSpec((1, tn), lambda j: (0, j))],
        out_specs=pl.BlockSpec((m, tn), lambda j: (0, j)),
        out_shape=jax.ShapeDtypeStruct((m, n), F32),
        compiler_params=_cparams("arbitrary"),
        name="ada_mod",
    )(c, w, b.reshape(1, n))


def _mod_spec(mod, tm):
    if mod.shape[1] == 1:
        return pl.BlockSpec((None, 1, D_MODEL), lambda b, i, *_: (b, 0, 0))
    return pl.BlockSpec((None, tm, D_MODEL), lambda b, i, *_: (b, i, 0))


def _inproj_kernel(x_ref, sc_ref, sh_ref, nw_ref, w_ref, o_ref, sm_ref, h_ref, *, tm, sub):
    @pl.when(pl.program_id(2) == 0)
    def _():
        per_token = sc_ref.shape[0] != 1

        def body(r, carry):
            rows = pl.ds(pl.multiple_of(r * sub, sub), sub)
            sc = sc_ref[rows, :] if per_token else sc_ref[...]
            sh = sh_ref[rows, :] if per_token else sh_ref[...]
            h = _rms(x_ref[rows, :], nw_ref[...]) * (1.0 + sc) + sh
            h_ref[rows, :] = h.astype(BF16)
            return carry

        lax.fori_loop(0, tm // sub, body, 0)

    acc = jnp.dot(h_ref[...], w_ref[...], preferred_element_type=F32)
    o_ref[...] = acc.astype(BF16)

    @pl.when(pl.program_id(2) == COL_SMALL // PROJ_TN)
    def _():
        sm_ref[...] = acc[:, COL_SMALL % PROJ_TN:COL_SMALL % PROJ_TN + LANES]


def _inproj(x3, sc, sh, nw, w_cat, tm):
    bx, lx, d = x3.shape
    n = w_cat.shape[1]
    kern = functools.partial(_inproj_kernel, tm=tm, sub=min(tm, 256))
    return pl.pallas_call(
        kern,
        grid=(bx, lx // tm, n // PROJ_TN),
        in_specs=[pl.BlockSpec((None, tm, d), lambda b, i, j: (b, i, 0)),
                  _mod_spec(sc, tm), _mod_spec(sh, tm),
                  pl.BlockSpec((1, d), lambda b, i, j: (0, 0)),
                  pl.BlockSpec((d, PROJ_TN), lambda b, i, j: (0, j))],
        out_specs=[pl.BlockSpec((None, tm, PROJ_TN), lambda b, i, j: (b, i, j)),
                   pl.BlockSpec((None, tm, LANES), lambda b, i, j: (b, i, 0))],
        out_shape=[jax.ShapeDtypeStruct((bx, lx, n), BF16), jax.ShapeDtypeStruct((bx, lx, LANES), F32)],
        scratch_shapes=[pltpu.VMEM((tm, d), BF16)],
        compiler_params=_cparams("arbitrary", "arbitrary", "arbitrary"),
        name="norm_inproj",
    )(x3, sc, sh, nw.reshape(1, d), w_cat)


def _conv_silu(u, ext_ref, bb, cw_ref, c, lc, bias=None):
    ext_ref[bb, SUBLANES:SUBLANES + c, :] = u
    y = ext_ref[bb, CARRY_ROW:CARRY_ROW + c, :] * cw_ref[0:1, :]
    for i in range(1, CONV_K):
        y = y + ext_ref[bb, CARRY_ROW + i:CARRY_ROW + i + c, :] * cw_ref[i:i + 1, :]
    ext_ref[bb, CARRY_ROW:SUBLANES, :] = ext_ref[bb, CARRY_ROW + lc:SUBLANES + lc, :]
    if bias is not None:
        y = y + bias
    return _silu(y)


def _chunk_reader(ref, r, c):
    if r == c:
        whole = ref[...].astype(F32)
        return lambda bb, r0: whole[bb * c:(bb + 1) * c]
    return lambda bb, r0: ref[pl.ds(bb * r + r0, c), :].astype(F32)


def _chunk_writer(ref, bt, r, c):
    if r != c:
        return (lambda bb, r0, val: ref.__setitem__((pl.ds(bb * r + r0, c), slice(None)), val.astype(ref.dtype)),
                lambda: None)
    parts = {}

    def flush():
        ref[...] = jnp.concatenate([parts[bb] for bb in range(bt)], axis=0).astype(ref.dtype)
    return (lambda bb, r0, val: parts.__setitem__(bb, val)), flush


def _run_rounds(process, bt, r, c, cpi):
    g = r // c
    if g == 1:
        process([[(bb, 0) for bb in range(bt)]])
    else:
        def body(ci, carry):
            process([[(bb, pl.multiple_of((ci * cpi + t) * c, c)) for bb in range(bt)] for t in range(cpi)])
            return carry
        lax.fori_loop(0, g // cpi, body, 0)


INV_BASE = SUBLANES


def _inverse_masks(row, col, c):
    sh = lambda x, s: jnp.right_shift(x, int(math.log2(s)))
    diag = sh(row, INV_BASE) == sh(col, INV_BASE)
    merges = []
    s = INV_BASE
    while s < c:
        merges.append((sh(row, 2 * s) == sh(col, 2 * s))
                      & (jnp.bitwise_and(sh(row, s), 1) == 1) & (jnp.bitwise_and(sh(col, s), 1) == 0))
        s *= 2
    return diag, merges


def _unit_lower_inverses(lmats, eye_f, masks):
    diag, merges = masks
    npows = [-jnp.where(diag, l, 0.0) for l in lmats]
    ps = [eye_f + n for n in npows]
    for _ in range(int(math.log2(INV_BASE)) - 1):
        npows = [_dot(n, n) for n in npows]
        ps = [p + _dot(p, n) for p, n in zip(ps, npows)]
    for m in merges:
        ts = [_dot(p, jnp.where(m, l, 0.0)) for p, l in zip(ps, lmats)]
        ps = [p - _dot(t, p) for t, p in zip(ts, ps)]
    return ps


def _dn_kernel(qkv_ref, z_ref, sm_ref, cw_ref, alog_ref, bias_ref, nw_ref, cin_ref, sin_ref,
               o_ref, cout_ref, sout_ref, ext_ref, *, bt, r, c, lc, ng, cpi):
    j = pl.program_id(1)

    @pl.when(j == 0)
    def _():
        ext_ref[:, CARRY_ROW:SUBLANES, :] = cin_ref[...]
        sout_ref[...] = sin_ref[...]

    row = _iota((c, c), 0)
    col = _iota((c, c), 1)
    tri_incl = row >= col
    tri_strict = row > col
    eye_f = (row == col).astype(F32)
    tril_bf = tri_incl.astype(BF16)
    valid = _iota((c, 1), 0) < lc
    nexp_a = -jnp.exp(alog_ref[...])
    bias = bias_ref[...]
    inv_masks = _inverse_masks(row, col, c)
    heads = range(DN_HEADS)
    read_qkv, read_z, read_sm = (_chunk_reader(ref, r, c) for ref in (qkv_ref, z_ref, sm_ref))
    put_o, flush_o = _chunk_writer(o_ref, bt, r, c)

    def prep(bb, r0):
        y = _conv_silu(read_qkv(bb, r0), ext_ref, bb, cw_ref, c, lc)
        sm = read_sm(bb, r0)
        g_all = nexp_a * _softplus(sm + bias)
        beta_all = _sigmoid(sm)
        if lc < c:
            g_all = jnp.where(valid, g_all, 0.0)
            beta_all = jnp.where(valid, beta_all, 0.0)
        cs_all = _dot_exact_lhs(tril_bf, g_all)
        cs_t = cs_all.T
        ecs_all = jnp.exp(cs_all)
        per_head = []
        for h in heads:
            q = y[:, h * DN_DK:(h + 1) * DN_DK]
            k = y[:, DN_KEY + h * DN_DK:DN_KEY + (h + 1) * DN_DK]
            v = y[:, 2 * DN_KEY + h * DN_DV:2 * DN_KEY + (h + 1) * DN_DV]
            q = q * lax.rsqrt(jnp.sum(q * q, axis=-1, keepdims=True) + EPS) * (DN_DK ** -0.5)
            k = k * lax.rsqrt(jnp.sum(k * k, axis=-1, keepdims=True) + EPS)
            if lc < c:
                k = jnp.where(valid, k, 0.0)
            cs = cs_all[:, SM_A + h:SM_A + h + 1]
            ecs = ecs_all[:, SM_A + h:SM_A + h + 1]
            beta = beta_all[:, SM_B + h:SM_B + h + 1]
            last = cs_all[c - 1:c, SM_A + h:SM_A + h + 1]
            decay = jnp.exp(jnp.where(tri_incl, cs - cs_t[SM_A + h:SM_A + h + 1, :], -jnp.inf))
            per_head.append(dict(q_ecs=(q * ecs).astype(BF16), q=q.astype(BF16), k=k.astype(BF16), beta=beta,
                                 decay=decay, elast=jnp.exp(last),
                                 k_tail=(k * jnp.exp(last - cs)).astype(BF16),
                                 rhs=jnp.concatenate([v * beta, k * (beta * ecs)], axis=1).astype(BF16)))
        return per_head

    def process(rounds):
        items = [it for rnd in rounds for it in rnd]
        pre = [prep(bb, r0) for bb, r0 in items]
        chains = [(i, h) for i in range(len(items)) for h in heads]
        a = {ch: pre[ch[0]][ch[1]] for ch in chains}
        kk = {ch: _dot_nt(a[ch]["k"], a[ch]["k"]) for ch in chains}
        qk = {ch: _dot_nt(a[ch]["q"], a[ch]["k"]) * a[ch]["decay"] for ch in chains}
        lmats = [jnp.where(tri_strict, a[ch]["beta"] * kk[ch] * a[ch]["decay"], 0.0) for ch in chains]
        pinv = _unit_lower_inverses(lmats, eye_f, inv_masks)
        sol = {ch: _dot(p, a[ch]["rhs"]) for ch, p in zip(chains, pinv)}
        first = 0
        for rnd in rounds:
            idx = range(first, first + len(rnd))
            first += len(rnd)
            rch = [(i, h) for i in idx for h in heads]
            s = {ch: sout_ref[items[ch[0]][0], ch[1]] for ch in rch}
            ws_qs = {ch: _dot(jnp.concatenate([sol[ch][:, DN_DV:].astype(BF16), a[ch]["q_ecs"]], axis=0), s[ch])
                     for ch in rch}
            v_new = {ch: (sol[ch][:, :DN_DV] - ws_qs[ch][:c]).astype(BF16) for ch in rch}
            o = {ch: ws_qs[ch][c:] + _dot(qk[ch], v_new[ch]) for ch in rch}
            for ch in rch:
                sout_ref[items[ch[0]][0], ch[1]] = a[ch]["elast"] * s[ch] + _dot_tn(a[ch]["k_tail"], v_new[ch])
            for i in idx:
                bb, r0 = items[i]
                z = read_z(bb, r0)
                put_o(bb, r0, jnp.concatenate(
                    [_rms(o[(i, h)], nw_ref[...]) * _silu(z[:, h * DN_DV:(h + 1) * DN_DV]) for h in heads], axis=1))

    _run_rounds(process, bt, r, c, cpi)
    flush_o()

    @pl.when(j == ng - 1)
    def _():
        cout_ref[...] = ext_ref[:, CARRY_ROW:SUBLANES, :]


def _small_row(vals, offset, fill=0.0):
    row = jnp.full((1, LANES), fill, F32)
    return lax.dynamic_update_slice(row, vals.astype(F32).reshape(1, -1), (0, offset))


def _dn_branch(proj, small, conv_w, a_log, dt_bias, norm_w, conv_in, rec_in, *, bt, r, c, lc, ng, cpi):
    nb = conv_in.shape[0] // bt
    rows = bt * r
    t = proj.shape[0]
    kern = functools.partial(_dn_kernel, bt=bt, r=r, c=c, lc=lc, ng=ng, cpi=cpi)
    rowmap = lambda col: (lambda i, j: (i * ng + j, col))
    const = lambda i, j: (0, 0)
    return pl.pallas_call(
        kern,
        grid=(nb, ng),
        in_specs=[pl.BlockSpec((rows, DN_CONV_DIM), rowmap(COL_DN_QKV // DN_CONV_DIM)),
                  pl.BlockSpec((rows, DN_VAL), rowmap(COL_DN_Z // DN_VAL)),
                  pl.BlockSpec((rows, LANES), rowmap(0)),
                  pl.BlockSpec((CONV_K, DN_CONV_DIM), const),
                  pl.BlockSpec((1, LANES), const),
                  pl.BlockSpec((1, LANES), const),
                  pl.BlockSpec((1, DN_DV), const),
                  pl.BlockSpec((bt, CONV_K - 1, DN_CONV_DIM), lambda i, j: (i, 0, 0)),
                  pl.BlockSpec((bt, DN_HEADS, DN_DK, DN_DV), lambda i, j: (i, 0, 0, 0))],
        out_specs=[pl.BlockSpec((rows, DN_VAL), lambda i, j: (i * ng + j, 0)),
                   pl.BlockSpec((bt, CONV_K - 1, DN_CONV_DIM), lambda i, j: (i, 0, 0)),
                   pl.BlockSpec((bt, DN_HEADS, DN_DK, DN_DV), lambda i, j: (i, 0, 0, 0))],
        out_shape=[jax.ShapeDtypeStruct((t, DN_VAL), BF16),
                   jax.ShapeDtypeStruct(conv_in.shape, F32),
                   jax.ShapeDtypeStruct(rec_in.shape, F32)],
        scratch_shapes=[pltpu.VMEM((bt, c + SUBLANES, DN_CONV_DIM), F32)],
        compiler_params=_cparams("arbitrary", "arbitrary"),
        name="gated_delta",
    )(proj, proj, small, conv_w, _small_row(a_log, SM_A), _small_row(dt_bias, SM_A),
      norm_w.reshape(1, DN_DV), conv_in, rec_in)


def _ssd_kernel(xbc_ref, z_ref, sm_ref, cw_ref, cb_ref, alog_ref, bias_ref, dvec_ref, nw_ref, cin_ref,
                hin_ref, y_ref, cout_ref, hout_ref, ext_ref, *, bt, r, c, lc, ng, cpi):
    j = pl.program_id(1)

    @pl.when(j == 0)
    def _():
        ext_ref[:, CARRY_ROW:SUBLANES, :] = cin_ref[...]
        hout_ref[...] = hin_ref[...]

    tril_bf = (_iota((c, c), 0) >= _iota((c, c), 1)).astype(BF16)
    row2 = _iota((c, 2 * c), 0)
    lane2 = _iota((c, 2 * c), 1)
    left2 = lane2 < c
    tri2 = row2 >= jnp.where(left2, lane2, lane2 - c)
    left_x = _iota((c, LANES), 1) < SSM_HEADDIM
    top_h = _iota((2 * SSM_HEADDIM, 1), 0) < SSM_HEADDIM
    valid = _iota((c, 1), 0) < lc
    nexp_a = -jnp.exp(alog_ref[...])
    bias = bias_ref[...]
    gw = SSM_INNER // SSM_GROUPS
    pairs_per_group = SSM_HEADS // SSM_GROUPS // 2

    pairs = range(SSM_HEADS // 2)
    pair_rows = lambda p: slice(p * LANES, (p + 1) * LANES)
    read_xbc, read_z, read_sm = (_chunk_reader(ref, r, c) for ref in (xbc_ref, z_ref, sm_ref))
    put_y, flush_y = _chunk_writer(y_ref, bt, r, c)

    def prep(bb, r0):
        y = _conv_silu(read_xbc(bb, r0), ext_ref, bb, cw_ref, c, lc, bias=cb_ref[...])
        dt_all = _softplus(read_sm(bb, r0) + bias)
        if lc < c:
            dt_all = jnp.where(valid, dt_all, 0.0)
        cs_all = _dot_exact_lhs(tril_bf, dt_all * nexp_a)
        cs_t2 = jnp.concatenate([cs_all, cs_all], axis=0).T
        last = cs_all[c - 1:c, :]
        bm = [y[:, SSM_INNER + g * SSM_STATE:SSM_INNER + (g + 1) * SSM_STATE].astype(BF16)
              for g in range(SSM_GROUPS)]
        cm = [y[:, SSM_INNER + (SSM_GROUPS + g) * SSM_STATE:SSM_INNER + (SSM_GROUPS + g + 1) * SSM_STATE].astype(BF16)
              for g in range(SSM_GROUPS)]
        return dict(y=y, dt=dt_all, cs=cs_all, cs_t2=cs_t2, ecs=jnp.exp(cs_all),
                    tail=jnp.exp(last - cs_all), elast=jnp.exp(last), bm=bm, cm=cm)

    def process(rounds):
        items = [it for rnd in rounds for it in rnd]
        pre = [prep(bb, r0) for bb, r0 in items]
        ids = range(len(items))
        cb2 = {(i, g): _dot_nt(pre[i]["cm"][g], jnp.concatenate([pre[i]["bm"][g]] * 2, axis=0))
               for i in ids for g in range(SSM_GROUPS)}
        y_intra, upd, xs_of = {}, {}, {}
        for i in ids:
            a = pre[i]
            for p in pairs:
                g = p // pairs_per_group
                la, lb = SM_DT + 2 * p, SM_DT + 2 * p + 1
                both = lambda v, mask=left_x: jnp.where(mask, v[:, la:la + 1], v[:, lb:lb + 1])
                xs = a["y"][:, pair_rows(p)]
                xdt = xs * both(a["dt"])
                diff = both(a["cs"], left2) - jnp.where(left2[0:1], a["cs_t2"][la:la + 1, :], a["cs_t2"][lb:lb + 1, :])
                decay2 = jnp.exp(jnp.where(tri2, diff, -jnp.inf))
                rhs = jnp.concatenate([jnp.where(left_x, xdt, 0.0), jnp.where(left_x, 0.0, xdt)], axis=0)
                y_intra[i, p] = _dot(cb2[i, g] * decay2, rhs)
                upd[i, p] = _dot_tn(xdt * both(a["tail"]), a["bm"][g])
                xs_of[i, p] = xs
        first = 0
        for rnd in rounds:
            idx = range(first, first + len(rnd))
            first += len(rnd)
            for i in idx:
                a = pre[i]
                bb, r0 = items[i]
                outs = []
                for p in pairs:
                    g = p // pairs_per_group
                    la, lb = SM_DT + 2 * p, SM_DT + 2 * p + 1
                    hs = hout_ref[bb, pair_rows(p), :]
                    y_inter = _dot_nt(a["cm"][g], hs) * jnp.where(left_x, a["ecs"][:, la:la + 1], a["ecs"][:, lb:lb + 1])
                    hout_ref[bb, pair_rows(p), :] = (
                        jnp.where(top_h, a["elast"][:, la:la + 1], a["elast"][:, lb:lb + 1]) * hs + upd[i, p])
                    outs.append(y_intra[i, p] + y_inter + dvec_ref[:, pair_rows(p)] * xs_of[i, p])
                yz = jnp.concatenate(outs, axis=1) * _silu(read_z(bb, r0))
                put_y(bb, r0, jnp.concatenate(
                    [_rms(yz[:, g * gw:(g + 1) * gw], nw_ref[:, g * gw:(g + 1) * gw]) for g in range(SSM_GROUPS)],
                    axis=1))

    _run_rounds(process, bt, r, c, cpi)
    flush_y()

    @pl.when(j == ng - 1)
    def _():
        cout_ref[...] = ext_ref[:, CARRY_ROW:SUBLANES, :]


def _ssd_branch(proj, small, conv_w, conv_b, a_log, dt_bias, d_skip, norm_w, conv_in, rec_in, *, bt, r, c, lc, ng, cpi):
    nb = conv_in.shape[0] // bt
    rows = bt * r
    t = proj.shape[0]
    hrows = SSM_HEADS * SSM_HEADDIM
    rec2 = rec_in.reshape(rec_in.shape[0], hrows, SSM_STATE)
    kern = functools.partial(_ssd_kernel, bt=bt, r=r, c=c, lc=lc, ng=ng, cpi=cpi)
    rowmap = lambda col: (lambda i, j: (i * ng + j, col))
    const = lambda i, j: (0, 0)
    y, conv_out, rec_out = pl.pallas_call(
        kern,
        grid=(nb, ng),
        in_specs=[pl.BlockSpec((rows, SSM_CONV_DIM), rowmap(COL_SSM_XBC // SSM_CONV_DIM)),
                  pl.BlockSpec((rows, SSM_INNER), rowmap(COL_SSM_Z // SSM_INNER)),
                  pl.BlockSpec((rows, LANES), rowmap(0)),
                  pl.BlockSpec((CONV_K, SSM_CONV_DIM), const),
                  pl.BlockSpec((1, SSM_CONV_DIM), const),
                  pl.BlockSpec((1, LANES), const),
                  pl.BlockSpec((1, LANES), const),
                  pl.BlockSpec((1, SSM_INNER), const),
                  pl.BlockSpec((1, SSM_INNER), const),
                  pl.BlockSpec((bt, CONV_K - 1, SSM_CONV_DIM), lambda i, j: (i, 0, 0)),
                  pl.BlockSpec((bt, hrows, SSM_STATE), lambda i, j: (i, 0, 0))],
        out_specs=[pl.BlockSpec((rows, SSM_INNER), lambda i, j: (i * ng + j, 0)),
                   pl.BlockSpec((bt, CONV_K - 1, SSM_CONV_DIM), lambda i, j: (i, 0, 0)),
                   pl.BlockSpec((bt, hrows, SSM_STATE), lambda i, j: (i, 0, 0))],
        out_shape=[jax.ShapeDtypeStruct((t, SSM_INNER), BF16),
                   jax.ShapeDtypeStruct(conv_in.shape, F32),
                   jax.ShapeDtypeStruct(rec2.shape, F32)],
        scratch_shapes=[pltpu.VMEM((bt, c + SUBLANES, SSM_CONV_DIM), F32)],
        compiler_params=_cparams("arbitrary", "arbitrary"),
        name="ssd_scan",
    )(proj, proj, small, conv_w, conv_b.reshape(1, -1), _small_row(a_log, SM_DT), _small_row(dt_bias, SM_DT),
      jnp.repeat(d_skip.astype(F32), SSM_HEADDIM).reshape(1, SSM_INNER), norm_w.reshape(1, SSM_INNER),
      conv_in, rec2)
    return y, conv_out, rec_out.reshape(rec_in.shape)


PLAN_E0, PLAN_E1, PLAN_C0, PLAN_C1 = 0, 1, 2, 3


def _route(logits):
    lane = _iota(logits.shape, 1).astype(F32)
    big = float(LANES)
    is_group = (lane >= RT_GROUP) & (lane < RT_GROUP + MOE_GROUPS)
    gl = jnp.where(is_group, logits, -jnp.inf)
    gmax = jnp.max(gl, axis=-1, keepdims=True)
    g_sel = jnp.min(jnp.where(gl == gmax, lane, big), axis=-1, keepdims=True) - RT_GROUP
    p_group = 1.0 / jnp.sum(jnp.exp(gl - gmax), axis=-1, keepdims=True)
    e_lo = RT_EXPERT + MOE_PER_GROUP * g_sel
    in_grp = (lane >= e_lo) & (lane < e_lo + MOE_PER_GROUP)
    el = jnp.where(in_grp, logits, -jnp.inf)
    ee = jnp.exp(el - jnp.max(el, axis=-1, keepdims=True))
    pe = jnp.where(in_grp, ee / jnp.sum(ee, axis=-1, keepdims=True), -1.0)
    p1 = jnp.max(pe, axis=-1, keepdims=True)
    i1 = jnp.min(jnp.where(pe == p1, lane, big), axis=-1, keepdims=True)
    pe2 = jnp.where(lane == i1, -1.0, pe)
    p2 = jnp.max(pe2, axis=-1, keepdims=True)
    i2 = jnp.min(jnp.where(pe2 == p2, lane, big), axis=-1, keepdims=True)
    tot = p1 + p2
    plan = jnp.where(lane == PLAN_E0, i1 - RT_EXPERT, 0.0) + jnp.where(lane == PLAN_E1, i2 - RT_EXPERT, 0.0)
    return plan + jnp.where(lane == PLAN_C0, p_group * p1 / tot, 0.0) + jnp.where(lane == PLAN_C1, p_group * p2 / tot, 0.0)


def _merge_kernel(x_ref, og_ref, ys_ref, gates_ref, gt1_ref, sc2_ref, sh2_ref, nw2_ref, wdn_ref, wssm_ref,
                  wout_ref, wr_ref, br_ref, x1_ref, h2_ref, plan_ref):
    y_dn = _dot(og_ref[...], wdn_ref[...])
    y_ssm = _dot(ys_ref[...], wssm_ref[...])
    merged = (_sigmoid(gates_ref[:, :D_MODEL].astype(F32)) * y_dn
              + _sigmoid(gates_ref[:, D_MODEL:].astype(F32)) * y_ssm)
    x1 = x_ref[...] + gt1_ref[...] * _dot(merged, wout_ref[...])
    x1_ref[...] = x1
    h2 = _rms(x1, nw2_ref[...]) * (1.0 + sc2_ref[...]) + sh2_ref[...]
    h2_ref[...] = h2.astype(BF16)
    plan_ref[...] = _route(_dot_x3(h2, wr_ref[...]) + br_ref[...])


def _merge(x3, og, ys, proj, gt1, sc2, sh2, nw2, wdn, wssm, wout, wr, br, tm):
    bx, lx, d = x3.shape
    nl = lx // tm
    rowmap = lambda col: (lambda b, i: (b * nl + i, col))
    const = lambda b, i: (0, 0)
    tok = lambda w, dt: jax.ShapeDtypeStruct((bx, lx, w), dt)
    return pl.pallas_call(
        _merge_kernel,
        grid=(bx, nl),
        in_specs=[pl.BlockSpec((None, tm, d), lambda b, i: (b, i, 0)),
                  pl.BlockSpec((tm, DN_VAL), rowmap(0)),
                  pl.BlockSpec((tm, SSM_INNER), rowmap(0)),
                  pl.BlockSpec((tm, 2 * d), rowmap(COL_GATES // (2 * d))),
                  _mod_spec(gt1, tm), _mod_spec(sc2, tm), _mod_spec(sh2, tm),
                  pl.BlockSpec((1, d), const),
                  pl.BlockSpec(wdn.shape, const), pl.BlockSpec(wssm.shape, const), pl.BlockSpec(wout.shape, const),
                  pl.BlockSpec(wr.shape, const), pl.BlockSpec((1, LANES), const)],
        out_specs=[pl.BlockSpec((None, tm, d), lambda b, i: (b, i, 0)),
                   pl.BlockSpec((None, tm, d), lambda b, i: (b, i, 0)),
                   pl.BlockSpec((None, tm, LANES), lambda b, i: (b, i, 0))],
        out_shape=[tok(d, F32), tok(d, BF16), tok(LANES, F32)],
        compiler_params=_cparams("arbitrary", "arbitrary"),
        name="merge_route",
    )(x3, og, ys, proj, gt1, sc2, sh2, nw2.reshape(1, d), wdn, wssm, wout, wr, br)


FFN_ROWS = 512
TOK_TILE = 512
RUN_ALIGN = 2 * SUBLANES
RUN_SIZES = tuple(RUN_ALIGN << s for s in range(8, -1, -1))
LOCAL_ROWS = 2 * TOK_TILE + MOE_EXPERTS * RUN_ALIGN


def _plan_kernel(plan_ref, rank_ref, cnt_ref):
    plan = plan_ref[...]
    tp = plan.shape[0]
    lane = _iota(plan.shape, 1).astype(F32)
    sel0 = lane == plan[:, PLAN_E0:PLAN_E0 + 1]
    sel1 = lane == plan[:, PLAN_E1:PLAN_E1 + 1]
    sel = (sel0 | sel1).astype(BF16)
    before = (_iota((tp, tp), 0) > _iota((tp, tp), 1)).astype(BF16)
    excl = jnp.dot(before, sel, preferred_element_type=F32)
    r0 = jnp.sum(jnp.where(sel0, excl, 0.0), axis=-1, keepdims=True)
    r1 = jnp.sum(jnp.where(sel1, excl, 0.0), axis=-1, keepdims=True)
    rank_ref[...] = jnp.where(lane == PLAN_E0, r0, 0.0) + jnp.where(lane == PLAN_E1, r1, 0.0)
    cnt_ref[...] = jnp.sum(sel.astype(F32), axis=0, keepdims=True)


def _plan(plan2, tp):
    t = plan2.shape[0]
    return pl.pallas_call(
        _plan_kernel,
        grid=(t // tp,),
        in_specs=[pl.BlockSpec((tp, LANES), lambda i: (i, 0))],
        out_specs=[pl.BlockSpec((tp, LANES), lambda i: (i, 0)), pl.BlockSpec((None, 1, LANES), lambda i: (i, 0, 0))],
        out_shape=[jax.ShapeDtypeStruct((t, LANES), F32), jax.ShapeDtypeStruct((t // tp, 1, LANES), F32)],
        compiler_params=_cparams("arbitrary"),
        name="moe_plan",
    )(plan2)


def _local_slots(plan, rank, off_row):
    lane = _iota(plan.shape, 1).astype(F32)
    slots = []
    for k in (PLAN_E0, PLAN_E1):
        off = jnp.sum(jnp.where(lane == plan[:, k:k + 1], off_row, 0.0), axis=-1, keepdims=True)
        slots.append(off + rank[:, k:k + 1])
    return slots


def _run_blocks(p8_ref, loff_ref, base_ref, act):
    for e in range(MOE_EXPERTS):
        n = p8_ref[0, e]
        lo = loff_ref[0, e]
        go = base_ref[0, e]
        for size in RUN_SIZES:
            if size > LOCAL_ROWS:
                continue

            @pl.when(jnp.bitwise_and(n, size) != 0)
            def _(size=size):
                done = jnp.bitwise_and(n, ~(2 * size - 1))
                act(pl.multiple_of(lo + done, RUN_ALIGN), pl.multiple_of(go + done, RUN_ALIGN), size)


def _dispatch_kernel(tv_ref, p8_ref, loff_ref, base_ref, plan_ref, rank_ref, off_ref, h2_ref, xs_ref,
                     buf_ref, zero_ref, sem, zsem):
    @pl.when(pl.program_id(0) == 0)
    def _():
        zero_ref[...] = jnp.zeros_like(zero_ref)

        def zero_copy(i):
            rows = pl.ds(pl.multiple_of(i * FFN_ROWS, FFN_ROWS), FFN_ROWS)
            return pltpu.make_async_copy(zero_ref, xs_ref.at[rows], zsem)

        def zero_tiles(act):
            def body(i, carry):
                @pl.when(tv_ref[i] < FFN_ROWS)
                def _():
                    act(zero_copy(i))
                return carry
            lax.fori_loop(0, xs_ref.shape[0] // FFN_ROWS, body, 0)

        zero_tiles(lambda cp: cp.start())
        zero_tiles(lambda cp: cp.wait())

    s0, s1 = _local_slots(plan_ref[...], rank_ref[...], off_ref[...])
    row = _iota((1, LOCAL_ROWS), 1).astype(F32)
    onehot_t = ((row == s0) | (row == s1)).astype(BF16)
    buf_ref[...] = lax.dot_general(onehot_t, h2_ref[...], (((0,), (0,)), ((), ())),
                                   preferred_element_type=F32).astype(BF16)

    def copy(lo, go, size):
        return pltpu.make_async_copy(buf_ref.at[pl.ds(lo, size)], xs_ref.at[pl.ds(go, size)], sem)

    _run_blocks(p8_ref, loff_ref, base_ref, lambda lo, go, size: copy(lo, go, size).start())
    _run_blocks(p8_ref, loff_ref, base_ref, lambda lo, go, size: copy(lo, go, size).wait())


def _tile_scalars():
    return pl.BlockSpec((None, 1, LANES), lambda i, *_: (i, 0, 0), memory_space=pltpu.SMEM)


def _dispatch(tile_valid, p8, loff, base, plan2, rank, off_f, h2, rows):
    t, d = h2.shape
    tt = TOK_TILE
    tok = lambda w: pl.BlockSpec((tt, w), lambda i, tv: (i, 0))
    return pl.pallas_call(
        _dispatch_kernel,
        grid_spec=pltpu.PrefetchScalarGridSpec(
            num_scalar_prefetch=1,
            grid=(t // tt,),
            in_specs=[_tile_scalars(), _tile_scalars(), _tile_scalars(),
                      tok(LANES), tok(LANES),
                      pl.BlockSpec((None, 1, LANES), lambda i, tv: (i, 0, 0)),
                      tok(d)],
            out_specs=pl.BlockSpec(memory_space=pl.ANY),
            scratch_shapes=[pltpu.VMEM((LOCAL_ROWS, d), BF16), pltpu.VMEM((FFN_ROWS, d), BF16),
                            pltpu.SemaphoreType.DMA(()), pltpu.SemaphoreType.DMA(())]),
        out_shape=jax.ShapeDtypeStruct((rows, d), BF16),
        compiler_params=_cparams("arbitrary"),
        name="moe_dispatch",
    )(tile_valid, p8, loff, base, plan2, rank, off_f, h2)


def _ffn_kernel(te_ref, tv_ref, xs_ref, wg_ref, wu_ref, wd_ref, o_ref):
    nv = tv_ref[pl.program_id(0)]

    @pl.when(nv > 0)
    def _():
        x = xs_ref[...]
        gate = jnp.dot(x, wg_ref[...], preferred_element_type=F32)
        up = jnp.dot(x, wu_ref[...], preferred_element_type=F32)
        o_ref[...] = _dot(_silu(gate) * up, wd_ref[...]).astype(BF16)

    @pl.when(nv <= 0)
    def _():
        o_ref[...] = jnp.zeros_like(o_ref)


def _ffn(tile_expert, tile_valid, xs, wg, wu, wd):
    rows, d = xs.shape
    wmap = lambda i, te, tv: (te[i], 0, 0)
    return pl.pallas_call(
        _ffn_kernel,
        grid_spec=pltpu.PrefetchScalarGridSpec(
            num_scalar_prefetch=2,
            grid=(rows // FFN_ROWS,),
            in_specs=[pl.BlockSpec((FFN_ROWS, d), lambda i, te, tv: (i, 0)),
                      pl.BlockSpec((None, d, MOE_FF), wmap),
                      pl.BlockSpec((None, d, MOE_FF), wmap),
                      pl.BlockSpec((None, MOE_FF, d), wmap)],
            out_specs=pl.BlockSpec((FFN_ROWS, d), lambda i, te, tv: (i, 0))),
        out_shape=jax.ShapeDtypeStruct((rows, d), BF16),
        compiler_params=_cparams("arbitrary"),
        name="moe_ffn",
    )(tile_expert, tile_valid, xs, wg, wu, wd)


def _combine_kernel(p8_ref, loff_ref, base_ref, ys_ref, plan_ref, rank_ref, off_ref, x1_ref, gt2_ref, fsc_ref,
                    fsh_ref, fnw_ref, y_ref, buf_ref, sem):
    def copy(lo, go, size):
        return pltpu.make_async_copy(ys_ref.at[pl.ds(go, size)], buf_ref.at[pl.ds(lo, size)], sem)

    _run_blocks(p8_ref, loff_ref, base_ref, lambda lo, go, size: copy(lo, go, size).start())
    plan = plan_ref[...]
    s0, s1 = _local_slots(plan, rank_ref[...], off_ref[...])
    row = _iota((1, LOCAL_ROWS), 1).astype(F32)
    weights = (jnp.where(row == s0, plan[:, PLAN_C0:PLAN_C0 + 1], 0.0)
               + jnp.where(row == s1, plan[:, PLAN_C1:PLAN_C1 + 1], 0.0))
    _run_blocks(p8_ref, loff_ref, base_ref, lambda lo, go, size: copy(lo, go, size).wait())
    total = loff_ref[0, MOE_EXPERTS - 1] + p8_ref[0, MOE_EXPERTS - 1]
    filled = _iota((LOCAL_ROWS, 1), 0) < total
    moe = _dot(weights, jnp.where(filled, buf_ref[...], jnp.zeros((), BF16)))
    x2 = x1_ref[...] + gt2_ref[...] * moe
    y_ref[...] = _rms(x2, fnw_ref[...]) * (1.0 + fsc_ref[...]) + fsh_ref[...]


def _combine(p8, loff, base, ys, plan, rank3, off_f, x1, gt2, fsc, fsh, fnw):
    bx, lx, d = x1.shape
    tt = TOK_TILE
    nl = lx // tt
    tile = lambda b, i: b * nl + i
    scal = pl.BlockSpec((None, 1, LANES), lambda b, i: (tile(b, i), 0, 0), memory_space=pltpu.SMEM)
    tokspec = lambda w: pl.BlockSpec((None, tt, w), lambda b, i: (b, i, 0))
    return pl.pallas_call(
        _combine_kernel,
        grid=(bx, nl),
        in_specs=[scal, scal, scal,
                  pl.BlockSpec(memory_space=pl.ANY),
                  tokspec(LANES), tokspec(LANES),
                  pl.BlockSpec((None, 1, LANES), lambda b, i: (tile(b, i), 0, 0)),
                  tokspec(d), _mod_spec(gt2, tt), _mod_spec(fsc, tt), _mod_spec(fsh, tt),
                  pl.BlockSpec((1, d), lambda b, i: (0, 0))],
        out_specs=tokspec(d),
        out_shape=jax.ShapeDtypeStruct((bx, lx, d), F32),
        scratch_shapes=[pltpu.VMEM((LOCAL_ROWS, d), BF16), pltpu.SemaphoreType.DMA(())],
        compiler_params=_cparams("arbitrary", "arbitrary"),
        name="moe_combine_final",
    )(p8, loff, base, ys, plan, rank3, off_f, x1, gt2, fsc, fsh, fnw.reshape(1, d))


def _moe(h2, plan, wg, wu, wd, x1, gt2, fsc, fsh, fnw):
    bx, lx, d = x1.shape
    t = bx * lx
    assert lx % TOK_TILE == 0
    plan2 = plan.reshape(t, LANES)
    rank, cnt = _plan(plan2, TOK_TILE)
    n = cnt[:, 0, :MOE_EXPERTS].astype(jnp.int32)
    p8 = (n + RUN_ALIGN - 1) // RUN_ALIGN * RUN_ALIGN
    loff = jnp.cumsum(p8, axis=1) - p8
    erows = jnp.sum(p8, axis=0)
    epad = (erows + FFN_ROWS - 1) // FFN_ROWS * FFN_ROWS
    ends = jnp.cumsum(epad)
    starts = ends - epad
    base = starts[None, :] + jnp.cumsum(p8, axis=0) - p8
    lanes = lambda a: jnp.pad(a, ((0, 0), (0, LANES - MOE_EXPERTS)))[:, None, :]
    rows = (-(-(2 * t + (t // TOK_TILE) * MOE_EXPERTS * (RUN_ALIGN - 1)) // FFN_ROWS) + MOE_EXPERTS) * FFN_ROWS
    tile_start = jnp.arange(rows // FFN_ROWS, dtype=jnp.int32) * FFN_ROWS
    tile_expert = jnp.minimum(jnp.sum(tile_start[:, None] >= ends[None, :], axis=1), MOE_EXPERTS - 1).astype(jnp.int32)
    hot = tile_expert[:, None] == jnp.arange(MOE_EXPERTS)[None, :]
    tile_valid = jnp.clip(jnp.sum(jnp.where(hot, erows - (tile_start[:, None] - starts), 0), axis=1),
                          0, FFN_ROWS).astype(jnp.int32)
    p8l, loffl, basel = lanes(p8), lanes(loff), lanes(base)
    off_f = loffl.astype(F32)
    xs = _dispatch(tile_valid, p8l, loffl, basel, plan2, rank, off_f, h2.reshape(t, d), rows)
    ys = _ffn(tile_expert, tile_valid, xs, wg, wu, wd)
    return _combine(p8l, loffl, basel, ys, plan, rank.reshape(bx, lx, LANES), off_f, x1, gt2, fsc, fsh, fnw)


def _prep_layer(lp):
    w_in = lp["w_in"]
    offs = [0]
    for s in (DN_CONV_DIM, DN_VAL, DN_HEADS, DN_HEADS, SSM_CONV_DIM, SSM_INNER, SSM_HEADS, D_MODEL, D_MODEL):
        offs.append(offs[-1] + s)
    seg = lambda i: w_in[:, offs[i]:offs[i + 1]]
    small = jnp.concatenate([seg(2), seg(3), seg(6)], axis=1)
    pad = jnp.zeros((D_MODEL, PROJ_N - COL_SMALL - small.shape[1]), F32)
    w_cat = jnp.concatenate([seg(0), seg(4), seg(5), seg(7), seg(8), seg(1), small, pad], axis=1).astype(BF16)
    wr = jnp.concatenate([lp["w_group_router"], lp["w_expert_router"],
                          jnp.zeros((D_MODEL, LANES - MOE_GROUPS - MOE_EXPERTS), F32)], axis=1)
    br = jnp.concatenate([lp["b_group_router"], lp["b_expert_router"],
                          jnp.zeros((LANES - MOE_GROUPS - MOE_EXPERTS,), F32)]).reshape(1, LANES)
    return dict(lp, w_cat=w_cat, wr=wr, br=br,
                wdn=lp["w_dn_out"].astype(BF16), wssm=lp["w_ssm_out"].astype(BF16), wout=lp["w_out"].astype(BF16),
                wg=lp["w_exp_gate"].astype(BF16), wu=lp["w_exp_up"].astype(BF16), wd=lp["w_exp_down"].astype(BF16))


def _trunk(x3, mods, fins, states, layers, final_norm_w, cfg):
    bx, lx, d = x3.shape
    tm = cfg["tm"]
    scan = dict(bt=cfg["bt"], r=cfg["r"], c=cfg["c"], lc=cfg["lc"], ng=cfg["ng"], cpi=cfg["cpi"])
    new_states = []
    n_layers = len(layers)
    for l, lp in enumerate(layers):
        sh1, sc1, gt1, sh2, sc2, gt2 = mods[l]
        dn_conv, dn_rec, ssm_conv, ssm_rec = states[l]
        proj, small = _inproj(x3, sc1, sh1, lp["norm_mix_w"], lp["w_cat"], tm)
        proj, small = proj.reshape(bx * lx, PROJ_N), small.reshape(bx * lx, LANES)
        og, dn_conv_new, dn_rec_new = _dn_branch(proj, small, lp["dn_conv_w"], lp["dn_A_log"], lp["dn_dt_bias"],
                                                 lp["dn_norm_w"], dn_conv, dn_rec, **scan)
        ys, ssm_conv_new, ssm_rec_new = _ssd_branch(proj, small, lp["ssm_conv_w"], lp["ssm_conv_b"], lp["ssm_A_log"],
                                                    lp["ssm_dt_bias"], lp["ssm_D"], lp["ssm_norm_w"],
                                                    ssm_conv, ssm_rec, **scan)
        x1, h2, plan = _merge(x3, og, ys, proj, gt1, sc2, sh2, lp["norm_ffn_w"], lp["wdn"], lp["wssm"], lp["wout"],
                              lp["wr"], lp["br"], cfg["tm_merge"])
        if l == n_layers - 1:
            fsh, fsc, fnw = fins[0], fins[1], final_norm_w
            x3 = _moe(h2, plan, lp["wg"], lp["wu"], lp["wd"], x1, gt2, fsc, fsh, fnw)
        else:
            raise NotImplementedError("only the last layer fuses the final norm; depth is 1 here")
        new_states.append((dn_conv_new, dn_rec_new, ssm_conv_new, ssm_rec_new))
    return x3, new_states


def _per_seq(m):
    return m[:, None, :]


def kernel(x_prompt, x_sample, c_prompt, c_sample, state_dn_conv, state_dn_rec, state_ssm_conv, state_ssm_rec, w_ada, b_ada, norm_mix_w, w_in, dn_conv_w, dn_A_log, dn_dt_bias, dn_norm_w, w_dn_out, ssm_conv_w, ssm_conv_b, ssm_A_log, ssm_dt_bias, ssm_D, ssm_norm_w, w_ssm_out, w_out, norm_ffn_w, w_group_router, b_group_router, w_expert_router, b_expert_router, w_exp_gate, w_exp_up, w_exp_down, w_ada_final, b_ada_final, final_norm_w):
    depth = w_ada.shape[0]
    assert depth == 1
    per_layer = dict(w_ada=w_ada, b_ada=b_ada, norm_mix_w=norm_mix_w, w_in=w_in, dn_conv_w=dn_conv_w,
                     dn_A_log=dn_A_log, dn_dt_bias=dn_dt_bias, dn_norm_w=dn_norm_w, w_dn_out=w_dn_out,
                     ssm_conv_w=ssm_conv_w, ssm_conv_b=ssm_conv_b, ssm_A_log=ssm_A_log, ssm_dt_bias=ssm_dt_bias,
                     ssm_D=ssm_D, ssm_norm_w=ssm_norm_w, w_ssm_out=w_ssm_out, w_out=w_out, norm_ffn_w=norm_ffn_w,
                     w_group_router=w_group_router, b_group_router=b_group_router,
                     w_expert_router=w_expert_router, b_expert_router=b_expert_router,
                     w_exp_gate=w_exp_gate, w_exp_up=w_exp_up, w_exp_down=w_exp_down)
    layers = [_prep_layer({k: v[l] for k, v in per_layer.items()}) for l in range(depth)]

    nbp, lp_, d = x_prompt.shape
    nbs, ls, _ = x_sample.shape
    c_all = jnp.concatenate([c_prompt, c_sample], axis=0)
    mod_all = [_ada(c_all, lyr["w_ada"], lyr["b_ada"]) for lyr in layers]
    fin_all = _ada(c_all, w_ada_final, b_ada_final)

    mods_p = [[_per_seq(m) for m in jnp.split(ma[:nbp], 6, axis=-1)] for ma in mod_all]
    fins_p = [_per_seq(m) for m in jnp.split(fin_all[:nbp], 2, axis=-1)]
    zeros_p = [(jnp.zeros((nbp, CONV_K - 1, DN_CONV_DIM), F32), jnp.zeros((nbp, DN_HEADS, DN_DK, DN_DV), F32),
                jnp.zeros((nbp, CONV_K - 1, SSM_CONV_DIM), F32),
                jnp.zeros((nbp, SSM_HEADS, SSM_HEADDIM, SSM_STATE), F32)) for _ in range(depth)]
    c_p = min(SCAN_CHUNK, lp_)
    r_p = min(lp_, 8 * c_p)
    cfg_p = dict(tm=min(lp_, 1024), tm_merge=min(lp_, 512), tm_moe=min(lp_, 1024), bt=1, r=r_p, c=c_p, lc=c_p, ng=lp_ // r_p,
                 cpi=4 if (r_p // c_p) % 4 == 0 else 1)
    y_p, st_p = _trunk(x_prompt, mods_p, fins_p, zeros_p, layers, final_norm_w, cfg_p)

    lpad = -(-ls // SUBLANES) * SUBLANES
    bt_s = 8
    xs = jnp.pad(x_sample, ((0, 0), (0, lpad - ls), (0, 0))).reshape(1, nbs * lpad, d)
    per_tok = lambda m: jnp.repeat(m, lpad, axis=0)[None]
    mods_s = [[per_tok(m) for m in jnp.split(ma[nbp:], 6, axis=-1)] for ma in mod_all]
    fins_s = [per_tok(m) for m in jnp.split(fin_all[nbp:], 2, axis=-1)]
    st_in = [(state_dn_conv[l], state_dn_rec[l], state_ssm_conv[l], state_ssm_rec[l]) for l in range(depth)]
    ts = nbs * lpad
    cfg_s = dict(tm=ts, tm_merge=min(ts, 512), tm_moe=min(ts, 512), bt=bt_s, r=lpad, c=lpad, lc=ls, ng=1, cpi=1)
    y_s, st_s = _trunk(xs, mods_s, fins_s, st_in, layers, final_norm_w, cfg_s)
    y_s = y_s.reshape(nbs, lpad, d)[:, :ls]

    stack = lambda sts, i: jnp.stack([s[i] for s in sts])
    return (y_p, y_s, stack(st_p, 0), stack(st_p, 1), stack(st_p, 2), stack(st_p, 3),
            stack(st_s, 0), stack(st_s, 1), stack(st_s, 2), stack(st_s, 3))
```

```python
import functools
import math

import jax
import jax.numpy as jnp
from jax import lax
from jax.experimental import pallas as pl
from jax.experimental.pallas import tpu as pltpu

F32 = jnp.float32
BF16 = jnp.bfloat16

D_MODEL = 1024
DN_HEADS = 4
DN_DK = 128
DN_DV = 128
DN_KEY = DN_HEADS * DN_DK
DN_VAL = DN_HEADS * DN_DV
CONV_K = 4
DN_CONV_DIM = 2 * DN_KEY + DN_VAL
SSM_INNER = D_MODEL
SSM_HEADDIM = 64
SSM_HEADS = SSM_INNER // SSM_HEADDIM
SSM_GROUPS = 2
SSM_STATE = 128
SSM_CONV_DIM = SSM_INNER + 2 * SSM_GROUPS * SSM_STATE
MOE_GROUPS = 4
MOE_PER_GROUP = 8
MOE_EXPERTS = MOE_GROUPS * MOE_PER_GROUP
MOE_FF = D_MODEL // 4
EPS = 1e-6
SCAN_CHUNK = 64

LANES = 128
SUBLANES = 8
CARRY_ROW = SUBLANES - (CONV_K - 1)

COL_DN_QKV = 0
COL_SSM_XBC = COL_DN_QKV + DN_CONV_DIM
COL_SSM_Z = COL_SSM_XBC + SSM_CONV_DIM
COL_GATES = COL_SSM_Z + SSM_INNER
COL_DN_Z = COL_GATES + 2 * D_MODEL
COL_SMALL = COL_DN_Z + DN_VAL
PROJ_TN = 1024
PROJ_N = 7 * PROJ_TN
SM_A, SM_B, SM_DT = 0, DN_HEADS, 2 * DN_HEADS
RT_GROUP, RT_EXPERT = 0, MOE_GROUPS

VMEM_LIMIT = 56 * 1024 * 1024


def _cparams(*sem):
    return pltpu.CompilerParams(dimension_semantics=sem, vmem_limit_bytes=VMEM_LIMIT)


def _dot(a, b):
    return jnp.dot(a.astype(BF16), b.astype(BF16), preferred_element_type=F32)


def _dot_nt(a, b):
    return lax.dot_general(a.astype(BF16), b.astype(BF16), (((1,), (1,)), ((), ())),
                           preferred_element_type=F32)


def _dot_tn(a, b):
    return lax.dot_general(a.astype(BF16), b.astype(BF16), (((0,), (0,)), ((), ())),
                           preferred_element_type=F32)


def _split3(x):
    hi = x.astype(BF16)
    r = x - hi.astype(F32)
    mid = r.astype(BF16)
    lo = (r - mid.astype(F32)).astype(BF16)
    return hi, mid, lo


def _dot_exact_lhs(a_bf, b):
    hi, mid, lo = _split3(b)
    d = functools.partial(jnp.dot, preferred_element_type=F32)
    return d(a_bf, hi) + (d(a_bf, mid) + d(a_bf, lo))


def _dot_x3(a, b):
    a_hi = a.astype(BF16)
    a_lo = (a - a_hi.astype(F32)).astype(BF16)
    b_hi = b.astype(BF16)
    b_lo = (b - b_hi.astype(F32)).astype(BF16)
    d = functools.partial(jnp.dot, preferred_element_type=F32)
    return d(a_hi, b_hi) + (d(a_hi, b_lo) + d(a_lo, b_hi))


def _sigmoid(x):
    return 1.0 / (1.0 + jnp.exp(-x))


def _silu(x):
    return x * _sigmoid(x)


def _softplus(x):
    return jnp.maximum(x, 0.0) + jnp.log1p(jnp.exp(-jnp.abs(x)))


def _rms(x, w):
    return x * lax.rsqrt(jnp.mean(x * x, axis=-1, keepdims=True) + EPS) * w


def _iota(shape, dim):
    return lax.broadcasted_iota(jnp.int32, shape, dim)


def _ada_kernel(c_ref, w_ref, b_ref, o_ref):
    o_ref[...] = _dot(_silu(c_ref[...]), w_ref[...]) + b_ref[...]


def _ada(c, w, b, tn=512):
    m, d = c.shape
    n = w.shape[1]
    return pl.pallas_call(
        _ada_kernel,
        grid=(n // tn,),
        in_specs=[pl.BlockSpec((m, d), lambda j: (0, 0)),
                  pl.BlockSpec((d, tn), lambda j: (0, j)),
                  pl.BlockSpec((1, tn), lambda j: (0, j))],
        out_specs=pl.BlockSpec((m, tn), lambda j: (0, j)),
        out_shape=jax.ShapeDtypeStruct((m, n), F32),
        compiler_params=_cparams("arbitrary"),
        name="ada_mod",
    )(c, w, b.reshape(1, n))


def _mod_spec(mod, tm):
    if mod.shape[1] == 1:
        return pl.BlockSpec((None, 1, D_MODEL), lambda b, i, *_: (b, 0, 0))
    return pl.BlockSpec((None, tm, D_MODEL), lambda b, i, *_: (b, i, 0))


def _inproj_kernel(x_ref, sc_ref, sh_ref, nw_ref, w_ref, o_ref, sm_ref, h_ref, *, tm, sub):
    @pl.when(pl.program_id(2) == 0)
    def _():
        per_token = sc_ref.shape[0] != 1

        def body(r, carry):
            rows = pl.ds(pl.multiple_of(r * sub, sub), sub)
            sc = sc_ref[rows, :] if per_token else sc_ref[...]
            sh = sh_ref[rows, :] if per_token else sh_ref[...]
            h = _rms(x_ref[rows, :], nw_ref[...]) * (1.0 + sc) + sh
            h_ref[rows, :] = h.astype(BF16)
            return carry

        lax.fori_loop(0, tm // sub, body, 0)

    acc = jnp.dot(h_ref[...], w_ref[...], preferred_element_type=F32)
    o_ref[...] = acc.astype(BF16)

    @pl.when(pl.program_id(2) == COL_SMALL // PROJ_TN)
    def _():
        sm_ref[...] = acc[:, COL_SMALL % PROJ_TN:COL_SMALL % PROJ_TN + LANES]


def _inproj(x3, sc, sh, nw, w_cat, tm):
    bx, lx, d = x3.shape
    n = w_cat.shape[1]
    kern = functools.partial(_inproj_kernel, tm=tm, sub=min(tm, 256))
    return pl.pallas_call(
        kern,
        grid=(bx, lx // tm, n // PROJ_TN),
        in_specs=[pl.BlockSpec((None, tm, d), lambda b, i, j: (b, i, 0)),
                  _mod_spec(sc, tm), _mod_spec(sh, tm),
                  pl.BlockSpec((1, d), lambda b, i, j: (0, 0)),
                  pl.BlockSpec((d, PROJ_TN), lambda b, i, j: (0, j))],
        out_specs=[pl.BlockSpec((None, tm, PROJ_TN), lambda b, i, j: (b, i, j)),
                   pl.BlockSpec((None, tm, LANES), lambda b, i, j: (b, i, 0))],
        out_shape=[jax.ShapeDtypeStruct((bx, lx, n), BF16), jax.ShapeDtypeStruct((bx, lx, LANES), F32)],
        scratch_shapes=[pltpu.VMEM((tm, d), BF16)],
        compiler_params=_cparams("arbitrary", "arbitrary", "arbitrary"),
        name="norm_inproj",
    )(x3, sc, sh, nw.reshape(1, d), w_cat)


def _conv_silu(u, ext_ref, bb, cw_ref, c, lc, bias=None):
    ext_ref[bb, SUBLANES:SUBLANES + c, :] = u
    y = ext_ref[bb, CARRY_ROW:CARRY_ROW + c, :] * cw_ref[0:1, :]
    for i in range(1, CONV_K):
        y = y + ext_ref[bb, CARRY_ROW + i:CARRY_ROW + i + c, :] * cw_ref[i:i + 1, :]
    ext_ref[bb, CARRY_ROW:SUBLANES, :] = ext_ref[bb, CARRY_ROW + lc:SUBLANES + lc, :]
    if bias is not None:
        y = y + bias
    return _silu(y)


def _chunk_reader(ref, r, c):
    if r == c:
        whole = ref[...].astype(F32)
        return lambda bb, r0: whole[bb * c:(bb + 1) * c]
    return lambda bb, r0: ref[pl.ds(bb * r + r0, c), :].astype(F32)


def _chunk_writer(ref, bt, r, c):
    if r != c:
        return (lambda bb, r0, val: ref.__setitem__((pl.ds(bb * r + r0, c), slice(None)), val.astype(ref.dtype)),
                lambda: None)
    parts = {}

    def flush():
        ref[...] = jnp.concatenate([parts[bb] for bb in range(bt)], axis=0).astype(ref.dtype)
    return (lambda bb, r0, val: parts.__setitem__(bb, val)), flush


def _run_rounds(process, bt, r, c, cpi):
    g = r // c
    if g == 1:
        process([[(bb, 0) for bb in range(bt)]])
    else:
        def body(ci, carry):
            process([[(bb, pl.multiple_of((ci * cpi + t) * c, c)) for bb in range(bt)] for t in range(cpi)])
            return carry
        lax.fori_loop(0, g // cpi, body, 0)


INV_BASE = SUBLANES


def _inverse_masks(row, col, c):
    sh = lambda x, s: jnp.right_shift(x, int(math.log2(s)))
    diag = sh(row, INV_BASE) == sh(col, INV_BASE)
    merges = []
    s = INV_BASE
    while s < c:
        merges.append((sh(row, 2 * s) == sh(col, 2 * s))
                      & (jnp.bitwise_and(sh(row, s), 1) == 1) & (jnp.bitwise_and(sh(col, s), 1) == 0))
        s *= 2
    return diag, merges


def _unit_lower_inverses(lmats, eye_f, masks):
    diag, merges = masks
    npows = [-jnp.where(diag, l, 0.0) for l in lmats]
    ps = [eye_f + n for n in npows]
    for _ in range(int(math.log2(INV_BASE)) - 1):
        npows = [_dot(n, n) for n in npows]
        ps = [p + _dot(p, n) for p, n in zip(ps, npows)]
    for m in merges:
        ts = [_dot(p, jnp.where(m, l, 0.0)) for p, l in zip(ps, lmats)]
        ps = [p - _dot(t, p) for t, p in zip(ts, ps)]
    return ps


def _dn_kernel(qkv_ref, z_ref, sm_ref, cw_ref, alog_ref, bias_ref, nw_ref, cin_ref, sin_ref,
               o_ref, cout_ref, sout_ref, ext_ref, *, bt, r, c, lc, ng, cpi):
    j = pl.program_id(1)

    @pl.when(j == 0)
    def _():
        ext_ref[:, CARRY_ROW:SUBLANES, :] = cin_ref[...]
        sout_ref[...] = sin_ref[...]

    row = _iota((c, c), 0)
    col = _iota((c, c), 1)
    tri_incl = row >= col
    tri_strict = row > col
    eye_f = (row == col).astype(F32)
    tril_bf = tri_incl.astype(BF16)
    valid = _iota((c, 1), 0) < lc
    nexp_a = -jnp.exp(alog_ref[...])
    bias = bias_ref[...]
    inv_masks = _inverse_masks(row, col, c)
    heads = range(DN_HEADS)
    read_qkv, read_z, read_sm = (_chunk_reader(ref, r, c) for ref in (qkv_ref, z_ref, sm_ref))
    put_o, flush_o = _chunk_writer(o_ref, bt, r, c)

    def prep(bb, r0):
        y = _conv_silu(read_qkv(bb, r0), ext_ref, bb, cw_ref, c, lc)
        sm = read_sm(bb, r0)
        g_all = nexp_a * _softplus(sm + bias)
        beta_all = _sigmoid(sm)
        if lc < c:
            g_all = jnp.where(valid, g_all, 0.0)
            beta_all = jnp.where(valid, beta_all, 0.0)
        cs_all = _dot_exact_lhs(tril_bf, g_all)
        cs_t = cs_all.T
        ecs_all = jnp.exp(cs_all)
        per_head = []
        for h in heads:
            q = y[:, h * DN_DK:(h + 1) * DN_DK]
            k = y[:, DN_KEY + h * DN_DK:DN_KEY + (h + 1) * DN_DK]
            v = y[:, 2 * DN_KEY + h * DN_DV:2 * DN_KEY + (h + 1) * DN_DV]
            q = q * lax.rsqrt(jnp.sum(q * q, axis=-1, keepdims=True) + EPS) * (DN_DK ** -0.5)
            k = k * lax.rsqrt(jnp.sum(k * k, axis=-1, keepdims=True) + EPS)
            if lc < c:
                k = jnp.where(valid, k, 0.0)
            cs = cs_all[:, SM_A + h:SM_A + h + 1]
            ecs = ecs_all[:, SM_A + h:SM_A + h + 1]
            beta = beta_all[:, SM_B + h:SM_B + h + 1]
            last = cs_all[c - 1:c, SM_A + h:SM_A + h + 1]
            decay = jnp.exp(jnp.where(tri_incl, cs - cs_t[SM_A + h:SM_A + h + 1, :], -jnp.inf))
            per_head.append(dict(q_ecs=(q * ecs).astype(BF16), q=q.astype(BF16), k=k.astype(BF16), beta=beta,
                                 decay=decay, elast=jnp.exp(last),
                                 k_tail=(k * jnp.exp(last - cs)).astype(BF16),
                                 rhs=jnp.concatenate([v * beta, k * (beta * ecs)], axis=1).astype(BF16)))
        return per_head

    def process(rounds):
        items = [it for rnd in rounds for it in rnd]
        pre = [prep(bb, r0) for bb, r0 in items]
        chains = [(i, h) for i in range(len(items)) for h in heads]
        a = {ch: pre[ch[0]][ch[1]] for ch in chains}
        kk = {ch: _dot_nt(a[ch]["k"], a[ch]["k"]) for ch in chains}
        qk = {ch: _dot_nt(a[ch]["q"], a[ch]["k"]) * a[ch]["decay"] for ch in chains}
        lmats = [jnp.where(tri_strict, a[ch]["beta"] * kk[ch] * a[ch]["decay"], 0.0) for ch in chains]
        pinv = _unit_lower_inverses(lmats, eye_f, inv_masks)
        sol = {ch: _dot(p, a[ch]["rhs"]) for ch, p in zip(chains, pinv)}
        first = 0
        for rnd in rounds:
            idx = range(first, first + len(rnd))
            first += len(rnd)
            rch = [(i, h) for i in idx for h in heads]
            s = {ch: sout_ref[items[ch[0]][0], ch[1]] for ch in rch}
            ws_qs = {ch: _dot(jnp.concatenate([sol[ch][:, DN_DV:].astype(BF16), a[ch]["q_ecs"]], axis=0), s[ch])
                     for ch in rch}
            v_new = {ch: (sol[ch][:, :DN_DV] - ws_qs[ch][:c]).astype(BF16) for ch in rch}
            o = {ch: ws_qs[ch][c:] + _dot(qk[ch], v_new[ch]) for ch in rch}
            for ch in rch:
                sout_ref[items[ch[0]][0], ch[1]] = a[ch]["elast"] * s[ch] + _dot_tn(a[ch]["k_tail"], v_new[ch])
            for i in idx:
                bb, r0 = items[i]
                z = read_z(bb, r0)
                put_o(bb, r0, jnp.concatenate(
                    [_rms(o[(i, h)], nw_ref[...]) * _silu(z[:, h * DN_DV:(h + 1) * DN_DV]) for h in heads], axis=1))

    _run_rounds(process, bt, r, c, cpi)
    flush_o()

    @pl.when(j == ng - 1)
    def _():
        cout_ref[...] = ext_ref[:, CARRY_ROW:SUBLANES, :]


def _small_row(vals, offset, fill=0.0):
    row = jnp.full((1, LANES), fill, F32)
    return lax.dynamic_update_slice(row, vals.astype(F32).reshape(1, -1), (0, offset))


def _dn_branch(proj, small, conv_w, a_log, dt_bias, norm_w, conv_in, rec_in, *, bt, r, c, lc, ng, cpi):
    nb = conv_in.shape[0] // bt
    rows = bt * r
    t = proj.shape[0]
    kern = functools.partial(_dn_kernel, bt=bt, r=r, c=c, lc=lc, ng=ng, cpi=cpi)
    rowmap = lambda col: (lambda i, j: (i * ng + j, col))
    const = lambda i, j: (0, 0)
    return pl.pallas_call(
        kern,
        grid=(nb, ng),
        in_specs=[pl.BlockSpec((rows, DN_CONV_DIM), rowmap(COL_DN_QKV // DN_CONV_DIM)),
                  pl.BlockSpec((rows, DN_VAL), rowmap(COL_DN_Z // DN_VAL)),
                  pl.BlockSpec((rows, LANES), rowmap(0)),
                  pl.BlockSpec((CONV_K, DN_CONV_DIM), const),
                  pl.BlockSpec((1, LANES), const),
                  pl.BlockSpec((1, LANES), const),
                  pl.BlockSpec((1, DN_DV), const),
                  pl.BlockSpec((bt, CONV_K - 1, DN_CONV_DIM), lambda i, j: (i, 0, 0)),
                  pl.BlockSpec((bt, DN_HEADS, DN_DK, DN_DV), lambda i, j: (i, 0, 0, 0))],
        out_specs=[pl.BlockSpec((rows, DN_VAL), lambda i, j: (i * ng + j, 0)),
                   pl.BlockSpec((bt, CONV_K - 1, DN_CONV_DIM), lambda i, j: (i, 0, 0)),
                   pl.BlockSpec((bt, DN_HEADS, DN_DK, DN_DV), lambda i, j: (i, 0, 0, 0))],
        out_shape=[jax.ShapeDtypeStruct((t, DN_VAL), BF16),
                   jax.ShapeDtypeStruct(conv_in.shape, F32),
                   jax.ShapeDtypeStruct(rec_in.shape, F32)],
        scratch_shapes=[pltpu.VMEM((bt, c + SUBLANES, DN_CONV_DIM), F32)],
        compiler_params=_cparams("arbitrary", "arbitrary"),
        name="gated_delta",
    )(proj, proj, small, conv_w, _small_row(a_log, SM_A), _small_row(dt_bias, SM_A),
      norm_w.reshape(1, DN_DV), conv_in, rec_in)


def _ssd_kernel(xbc_ref, z_ref, sm_ref, cw_ref, cb_ref, alog_ref, bias_ref, dvec_ref, nw_ref, cin_ref,
                hin_ref, y_ref, cout_ref, hout_ref, ext_ref, *, bt, r, c, lc, ng, cpi):
    j = pl.program_id(1)

    @pl.when(j == 0)
    def _():
        ext_ref[:, CARRY_ROW:SUBLANES, :] = cin_ref[...]
        hout_ref[...] = hin_ref[...]

    tril_bf = (_iota((c, c), 0) >= _iota((c, c), 1)).astype(BF16)
    row2 = _iota((c, 2 * c), 0)
    lane2 = _iota((c, 2 * c), 1)
    left2 = lane2 < c
    tri2 = row2 >= jnp.where(left2, lane2, lane2 - c)
    left_x = _iota((c, LANES), 1) < SSM_HEADDIM
    top_h = _iota((2 * SSM_HEADDIM, 1), 0) < SSM_HEADDIM
    valid = _iota((c, 1), 0) < lc
    nexp_a = -jnp.exp(alog_ref[...])
    bias = bias_ref[...]
    gw = SSM_INNER // SSM_GROUPS
    pairs_per_group = SSM_HEADS // SSM_GROUPS // 2

    pairs = range(SSM_HEADS // 2)
    pair_rows = lambda p: slice(p * LANES, (p + 1) * LANES)
    read_xbc, read_z, read_sm = (_chunk_reader(ref, r, c) for ref in (xbc_ref, z_ref, sm_ref))
    put_y, flush_y = _chunk_writer(y_ref, bt, r, c)

    def prep(bb, r0):
        y = _conv_silu(read_xbc(bb, r0), ext_ref, bb, cw_ref, c, lc, bias=cb_ref[...])
        dt_all = _softplus(read_sm(bb, r0) + bias)
        if lc < c:
            dt_all = jnp.where(valid, dt_all, 0.0)
        cs_all = _dot_exact_lhs(tril_bf, dt_all * nexp_a)
        cs_t2 = jnp.concatenate([cs_all, cs_all], axis=0).T
        last = cs_all[c - 1:c, :]
        bm = [y[:, SSM_INNER + g * SSM_STATE:SSM_INNER + (g + 1) * SSM_STATE].astype(BF16)
              for g in range(SSM_GROUPS)]
        cm = [y[:, SSM_INNER + (SSM_GROUPS + g) * SSM_STATE:SSM_INNER + (SSM_GROUPS + g + 1) * SSM_STATE].astype(BF16)
              for g in range(SSM_GROUPS)]
        return dict(y=y, dt=dt_all, cs=cs_all, cs_t2=cs_t2, ecs=jnp.exp(cs_all),
                    tail=jnp.exp(last - cs_all), elast=jnp.exp(last), bm=bm, cm=cm)

    def process(rounds):
        items = [it for rnd in rounds for it in rnd]
        pre = [prep(bb, r0) for bb, r0 in items]
        ids = range(len(items))
        cb2 = {(i, g): _dot_nt(pre[i]["cm"][g], jnp.concatenate([pre[i]["bm"][g]] * 2, axis=0))
               for i in ids for g in range(SSM_GROUPS)}
        y_intra, upd, xs_of = {}, {}, {}
        for i in ids:
            a = pre[i]
            for p in pairs:
                g = p // pairs_per_group
                la, lb = SM_DT + 2 * p, SM_DT + 2 * p + 1
                both = lambda v, mask=left_x: jnp.where(mask, v[:, la:la + 1], v[:, lb:lb + 1])
                xs = a["y"][:, pair_rows(p)]
                xdt = xs * both(a["dt"])
                diff = both(a["cs"], left2) - jnp.where(left2[0:1], a["cs_t2"][la:la + 1, :], a["cs_t2"][lb:lb + 1, :])
                decay2 = jnp.exp(jnp.where(tri2, diff, -jnp.inf))
                rhs = jnp.concatenate([jnp.where(left_x, xdt, 0.0), jnp.where(left_x, 0.0, xdt)], axis=0)
                y_intra[i, p] = _dot(cb2[i, g] * decay2, rhs)
                upd[i, p] = _dot_tn(xdt * both(a["tail"]), a["bm"][g])
                xs_of[i, p] = xs
        first = 0
        for rnd in rounds:
            idx = range(first, first + len(rnd))
            first += len(rnd)
            for i in idx:
                a = pre[i]
                bb, r0 = items[i]
                outs = []
                for p in pairs:
                    g = p // pairs_per_group
                    la, lb = SM_DT + 2 * p, SM_DT + 2 * p + 1
                    hs = hout_ref[bb, pair_rows(p), :]
                    y_inter = _dot_nt(a["cm"][g], hs) * jnp.where(left_x, a["ecs"][:, la:la + 1], a["ecs"][:, lb:lb + 1])
                    hout_ref[bb, pair_rows(p), :] = (
                        jnp.where(top_h, a["elast"][:, la:la + 1], a["elast"][:, lb:lb + 1]) * hs + upd[i, p])
                    outs.append(y_intra[i, p] + y_inter + dvec_ref[:, pair_rows(p)] * xs_of[i, p])
                yz = jnp.concatenate(outs, axis=1) * _silu(read_z(bb, r0))
                put_y(bb, r0, jnp.concatenate(
                    [_rms(yz[:, g * gw:(g + 1) * gw], nw_ref[:, g * gw:(g + 1) * gw]) for g in range(SSM_GROUPS)],
                    axis=1))

    _run_rounds(process, bt, r, c, cpi)
    flush_y()

    @pl.when(j == ng - 1)
    def _():
        cout_ref[...] = ext_ref[:, CARRY_ROW:SUBLANES, :]


def _ssd_branch(proj, small, conv_w, conv_b, a_log, dt_bias, d_skip, norm_w, conv_in, rec_in, *, bt, r, c, lc, ng, cpi):
    nb = conv_in.shape[0] // bt
    rows = bt * r
    t = proj.shape[0]
    hrows = SSM_HEADS * SSM_HEADDIM
    rec2 = rec_in.reshape(rec_in.shape[0], hrows, SSM_STATE)
    kern = functools.partial(_ssd_kernel, bt=bt, r=r, c=c, lc=lc, ng=ng, cpi=cpi)
    rowmap = lambda col: (lambda i, j: (i * ng + j, col))
    const = lambda i, j: (0, 0)
    y, conv_out, rec_out = pl.pallas_call(
        kern,
        grid=(nb, ng),
        in_specs=[pl.BlockSpec((rows, SSM_CONV_DIM), rowmap(COL_SSM_XBC // SSM_CONV_DIM)),
                  pl.BlockSpec((rows, SSM_INNER), rowmap(COL_SSM_Z // SSM_INNER)),
                  pl.BlockSpec((rows, LANES), rowmap(0)),
                  pl.BlockSpec((CONV_K, SSM_CONV_DIM), const),
                  pl.BlockSpec((1, SSM_CONV_DIM), const),
                  pl.BlockSpec((1, LANES), const),
                  pl.BlockSpec((1, LANES), const),
                  pl.BlockSpec((1, SSM_INNER), const),
                  pl.BlockSpec((1, SSM_INNER), const),
                  pl.BlockSpec((bt, CONV_K - 1, SSM_CONV_DIM), lambda i, j: (i, 0, 0)),
                  pl.BlockSpec((bt, hrows, SSM_STATE), lambda i, j: (i, 0, 0))],
        out_specs=[pl.BlockSpec((rows, SSM_INNER), lambda i, j: (i * ng + j, 0)),
                   pl.BlockSpec((bt, CONV_K - 1, SSM_CONV_DIM), lambda i, j: (i, 0, 0)),
                   pl.BlockSpec((bt, hrows, SSM_STATE), lambda i, j: (i, 0, 0))],
        out_shape=[jax.ShapeDtypeStruct((t, SSM_INNER), BF16),
                   jax.ShapeDtypeStruct(conv_in.shape, F32),
                   jax.ShapeDtypeStruct(rec2.shape, F32)],
        scratch_shapes=[pltpu.VMEM((bt, c + SUBLANES, SSM_CONV_DIM), F32)],
        compiler_params=_cparams("arbitrary", "arbitrary"),
        name="ssd_scan",
    )(proj, proj, small, conv_w, conv_b.reshape(1, -1), _small_row(a_log, SM_DT), _small_row(dt_bias, SM_DT),
      jnp.repeat(d_skip.astype(F32), SSM_HEADDIM).reshape(1, SSM_INNER), norm_w.reshape(1, SSM_INNER),
      conv_in, rec2)
    return y, conv_out, rec_out.reshape(rec_in.shape)


PLAN_E0, PLAN_E1, PLAN_C0, PLAN_C1 = 0, 1, 2, 3


def _route(logits):
    lane = _iota(logits.shape, 1).astype(F32)
    big = float(LANES)
    is_group = (lane >= RT_GROUP) & (lane < RT_GROUP + MOE_GROUPS)
    gl = jnp.where(is_group, logits, -jnp.inf)
    gmax = jnp.max(gl, axis=-1, keepdims=True)
    g_sel = jnp.min(jnp.where(gl == gmax, lane, big), axis=-1, keepdims=True) - RT_GROUP
    p_group = 1.0 / jnp.sum(jnp.exp(gl - gmax), axis=-1, keepdims=True)
    e_lo = RT_EXPERT + MOE_PER_GROUP * g_sel
    in_grp = (lane >= e_lo) & (lane < e_lo + MOE_PER_GROUP)
    el = jnp.where(in_grp, logits, -jnp.inf)
    ee = jnp.exp(el - jnp.max(el, axis=-1, keepdims=True))
    pe = jnp.where(in_grp, ee / jnp.sum(ee, axis=-1, keepdims=True), -1.0)
    p1 = jnp.max(pe, axis=-1, keepdims=True)
    i1 = jnp.min(jnp.where(pe == p1, lane, big), axis=-1, keepdims=True)
    pe2 = jnp.where(lane == i1, -1.0, pe)
    p2 = jnp.max(pe2, axis=-1, keepdims=True)
    i2 = jnp.min(jnp.where(pe2 == p2, lane, big), axis=-1, keepdims=True)
    tot = p1 + p2
    plan = jnp.where(lane == PLAN_E0, i1 - RT_EXPERT, 0.0) + jnp.where(lane == PLAN_E1, i2 - RT_EXPERT, 0.0)
    return plan + jnp.where(lane == PLAN_C0, p_group * p1 / tot, 0.0) + jnp.where(lane == PLAN_C1, p_group * p2 / tot, 0.0)


def _merge_kernel(x_ref, og_ref, ys_ref, gates_ref, gt1_ref, sc2_ref, sh2_ref, nw2_ref, wdn_ref, wssm_ref,
                  wout_ref, wr_ref, br_ref, x1_ref, h2_ref, plan_ref):
    y_dn = _dot(og_ref[...], wdn_ref[...])
    y_ssm = _dot(ys_ref[...], wssm_ref[...])
    merged = (_sigmoid(gates_ref[:, :D_MODEL].astype(F32)) * y_dn
              + _sigmoid(gates_ref[:, D_MODEL:].astype(F32)) * y_ssm)
    x1 = x_ref[...] + gt1_ref[...] * _dot(merged, wout_ref[...])
    x1_ref[...] = x1
    h2 = _rms(x1, nw2_ref[...]) * (1.0 + sc2_ref[...]) + sh2_ref[...]
    h2_ref[...] = h2.astype(BF16)
    plan_ref[...] = _route(_dot_x3(h2, wr_ref[...]) + br_ref[...])


def _merge(x3, og, ys, proj, gt1, sc2, sh2, nw2, wdn, wssm, wout, wr, br, tm):
    bx, lx, d = x3.shape
    nl = lx // tm
    rowmap = lambda col: (lambda b, i: (b * nl + i, col))
    const = lambda b, i: (0, 0)
    tok = lambda w, dt: jax.ShapeDtypeStruct((bx, lx, w), dt)
    return pl.pallas_call(
        _merge_kernel,
        grid=(bx, nl),
        in_specs=[pl.BlockSpec((None, tm, d), lambda b, i: (b, i, 0)),
                  pl.BlockSpec((tm, DN_VAL), rowmap(0)),
                  pl.BlockSpec((tm, SSM_INNER), rowmap(0)),
                  pl.BlockSpec((tm, 2 * d), rowmap(COL_GATES // (2 * d))),
                  _mod_spec(gt1, tm), _mod_spec(sc2, tm), _mod_spec(sh2, tm),
                  pl.BlockSpec((1, d), const),
                  pl.BlockSpec(wdn.shape, const), pl.BlockSpec(wssm.shape, const), pl.BlockSpec(wout.shape, const),
                  pl.BlockSpec(wr.shape, const), pl.BlockSpec((1, LANES), const)],
        out_specs=[pl.BlockSpec((None, tm, d), lambda b, i: (b, i, 0)),
                   pl.BlockSpec((None, tm, d), lambda b, i: (b, i, 0)),
                   pl.BlockSpec((None, tm, LANES), lambda b, i: (b, i, 0))],
        out_shape=[tok(d, F32), tok(d, BF16), tok(LANES, F32)],
        compiler_params=_cparams("arbitrary", "arbitrary"),
        name="merge_route",
    )(x3, og, ys, proj, gt1, sc2, sh2, nw2.reshape(1, d), wdn, wssm, wout, wr, br)


FFN_ROWS = 512
TOK_TILE = 512
RUN_ALIGN = 2 * SUBLANES
RUN_BIG = 4 * RUN_ALIGN
RUN_SMALL = (2 * RUN_ALIGN, RUN_ALIGN)
LOCAL_ROWS = 2 * TOK_TILE + MOE_EXPERTS * RUN_ALIGN


def _plan_kernel(plan_ref, rank_ref, cnt_ref):
    plan = plan_ref[...]
    tp = plan.shape[0]
    lane = _iota(plan.shape, 1).astype(F32)
    sel0 = lane == plan[:, PLAN_E0:PLAN_E0 + 1]
    sel1 = lane == plan[:, PLAN_E1:PLAN_E1 + 1]
    sel = (sel0 | sel1).astype(BF16)
    before = (_iota((tp, tp), 0) > _iota((tp, tp), 1)).astype(BF16)
    excl = jnp.dot(before, sel, preferred_element_type=F32)
    r0 = jnp.sum(jnp.where(sel0, excl, 0.0), axis=-1, keepdims=True)
    r1 = jnp.sum(jnp.where(sel1, excl, 0.0), axis=-1, keepdims=True)
    rank_ref[...] = jnp.where(lane == PLAN_E0, r0, 0.0) + jnp.where(lane == PLAN_E1, r1, 0.0)
    cnt_ref[...] = jnp.sum(sel.astype(F32), axis=0, keepdims=True)


def _plan(plan2, tp):
    t = plan2.shape[0]
    return pl.pallas_call(
        _plan_kernel,
        grid=(t // tp,),
        in_specs=[pl.BlockSpec((tp, LANES), lambda i: (i, 0))],
        out_specs=[pl.BlockSpec((tp, LANES), lambda i: (i, 0)), pl.BlockSpec((None, 1, LANES), lambda i: (i, 0, 0))],
        out_shape=[jax.ShapeDtypeStruct((t, LANES), F32), jax.ShapeDtypeStruct((t // tp, 1, LANES), F32)],
        compiler_params=_cparams("arbitrary"),
        name="moe_plan",
    )(plan2)


def _local_slots(plan, rank, off_row):
    lane = _iota(plan.shape, 1).astype(F32)
    slots = []
    for k in (PLAN_E0, PLAN_E1):
        off = jnp.sum(jnp.where(lane == plan[:, k:k + 1], off_row, 0.0), axis=-1, keepdims=True)
        slots.append(off + rank[:, k:k + 1])
    return slots


def _run_blocks(p8_ref, loff_ref, base_ref, act):
    for e in range(MOE_EXPERTS):
        n = p8_ref[0, e]
        lo = loff_ref[0, e]
        go = base_ref[0, e]

        def big(k, carry, lo=lo, go=go):
            off = k * RUN_BIG
            act(pl.multiple_of(lo + off, RUN_ALIGN), pl.multiple_of(go + off, RUN_ALIGN), RUN_BIG)
            return carry

        lax.fori_loop(0, lax.shift_right_logical(n, RUN_BIG.bit_length() - 1), big, 0)
        for size in RUN_SMALL:
            @pl.when(jnp.bitwise_and(n, size) != 0)
            def _(size=size, n=n, lo=lo, go=go):
                done = jnp.bitwise_and(n, ~(2 * size - 1))
                act(pl.multiple_of(lo + done, RUN_ALIGN), pl.multiple_of(go + done, RUN_ALIGN), size)


def _dispatch_kernel(tv_ref, p8_ref, loff_ref, base_ref, plan_ref, rank_ref, off_ref, h2_ref, xs_ref,
                     buf_ref, zero_ref, sem, zsem):
    @pl.when(pl.program_id(0) == 0)
    def _():
        zero_ref[...] = jnp.zeros_like(zero_ref)

        def zero_copy(i):
            rows = pl.ds(pl.multiple_of(i * FFN_ROWS, FFN_ROWS), FFN_ROWS)
            return pltpu.make_async_copy(zero_ref, xs_ref.at[rows], zsem)

        def zero_tiles(act):
            def body(i, carry):
                @pl.when(tv_ref[i] < FFN_ROWS)
                def _():
                    act(zero_copy(i))
                return carry
            lax.fori_loop(0, xs_ref.shape[0] // FFN_ROWS, body, 0)

        zero_tiles(lambda cp: cp.start())
        zero_tiles(lambda cp: cp.wait())

    s0, s1 = _local_slots(plan_ref[...], rank_ref[...], off_ref[...])
    row = _iota((1, LOCAL_ROWS), 1).astype(F32)
    onehot_t = ((row == s0) | (row == s1)).astype(BF16)
    buf_ref[...] = lax.dot_general(onehot_t, h2_ref[...], (((0,), (0,)), ((), ())),
                                   preferred_element_type=F32).astype(BF16)

    def copy(lo, go, size):
        return pltpu.make_async_copy(buf_ref.at[pl.ds(lo, size)], xs_ref.at[pl.ds(go, size)], sem)

    _run_blocks(p8_ref, loff_ref, base_ref, lambda lo, go, size: copy(lo, go, size).start())
    _run_blocks(p8_ref, loff_ref, base_ref, lambda lo, go, size: copy(lo, go, size).wait())


def _tile_scalars():
    return pl.BlockSpec((None, 1, LANES), lambda i, *_: (i, 0, 0), memory_space=pltpu.SMEM)


def _dispatch(tile_valid, p8, loff, base, plan2, rank, off_f, h2, rows):
    t, d = h2.shape
    tt = TOK_TILE
    tok = lambda w: pl.BlockSpec((tt, w), lambda i, tv: (i, 0))
    return pl.pallas_call(
        _dispatch_kernel,
        grid_spec=pltpu.PrefetchScalarGridSpec(
            num_scalar_prefetch=1,
            grid=(t // tt,),
            in_specs=[_tile_scalars(), _tile_scalars(), _tile_scalars(),
                      tok(LANES), tok(LANES),
                      pl.BlockSpec((None, 1, LANES), lambda i, tv: (i, 0, 0)),
                      tok(d)],
            out_specs=pl.BlockSpec(memory_space=pl.ANY),
            scratch_shapes=[pltpu.VMEM((LOCAL_ROWS, d), BF16), pltpu.VMEM((FFN_ROWS, d), BF16),
                            pltpu.SemaphoreType.DMA(()), pltpu.SemaphoreType.DMA(())]),
        out_shape=jax.ShapeDtypeStruct((rows, d), BF16),
        compiler_params=_cparams("arbitrary"),
        name="moe_dispatch",
    )(tile_valid, p8, loff, base, plan2, rank, off_f, h2)


def _ffn_kernel(te_ref, tv_ref, xs_ref, wg_ref, wu_ref, wd_ref, o_ref):
    nv = tv_ref[pl.program_id(0)]

    half = FFN_ROWS // 2

    def swiglu(rows):
        x = xs_ref[rows, :]
        gate = jnp.dot(x, wg_ref[...], preferred_element_type=F32)
        up = jnp.dot(x, wu_ref[...], preferred_element_type=F32)
        o_ref[rows, :] = _dot(_silu(gate) * up, wd_ref[...]).astype(BF16)

    @pl.when(nv > half)
    def _():
        swiglu(slice(None))

    @pl.when((nv > 0) & (nv <= half))
    def _():
        swiglu(slice(0, half))
        o_ref[half:, :] = jnp.zeros((half, o_ref.shape[1]), o_ref.dtype)

    @pl.when(nv <= 0)
    def _():
        o_ref[...] = jnp.zeros_like(o_ref)


def _ffn(tile_expert, tile_valid, xs, wg, wu, wd):
    rows, d = xs.shape
    wmap = lambda i, te, tv: (te[i], 0, 0)
    return pl.pallas_call(
        _ffn_kernel,
        grid_spec=pltpu.PrefetchScalarGridSpec(
            num_scalar_prefetch=2,
            grid=(rows // FFN_ROWS,),
            in_specs=[pl.BlockSpec((FFN_ROWS, d), lambda i, te, tv: (i, 0)),
                      pl.BlockSpec((None, d, MOE_FF), wmap),
                      pl.BlockSpec((None, d, MOE_FF), wmap),
                      pl.BlockSpec((None, MOE_FF, d), wmap)],
            out_specs=pl.BlockSpec((FFN_ROWS, d), lambda i, te, tv: (i, 0))),
        out_shape=jax.ShapeDtypeStruct((rows, d), BF16),
        compiler_params=_cparams("arbitrary"),
        name="moe_ffn",
    )(tile_expert, tile_valid, xs, wg, wu, wd)


def _combine_kernel(p8_ref, loff_ref, base_ref, ys_ref, plan_ref, rank_ref, off_ref, x1_ref, gt2_ref, fsc_ref,
                    fsh_ref, fnw_ref, y_ref, buf_ref, sem):
    def copy(lo, go, size):
        return pltpu.make_async_copy(ys_ref.at[pl.ds(go, size)], buf_ref.at[pl.ds(lo, size)], sem)

    _run_blocks(p8_ref, loff_ref, base_ref, lambda lo, go, size: copy(lo, go, size).start())
    plan = plan_ref[...]
    s0, s1 = _local_slots(plan, rank_ref[...], off_ref[...])
    row = _iota((1, LOCAL_ROWS), 1).astype(F32)
    weights = (jnp.where(row == s0, plan[:, PLAN_C0:PLAN_C0 + 1], 0.0)
               + jnp.where(row == s1, plan[:, PLAN_C1:PLAN_C1 + 1], 0.0))
    _run_blocks(p8_ref, loff_ref, base_ref, lambda lo, go, size: copy(lo, go, size).wait())
    total = loff_ref[0, MOE_EXPERTS - 1] + p8_ref[0, MOE_EXPERTS - 1]
    filled = _iota((LOCAL_ROWS, 1), 0) < total
    moe = _dot(weights, jnp.where(filled, buf_ref[...], jnp.zeros((), BF16)))
    x2 = x1_ref[...] + gt2_ref[...] * moe
    y_ref[...] = _rms(x2, fnw_ref[...]) * (1.0 + fsc_ref[...]) + fsh_ref[...]


def _combine(p8, loff, base, ys, plan, rank3, off_f, x1, gt2, fsc, fsh, fnw):
    bx, lx, d = x1.shape
    tt = TOK_TILE
    nl = lx // tt
    tile = lambda b, i: b * nl + i
    scal = pl.BlockSpec((None, 1, LANES), lambda b, i: (tile(b, i), 0, 0), memory_space=pltpu.SMEM)
    tokspec = lambda w: pl.BlockSpec((None, tt, w), lambda b, i: (b, i, 0))
    return pl.pallas_call(
        _combine_kernel,
        grid=(bx, nl),
        in_specs=[scal, scal, scal,
                  pl.BlockSpec(memory_space=pl.ANY),
                  tokspec(LANES), tokspec(LANES),
                  pl.BlockSpec((None, 1, LANES), lambda b, i: (tile(b, i), 0, 0)),
                  tokspec(d), _mod_spec(gt2, tt), _mod_spec(fsc, tt), _mod_spec(fsh, tt),
                  pl.BlockSpec((1, d), lambda b, i: (0, 0))],
        out_specs=tokspec(d),
        out_shape=jax.ShapeDtypeStruct((bx, lx, d), F32),
        scratch_shapes=[pltpu.VMEM((LOCAL_ROWS, d), BF16), pltpu.SemaphoreType.DMA(())],
        compiler_params=_cparams("arbitrary", "arbitrary"),
        name="moe_combine_final",
    )(p8, loff, base, ys, plan, rank3, off_f, x1, gt2, fsc, fsh, fnw.reshape(1, d))


def _moe(h2, plan, wg, wu, wd, x1, gt2, fsc, fsh, fnw):
    bx, lx, d = x1.shape
    t = bx * lx
    assert lx % TOK_TILE == 0
    plan2 = plan.reshape(t, LANES)
    rank, cnt = _plan(plan2, TOK_TILE)
    n = cnt[:, 0, :MOE_EXPERTS].astype(jnp.int32)
    p8 = (n + RUN_ALIGN - 1) // RUN_ALIGN * RUN_ALIGN
    loff = jnp.cumsum(p8, axis=1) - p8
    erows = jnp.sum(p8, axis=0)
    epad = (erows + FFN_ROWS - 1) // FFN_ROWS * FFN_ROWS
    ends = jnp.cumsum(epad)
    starts = ends - epad
    base = starts[None, :] + jnp.cumsum(p8, axis=0) - p8
    lanes = lambda a: jnp.pad(a, ((0, 0), (0, LANES - MOE_EXPERTS)))[:, None, :]
    rows = (-(-(2 * t + (t // TOK_TILE) * MOE_EXPERTS * (RUN_ALIGN - 1)) // FFN_ROWS) + MOE_EXPERTS) * FFN_ROWS
    tile_start = jnp.arange(rows // FFN_ROWS, dtype=jnp.int32) * FFN_ROWS
    tile_expert = jnp.minimum(jnp.sum(tile_start[:, None] >= ends[None, :], axis=1), MOE_EXPERTS - 1).astype(jnp.int32)
    hot = tile_expert[:, None] == jnp.arange(MOE_EXPERTS)[None, :]
    tile_valid = jnp.clip(jnp.sum(jnp.where(hot, erows - (tile_start[:, None] - starts), 0), axis=1),
                          0, FFN_ROWS).astype(jnp.int32)
    p8l, loffl, basel = lanes(p8), lanes(loff), lanes(base)
    off_f = loffl.astype(F32)
    xs = _dispatch(tile_valid, p8l, loffl, basel, plan2, rank, off_f, h2.reshape(t, d), rows)
    ys = _ffn(tile_expert, tile_valid, xs, wg, wu, wd)
    return _combine(p8l, loffl, basel, ys, plan, rank.reshape(bx, lx, LANES), off_f, x1, gt2, fsc, fsh, fnw)


def _prep_layer(lp):
    w_in = lp["w_in"]
    offs = [0]
    for s in (DN_CONV_DIM, DN_VAL, DN_HEADS, DN_HEADS, SSM_CONV_DIM, SSM_INNER, SSM_HEADS, D_MODEL, D_MODEL):
        offs.append(offs[-1] + s)
    seg = lambda i: w_in[:, offs[i]:offs[i + 1]]
    small = jnp.concatenate([seg(2), seg(3), seg(6)], axis=1)
    pad = jnp.zeros((D_MODEL, PROJ_N - COL_SMALL - small.shape[1]), F32)
    w_cat = jnp.concatenate([seg(0), seg(4), seg(5), seg(7), seg(8), seg(1), small, pad], axis=1).astype(BF16)
    wr = jnp.concatenate([lp["w_group_router"], lp["w_expert_router"],
                          jnp.zeros((D_MODEL, LANES - MOE_GROUPS - MOE_EXPERTS), F32)], axis=1)
    br = jnp.concatenate([lp["b_group_router"], lp["b_expert_router"],
                          jnp.zeros((LANES - MOE_GROUPS - MOE_EXPERTS,), F32)]).reshape(1, LANES)
    return dict(lp, w_cat=w_cat, wr=wr, br=br,
                wdn=lp["w_dn_out"].astype(BF16), wssm=lp["w_ssm_out"].astype(BF16), wout=lp["w_out"].astype(BF16),
                wg=lp["w_exp_gate"].astype(BF16), wu=lp["w_exp_up"].astype(BF16), wd=lp["w_exp_down"].astype(BF16))


def _trunk(x3, mods, fins, states, layers, final_norm_w, cfg):
    bx, lx, d = x3.shape
    tm = cfg["tm"]
    scan = dict(bt=cfg["bt"], r=cfg["r"], c=cfg["c"], lc=cfg["lc"], ng=cfg["ng"], cpi=cfg["cpi"])
    new_states = []
    n_layers = len(layers)
    for l, lp in enumerate(layers):
        sh1, sc1, gt1, sh2, sc2, gt2 = mods[l]
        dn_conv, dn_rec, ssm_conv, ssm_rec = states[l]
        proj, small = _inproj(x3, sc1, sh1, lp["norm_mix_w"], lp["w_cat"], tm)
        proj, small = proj.reshape(bx * lx, PROJ_N), small.reshape(bx * lx, LANES)
        og, dn_conv_new, dn_rec_new = _dn_branch(proj, small, lp["dn_conv_w"], lp["dn_A_log"], lp["dn_dt_bias"],
                                                 lp["dn_norm_w"], dn_conv, dn_rec, **scan)
        ys, ssm_conv_new, ssm_rec_new = _ssd_branch(proj, small, lp["ssm_conv_w"], lp["ssm_conv_b"], lp["ssm_A_log"],
                                                    lp["ssm_dt_bias"], lp["ssm_D"], lp["ssm_norm_w"],
                                                    ssm_conv, ssm_rec, **scan)
        x1, h2, plan = _merge(x3, og, ys, proj, gt1, sc2, sh2, lp["norm_ffn_w"], lp["wdn"], lp["wssm"], lp["wout"],
                              lp["wr"], lp["br"], cfg["tm_merge"])
        if l == n_layers - 1:
            fsh, fsc, fnw = fins[0], fins[1], final_norm_w
            x3 = _moe(h2, plan, lp["wg"], lp["wu"], lp["wd"], x1, gt2, fsc, fsh, fnw)
        else:
            raise NotImplementedError("only the last layer fuses the final norm; depth is 1 here")
        new_states.append((dn_conv_new, dn_rec_new, ssm_conv_new, ssm_rec_new))
    return x3, new_states


def _per_seq(m):
    return m[:, None, :]


def kernel(x_prompt, x_sample, c_prompt, c_sample, state_dn_conv, state_dn_rec, state_ssm_conv, state_ssm_rec, w_ada, b_ada, norm_mix_w, w_in, dn_conv_w, dn_A_log, dn_dt_bias, dn_norm_w, w_dn_out, ssm_conv_w, ssm_conv_b, ssm_A_log, ssm_dt_bias, ssm_D, ssm_norm_w, w_ssm_out, w_out, norm_ffn_w, w_group_router, b_group_router, w_expert_router, b_expert_router, w_exp_gate, w_exp_up, w_exp_down, w_ada_final, b_ada_final, final_norm_w):
    depth = w_ada.shape[0]
    assert depth == 1
    per_layer = dict(w_ada=w_ada, b_ada=b_ada, norm_mix_w=norm_mix_w, w_in=w_in, dn_conv_w=dn_conv_w,
                     dn_A_log=dn_A_log, dn_dt_bias=dn_dt_bias, dn_norm_w=dn_norm_w, w_dn_out=w_dn_out,
                     ssm_conv_w=ssm_conv_w, ssm_conv_b=ssm_conv_b, ssm_A_log=ssm_A_log, ssm_dt_bias=ssm_dt_bias,
                     ssm_D=ssm_D, ssm_norm_w=ssm_norm_w, w_ssm_out=w_ssm_out, w_out=w_out, norm_ffn_w=norm_ffn_w,
                     w_group_router=w_group_router, b_group_router=b_group_router,
                     w_expert_router=w_expert_router, b_expert_router=b_expert_router,
                     w_exp_gate=w_exp_gate, w_exp_up=w_exp_up, w_exp_down=w_exp_down)
    layers = [_prep_layer({k: v[l] for k, v in per_layer.items()}) for l in range(depth)]

    nbp, lp_, d = x_prompt.shape
    nbs, ls, _ = x_sample.shape
    c_all = jnp.concatenate([c_prompt, c_sample], axis=0)
    mod_all = [_ada(c_all, lyr["w_ada"], lyr["b_ada"]) for lyr in layers]
    fin_all = _ada(c_all, w_ada_final, b_ada_final)

    mods_p = [[_per_seq(m) for m in jnp.split(ma[:nbp], 6, axis=-1)] for ma in mod_all]
    fins_p = [_per_seq(m) for m in jnp.split(fin_all[:nbp], 2, axis=-1)]
    zeros_p = [(jnp.zeros((nbp, CONV_K - 1, DN_CONV_DIM), F32), jnp.zeros((nbp, DN_HEADS, DN_DK, DN_DV), F32),
                jnp.zeros((nbp, CONV_K - 1, SSM_CONV_DIM), F32),
                jnp.zeros((nbp, SSM_HEADS, SSM_HEADDIM, SSM_STATE), F32)) for _ in range(depth)]
    c_p = min(SCAN_CHUNK, lp_)
    r_p = min(lp_, 8 * c_p)
    cfg_p = dict(tm=min(lp_, 2048), tm_merge=min(lp_, 512), tm_moe=min(lp_, 1024), bt=1, r=r_p, c=c_p, lc=c_p, ng=lp_ // r_p,
                 cpi=4 if (r_p // c_p) % 4 == 0 else 1)
    y_p, st_p = _trunk(x_prompt, mods_p, fins_p, zeros_p, layers, final_norm_w, cfg_p)

    lpad = -(-ls // SUBLANES) * SUBLANES
    bt_s = 8
    xs = jnp.pad(x_sample, ((0, 0), (0, lpad - ls), (0, 0))).reshape(1, nbs * lpad, d)
    per_tok = lambda m: jnp.repeat(m, lpad, axis=0)[None]
    mods_s = [[per_tok(m) for m in jnp.split(ma[nbp:], 6, axis=-1)] for ma in mod_all]
    fins_s = [per_tok(m) for m in jnp.split(fin_all[nbp:], 2, axis=-1)]
    st_in = [(state_dn_conv[l], state_dn_rec[l], state_ssm_conv[l], state_ssm_rec[l]) for l in range(depth)]
    ts = nbs * lpad
    cfg_s = dict(tm=ts, tm_merge=min(ts, 512), tm_moe=min(ts, 512), bt=bt_s, r=lpad, c=lpad, lc=ls, ng=1, cpi=1)
    y_s, st_s = _trunk(xs, mods_s, fins_s, st_in, layers, final_norm_w, cfg_s)
    y_s = y_s.reshape(nbs, lpad, d)[:, :ls]

    stack = lambda sts, i: jnp.stack([s[i] for s in sts])
    return (y_p, y_s, stack(st_p, 0), stack(st_p, 1), stack(st_p, 2), stack(st_p, 3),
            stack(st_s, 0), stack(st_s, 1), stack(st_s, 2), stack(st_s, 3))
```

```python
import functools
import math

import jax
import jax.numpy as jnp
from jax import lax
from jax.experimental import pallas as pl
from jax.experimental.pallas import tpu as pltpu

F32 = jnp.float32
BF16 = jnp.bfloat16

D_MODEL = 1024
DN_HEADS = 4
DN_DK = 128
DN_DV = 128
DN_KEY = DN_HEADS * DN_DK
DN_VAL = DN_HEADS * DN_DV
CONV_K = 4
DN_CONV_DIM = 2 * DN_KEY + DN_VAL
SSM_INNER = D_MODEL
SSM_HEADDIM = 64
SSM_HEADS = SSM_INNER // SSM_HEADDIM
SSM_GROUPS = 2
SSM_STATE = 128
SSM_CONV_DIM = SSM_INNER + 2 * SSM_GROUPS * SSM_STATE
MOE_GROUPS = 4
MOE_PER_GROUP = 8
MOE_EXPERTS = MOE_GROUPS * MOE_PER_GROUP
MOE_FF = D_MODEL // 4
EPS = 1e-6
SCAN_CHUNK = 64

LANES = 128
SUBLANES = 8
CARRY_ROW = SUBLANES - (CONV_K - 1)

COL_DN_QKV = 0
COL_SSM_XBC = COL_DN_QKV + DN_CONV_DIM
COL_SSM_Z = COL_SSM_XBC + SSM_CONV_DIM
COL_GATES = COL_SSM_Z + SSM_INNER
COL_DN_Z = COL_GATES + 2 * D_MODEL
COL_SMALL = COL_DN_Z + DN_VAL
PROJ_TN = 1024
PROJ_N = 7 * PROJ_TN
SM_A, SM_B, SM_DT = 0, DN_HEADS, 2 * DN_HEADS
RT_GROUP, RT_EXPERT = 0, MOE_GROUPS

VMEM_LIMIT = 56 * 1024 * 1024


def _cparams(*sem):
    return pltpu.CompilerParams(dimension_semantics=sem, vmem_limit_bytes=VMEM_LIMIT)


def _dot(a, b):
    return jnp.dot(a.astype(BF16), b.astype(BF16), preferred_element_type=F32)


def _dot_nt(a, b):
    return lax.dot_general(a.astype(BF16), b.astype(BF16), (((1,), (1,)), ((), ())),
                           preferred_element_type=F32)


def _dot_tn(a, b):
    return lax.dot_general(a.astype(BF16), b.astype(BF16), (((0,), (0,)), ((), ())),
                           preferred_element_type=F32)


def _split3(x):
    hi = x.astype(BF16)
    r = x - hi.astype(F32)
    mid = r.astype(BF16)
    lo = (r - mid.astype(F32)).astype(BF16)
    return hi, mid, lo


def _dot_exact_lhs(a_bf, b):
    hi, mid, lo = _split3(b)
    d = functools.partial(jnp.dot, preferred_element_type=F32)
    return d(a_bf, hi) + (d(a_bf, mid) + d(a_bf, lo))


def _dot_x3(a, b):
    a_hi = a.astype(BF16)
    a_lo = (a - a_hi.astype(F32)).astype(BF16)
    b_hi = b.astype(BF16)
    b_lo = (b - b_hi.astype(F32)).astype(BF16)
    d = functools.partial(jnp.dot, preferred_element_type=F32)
    return d(a_hi, b_hi) + (d(a_hi, b_lo) + d(a_lo, b_hi))


def _sigmoid(x):
    return 1.0 / (1.0 + jnp.exp(-x))


def _silu(x):
    return x * _sigmoid(x)


def _softplus(x):
    return jnp.maximum(x, 0.0) + jnp.log1p(jnp.exp(-jnp.abs(x)))


def _rms(x, w):
    return x * lax.rsqrt(jnp.mean(x * x, axis=-1, keepdims=True) + EPS) * w


def _iota(shape, dim):
    return lax.broadcasted_iota(jnp.int32, shape, dim)


def _ada_kernel(c_ref, w_ref, b_ref, o_ref):
    o_ref[...] = _dot(_silu(c_ref[...]), w_ref[...]) + b_ref[...]


def _ada(c, w, b, tn=512):
    m, d = c.shape
    n = w.shape[1]
    return pl.pallas_call(
        _ada_kernel,
        grid=(n // tn,),
        in_specs=[pl.BlockSpec((m, d), lambda j: (0, 0)),
                  pl.BlockSpec((d, tn), lambda j: (0, j)),
                  pl.BlockSpec((1, tn), lambda j: (0, j))],
        out_specs=pl.BlockSpec((m, tn), lambda j: (0, j)),
        out_shape=jax.ShapeDtypeStruct((m, n), F32),
        compiler_params=_cparams("arbitrary"),
        name="ada_mod",
    )(c, w, b.reshape(1, n))


def _mod_spec(mod, tm):
    if mod.shape[1] == 1:
        return pl.BlockSpec((None, 1, D_MODEL), lambda b, i, *_: (b, 0, 0))
    return pl.BlockSpec((None, tm, D_MODEL), lambda b, i, *_: (b, i, 0))


def _inproj_kernel(x_ref, sc_ref, sh_ref, nw_ref, w_ref, o_ref, sm_ref, h_ref, *, tm, sub):
    @pl.when(pl.program_id(2) == 0)
    def _():
        per_token = sc_ref.shape[0] != 1

        def body(r, carry):
            rows = pl.ds(pl.multiple_of(r * sub, sub), sub)
            sc = sc_ref[rows, :] if per_token else sc_ref[...]
            sh = sh_ref[rows, :] if per_token else sh_ref[...]
            h = _rms(x_ref[rows, :], nw_ref[...]) * (1.0 + sc) + sh
            h_ref[rows, :] = h.astype(BF16)
            return carry

        lax.fori_loop(0, tm // sub, body, 0)

    acc = jnp.dot(h_ref[...], w_ref[...], preferred_element_type=F32)
    o_ref[...] = acc.astype(BF16)

    @pl.when(pl.program_id(2) == COL_SMALL // PROJ_TN)
    def _():
        sm_ref[...] = acc[:, COL_SMALL % PROJ_TN:COL_SMALL % PROJ_TN + LANES]


def _inproj(x3, sc, sh, nw, w_cat, tm):
    bx, lx, d = x3.shape
    n = w_cat.shape[1]
    kern = functools.partial(_inproj_kernel, tm=tm, sub=min(tm, 256))
    return pl.pallas_call(
        kern,
        grid=(bx, lx // tm, n // PROJ_TN),
        in_specs=[pl.BlockSpec((None, tm, d), lambda b, i, j: (b, i, 0)),
                  _mod_spec(sc, tm), _mod_spec(sh, tm),
                  pl.BlockSpec((1, d), lambda b, i, j: (0, 0)),
                  pl.BlockSpec((d, PROJ_TN), lambda b, i, j: (0, j))],
        out_specs=[pl.BlockSpec((None, tm, PROJ_TN), lambda b, i, j: (b, i, j)),
                   pl.BlockSpec((None, tm, LANES), lambda b, i, j: (b, i, 0))],
        out_shape=[jax.ShapeDtypeStruct((bx, lx, n), BF16), jax.ShapeDtypeStruct((bx, lx, LANES), F32)],
        scratch_shapes=[pltpu.VMEM((tm, d), BF16)],
        compiler_params=_cparams("arbitrary", "arbitrary", "arbitrary"),
        name="norm_inproj",
    )(x3, sc, sh, nw.reshape(1, d), w_cat)


def _conv_silu(u, ext_ref, bb, cw_ref, c, lc, bias=None):
    ext_ref[bb, SUBLANES:SUBLANES + c, :] = u
    y = ext_ref[bb, CARRY_ROW:CARRY_ROW + c, :] * cw_ref[0:1, :]
    for i in range(1, CONV_K):
        y = y + ext_ref[bb, CARRY_ROW + i:CARRY_ROW + i + c, :] * cw_ref[i:i + 1, :]
    ext_ref[bb, CARRY_ROW:SUBLANES, :] = ext_ref[bb, CARRY_ROW + lc:SUBLANES + lc, :]
    if bias is not None:
        y = y + bias
    return _silu(y)


def _conv_rounds(ref, rounds, ext_ref, cw_ref, r, c, lc, bias=None):
    read = _chunk_reader(ref, r, c)
    return [_conv_silu(read(bb, r0), ext_ref, bb, cw_ref, c, lc, bias) for rnd in rounds for bb, r0 in rnd]


def _chunk_reader(ref, r, c):
    if r == c:
        whole = ref[...].astype(F32)
        return lambda bb, r0: whole[bb * c:(bb + 1) * c]
    return lambda bb, r0: ref[pl.ds(bb * r + r0, c), :].astype(F32)


def _chunk_writer(ref, bt, r, c):
    if r != c:
        return (lambda bb, r0, val: ref.__setitem__((pl.ds(bb * r + r0, c), slice(None)), val.astype(ref.dtype)),
                lambda: None)
    parts = {}

    def flush():
        ref[...] = jnp.concatenate([parts[bb] for bb in range(bt)], axis=0).astype(ref.dtype)
    return (lambda bb, r0, val: parts.__setitem__(bb, val)), flush


def _run_rounds(process, bt, r, c, cpi):
    g = r // c
    if g == 1:
        process([[(bb, 0) for bb in range(bt)]])
    else:
        def body(ci, carry):
            process([[(bb, pl.multiple_of((ci * cpi + t) * c, c)) for bb in range(bt)] for t in range(cpi)])
            return carry
        lax.fori_loop(0, g // cpi, body, 0)


INV_BASE = SUBLANES


def _inverse_masks(row, col, c):
    sh = lambda x, s: jnp.right_shift(x, int(math.log2(s)))
    diag = sh(row, INV_BASE) == sh(col, INV_BASE)
    merges = []
    s = INV_BASE
    while s < c:
        merges.append((sh(row, 2 * s) == sh(col, 2 * s))
                      & (jnp.bitwise_and(sh(row, s), 1) == 1) & (jnp.bitwise_and(sh(col, s), 1) == 0))
        s *= 2
    return diag, merges


def _unit_lower_inverses(lmats, eye_f, masks):
    diag, merges = masks
    npows = [-jnp.where(diag, l, 0.0) for l in lmats]
    ps = [eye_f + n for n in npows]
    for _ in range(int(math.log2(INV_BASE)) - 1):
        npows = [_dot(n, n) for n in npows]
        ps = [p + _dot(p, n) for p, n in zip(ps, npows)]
    for m in merges:
        ts = [_dot(p, jnp.where(m, l, 0.0)) for p, l in zip(ps, lmats)]
        ps = [p - _dot(t, p) for t, p in zip(ts, ps)]
    return ps


def _dn_kernel(qkv_ref, z_ref, sm_ref, cw_ref, alog_ref, bias_ref, nw_ref, cin_ref, sin_ref,
               o_ref, cout_ref, sout_ref, ext_ref, *, bt, r, c, lc, ng, cpi):
    j = pl.program_id(1)

    @pl.when(j == 0)
    def _():
        ext_ref[:, CARRY_ROW:SUBLANES, :] = cin_ref[...]
        sout_ref[...] = sin_ref[...]

    row = _iota((c, c), 0)
    col = _iota((c, c), 1)
    tri_incl = row >= col
    tri_strict = row > col
    eye_f = (row == col).astype(F32)
    tril_bf = tri_incl.astype(BF16)
    valid = _iota((c, 1), 0) < lc
    nexp_a = -jnp.exp(alog_ref[...])
    bias = bias_ref[...]
    inv_masks = _inverse_masks(row, col, c)
    heads = range(DN_HEADS)
    read_z, read_sm = (_chunk_reader(ref, r, c) for ref in (z_ref, sm_ref))
    put_o, flush_o = _chunk_writer(o_ref, bt, r, c)

    def prep(bb, r0, y):
        sm = read_sm(bb, r0)
        g_all = nexp_a * _softplus(sm + bias)
        beta_all = _sigmoid(sm)
        if lc < c:
            g_all = jnp.where(valid, g_all, 0.0)
            beta_all = jnp.where(valid, beta_all, 0.0)
        cs_all = _dot_exact_lhs(tril_bf, g_all)
        cs_t = cs_all.T
        ecs_all = jnp.exp(cs_all)
        per_head = []
        for h in heads:
            q = y[:, h * DN_DK:(h + 1) * DN_DK]
            k = y[:, DN_KEY + h * DN_DK:DN_KEY + (h + 1) * DN_DK]
            v = y[:, 2 * DN_KEY + h * DN_DV:2 * DN_KEY + (h + 1) * DN_DV]
            q = q * lax.rsqrt(jnp.sum(q * q, axis=-1, keepdims=True) + EPS) * (DN_DK ** -0.5)
            k = k * lax.rsqrt(jnp.sum(k * k, axis=-1, keepdims=True) + EPS)
            if lc < c:
                k = jnp.where(valid, k, 0.0)
            cs = cs_all[:, SM_A + h:SM_A + h + 1]
            ecs = ecs_all[:, SM_A + h:SM_A + h + 1]
            beta = beta_all[:, SM_B + h:SM_B + h + 1]
            last = cs_all[c - 1:c, SM_A + h:SM_A + h + 1]
            decay = jnp.exp(jnp.where(tri_incl, cs - cs_t[SM_A + h:SM_A + h + 1, :], -jnp.inf))
            per_head.append(dict(q_ecs=(q * ecs).astype(BF16), q=q.astype(BF16), k=k.astype(BF16), beta=beta,
                                 decay=decay, elast=jnp.exp(last),
                                 k_tail=(k * jnp.exp(last - cs)).astype(BF16),
                                 rhs=jnp.concatenate([v * beta, k * (beta * ecs)], axis=1).astype(BF16)))
        return per_head

    def process(rounds):
        items = [it for rnd in rounds for it in rnd]
        conv = _conv_rounds(qkv_ref, rounds, ext_ref, cw_ref, r, c, lc)
        pre = [prep(bb, r0, y) for (bb, r0), y in zip(items, conv)]
        chains = [(i, h) for i in range(len(items)) for h in heads]
        a = {ch: pre[ch[0]][ch[1]] for ch in chains}
        kk = {ch: _dot_nt(a[ch]["k"], a[ch]["k"]) for ch in chains}
        qk = {ch: _dot_nt(a[ch]["q"], a[ch]["k"]) * a[ch]["decay"] for ch in chains}
        lmats = [jnp.where(tri_strict, a[ch]["beta"] * kk[ch] * a[ch]["decay"], 0.0) for ch in chains]
        pinv = _unit_lower_inverses(lmats, eye_f, inv_masks)
        sol = {ch: _dot(p, a[ch]["rhs"]) for ch, p in zip(chains, pinv)}
        first = 0
        for rnd in rounds:
            idx = range(first, first + len(rnd))
            first += len(rnd)
            rch = [(i, h) for i in idx for h in heads]
            s = {ch: sout_ref[items[ch[0]][0], ch[1]] for ch in rch}
            ws_qs = {ch: _dot(jnp.concatenate([sol[ch][:, DN_DV:].astype(BF16), a[ch]["q_ecs"]], axis=0), s[ch])
                     for ch in rch}
            v_new = {ch: (sol[ch][:, :DN_DV] - ws_qs[ch][:c]).astype(BF16) for ch in rch}
            o = {ch: ws_qs[ch][c:] + _dot(qk[ch], v_new[ch]) for ch in rch}
            for ch in rch:
                sout_ref[items[ch[0]][0], ch[1]] = a[ch]["elast"] * s[ch] + _dot_tn(a[ch]["k_tail"], v_new[ch])
            for i in idx:
                bb, r0 = items[i]
                z = read_z(bb, r0)
                put_o(bb, r0, jnp.concatenate(
                    [_rms(o[(i, h)], nw_ref[...]) * _silu(z[:, h * DN_DV:(h + 1) * DN_DV]) for h in heads], axis=1))

    _run_rounds(process, bt, r, c, cpi)
    flush_o()

    @pl.when(j == ng - 1)
    def _():
        cout_ref[...] = ext_ref[:, CARRY_ROW:SUBLANES, :]


def _small_row(vals, offset, fill=0.0):
    row = jnp.full((1, LANES), fill, F32)
    return lax.dynamic_update_slice(row, vals.astype(F32).reshape(1, -1), (0, offset))


def _dn_branch(proj, small, conv_w, a_log, dt_bias, norm_w, conv_in, rec_in, *, bt, r, c, lc, ng, cpi):
    nb = conv_in.shape[0] // bt
    rows = bt * r
    t = proj.shape[0]
    kern = functools.partial(_dn_kernel, bt=bt, r=r, c=c, lc=lc, ng=ng, cpi=cpi)
    rowmap = lambda col: (lambda i, j: (i * ng + j, col))
    const = lambda i, j: (0, 0)
    return pl.pallas_call(
        kern,
        grid=(nb, ng),
        in_specs=[pl.BlockSpec((rows, DN_CONV_DIM), rowmap(COL_DN_QKV // DN_CONV_DIM)),
                  pl.BlockSpec((rows, DN_VAL), rowmap(COL_DN_Z // DN_VAL)),
                  pl.BlockSpec((rows, LANES), rowmap(0)),
                  pl.BlockSpec((CONV_K, DN_CONV_DIM), const),
                  pl.BlockSpec((1, LANES), const),
                  pl.BlockSpec((1, LANES), const),
                  pl.BlockSpec((1, DN_DV), const),
                  pl.BlockSpec((bt, CONV_K - 1, DN_CONV_DIM), lambda i, j: (i, 0, 0)),
                  pl.BlockSpec((bt, DN_HEADS, DN_DK, DN_DV), lambda i, j: (i, 0, 0, 0))],
        out_specs=[pl.BlockSpec((rows, DN_VAL), lambda i, j: (i * ng + j, 0)),
                   pl.BlockSpec((bt, CONV_K - 1, DN_CONV_DIM), lambda i, j: (i, 0, 0)),
                   pl.BlockSpec((bt, DN_HEADS, DN_DK, DN_DV), lambda i, j: (i, 0, 0, 0))],
        out_shape=[jax.ShapeDtypeStruct((t, DN_VAL), BF16),
                   jax.ShapeDtypeStruct(conv_in.shape, F32),
                   jax.ShapeDtypeStruct(rec_in.shape, F32)],
        scratch_shapes=[pltpu.VMEM((bt, c + SUBLANES, DN_CONV_DIM), F32)],
        compiler_params=_cparams("arbitrary", "arbitrary"),
        name="gated_delta",
    )(proj, proj, small, conv_w, _small_row(a_log, SM_A), _small_row(dt_bias, SM_A),
      norm_w.reshape(1, DN_DV), conv_in, rec_in)


def _ssd_kernel(xbc_ref, z_ref, sm_ref, cw_ref, cb_ref, alog_ref, bias_ref, dvec_ref, nw_ref, cin_ref,
                hin_ref, y_ref, cout_ref, hout_ref, ext_ref, *, bt, r, c, lc, ng, cpi):
    j = pl.program_id(1)

    @pl.when(j == 0)
    def _():
        ext_ref[:, CARRY_ROW:SUBLANES, :] = cin_ref[...]
        hout_ref[...] = hin_ref[...]

    tril_bf = (_iota((c, c), 0) >= _iota((c, c), 1)).astype(BF16)
    row2 = _iota((c, 2 * c), 0)
    lane2 = _iota((c, 2 * c), 1)
    left2 = lane2 < c
    tri2 = row2 >= jnp.where(left2, lane2, lane2 - c)
    left_x = _iota((c, LANES), 1) < SSM_HEADDIM
    top_h = _iota((2 * SSM_HEADDIM, 1), 0) < SSM_HEADDIM
    valid = _iota((c, 1), 0) < lc
    nexp_a = -jnp.exp(alog_ref[...])
    bias = bias_ref[...]
    gw = SSM_INNER // SSM_GROUPS
    pairs_per_group = SSM_HEADS // SSM_GROUPS // 2

    pairs = range(SSM_HEADS // 2)
    pair_rows = lambda p: slice(p * LANES, (p + 1) * LANES)
    read_z, read_sm = (_chunk_reader(ref, r, c) for ref in (z_ref, sm_ref))
    put_y, flush_y = _chunk_writer(y_ref, bt, r, c)

    def prep(bb, r0, y):
        dt_all = _softplus(read_sm(bb, r0) + bias)
        if lc < c:
            dt_all = jnp.where(valid, dt_all, 0.0)
        cs_all = _dot_exact_lhs(tril_bf, dt_all * nexp_a)
        cs_t2 = jnp.concatenate([cs_all, cs_all], axis=0).T
        last = cs_all[c - 1:c, :]
        bm = [y[:, SSM_INNER + g * SSM_STATE:SSM_INNER + (g + 1) * SSM_STATE].astype(BF16)
              for g in range(SSM_GROUPS)]
        cm = [y[:, SSM_INNER + (SSM_GROUPS + g) * SSM_STATE:SSM_INNER + (SSM_GROUPS + g + 1) * SSM_STATE].astype(BF16)
              for g in range(SSM_GROUPS)]
        return dict(y=y, dt=dt_all, cs=cs_all, cs_t2=cs_t2, ecs=jnp.exp(cs_all),
                    tail=jnp.exp(last - cs_all), elast=jnp.exp(last), bm=bm, cm=cm)

    def process(rounds):
        items = [it for rnd in rounds for it in rnd]
        conv = _conv_rounds(xbc_ref, rounds, ext_ref, cw_ref, r, c, lc, bias=cb_ref[...])
        pre = [prep(bb, r0, y) for (bb, r0), y in zip(items, conv)]
        ids = range(len(items))
        cb2 = {(i, g): _dot_nt(pre[i]["cm"][g], jnp.concatenate([pre[i]["bm"][g]] * 2, axis=0))
               for i in ids for g in range(SSM_GROUPS)}
        y_intra, upd, xs_of = {}, {}, {}
        for i in ids:
            a = pre[i]
            for p in pairs:
                g = p // pairs_per_group
                la, lb = SM_DT + 2 * p, SM_DT + 2 * p + 1
                both = lambda v, mask=left_x: jnp.where(mask, v[:, la:la + 1], v[:, lb:lb + 1])
                xs = a["y"][:, pair_rows(p)]
                xdt = xs * both(a["dt"])
                diff = both(a["cs"], left2) - jnp.where(left2[0:1], a["cs_t2"][la:la + 1, :], a["cs_t2"][lb:lb + 1, :])
                decay2 = jnp.exp(jnp.where(tri2, diff, -jnp.inf))
                rhs = jnp.concatenate([jnp.where(left_x, xdt, 0.0), jnp.where(left_x, 0.0, xdt)], axis=0)
                y_intra[i, p] = _dot(cb2[i, g] * decay2, rhs)
                upd[i, p] = _dot_tn(xdt * both(a["tail"]), a["bm"][g])
                xs_of[i, p] = xs
        first = 0
        for rnd in rounds:
            idx = range(first, first + len(rnd))
            first += len(rnd)
            for i in idx:
                a = pre[i]
                bb, r0 = items[i]
                outs = []
                for p in pairs:
                    g = p // pairs_per_group
                    la, lb = SM_DT + 2 * p, SM_DT + 2 * p + 1
                    hs = hout_ref[bb, pair_rows(p), :]
                    y_inter = _dot_nt(a["cm"][g], hs) * jnp.where(left_x, a["ecs"][:, la:la + 1], a["ecs"][:, lb:lb + 1])
                    hout_ref[bb, pair_rows(p), :] = (
                        jnp.where(top_h, a["elast"][:, la:la + 1], a["elast"][:, lb:lb + 1]) * hs + upd[i, p])
                    outs.append(y_intra[i, p] + y_inter + dvec_ref[:, pair_rows(p)] * xs_of[i, p])
                yz = jnp.concatenate(outs, axis=1) * _silu(read_z(bb, r0))
                put_y(bb, r0, jnp.concatenate(
                    [_rms(yz[:, g * gw:(g + 1) * gw], nw_ref[:, g * gw:(g + 1) * gw]) for g in range(SSM_GROUPS)],
                    axis=1))

    _run_rounds(process, bt, r, c, cpi)
    flush_y()

    @pl.when(j == ng - 1)
    def _():
        cout_ref[...] = ext_ref[:, CARRY_ROW:SUBLANES, :]


def _ssd_branch(proj, small, conv_w, conv_b, a_log, dt_bias, d_skip, norm_w, conv_in, rec_in, *, bt, r, c, lc, ng, cpi):
    nb = conv_in.shape[0] // bt
    rows = bt * r
    t = proj.shape[0]
    hrows = SSM_HEADS * SSM_HEADDIM
    rec2 = rec_in.reshape(rec_in.shape[0], hrows, SSM_STATE)
    kern = functools.partial(_ssd_kernel, bt=bt, r=r, c=c, lc=lc, ng=ng, cpi=cpi)
    rowmap = lambda col: (lambda i, j: (i * ng + j, col))
    const = lambda i, j: (0, 0)
    y, conv_out, rec_out = pl.pallas_call(
        kern,
        grid=(nb, ng),
        in_specs=[pl.BlockSpec((rows, SSM_CONV_DIM), rowmap(COL_SSM_XBC // SSM_CONV_DIM)),
                  pl.BlockSpec((rows, SSM_INNER), rowmap(COL_SSM_Z // SSM_INNER)),
                  pl.BlockSpec((rows, LANES), rowmap(0)),
                  pl.BlockSpec((CONV_K, SSM_CONV_DIM), const),
                  pl.BlockSpec((1, SSM_CONV_DIM), const),
                  pl.BlockSpec((1, LANES), const),
                  pl.BlockSpec((1, LANES), const),
                  pl.BlockSpec((1, SSM_INNER), const),
                  pl.BlockSpec((1, SSM_INNER), const),
                  pl.BlockSpec((bt, CONV_K - 1, SSM_CONV_DIM), lambda i, j: (i, 0, 0)),
                  pl.BlockSpec((bt, hrows, SSM_STATE), lambda i, j: (i, 0, 0))],
        out_specs=[pl.BlockSpec((rows, SSM_INNER), lambda i, j: (i * ng + j, 0)),
                   pl.BlockSpec((bt, CONV_K - 1, SSM_CONV_DIM), lambda i, j: (i, 0, 0)),
                   pl.BlockSpec((bt, hrows, SSM_STATE), lambda i, j: (i, 0, 0))],
        out_shape=[jax.ShapeDtypeStruct((t, SSM_INNER), BF16),
                   jax.ShapeDtypeStruct(conv_in.shape, F32),
                   jax.ShapeDtypeStruct(rec2.shape, F32)],
        scratch_shapes=[pltpu.VMEM((bt, c + SUBLANES, SSM_CONV_DIM), F32)],
        compiler_params=_cparams("arbitrary", "arbitrary"),
        name="ssd_scan",
    )(proj, proj, small, conv_w, conv_b.reshape(1, -1), _small_row(a_log, SM_DT), _small_row(dt_bias, SM_DT),
      jnp.repeat(d_skip.astype(F32), SSM_HEADDIM).reshape(1, SSM_INNER), norm_w.reshape(1, SSM_INNER),
      conv_in, rec2)
    return y, conv_out, rec_out.reshape(rec_in.shape)


PLAN_E0, PLAN_E1, PLAN_C0, PLAN_C1 = 0, 1, 2, 3


def _route(logits):
    lane = _iota(logits.shape, 1).astype(F32)
    big = float(LANES)
    is_group = (lane >= RT_GROUP) & (lane < RT_GROUP + MOE_GROUPS)
    gl = jnp.where(is_group, logits, -jnp.inf)
    gmax = jnp.max(gl, axis=-1, keepdims=True)
    g_sel = jnp.min(jnp.where(gl == gmax, lane, big), axis=-1, keepdims=True) - RT_GROUP
    p_group = 1.0 / jnp.sum(jnp.exp(gl - gmax), axis=-1, keepdims=True)
    e_lo = RT_EXPERT + MOE_PER_GROUP * g_sel
    in_grp = (lane >= e_lo) & (lane < e_lo + MOE_PER_GROUP)
    el = jnp.where(in_grp, logits, -jnp.inf)
    ee = jnp.exp(el - jnp.max(el, axis=-1, keepdims=True))
    pe = jnp.where(in_grp, ee / jnp.sum(ee, axis=-1, keepdims=True), -1.0)
    p1 = jnp.max(pe, axis=-1, keepdims=True)
    i1 = jnp.min(jnp.where(pe == p1, lane, big), axis=-1, keepdims=True)
    pe2 = jnp.where(lane == i1, -1.0, pe)
    p2 = jnp.max(pe2, axis=-1, keepdims=True)
    i2 = jnp.min(jnp.where(pe2 == p2, lane, big), axis=-1, keepdims=True)
    tot = p1 + p2
    plan = jnp.where(lane == PLAN_E0, i1 - RT_EXPERT, 0.0) + jnp.where(lane == PLAN_E1, i2 - RT_EXPERT, 0.0)
    return plan + jnp.where(lane == PLAN_C0, p_group * p1 / tot, 0.0) + jnp.where(lane == PLAN_C1, p_group * p2 / tot, 0.0)


def _tile_ranks(plan):
    tp = plan.shape[0]
    lane = _iota(plan.shape, 1).astype(F32)
    sel0 = lane == plan[:, PLAN_E0:PLAN_E0 + 1]
    sel1 = lane == plan[:, PLAN_E1:PLAN_E1 + 1]
    sel = (sel0 | sel1).astype(BF16)
    before = (_iota((tp, tp), 0) > _iota((tp, tp), 1)).astype(BF16)
    excl = jnp.dot(before, sel, preferred_element_type=F32)
    r0 = jnp.sum(jnp.where(sel0, excl, 0.0), axis=-1, keepdims=True)
    r1 = jnp.sum(jnp.where(sel1, excl, 0.0), axis=-1, keepdims=True)
    rank = jnp.where(lane == PLAN_E0, r0, 0.0) + jnp.where(lane == PLAN_E1, r1, 0.0)
    return rank, jnp.sum(sel.astype(F32), axis=0, keepdims=True)


def _merge_kernel(x_ref, og_ref, ys_ref, gates_ref, gt1_ref, sc2_ref, sh2_ref, nw2_ref, wdn_ref, wssm_ref,
                  wout_ref, wr_ref, br_ref, x1_ref, h2_ref, plan_ref, rank_ref, cnt_ref):
    y_dn = _dot(og_ref[...], wdn_ref[...])
    y_ssm = _dot(ys_ref[...], wssm_ref[...])
    merged = (_sigmoid(gates_ref[:, :D_MODEL].astype(F32)) * y_dn
              + _sigmoid(gates_ref[:, D_MODEL:].astype(F32)) * y_ssm)
    x1 = x_ref[...] + gt1_ref[...] * _dot(merged, wout_ref[...])
    x1_ref[...] = x1
    h2 = _rms(x1, nw2_ref[...]) * (1.0 + sc2_ref[...]) + sh2_ref[...]
    h2_ref[...] = h2.astype(BF16)
    plan = _route(_dot_x3(h2, wr_ref[...]) + br_ref[...])
    plan_ref[...] = plan
    rank_ref[...], cnt_ref[...] = _tile_ranks(plan)


def _merge(x3, og, ys, proj, gt1, sc2, sh2, nw2, wdn, wssm, wout, wr, br, tm):
    bx, lx, d = x3.shape
    assert tm == TOK_TILE
    nl = lx // tm
    rowmap = lambda col: (lambda b, i: (b * nl + i, col))
    const = lambda b, i: (0, 0)
    tok = lambda w, dt: jax.ShapeDtypeStruct((bx, lx, w), dt)
    return pl.pallas_call(
        _merge_kernel,
        grid=(bx, nl),
        in_specs=[pl.BlockSpec((None, tm, d), lambda b, i: (b, i, 0)),
                  pl.BlockSpec((tm, DN_VAL), rowmap(0)),
                  pl.BlockSpec((tm, SSM_INNER), rowmap(0)),
                  pl.BlockSpec((tm, 2 * d), rowmap(COL_GATES // (2 * d))),
                  _mod_spec(gt1, tm), _mod_spec(sc2, tm), _mod_spec(sh2, tm),
                  pl.BlockSpec((1, d), const),
                  pl.BlockSpec(wdn.shape, const), pl.BlockSpec(wssm.shape, const), pl.BlockSpec(wout.shape, const),
                  pl.BlockSpec(wr.shape, const), pl.BlockSpec((1, LANES), const)],
        out_specs=[pl.BlockSpec((None, tm, d), lambda b, i: (b, i, 0)),
                   pl.BlockSpec((None, tm, d), lambda b, i: (b, i, 0)),
                   pl.BlockSpec((None, tm, LANES), lambda b, i: (b, i, 0)),
                   pl.BlockSpec((None, tm, LANES), lambda b, i: (b, i, 0)),
                   pl.BlockSpec((None, 1, LANES), lambda b, i: (b * nl + i, 0, 0))],
        out_shape=[tok(d, F32), tok(d, BF16), tok(LANES, F32), tok(LANES, F32),
                   jax.ShapeDtypeStruct((bx * nl, 1, LANES), F32)],
        compiler_params=_cparams("arbitrary", "arbitrary"),
        name="merge_route",
    )(x3, og, ys, proj, gt1, sc2, sh2, nw2.reshape(1, d), wdn, wssm, wout, wr, br)


FFN_ROWS = 512
TOK_TILE = 512
RUN_ALIGN = 2 * SUBLANES
RUN_BIG = 4 * RUN_ALIGN
RUN_SMALL = (2 * RUN_ALIGN, RUN_ALIGN)
LOCAL_ROWS = 2 * TOK_TILE + MOE_EXPERTS * RUN_ALIGN


def _local_slots(plan, rank, off_row):
    lane = _iota(plan.shape, 1).astype(F32)
    slots = []
    for k in (PLAN_E0, PLAN_E1):
        off = jnp.sum(jnp.where(lane == plan[:, k:k + 1], off_row, 0.0), axis=-1, keepdims=True)
        slots.append(off + rank[:, k:k + 1])
    return slots


def _run_blocks(p8_ref, loff_ref, base_ref, act):
    for e in range(MOE_EXPERTS):
        n = p8_ref[0, e]
        lo = loff_ref[0, e]
        go = base_ref[0, e]

        def big(k, carry, lo=lo, go=go):
            off = k * RUN_BIG
            act(pl.multiple_of(lo + off, RUN_ALIGN), pl.multiple_of(go + off, RUN_ALIGN), RUN_BIG)
            return carry

        lax.fori_loop(0, lax.shift_right_logical(n, RUN_BIG.bit_length() - 1), big, 0)
        for size in RUN_SMALL:
            @pl.when(jnp.bitwise_and(n, size) != 0)
            def _(size=size, n=n, lo=lo, go=go):
                done = jnp.bitwise_and(n, ~(2 * size - 1))
                act(pl.multiple_of(lo + done, RUN_ALIGN), pl.multiple_of(go + done, RUN_ALIGN), size)


def _dispatch_kernel(tv_ref, p8_ref, loff_ref, base_ref, plan_ref, rank_ref, off_ref, h2_ref, xs_ref,
                     buf_ref, zero_ref, sem, zsem):
    @pl.when(pl.program_id(0) == 0)
    def _():
        zero_ref[...] = jnp.zeros_like(zero_ref)

        def zero_copy(i):
            rows = pl.ds(pl.multiple_of(i * FFN_ROWS, FFN_ROWS), FFN_ROWS)
            return pltpu.make_async_copy(zero_ref, xs_ref.at[rows], zsem)

        def zero_tiles(act):
            def body(i, carry):
                @pl.when(tv_ref[i] < FFN_ROWS)
                def _():
                    act(zero_copy(i))
                return carry
            lax.fori_loop(0, xs_ref.shape[0] // FFN_ROWS, body, 0)

        zero_tiles(lambda cp: cp.start())
        zero_tiles(lambda cp: cp.wait())

    s0, s1 = _local_slots(plan_ref[...], rank_ref[...], off_ref[...])
    row = _iota((1, LOCAL_ROWS), 1).astype(F32)
    onehot_t = ((row == s0) | (row == s1)).astype(BF16)
    buf_ref[...] = lax.dot_general(onehot_t, h2_ref[...], (((0,), (0,)), ((), ())),
                                   preferred_element_type=F32).astype(BF16)

    def copy(lo, go, size):
        return pltpu.make_async_copy(buf_ref.at[pl.ds(lo, size)], xs_ref.at[pl.ds(go, size)], sem)

    _run_blocks(p8_ref, loff_ref, base_ref, lambda lo, go, size: copy(lo, go, size).start())
    _run_blocks(p8_ref, loff_ref, base_ref, lambda lo, go, size: copy(lo, go, size).wait())


def _tile_scalars():
    return pl.BlockSpec((None, 1, LANES), lambda i, *_: (i, 0, 0), memory_space=pltpu.SMEM)


def _dispatch(tile_valid, p8, loff, base, plan2, rank, off_f, h2, rows):
    t, d = h2.shape
    tt = TOK_TILE
    tok = lambda w: pl.BlockSpec((tt, w), lambda i, tv: (i, 0))
    return pl.pallas_call(
        _dispatch_kernel,
        grid_spec=pltpu.PrefetchScalarGridSpec(
            num_scalar_prefetch=1,
            grid=(t // tt,),
            in_specs=[_tile_scalars(), _tile_scalars(), _tile_scalars(),
                      tok(LANES), tok(LANES),
                      pl.BlockSpec((None, 1, LANES), lambda i, tv: (i, 0, 0)),
                      tok(d)],
            out_specs=pl.BlockSpec(memory_space=pl.ANY),
            scratch_shapes=[pltpu.VMEM((LOCAL_ROWS, d), BF16), pltpu.VMEM((FFN_ROWS, d), BF16),
                            pltpu.SemaphoreType.DMA(()), pltpu.SemaphoreType.DMA(())]),
        out_shape=jax.ShapeDtypeStruct((rows, d), BF16),
        compiler_params=_cparams("arbitrary"),
        name="moe_dispatch",
    )(tile_valid, p8, loff, base, plan2, rank, off_f, h2)


def _ffn_kernel(te_ref, tv_ref, xs_ref, wg_ref, wu_ref, wd_ref, o_ref):
    nv = tv_ref[pl.program_id(0)]

    half = FFN_ROWS // 2

    def swiglu(rows):
        x = xs_ref[rows, :]
        o_ref[rows, :] = _dot(_silu(_dot(x, wg_ref[...])) * _dot(x, wu_ref[...]), wd_ref[...]).astype(BF16)

    @pl.when(nv > half)
    def _():
        swiglu(slice(None))

    @pl.when((nv > 0) & (nv <= half))
    def _():
        swiglu(slice(0, half))
        o_ref[half:, :] = jnp.zeros((half, o_ref.shape[1]), o_ref.dtype)

    @pl.when(nv <= 0)
    def _():
        o_ref[...] = jnp.zeros_like(o_ref)


def _ffn(tile_expert, tile_valid, xs, wg, wu, wd):
    rows, d = xs.shape
    wmap = lambda i, te, tv: (te[i], 0, 0)
    return pl.pallas_call(
        _ffn_kernel,
        grid_spec=pltpu.PrefetchScalarGridSpec(
            num_scalar_prefetch=2,
            grid=(rows // FFN_ROWS,),
            in_specs=[pl.BlockSpec((FFN_ROWS, d), lambda i, te, tv: (i, 0)),
                      pl.BlockSpec((None, d, MOE_FF), wmap),
                      pl.BlockSpec((None, d, MOE_FF), wmap),
                      pl.BlockSpec((None, MOE_FF, d), wmap)],
            out_specs=pl.BlockSpec((FFN_ROWS, d), lambda i, te, tv: (i, 0))),
        out_shape=jax.ShapeDtypeStruct((rows, d), BF16),
        compiler_params=_cparams("arbitrary"),
        name="moe_ffn",
    )(tile_expert, tile_valid, xs, wg, wu, wd)


def _combine_kernel(p8_ref, loff_ref, base_ref, ys_ref, plan_ref, rank_ref, off_ref, x1_ref, gt2_ref, fsc_ref,
                    fsh_ref, fnw_ref, y_ref, buf_ref, sem):
    @pl.when((pl.program_id(0) == 0) & (pl.program_id(1) == 0))
    def _():
        buf_ref[...] = jnp.zeros_like(buf_ref)

    def copy(lo, go, size):
        return pltpu.make_async_copy(ys_ref.at[pl.ds(go, size)], buf_ref.at[pl.ds(lo, size)], sem)

    _run_blocks(p8_ref, loff_ref, base_ref, lambda lo, go, size: copy(lo, go, size).start())
    plan = plan_ref[...]
    s0, s1 = _local_slots(plan, rank_ref[...], off_ref[...])
    row = _iota((1, LOCAL_ROWS), 1).astype(F32)
    weights = (jnp.where(row == s0, plan[:, PLAN_C0:PLAN_C0 + 1], 0.0)
               + jnp.where(row == s1, plan[:, PLAN_C1:PLAN_C1 + 1], 0.0))
    _run_blocks(p8_ref, loff_ref, base_ref, lambda lo, go, size: copy(lo, go, size).wait())
    total = loff_ref[0, MOE_EXPERTS - 1] + p8_ref[0, MOE_EXPERTS - 1]
    filled = _iota((LOCAL_ROWS, 1), 0) < total
    moe = _dot(weights, jnp.where(filled, buf_ref[...], jnp.zeros((), BF16)))
    x2 = x1_ref[...] + gt2_ref[...] * moe
    y_ref[...] = _rms(x2, fnw_ref[...]) * (1.0 + fsc_ref[...]) + fsh_ref[...]


def _combine(p8, loff, base, ys, plan, rank3, off_f, x1, gt2, fsc, fsh, fnw):
    bx, lx, d = x1.shape
    tt = TOK_TILE
    nl = lx // tt
    tile = lambda b, i: b * nl + i
    scal = pl.BlockSpec((None, 1, LANES), lambda b, i: (tile(b, i), 0, 0), memory_space=pltpu.SMEM)
    tokspec = lambda w: pl.BlockSpec((None, tt, w), lambda b, i: (b, i, 0))
    return pl.pallas_call(
        _combine_kernel,
        grid=(bx, nl),
        in_specs=[scal, scal, scal,
                  pl.BlockSpec(memory_space=pl.ANY),
                  tokspec(LANES), tokspec(LANES),
                  pl.BlockSpec((None, 1, LANES), lambda b, i: (tile(b, i), 0, 0)),
                  tokspec(d), _mod_spec(gt2, tt), _mod_spec(fsc, tt), _mod_spec(fsh, tt),
                  pl.BlockSpec((1, d), lambda b, i: (0, 0))],
        out_specs=tokspec(d),
        out_shape=jax.ShapeDtypeStruct((bx, lx, d), F32),
        scratch_shapes=[pltpu.VMEM((LOCAL_ROWS, d), BF16), pltpu.SemaphoreType.DMA(())],
        compiler_params=_cparams("arbitrary", "arbitrary"),
        name="moe_combine_final",
    )(p8, loff, base, ys, plan, rank3, off_f, x1, gt2, fsc, fsh, fnw.reshape(1, d))


def _moe(h2, plan, rank, cnt, wg, wu, wd, x1, gt2, fsc, fsh, fnw):
    bx, lx, d = x1.shape
    t = bx * lx
    assert lx % TOK_TILE == 0
    plan2, rank = plan.reshape(t, LANES), rank.reshape(t, LANES)
    n = cnt[:, 0, :MOE_EXPERTS].astype(jnp.int32)
    p8 = (n + RUN_ALIGN - 1) // RUN_ALIGN * RUN_ALIGN
    loff = jnp.cumsum(p8, axis=1) - p8
    erows = jnp.sum(p8, axis=0)
    epad = (erows + FFN_ROWS - 1) // FFN_ROWS * FFN_ROWS
    ends = jnp.cumsum(epad)
    starts = ends - epad
    base = starts[None, :] + jnp.cumsum(p8, axis=0) - p8
    lanes = lambda a: jnp.pad(a, ((0, 0), (0, LANES - MOE_EXPERTS)))[:, None, :]
    rows = (-(-(2 * t + (t // TOK_TILE) * MOE_EXPERTS * (RUN_ALIGN - 1)) // FFN_ROWS) + MOE_EXPERTS) * FFN_ROWS
    tile_start = jnp.arange(rows // FFN_ROWS, dtype=jnp.int32) * FFN_ROWS
    tile_expert = jnp.minimum(jnp.sum(tile_start[:, None] >= ends[None, :], axis=1), MOE_EXPERTS - 1).astype(jnp.int32)
    hot = tile_expert[:, None] == jnp.arange(MOE_EXPERTS)[None, :]
    tile_valid = jnp.clip(jnp.sum(jnp.where(hot, erows - (tile_start[:, None] - starts), 0), axis=1),
                          0, FFN_ROWS).astype(jnp.int32)
    p8l, loffl, basel = lanes(p8), lanes(loff), lanes(base)
    off_f = loffl.astype(F32)
    xs = _dispatch(tile_valid, p8l, loffl, basel, plan2, rank, off_f, h2.reshape(t, d), rows)
    ys = _ffn(tile_expert, tile_valid, xs, wg, wu, wd)
    return _combine(p8l, loffl, basel, ys, plan, rank.reshape(bx, lx, LANES), off_f, x1, gt2, fsc, fsh, fnw)


def _prep_layer(lp):
    w_in = lp["w_in"]
    offs = [0]
    for s in (DN_CONV_DIM, DN_VAL, DN_HEADS, DN_HEADS, SSM_CONV_DIM, SSM_INNER, SSM_HEADS, D_MODEL, D_MODEL):
        offs.append(offs[-1] + s)
    seg = lambda i: w_in[:, offs[i]:offs[i + 1]]
    small = jnp.concatenate([seg(2), seg(3), seg(6)], axis=1)
    pad = jnp.zeros((D_MODEL, PROJ_N - COL_SMALL - small.shape[1]), F32)
    w_cat = jnp.concatenate([seg(0), seg(4), seg(5), seg(7), seg(8), seg(1), small, pad], axis=1).astype(BF16)
    wr = jnp.concatenate([lp["w_group_router"], lp["w_expert_router"],
                          jnp.zeros((D_MODEL, LANES - MOE_GROUPS - MOE_EXPERTS), F32)], axis=1)
    br = jnp.concatenate([lp["b_group_router"], lp["b_expert_router"],
                          jnp.zeros((LANES - MOE_GROUPS - MOE_EXPERTS,), F32)]).reshape(1, LANES)
    return dict(lp, w_cat=w_cat, wr=wr, br=br,
                wdn=lp["w_dn_out"].astype(BF16), wssm=lp["w_ssm_out"].astype(BF16), wout=lp["w_out"].astype(BF16),
                wg=lp["w_exp_gate"], wu=lp["w_exp_up"], wd=lp["w_exp_down"])


def _trunk(x3, mods, fins, states, layers, final_norm_w, cfg):
    bx, lx, d = x3.shape
    tm = cfg["tm"]
    scan = dict(bt=cfg["bt"], r=cfg["r"], c=cfg["c"], lc=cfg["lc"], ng=cfg["ng"], cpi=cfg["cpi"])
    new_states = []
    n_layers = len(layers)
    for l, lp in enumerate(layers):
        sh1, sc1, gt1, sh2, sc2, gt2 = mods[l]
        dn_conv, dn_rec, ssm_conv, ssm_rec = states[l]
        proj, small = _inproj(x3, sc1, sh1, lp["norm_mix_w"], lp["w_cat"], tm)
        proj, small = proj.reshape(bx * lx, PROJ_N), small.reshape(bx * lx, LANES)
        og, dn_conv_new, dn_rec_new = _dn_branch(proj, small, lp["dn_conv_w"], lp["dn_A_log"], lp["dn_dt_bias"],
                                                 lp["dn_norm_w"], dn_conv, dn_rec, **scan)
        ys, ssm_conv_new, ssm_rec_new = _ssd_branch(proj, small, lp["ssm_conv_w"], lp["ssm_conv_b"], lp["ssm_A_log"],
                                                    lp["ssm_dt_bias"], lp["ssm_D"], lp["ssm_norm_w"],
                                                    ssm_conv, ssm_rec, **scan)
        x1, h2, plan, rank, cnt = _merge(x3, og, ys, proj, gt1, sc2, sh2, lp["norm_ffn_w"], lp["wdn"], lp["wssm"], lp["wout"],
                              lp["wr"], lp["br"], cfg["tm_merge"])
        if l == n_layers - 1:
            fsh, fsc, fnw = fins[0], fins[1], final_norm_w
            x3 = _moe(h2, plan, rank, cnt, lp["wg"], lp["wu"], lp["wd"], x1, gt2, fsc, fsh, fnw)
        else:
            raise NotImplementedError("only the last layer fuses the final norm; depth is 1 here")
        new_states.append((dn_conv_new, dn_rec_new, ssm_conv_new, ssm_rec_new))
    return x3, new_states


def _per_seq(m):
    return m[:, None, :]


def kernel(x_prompt, x_sample, c_prompt, c_sample, state_dn_conv, state_dn_rec, state_ssm_conv, state_ssm_rec, w_ada, b_ada, norm_mix_w, w_in, dn_conv_w, dn_A_log, dn_dt_bias, dn_norm_w, w_dn_out, ssm_conv_w, ssm_conv_b, ssm_A_log, ssm_dt_bias, ssm_D, ssm_norm_w, w_ssm_out, w_out, norm_ffn_w, w_group_router, b_group_router, w_expert_router, b_expert_router, w_exp_gate, w_exp_up, w_exp_down, w_ada_final, b_ada_final, final_norm_w):
    depth = w_ada.shape[0]
    assert depth == 1
    per_layer = dict(w_ada=w_ada, b_ada=b_ada, norm_mix_w=norm_mix_w, w_in=w_in, dn_conv_w=dn_conv_w,
                     dn_A_log=dn_A_log, dn_dt_bias=dn_dt_bias, dn_norm_w=dn_norm_w, w_dn_out=w_dn_out,
                     ssm_conv_w=ssm_conv_w, ssm_conv_b=ssm_conv_b, ssm_A_log=ssm_A_log, ssm_dt_bias=ssm_dt_bias,
                     ssm_D=ssm_D, ssm_norm_w=ssm_norm_w, w_ssm_out=w_ssm_out, w_out=w_out, norm_ffn_w=norm_ffn_w,
                     w_group_router=w_group_router, b_group_router=b_group_router,
                     w_expert_router=w_expert_router, b_expert_router=b_expert_router,
                     w_exp_gate=w_exp_gate, w_exp_up=w_exp_up, w_exp_down=w_exp_down)
    layers = [_prep_layer({k: v[l] for k, v in per_layer.items()}) for l in range(depth)]

    nbp, lp_, d = x_prompt.shape
    nbs, ls, _ = x_sample.shape
    c_all = jnp.concatenate([c_prompt, c_sample], axis=0)
    mod_all = [_ada(c_all, lyr["w_ada"], lyr["b_ada"]) for lyr in layers]
    fin_all = _ada(c_all, w_ada_final, b_ada_final)

    mods_p = [[_per_seq(m) for m in jnp.split(ma[:nbp], 6, axis=-1)] for ma in mod_all]
    fins_p = [_per_seq(m) for m in jnp.split(fin_all[:nbp], 2, axis=-1)]
    zeros_p = [(jnp.zeros((nbp, CONV_K - 1, DN_CONV_DIM), F32), jnp.zeros((nbp, DN_HEADS, DN_DK, DN_DV), F32),
                jnp.zeros((nbp, CONV_K - 1, SSM_CONV_DIM), F32),
                jnp.zeros((nbp, SSM_HEADS, SSM_HEADDIM, SSM_STATE), F32)) for _ in range(depth)]
    c_p = min(SCAN_CHUNK, lp_)
    r_p = min(lp_, 8 * c_p)
    cfg_p = dict(tm=min(lp_, 2048), tm_merge=min(lp_, 512), tm_moe=min(lp_, 1024), bt=1, r=r_p, c=c_p, lc=c_p, ng=lp_ // r_p,
                 cpi=4 if (r_p // c_p) % 4 == 0 else 1)
    y_p, st_p = _trunk(x_prompt, mods_p, fins_p, zeros_p, layers, final_norm_w, cfg_p)

    lpad = -(-ls // SUBLANES) * SUBLANES
    bt_s = 8
    xs = jnp.pad(x_sample, ((0, 0), (0, lpad - ls), (0, 0))).reshape(1, nbs * lpad, d)
    per_tok = lambda m: jnp.repeat(m, lpad, axis=0)[None]
    mods_s = [[per_tok(m) for m in jnp.split(ma[nbp:], 6, axis=-1)] for ma in mod_all]
    fins_s = [per_tok(m) for m in jnp.split(fin_all[nbp:], 2, axis=-1)]
    st_in = [(state_dn_conv[l], state_dn_rec[l], state_ssm_conv[l], state_ssm_rec[l]) for l in range(depth)]
    ts = nbs * lpad
    cfg_s = dict(tm=ts, tm_merge=min(ts, 512), tm_moe=min(ts, 512), bt=bt_s, r=lpad, c=lpad, lc=ls, ng=1, cpi=1)
    y_s, st_s = _trunk(xs, mods_s, fins_s, st_in, layers, final_norm_w, cfg_s)
    y_s = y_s.reshape(nbs, lpad, d)[:, :ls]

    stack = lambda sts, i: jnp.stack([s[i] for s in sts])
    return (y_p, y_s, stack(st_p, 0), stack(st_p, 1), stack(st_p, 2), stack(st_p, 3),
            stack(st_s, 0), stack(st_s, 1), stack(st_s, 2), stack(st_s, 3))
```

```python
import functools
import math

import jax
import jax.numpy as jnp
from jax import lax
from jax.experimental import pallas as pl
from jax.experimental.pallas import tpu as pltpu

F32 = jnp.float32
BF16 = jnp.bfloat16

D_MODEL = 1024
DN_HEADS = 4
DN_DK = 128
DN_DV = 128
DN_KEY = DN_HEADS * DN_DK
DN_VAL = DN_HEADS * DN_DV
CONV_K = 4
DN_CONV_DIM = 2 * DN_KEY + DN_VAL
SSM_INNER = D_MODEL
SSM_HEADDIM = 64
SSM_HEADS = SSM_INNER // SSM_HEADDIM
SSM_GROUPS = 2
SSM_STATE = 128
SSM_CONV_DIM = SSM_INNER + 2 * SSM_GROUPS * SSM_STATE
MOE_GROUPS = 4
MOE_PER_GROUP = 8
MOE_EXPERTS = MOE_GROUPS * MOE_PER_GROUP
MOE_FF = D_MODEL // 4
EPS = 1e-6
SCAN_CHUNK = 64

LANES = 128
SUBLANES = 8
CARRY_ROW = SUBLANES - (CONV_K - 1)

COL_DN_QKV = 0
COL_SSM_XBC = COL_DN_QKV + DN_CONV_DIM
COL_SSM_Z = COL_SSM_XBC + SSM_CONV_DIM
COL_GATES = COL_SSM_Z + SSM_INNER
COL_DN_Z = COL_GATES + 2 * D_MODEL
COL_SMALL = COL_DN_Z + DN_VAL
PROJ_TN = 1024
PROJ_N = 7 * PROJ_TN
SM_A, SM_B, SM_DT = 0, DN_HEADS, 2 * DN_HEADS
RT_GROUP, RT_EXPERT = 0, MOE_GROUPS

VMEM_LIMIT = 56 * 1024 * 1024


def _cparams(*sem):
    return pltpu.CompilerParams(dimension_semantics=sem, vmem_limit_bytes=VMEM_LIMIT)


def _dot(a, b):
    return jnp.dot(a.astype(BF16), b.astype(BF16), preferred_element_type=F32)


def _dot_nt(a, b):
    return lax.dot_general(a.astype(BF16), b.astype(BF16), (((1,), (1,)), ((), ())),
                           preferred_element_type=F32)


def _dot_tn(a, b):
    return lax.dot_general(a.astype(BF16), b.astype(BF16), (((0,), (0,)), ((), ())),
                           preferred_element_type=F32)


def _split3(x):
    hi = x.astype(BF16)
    r = x - hi.astype(F32)
    mid = r.astype(BF16)
    lo = (r - mid.astype(F32)).astype(BF16)
    return hi, mid, lo


def _dot_exact_lhs(a_bf, b):
    hi, mid, lo = _split3(b)
    d = functools.partial(jnp.dot, preferred_element_type=F32)
    return d(a_bf, hi) + (d(a_bf, mid) + d(a_bf, lo))


def _dot_x3(a, b):
    a_hi = a.astype(BF16)
    a_lo = (a - a_hi.astype(F32)).astype(BF16)
    b_hi = b.astype(BF16)
    b_lo = (b - b_hi.astype(F32)).astype(BF16)
    d = functools.partial(jnp.dot, preferred_element_type=F32)
    return d(a_hi, b_hi) + (d(a_hi, b_lo) + d(a_lo, b_hi))


def _sigmoid(x):
    return 1.0 / (1.0 + jnp.exp(-x))


def _silu(x):
    return x * _sigmoid(x)


def _softplus(x):
    return jnp.maximum(x, 0.0) + jnp.log1p(jnp.exp(-jnp.abs(x)))


def _rms(x, w):
    return x * lax.rsqrt(jnp.mean(x * x, axis=-1, keepdims=True) + EPS) * w


def _iota(shape, dim):
    return lax.broadcasted_iota(jnp.int32, shape, dim)


def _ada_kernel(c_ref, w_ref, b_ref, o_ref):
    o_ref[...] = _dot(_silu(c_ref[...]), w_ref[...]) + b_ref[...]


def _ada(c, w, b, tn=512):
    m, d = c.shape
    n = w.shape[1]
    return pl.pallas_call(
        _ada_kernel,
        grid=(n // tn,),
        in_specs=[pl.BlockSpec((m, d), lambda j: (0, 0)),
                  pl.BlockSpec((d, tn), lambda j: (0, j)),
                  pl.BlockSpec((1, tn), lambda j: (0, j))],
        out_specs=pl.BlockSpec((m, tn), lambda j: (0, j)),
        out_shape=jax.ShapeDtypeStruct((m, n), F32),
        compiler_params=_cparams("arbitrary"),
        name="ada_mod",
    )(c, w, b.reshape(1, n))


def _mod_spec(mod, tm):
    if mod.shape[1] == 1:
        return pl.BlockSpec((None, 1, D_MODEL), lambda b, i, *_: (b, 0, 0))
    return pl.BlockSpec((None, tm, D_MODEL), lambda b, i, *_: (b, i, 0))


def _inproj_kernel(x_ref, sc_ref, sh_ref, nw_ref, w_ref, o_ref, sm_ref, h_ref, *, tm, sub):
    @pl.when(pl.program_id(2) == 0)
    def _():
        per_token = sc_ref.shape[0] != 1

        def body(r, carry):
            rows = pl.ds(pl.multiple_of(r * sub, sub), sub)
            sc = sc_ref[rows, :] if per_token else sc_ref[...]
            sh = sh_ref[rows, :] if per_token else sh_ref[...]
            h = _rms(x_ref[rows, :], nw_ref[...]) * (1.0 + sc) + sh
            h_ref[rows, :] = h.astype(BF16)
            return carry

        lax.fori_loop(0, tm // sub, body, 0)

    acc = jnp.dot(h_ref[...], w_ref[...], preferred_element_type=F32)
    o_ref[...] = acc.astype(BF16)

    @pl.when(pl.program_id(2) == COL_SMALL // PROJ_TN)
    def _():
        sm_ref[...] = acc[:, COL_SMALL % PROJ_TN:COL_SMALL % PROJ_TN + LANES]


def _inproj(x3, sc, sh, nw, w_cat, tm):
    bx, lx, d = x3.shape
    n = w_cat.shape[1]
    kern = functools.partial(_inproj_kernel, tm=tm, sub=min(tm, 256))
    return pl.pallas_call(
        kern,
        grid=(bx, lx // tm, n // PROJ_TN),
        in_specs=[pl.BlockSpec((None, tm, d), lambda b, i, j: (b, i, 0)),
                  _mod_spec(sc, tm), _mod_spec(sh, tm),
                  pl.BlockSpec((1, d), lambda b, i, j: (0, 0)),
                  pl.BlockSpec((d, PROJ_TN), lambda b, i, j: (0, j))],
        out_specs=[pl.BlockSpec((None, tm, PROJ_TN), lambda b, i, j: (b, i, j)),
                   pl.BlockSpec((None, tm, LANES), lambda b, i, j: (b, i, 0))],
        out_shape=[jax.ShapeDtypeStruct((bx, lx, n), BF16), jax.ShapeDtypeStruct((bx, lx, LANES), F32)],
        scratch_shapes=[pltpu.VMEM((tm, d), BF16)],
        compiler_params=_cparams("arbitrary", "arbitrary", "arbitrary"),
        name="norm_inproj",
    )(x3, sc, sh, nw.reshape(1, d), w_cat)


def _conv_silu(u, ext_ref, bb, cw_ref, c, lc, bias=None):
    ext_ref[bb, SUBLANES:SUBLANES + c, :] = u
    y = ext_ref[bb, CARRY_ROW:CARRY_ROW + c, :] * cw_ref[0:1, :]
    for i in range(1, CONV_K):
        y = y + ext_ref[bb, CARRY_ROW + i:CARRY_ROW + i + c, :] * cw_ref[i:i + 1, :]
    ext_ref[bb, CARRY_ROW:SUBLANES, :] = ext_ref[bb, CARRY_ROW + lc:SUBLANES + lc, :]
    if bias is not None:
        y = y + bias
    return _silu(y)


def _conv_rounds(ref, rounds, ext_ref, cw_ref, r, c, lc, bias=None):
    read = _chunk_reader(ref, r, c)
    return [_conv_silu(read(bb, r0), ext_ref, bb, cw_ref, c, lc, bias) for rnd in rounds for bb, r0 in rnd]


def _chunk_reader(ref, r, c):
    if r == c:
        whole = ref[...].astype(F32)
        return lambda bb, r0: whole[bb * c:(bb + 1) * c]
    return lambda bb, r0: ref[pl.ds(bb * r + r0, c), :].astype(F32)


def _chunk_writer(ref, bt, r, c):
    if r != c:
        return (lambda bb, r0, val: ref.__setitem__((pl.ds(bb * r + r0, c), slice(None)), val.astype(ref.dtype)),
                lambda: None)
    parts = {}

    def flush():
        ref[...] = jnp.concatenate([parts[bb] for bb in range(bt)], axis=0).astype(ref.dtype)
    return (lambda bb, r0, val: parts.__setitem__(bb, val)), flush


def _run_rounds(process, bt, r, c, cpi):
    g = r // c
    if g == 1:
        process([[(bb, 0) for bb in range(bt)]])
    else:
        def body(ci, carry):
            process([[(bb, pl.multiple_of((ci * cpi + t) * c, c)) for bb in range(bt)] for t in range(cpi)])
            return carry
        lax.fori_loop(0, g // cpi, body, 0)


INV_BASE = SUBLANES


def _inverse_masks(row, col, c):
    sh = lambda x, s: jnp.right_shift(x, int(math.log2(s)))
    diag = sh(row, INV_BASE) == sh(col, INV_BASE)
    merges = []
    s = INV_BASE
    while s < c:
        merges.append((sh(row, 2 * s) == sh(col, 2 * s))
                      & (jnp.bitwise_and(sh(row, s), 1) == 1) & (jnp.bitwise_and(sh(col, s), 1) == 0))
        s *= 2
    return diag, merges


def _unit_lower_inverses(lmats, eye_f, masks):
    diag, merges = masks
    npows = [-jnp.where(diag, l, 0.0) for l in lmats]
    ps = [eye_f + n for n in npows]
    for _ in range(int(math.log2(INV_BASE)) - 1):
        npows = [_dot(n, n) for n in npows]
        ps = [p + _dot(p, n) for p, n in zip(ps, npows)]
    for m in merges:
        ts = [_dot(p, jnp.where(m, l, 0.0)) for p, l in zip(ps, lmats)]
        ps = [p - _dot(t, p) for t, p in zip(ts, ps)]
    return ps


def _dn_kernel(qkv_ref, z_ref, sm_ref, cw_ref, alog_ref, bias_ref, nw_ref, cin_ref, sin_ref,
               o_ref, cout_ref, sout_ref, ext_ref, *, bt, r, c, lc, ng, cpi):
    j = pl.program_id(1)

    @pl.when(j == 0)
    def _():
        ext_ref[:, CARRY_ROW:SUBLANES, :] = cin_ref[...]
        sout_ref[...] = sin_ref[...]

    row = _iota((c, c), 0)
    col = _iota((c, c), 1)
    tri_incl = row >= col
    tri_strict = row > col
    eye_f = (row == col).astype(F32)
    tril_bf = tri_incl.astype(BF16)
    valid = _iota((c, 1), 0) < lc
    nexp_a = -jnp.exp(alog_ref[...])
    bias = bias_ref[...]
    inv_masks = _inverse_masks(row, col, c)
    heads = range(DN_HEADS)
    read_z, read_sm = (_chunk_reader(ref, r, c) for ref in (z_ref, sm_ref))
    put_o, flush_o = _chunk_writer(o_ref, bt, r, c)

    def prep(bb, r0, y):
        sm = read_sm(bb, r0)
        g_all = nexp_a * _softplus(sm + bias)
        beta_all = _sigmoid(sm)
        if lc < c:
            g_all = jnp.where(valid, g_all, 0.0)
            beta_all = jnp.where(valid, beta_all, 0.0)
        cs_all = _dot_exact_lhs(tril_bf, g_all)
        cs_t = cs_all.T
        ecs_all = jnp.exp(cs_all)
        per_head = []
        for h in heads:
            q = y[:, h * DN_DK:(h + 1) * DN_DK]
            k = y[:, DN_KEY + h * DN_DK:DN_KEY + (h + 1) * DN_DK]
            v = y[:, 2 * DN_KEY + h * DN_DV:2 * DN_KEY + (h + 1) * DN_DV]
            q = q * lax.rsqrt(jnp.sum(q * q, axis=-1, keepdims=True) + EPS) * (DN_DK ** -0.5)
            k = k * lax.rsqrt(jnp.sum(k * k, axis=-1, keepdims=True) + EPS)
            if lc < c:
                k = jnp.where(valid, k, 0.0)
            cs = cs_all[:, SM_A + h:SM_A + h + 1]
            ecs = ecs_all[:, SM_A + h:SM_A + h + 1]
            beta = beta_all[:, SM_B + h:SM_B + h + 1]
            last = cs_all[c - 1:c, SM_A + h:SM_A + h + 1]
            decay = jnp.exp(jnp.where(tri_incl, cs - cs_t[SM_A + h:SM_A + h + 1, :], -jnp.inf))
            per_head.append(dict(q_ecs=(q * ecs).astype(BF16), q=q.astype(BF16), k=k.astype(BF16), beta=beta,
                                 decay=decay, elast=jnp.exp(last),
                                 k_tail=(k * jnp.exp(last - cs)).astype(BF16),
                                 rhs=jnp.concatenate([v * beta, k * (beta * ecs)], axis=1).astype(BF16)))
        return per_head

    def process(rounds):
        items = [it for rnd in rounds for it in rnd]
        conv = _conv_rounds(qkv_ref, rounds, ext_ref, cw_ref, r, c, lc)
        pre = [prep(bb, r0, y) for (bb, r0), y in zip(items, conv)]
        chains = [(i, h) for i in range(len(items)) for h in heads]
        a = {ch: pre[ch[0]][ch[1]] for ch in chains}
        kk = {ch: _dot_nt(a[ch]["k"], a[ch]["k"]) for ch in chains}
        qk = {ch: _dot_nt(a[ch]["q"], a[ch]["k"]) * a[ch]["decay"] for ch in chains}
        lmats = [jnp.where(tri_strict, a[ch]["beta"] * kk[ch] * a[ch]["decay"], 0.0) for ch in chains]
        pinv = _unit_lower_inverses(lmats, eye_f, inv_masks)
        sol = {ch: _dot(p, a[ch]["rhs"]) for ch, p in zip(chains, pinv)}
        first = 0
        for rnd in rounds:
            idx = range(first, first + len(rnd))
            first += len(rnd)
            rch = [(i, h) for i in idx for h in heads]
            s = {ch: sout_ref[items[ch[0]][0], ch[1]] for ch in rch}
            ws_qs = {ch: _dot(jnp.concatenate([sol[ch][:, DN_DV:].astype(BF16), a[ch]["q_ecs"]], axis=0), s[ch])
                     for ch in rch}
            v_new = {ch: (sol[ch][:, :DN_DV] - ws_qs[ch][:c]).astype(BF16) for ch in rch}
            o = {ch: ws_qs[ch][c:] + _dot(qk[ch], v_new[ch]) for ch in rch}
            for ch in rch:
                sout_ref[items[ch[0]][0], ch[1]] = a[ch]["elast"] * s[ch] + _dot_tn(a[ch]["k_tail"], v_new[ch])
            for i in idx:
                bb, r0 = items[i]
                z = read_z(bb, r0)
                put_o(bb, r0, jnp.concatenate(
                    [_rms(o[(i, h)], nw_ref[...]) * _silu(z[:, h * DN_DV:(h + 1) * DN_DV]) for h in heads], axis=1))

    _run_rounds(process, bt, r, c, cpi)
    flush_o()

    @pl.when(j == ng - 1)
    def _():
        cout_ref[...] = ext_ref[:, CARRY_ROW:SUBLANES, :]


def _small_row(vals, offset, fill=0.0):
    row = jnp.full((1, LANES), fill, F32)
    return lax.dynamic_update_slice(row, vals.astype(F32).reshape(1, -1), (0, offset))


def _dn_branch(proj, small, conv_w, a_log, dt_bias, norm_w, conv_in, rec_in, *, bt, r, c, lc, ng, cpi):
    nb = conv_in.shape[0] // bt
    rows = bt * r
    t = proj.shape[0]
    kern = functools.partial(_dn_kernel, bt=bt, r=r, c=c, lc=lc, ng=ng, cpi=cpi)
    rowmap = lambda col: (lambda i, j: (i * ng + j, col))
    const = lambda i, j: (0, 0)
    return pl.pallas_call(
        kern,
        grid=(nb, ng),
        in_specs=[pl.BlockSpec((rows, DN_CONV_DIM), rowmap(COL_DN_QKV // DN_CONV_DIM)),
                  pl.BlockSpec((rows, DN_VAL), rowmap(COL_DN_Z // DN_VAL)),
                  pl.BlockSpec((rows, LANES), rowmap(0)),
                  pl.BlockSpec((CONV_K, DN_CONV_DIM), const),
                  pl.BlockSpec((1, LANES), const),
                  pl.BlockSpec((1, LANES), const),
                  pl.BlockSpec((1, DN_DV), const),
                  pl.BlockSpec((bt, CONV_K - 1, DN_CONV_DIM), lambda i, j: (i, 0, 0)),
                  pl.BlockSpec((bt, DN_HEADS, DN_DK, DN_DV), lambda i, j: (i, 0, 0, 0))],
        out_specs=[pl.BlockSpec((rows, DN_VAL), lambda i, j: (i * ng + j, 0)),
                   pl.BlockSpec((bt, CONV_K - 1, DN_CONV_DIM), lambda i, j: (i, 0, 0)),
                   pl.BlockSpec((bt, DN_HEADS, DN_DK, DN_DV), lambda i, j: (i, 0, 0, 0))],
        out_shape=[jax.ShapeDtypeStruct((t, DN_VAL), BF16),
                   jax.ShapeDtypeStruct(conv_in.shape, F32),
                   jax.ShapeDtypeStruct(rec_in.shape, F32)],
        scratch_shapes=[pltpu.VMEM((bt, c + SUBLANES, DN_CONV_DIM), F32)],
        compiler_params=_cparams("arbitrary", "arbitrary"),
        name="gated_delta",
    )(proj, proj, small, conv_w, _small_row(a_log, SM_A), _small_row(dt_bias, SM_A),
      norm_w.reshape(1, DN_DV), conv_in, rec_in)


def _ssd_kernel(xbc_ref, z_ref, sm_ref, cw_ref, cb_ref, alog_ref, bias_ref, dvec_ref, nw_ref, cin_ref,
                hin_ref, y_ref, cout_ref, hout_ref, ext_ref, *, bt, r, c, lc, ng, cpi):
    j = pl.program_id(1)

    @pl.when(j == 0)
    def _():
        ext_ref[:, CARRY_ROW:SUBLANES, :] = cin_ref[...]
        hout_ref[...] = hin_ref[...]

    tril_bf = (_iota((c, c), 0) >= _iota((c, c), 1)).astype(BF16)
    row2 = _iota((c, 2 * c), 0)
    lane2 = _iota((c, 2 * c), 1)
    left2 = lane2 < c
    tri2 = row2 >= jnp.where(left2, lane2, lane2 - c)
    left_x = _iota((c, LANES), 1) < SSM_HEADDIM
    top_h = _iota((2 * SSM_HEADDIM, 1), 0) < SSM_HEADDIM
    valid = _iota((c, 1), 0) < lc
    nexp_a = -jnp.exp(alog_ref[...])
    bias = bias_ref[...]
    gw = SSM_INNER // SSM_GROUPS
    pairs_per_group = SSM_HEADS // SSM_GROUPS // 2

    pairs = range(SSM_HEADS // 2)
    pair_rows = lambda p: slice(p * LANES, (p + 1) * LANES)
    read_z, read_sm = (_chunk_reader(ref, r, c) for ref in (z_ref, sm_ref))
    put_y, flush_y = _chunk_writer(y_ref, bt, r, c)

    def prep(bb, r0, y):
        dt_all = _softplus(read_sm(bb, r0) + bias)
        if lc < c:
            dt_all = jnp.where(valid, dt_all, 0.0)
        cs_all = _dot_exact_lhs(tril_bf, dt_all * nexp_a)
        cs_t2 = jnp.concatenate([cs_all, cs_all], axis=0).T
        last = cs_all[c - 1:c, :]
        bm = [y[:, SSM_INNER + g * SSM_STATE:SSM_INNER + (g + 1) * SSM_STATE].astype(BF16)
              for g in range(SSM_GROUPS)]
        cm = [y[:, SSM_INNER + (SSM_GROUPS + g) * SSM_STATE:SSM_INNER + (SSM_GROUPS + g + 1) * SSM_STATE].astype(BF16)
              for g in range(SSM_GROUPS)]
        return dict(y=y, dt=dt_all, cs=cs_all, cs_t2=cs_t2, ecs=jnp.exp(cs_all),
                    tail=jnp.exp(last - cs_all), elast=jnp.exp(last), bm=bm, cm=cm)

    def process(rounds):
        items = [it for rnd in rounds for it in rnd]
        conv = _conv_rounds(xbc_ref, rounds, ext_ref, cw_ref, r, c, lc, bias=cb_ref[...])
        pre = [prep(bb, r0, y) for (bb, r0), y in zip(items, conv)]
        ids = range(len(items))
        cb2 = {(i, g): _dot_nt(pre[i]["cm"][g], jnp.concatenate([pre[i]["bm"][g]] * 2, axis=0))
               for i in ids for g in range(SSM_GROUPS)}
        y_intra, upd, xs_of = {}, {}, {}
        for i in ids:
            a = pre[i]
            for p in pairs:
                g = p // pairs_per_group
                la, lb = SM_DT + 2 * p, SM_DT + 2 * p + 1
                both = lambda v, mask=left_x: jnp.where(mask, v[:, la:la + 1], v[:, lb:lb + 1])
                xs = a["y"][:, pair_rows(p)]
                xdt = xs * both(a["dt"])
                diff = both(a["cs"], left2) - jnp.where(left2[0:1], a["cs_t2"][la:la + 1, :], a["cs_t2"][lb:lb + 1, :])
                decay2 = jnp.exp(jnp.where(tri2, diff, -jnp.inf))
                rhs = jnp.concatenate([jnp.where(left_x, xdt, 0.0), jnp.where(left_x, 0.0, xdt)], axis=0)
                y_intra[i, p] = _dot(cb2[i, g] * decay2, rhs)
                upd[i, p] = _dot_tn(xdt * both(a["tail"]), a["bm"][g])
                xs_of[i, p] = xs
        first = 0
        for rnd in rounds:
            idx = range(first, first + len(rnd))
            first += len(rnd)
            for i in idx:
                a = pre[i]
                bb, r0 = items[i]
                outs = []
                for p in pairs:
                    g = p // pairs_per_group
                    la, lb = SM_DT + 2 * p, SM_DT + 2 * p + 1
                    hs = hout_ref[bb, pair_rows(p), :]
                    y_inter = _dot_nt(a["cm"][g], hs) * jnp.where(left_x, a["ecs"][:, la:la + 1], a["ecs"][:, lb:lb + 1])
                    hout_ref[bb, pair_rows(p), :] = (
                        jnp.where(top_h, a["elast"][:, la:la + 1], a["elast"][:, lb:lb + 1]) * hs + upd[i, p])
                    outs.append(y_intra[i, p] + y_inter + dvec_ref[:, pair_rows(p)] * xs_of[i, p])
                yz = jnp.concatenate(outs, axis=1) * _silu(read_z(bb, r0))
                put_y(bb, r0, jnp.concatenate(
                    [_rms(yz[:, g * gw:(g + 1) * gw], nw_ref[:, g * gw:(g + 1) * gw]) for g in range(SSM_GROUPS)],
                    axis=1))

    _run_rounds(process, bt, r, c, cpi)
    flush_y()

    @pl.when(j == ng - 1)
    def _():
        cout_ref[...] = ext_ref[:, CARRY_ROW:SUBLANES, :]


def _ssd_branch(proj, small, conv_w, conv_b, a_log, dt_bias, d_skip, norm_w, conv_in, rec_in, *, bt, r, c, lc, ng, cpi):
    nb = conv_in.shape[0] // bt
    rows = bt * r
    t = proj.shape[0]
    hrows = SSM_HEADS * SSM_HEADDIM
    rec2 = rec_in.reshape(rec_in.shape[0], hrows, SSM_STATE)
    kern = functools.partial(_ssd_kernel, bt=bt, r=r, c=c, lc=lc, ng=ng, cpi=cpi)
    rowmap = lambda col: (lambda i, j: (i * ng + j, col))
    const = lambda i, j: (0, 0)
    y, conv_out, rec_out = pl.pallas_call(
        kern,
        grid=(nb, ng),
        in_specs=[pl.BlockSpec((rows, SSM_CONV_DIM), rowmap(COL_SSM_XBC // SSM_CONV_DIM)),
                  pl.BlockSpec((rows, SSM_INNER), rowmap(COL_SSM_Z // SSM_INNER)),
                  pl.BlockSpec((rows, LANES), rowmap(0)),
                  pl.BlockSpec((CONV_K, SSM_CONV_DIM), const),
                  pl.BlockSpec((1, SSM_CONV_DIM), const),
                  pl.BlockSpec((1, LANES), const),
                  pl.BlockSpec((1, LANES), const),
                  pl.BlockSpec((1, SSM_INNER), const),
                  pl.BlockSpec((1, SSM_INNER), const),
                  pl.BlockSpec((bt, CONV_K - 1, SSM_CONV_DIM), lambda i, j: (i, 0, 0)),
                  pl.BlockSpec((bt, hrows, SSM_STATE), lambda i, j: (i, 0, 0))],
        out_specs=[pl.BlockSpec((rows, SSM_INNER), lambda i, j: (i * ng + j, 0)),
                   pl.BlockSpec((bt, CONV_K - 1, SSM_CONV_DIM), lambda i, j: (i, 0, 0)),
                   pl.BlockSpec((bt, hrows, SSM_STATE), lambda i, j: (i, 0, 0))],
        out_shape=[jax.ShapeDtypeStruct((t, SSM_INNER), BF16),
                   jax.ShapeDtypeStruct(conv_in.shape, F32),
                   jax.ShapeDtypeStruct(rec2.shape, F32)],
        scratch_shapes=[pltpu.VMEM((bt, c + SUBLANES, SSM_CONV_DIM), F32)],
        compiler_params=_cparams("arbitrary", "arbitrary"),
        name="ssd_scan",
    )(proj, proj, small, conv_w, conv_b.reshape(1, -1), _small_row(a_log, SM_DT), _small_row(dt_bias, SM_DT),
      jnp.repeat(d_skip.astype(F32), SSM_HEADDIM).reshape(1, SSM_INNER), norm_w.reshape(1, SSM_INNER),
      conv_in, rec2)
    return y, conv_out, rec_out.reshape(rec_in.shape)


PLAN_E0, PLAN_E1, PLAN_C0, PLAN_C1 = 0, 1, 2, 3


def _route(logits):
    lane = _iota(logits.shape, 1).astype(F32)
    big = float(LANES)
    is_group = (lane >= RT_GROUP) & (lane < RT_GROUP + MOE_GROUPS)
    gl = jnp.where(is_group, logits, -jnp.inf)
    gmax = jnp.max(gl, axis=-1, keepdims=True)
    g_sel = jnp.min(jnp.where(gl == gmax, lane, big), axis=-1, keepdims=True) - RT_GROUP
    p_group = 1.0 / jnp.sum(jnp.exp(gl - gmax), axis=-1, keepdims=True)
    e_lo = RT_EXPERT + MOE_PER_GROUP * g_sel
    in_grp = (lane >= e_lo) & (lane < e_lo + MOE_PER_GROUP)
    el = jnp.where(in_grp, logits, -jnp.inf)
    ee = jnp.exp(el - jnp.max(el, axis=-1, keepdims=True))
    pe = jnp.where(in_grp, ee / jnp.sum(ee, axis=-1, keepdims=True), -1.0)
    p1 = jnp.max(pe, axis=-1, keepdims=True)
    i1 = jnp.min(jnp.where(pe == p1, lane, big), axis=-1, keepdims=True)
    pe2 = jnp.where(lane == i1, -1.0, pe)
    p2 = jnp.max(pe2, axis=-1, keepdims=True)
    i2 = jnp.min(jnp.where(pe2 == p2, lane, big), axis=-1, keepdims=True)
    tot = p1 + p2
    plan = jnp.where(lane == PLAN_E0, i1 - RT_EXPERT, 0.0) + jnp.where(lane == PLAN_E1, i2 - RT_EXPERT, 0.0)
    return plan + jnp.where(lane == PLAN_C0, p_group * p1 / tot, 0.0) + jnp.where(lane == PLAN_C1, p_group * p2 / tot, 0.0)


def _tile_ranks(plan):
    tp = plan.shape[0]
    lane = _iota(plan.shape, 1).astype(F32)
    sel0 = lane == plan[:, PLAN_E0:PLAN_E0 + 1]
    sel1 = lane == plan[:, PLAN_E1:PLAN_E1 + 1]
    sel = (sel0 | sel1).astype(BF16)
    before = (_iota((tp, tp), 0) > _iota((tp, tp), 1)).astype(BF16)
    excl = jnp.dot(before, sel, preferred_element_type=F32)
    r0 = jnp.sum(jnp.where(sel0, excl, 0.0), axis=-1, keepdims=True)
    r1 = jnp.sum(jnp.where(sel1, excl, 0.0), axis=-1, keepdims=True)
    rank = jnp.where(lane == PLAN_E0, r0, 0.0) + jnp.where(lane == PLAN_E1, r1, 0.0)
    return rank, jnp.sum(sel.astype(F32), axis=0, keepdims=True)


def _merge_kernel(x_ref, og_ref, ys_ref, gates_ref, gt1_ref, sc2_ref, sh2_ref, nw2_ref, wdn_ref, wssm_ref,
                  wout_ref, wr_ref, br_ref, x1_ref, h2_ref, plan_ref, rank_ref, cnt_ref):
    y_dn = _dot(og_ref[...], wdn_ref[...])
    y_ssm = _dot(ys_ref[...], wssm_ref[...])
    merged = (_sigmoid(gates_ref[:, :D_MODEL].astype(F32)) * y_dn
              + _sigmoid(gates_ref[:, D_MODEL:].astype(F32)) * y_ssm)
    x1 = x_ref[...] + gt1_ref[...] * _dot(merged, wout_ref[...])
    x1_ref[...] = x1
    h2 = _rms(x1, nw2_ref[...]) * (1.0 + sc2_ref[...]) + sh2_ref[...]
    h2_ref[...] = h2.astype(BF16)
    plan = _route(_dot_x3(h2, wr_ref[...]) + br_ref[...])
    plan_ref[...] = plan
    rank_ref[...], cnt_ref[...] = _tile_ranks(plan)


def _merge(x3, og, ys, proj, gt1, sc2, sh2, nw2, wdn, wssm, wout, wr, br, tm):
    bx, lx, d = x3.shape
    assert tm == TOK_TILE
    nl = lx // tm
    rowmap = lambda col: (lambda b, i: (b * nl + i, col))
    const = lambda b, i: (0, 0)
    tok = lambda w, dt: jax.ShapeDtypeStruct((bx, lx, w), dt)
    return pl.pallas_call(
        _merge_kernel,
        grid=(bx, nl),
        in_specs=[pl.BlockSpec((None, tm, d), lambda b, i: (b, i, 0)),
                  pl.BlockSpec((tm, DN_VAL), rowmap(0)),
                  pl.BlockSpec((tm, SSM_INNER), rowmap(0)),
                  pl.BlockSpec((tm, 2 * d), rowmap(COL_GATES // (2 * d))),
                  _mod_spec(gt1, tm), _mod_spec(sc2, tm), _mod_spec(sh2, tm),
                  pl.BlockSpec((1, d), const),
                  pl.BlockSpec(wdn.shape, const), pl.BlockSpec(wssm.shape, const), pl.BlockSpec(wout.shape, const),
                  pl.BlockSpec(wr.shape, const), pl.BlockSpec((1, LANES), const)],
        out_specs=[pl.BlockSpec((None, tm, d), lambda b, i: (b, i, 0)),
                   pl.BlockSpec((None, tm, d), lambda b, i: (b, i, 0)),
                   pl.BlockSpec((None, tm, LANES), lambda b, i: (b, i, 0)),
                   pl.BlockSpec((None, tm, LANES), lambda b, i: (b, i, 0)),
                   pl.BlockSpec((None, 1, LANES), lambda b, i: (b * nl + i, 0, 0))],
        out_shape=[tok(d, F32), tok(d, BF16), tok(LANES, F32), tok(LANES, F32),
                   jax.ShapeDtypeStruct((bx * nl, 1, LANES), F32)],
        compiler_params=_cparams("arbitrary", "arbitrary"),
        name="merge_route",
    )(x3, og, ys, proj, gt1, sc2, sh2, nw2.reshape(1, d), wdn, wssm, wout, wr, br)


FFN_ROWS = 512
TOK_TILE = 512
RUN_ALIGN = 2 * SUBLANES
RUN_BIG = 4 * RUN_ALIGN
RUN_SMALL = (2 * RUN_ALIGN, RUN_ALIGN)
LOCAL_ROWS = 2 * TOK_TILE + MOE_EXPERTS * RUN_ALIGN


def _local_slots(plan, rank, off_row):
    lane = _iota(plan.shape, 1).astype(F32)
    slots = []
    for k in (PLAN_E0, PLAN_E1):
        off = jnp.sum(jnp.where(lane == plan[:, k:k + 1], off_row, 0.0), axis=-1, keepdims=True)
        slots.append(off + rank[:, k:k + 1])
    return slots


def _run_blocks(p8_ref, loff_ref, base_ref, act):
    for e in range(MOE_EXPERTS):
        n = p8_ref[0, e]
        lo = loff_ref[0, e]
        go = base_ref[0, e]

        def big(k, carry, lo=lo, go=go):
            off = k * RUN_BIG
            act(pl.multiple_of(lo + off, RUN_ALIGN), pl.multiple_of(go + off, RUN_ALIGN), RUN_BIG)
            return carry

        lax.fori_loop(0, lax.shift_right_logical(n, RUN_BIG.bit_length() - 1), big, 0)
        for size in RUN_SMALL:
            @pl.when(jnp.bitwise_and(n, size) != 0)
            def _(size=size, n=n, lo=lo, go=go):
                done = jnp.bitwise_and(n, ~(2 * size - 1))
                act(pl.multiple_of(lo + done, RUN_ALIGN), pl.multiple_of(go + done, RUN_ALIGN), size)


def _dispatch_kernel(tv_ref, p8_ref, loff_ref, base_ref, plan_ref, rank_ref, off_ref, h2_ref, *rest, first):
    xs_ref, buf_ref, zero_ref, sem, zsem = rest[-5:]

    @pl.when((pl.program_id(0) == 0) & first)
    def _():
        zero_ref[...] = jnp.zeros_like(zero_ref)

        def zero_copy(i):
            rows = pl.ds(pl.multiple_of(i * FFN_ROWS, FFN_ROWS), FFN_ROWS)
            return pltpu.make_async_copy(zero_ref, xs_ref.at[rows], zsem)

        def zero_tiles(act):
            def body(i, carry):
                @pl.when(tv_ref[i] < FFN_ROWS)
                def _():
                    act(zero_copy(i))
                return carry
            lax.fori_loop(0, xs_ref.shape[0] // FFN_ROWS, body, 0)

        zero_tiles(lambda cp: cp.start())
        zero_tiles(lambda cp: cp.wait())

    s0, s1 = _local_slots(plan_ref[...], rank_ref[...], off_ref[...])
    row = _iota((1, LOCAL_ROWS), 1).astype(F32)
    onehot_t = ((row == s0) | (row == s1)).astype(BF16)
    buf_ref[...] = lax.dot_general(onehot_t, h2_ref[...], (((0,), (0,)), ((), ())),
                                   preferred_element_type=F32).astype(BF16)

    def copy(lo, go, size):
        return pltpu.make_async_copy(buf_ref.at[pl.ds(lo, size)], xs_ref.at[pl.ds(go, size)], sem)

    _run_blocks(p8_ref, loff_ref, base_ref, lambda lo, go, size: copy(lo, go, size).start())
    _run_blocks(p8_ref, loff_ref, base_ref, lambda lo, go, size: copy(lo, go, size).wait())


def _tile_scalars():
    return pl.BlockSpec((None, 1, LANES), lambda i, *_: (i, 0, 0), memory_space=pltpu.SMEM)


def _dispatch(tile_valid, p8, loff, base, plan2, rank, off_f, h2, rows, xs_prev=None):
    t, d = h2.shape
    tt = TOK_TILE
    tok = lambda w: pl.BlockSpec((tt, w), lambda i, tv: (i, 0))
    operands = (tile_valid, p8, loff, base, plan2, rank, off_f, h2) + (() if xs_prev is None else (xs_prev,))
    return pl.pallas_call(
        functools.partial(_dispatch_kernel, first=xs_prev is None),
        grid_spec=pltpu.PrefetchScalarGridSpec(
            num_scalar_prefetch=1,
            grid=(t // tt,),
            in_specs=[_tile_scalars(), _tile_scalars(), _tile_scalars(),
                      tok(LANES), tok(LANES),
                      pl.BlockSpec((None, 1, LANES), lambda i, tv: (i, 0, 0)),
                      tok(d)] + ([] if xs_prev is None else [pl.BlockSpec(memory_space=pl.ANY)]),
            out_specs=pl.BlockSpec(memory_space=pl.ANY),
            scratch_shapes=[pltpu.VMEM((LOCAL_ROWS, d), BF16), pltpu.VMEM((FFN_ROWS, d), BF16),
                            pltpu.SemaphoreType.DMA(()), pltpu.SemaphoreType.DMA(())]),
        out_shape=jax.ShapeDtypeStruct((rows, d), BF16),
        input_output_aliases={} if xs_prev is None else {len(operands) - 1: 0},
        compiler_params=_cparams("arbitrary"),
        name="moe_dispatch",
    )(*operands)


def _ffn_kernel(te_ref, tv_ref, xs_ref, wg_ref, wu_ref, wd_ref, o_ref):
    nv = tv_ref[pl.program_id(0)]

    half = FFN_ROWS // 2

    def swiglu(rows):
        x = xs_ref[rows, :]
        o_ref[rows, :] = _dot(_silu(_dot(x, wg_ref[...])) * _dot(x, wu_ref[...]), wd_ref[...]).astype(BF16)

    @pl.when(nv > half)
    def _():
        swiglu(slice(None))

    @pl.when((nv > 0) & (nv <= half))
    def _():
        swiglu(slice(0, half))
        o_ref[half:, :] = jnp.zeros((half, o_ref.shape[1]), o_ref.dtype)

    @pl.when(nv <= 0)
    def _():
        o_ref[...] = jnp.zeros_like(o_ref)


def _ffn(tile_expert, tile_valid, xs, wg, wu, wd):
    rows, d = xs.shape
    wmap = lambda i, te, tv: (te[i], 0, 0)
    return pl.pallas_call(
        _ffn_kernel,
        grid_spec=pltpu.PrefetchScalarGridSpec(
            num_scalar_prefetch=2,
            grid=(rows // FFN_ROWS,),
            in_specs=[pl.BlockSpec((FFN_ROWS, d), lambda i, te, tv: (i, 0)),
                      pl.BlockSpec((None, d, MOE_FF), wmap),
                      pl.BlockSpec((None, d, MOE_FF), wmap),
                      pl.BlockSpec((None, MOE_FF, d), wmap)],
            out_specs=pl.BlockSpec((FFN_ROWS, d), lambda i, te, tv: (i, 0))),
        out_shape=jax.ShapeDtypeStruct((rows, d), BF16),
        compiler_params=_cparams("arbitrary"),
        name="moe_ffn",
    )(tile_expert, tile_valid, xs, wg, wu, wd)


def _combine_kernel(p8_ref, loff_ref, base_ref, ys_ref, plan_ref, rank_ref, off_ref, x1_ref, gt2_ref, fsc_ref,
                    fsh_ref, fnw_ref, y_ref, buf_ref, sem):
    @pl.when((pl.program_id(0) == 0) & (pl.program_id(1) == 0))
    def _():
        buf_ref[...] = jnp.zeros_like(buf_ref)

    def copy(lo, go, size):
        return pltpu.make_async_copy(ys_ref.at[pl.ds(go, size)], buf_ref.at[pl.ds(lo, size)], sem)

    _run_blocks(p8_ref, loff_ref, base_ref, lambda lo, go, size: copy(lo, go, size).start())
    plan = plan_ref[...]
    s0, s1 = _local_slots(plan, rank_ref[...], off_ref[...])
    row = _iota((1, LOCAL_ROWS), 1).astype(F32)
    weights = (jnp.where(row == s0, plan[:, PLAN_C0:PLAN_C0 + 1], 0.0)
               + jnp.where(row == s1, plan[:, PLAN_C1:PLAN_C1 + 1], 0.0))
    _run_blocks(p8_ref, loff_ref, base_ref, lambda lo, go, size: copy(lo, go, size).wait())
    total = loff_ref[0, MOE_EXPERTS - 1] + p8_ref[0, MOE_EXPERTS - 1]
    filled = _iota((LOCAL_ROWS, 1), 0) < total
    moe = _dot(weights, jnp.where(filled, buf_ref[...], jnp.zeros((), BF16)))
    x2 = x1_ref[...] + gt2_ref[...] * moe
    y_ref[...] = _rms(x2, fnw_ref[...]) * (1.0 + fsc_ref[...]) + fsh_ref[...]


def _combine(p8, loff, base, ys, plan, rank3, off_f, x1, gt2, fsc, fsh, fnw):
    bx, lx, d = x1.shape
    tt = TOK_TILE
    nl = lx // tt
    tile = lambda b, i: b * nl + i
    scal = pl.BlockSpec((None, 1, LANES), lambda b, i: (tile(b, i), 0, 0), memory_space=pltpu.SMEM)
    tokspec = lambda w: pl.BlockSpec((None, tt, w), lambda b, i: (b, i, 0))
    return pl.pallas_call(
        _combine_kernel,
        grid=(bx, nl),
        in_specs=[scal, scal, scal,
                  pl.BlockSpec(memory_space=pl.ANY),
                  tokspec(LANES), tokspec(LANES),
                  pl.BlockSpec((None, 1, LANES), lambda b, i: (tile(b, i), 0, 0)),
                  tokspec(d), _mod_spec(gt2, tt), _mod_spec(fsc, tt), _mod_spec(fsh, tt),
                  pl.BlockSpec((1, d), lambda b, i: (0, 0))],
        out_specs=tokspec(d),
        out_shape=jax.ShapeDtypeStruct((bx, lx, d), F32),
        scratch_shapes=[pltpu.VMEM((LOCAL_ROWS, d), BF16), pltpu.SemaphoreType.DMA(())],
        compiler_params=_cparams("arbitrary", "arbitrary"),
        name="moe_combine_final",
    )(p8, loff, base, ys, plan, rank3, off_f, x1, gt2, fsc, fsh, fnw.reshape(1, d))


def _moe(groups, wg, wu, wd, fnw):
    d = groups[0]["x1"].shape[-1]
    ntiles = [g["x1"].shape[0] * g["x1"].shape[1] // TOK_TILE for g in groups]
    t = sum(ntiles) * TOK_TILE
    n = jnp.concatenate([g["cnt"][:, 0, :MOE_EXPERTS] for g in groups], axis=0).astype(jnp.int32)
    p8 = (n + RUN_ALIGN - 1) // RUN_ALIGN * RUN_ALIGN
    loff = jnp.cumsum(p8, axis=1) - p8
    erows = jnp.sum(p8, axis=0)
    epad = (erows + FFN_ROWS - 1) // FFN_ROWS * FFN_ROWS
    ends = jnp.cumsum(epad)
    starts = ends - epad
    base = starts[None, :] + jnp.cumsum(p8, axis=0) - p8
    lanes = lambda a: jnp.pad(a, ((0, 0), (0, LANES - MOE_EXPERTS)))[:, None, :]
    rows = (-(-(2 * t + (t // TOK_TILE) * MOE_EXPERTS * (RUN_ALIGN - 1)) // FFN_ROWS) + MOE_EXPERTS) * FFN_ROWS
    tile_start = jnp.arange(rows // FFN_ROWS, dtype=jnp.int32) * FFN_ROWS
    tile_expert = jnp.minimum(jnp.sum(tile_start[:, None] >= ends[None, :], axis=1), MOE_EXPERTS - 1).astype(jnp.int32)
    hot = tile_expert[:, None] == jnp.arange(MOE_EXPERTS)[None, :]
    valid_rows = lambda er: jnp.clip(jnp.sum(jnp.where(hot, er - (tile_start[:, None] - starts), 0), axis=1),
                                     0, FFN_ROWS).astype(jnp.int32)
    tile_valid = valid_rows(erows)
    p8l, loffl, basel = lanes(p8), lanes(loff), lanes(base)
    off_f = loffl.astype(F32)
    spans, first = [], 0
    for nt in ntiles:
        spans.append(slice(first, first + nt))
        first += nt
    first_valid = valid_rows(jnp.sum(p8[spans[0]], axis=0))
    xs = None
    for g, sp in zip(groups, spans):
        tg = g["h2"].shape[0] * g["h2"].shape[1]
        xs = _dispatch(first_valid, p8l[sp], loffl[sp], basel[sp], g["plan"].reshape(tg, LANES),
                       g["rank"].reshape(tg, LANES), off_f[sp], g["h2"].reshape(tg, d), rows, xs_prev=xs)
    ys = _ffn(tile_expert, tile_valid, xs, wg, wu, wd)
    return [_combine(p8l[sp], loffl[sp], basel[sp], ys, g["plan"], g["rank"], off_f[sp], g["x1"], g["gt2"],
                     g["fsc"], g["fsh"], fnw) for g, sp in zip(groups, spans)]


def _prep_layer(lp):
    w_in = lp["w_in"]
    offs = [0]
    for s in (DN_CONV_DIM, DN_VAL, DN_HEADS, DN_HEADS, SSM_CONV_DIM, SSM_INNER, SSM_HEADS, D_MODEL, D_MODEL):
        offs.append(offs[-1] + s)
    seg = lambda i: w_in[:, offs[i]:offs[i + 1]]
    small = jnp.concatenate([seg(2), seg(3), seg(6)], axis=1)
    pad = jnp.zeros((D_MODEL, PROJ_N - COL_SMALL - small.shape[1]), F32)
    w_cat = jnp.concatenate([seg(0), seg(4), seg(5), seg(7), seg(8), seg(1), small, pad], axis=1).astype(BF16)
    wr = jnp.concatenate([lp["w_group_router"], lp["w_expert_router"],
                          jnp.zeros((D_MODEL, LANES - MOE_GROUPS - MOE_EXPERTS), F32)], axis=1)
    br = jnp.concatenate([lp["b_group_router"], lp["b_expert_router"],
                          jnp.zeros((LANES - MOE_GROUPS - MOE_EXPERTS,), F32)]).reshape(1, LANES)
    return dict(lp, w_cat=w_cat, wr=wr, br=br,
                wdn=lp["w_dn_out"].astype(BF16), wssm=lp["w_ssm_out"].astype(BF16), wout=lp["w_out"].astype(BF16),
                wg=lp["w_exp_gate"], wu=lp["w_exp_up"], wd=lp["w_exp_down"])


def _mixer(x3, mods, fins, states, lp, cfg):
    bx, lx, d = x3.shape
    scan = dict(bt=cfg["bt"], r=cfg["r"], c=cfg["c"], lc=cfg["lc"], ng=cfg["ng"])
    sh1, sc1, gt1, sh2, sc2, gt2 = mods
    dn_conv, dn_rec, ssm_conv, ssm_rec = states
    proj, small = _inproj(x3, sc1, sh1, lp["norm_mix_w"], lp["w_cat"], cfg["tm"])
    proj, small = proj.reshape(bx * lx, PROJ_N), small.reshape(bx * lx, LANES)
    og, dn_conv_new, dn_rec_new = _dn_branch(proj, small, lp["dn_conv_w"], lp["dn_A_log"], lp["dn_dt_bias"],
                                             lp["dn_norm_w"], dn_conv, dn_rec, cpi=cfg["cpi_dn"], **scan)
    ys, ssm_conv_new, ssm_rec_new = _ssd_branch(proj, small, lp["ssm_conv_w"], lp["ssm_conv_b"], lp["ssm_A_log"],
                                                lp["ssm_dt_bias"], lp["ssm_D"], lp["ssm_norm_w"],
                                                ssm_conv, ssm_rec, cpi=cfg["cpi_ssd"], **scan)
    x1, h2, plan, rank, cnt = _merge(x3, og, ys, proj, gt1, sc2, sh2, lp["norm_ffn_w"], lp["wdn"], lp["wssm"],
                                     lp["wout"], lp["wr"], lp["br"], cfg["tm_merge"])
    group = dict(x1=x1, h2=h2, plan=plan, rank=rank, cnt=cnt, gt2=gt2, fsh=fins[0], fsc=fins[1])
    return group, (dn_conv_new, dn_rec_new, ssm_conv_new, ssm_rec_new)


def _per_seq(m):
    return m[:, None, :]


def kernel(x_prompt, x_sample, c_prompt, c_sample, state_dn_conv, state_dn_rec, state_ssm_conv, state_ssm_rec, w_ada, b_ada, norm_mix_w, w_in, dn_conv_w, dn_A_log, dn_dt_bias, dn_norm_w, w_dn_out, ssm_conv_w, ssm_conv_b, ssm_A_log, ssm_dt_bias, ssm_D, ssm_norm_w, w_ssm_out, w_out, norm_ffn_w, w_group_router, b_group_router, w_expert_router, b_expert_router, w_exp_gate, w_exp_up, w_exp_down, w_ada_final, b_ada_final, final_norm_w):
    depth = w_ada.shape[0]
    assert depth == 1
    per_layer = dict(w_ada=w_ada, b_ada=b_ada, norm_mix_w=norm_mix_w, w_in=w_in, dn_conv_w=dn_conv_w,
                     dn_A_log=dn_A_log, dn_dt_bias=dn_dt_bias, dn_norm_w=dn_norm_w, w_dn_out=w_dn_out,
                     ssm_conv_w=ssm_conv_w, ssm_conv_b=ssm_conv_b, ssm_A_log=ssm_A_log, ssm_dt_bias=ssm_dt_bias,
                     ssm_D=ssm_D, ssm_norm_w=ssm_norm_w, w_ssm_out=w_ssm_out, w_out=w_out, norm_ffn_w=norm_ffn_w,
                     w_group_router=w_group_router, b_group_router=b_group_router,
                     w_expert_router=w_expert_router, b_expert_router=b_expert_router,
                     w_exp_gate=w_exp_gate, w_exp_up=w_exp_up, w_exp_down=w_exp_down)
    layers = [_prep_layer({k: v[l] for k, v in per_layer.items()}) for l in range(depth)]

    nbp, lp_, d = x_prompt.shape
    nbs, ls, _ = x_sample.shape
    c_all = jnp.concatenate([c_prompt, c_sample], axis=0)
    mod_all = [_ada(c_all, lyr["w_ada"], lyr["b_ada"]) for lyr in layers]
    fin_all = _ada(c_all, w_ada_final, b_ada_final)

    mods_p = [[_per_seq(m) for m in jnp.split(ma[:nbp], 6, axis=-1)] for ma in mod_all]
    fins_p = [_per_seq(m) for m in jnp.split(fin_all[:nbp], 2, axis=-1)]
    zeros_p = [(jnp.zeros((nbp, CONV_K - 1, DN_CONV_DIM), F32), jnp.zeros((nbp, DN_HEADS, DN_DK, DN_DV), F32),
                jnp.zeros((nbp, CONV_K - 1, SSM_CONV_DIM), F32),
                jnp.zeros((nbp, SSM_HEADS, SSM_HEADDIM, SSM_STATE), F32)) for _ in range(depth)]
    c_p = min(SCAN_CHUNK, lp_)
    r_p = min(lp_, 8 * c_p)
    cfg_p = dict(tm=min(lp_, 2048), tm_merge=min(lp_, 512), tm_moe=min(lp_, 1024), bt=1, r=r_p, c=c_p, lc=c_p, ng=lp_ // r_p,
                 cpi_dn=8 if (r_p // c_p) % 8 == 0 else 1, cpi_ssd=4 if (r_p // c_p) % 4 == 0 else 1)
    grp_p, st_p = _mixer(x_prompt, mods_p[0], fins_p, zeros_p[0], layers[0], cfg_p)

    lpad = -(-ls // SUBLANES) * SUBLANES
    bt_s = 8
    xs = jnp.pad(x_sample, ((0, 0), (0, lpad - ls), (0, 0))).reshape(1, nbs * lpad, d)
    per_tok = lambda m: jnp.repeat(m, lpad, axis=0)[None]
    mods_s = [[per_tok(m) for m in jnp.split(ma[nbp:], 6, axis=-1)] for ma in mod_all]
    fins_s = [per_tok(m) for m in jnp.split(fin_all[nbp:], 2, axis=-1)]
    st_in = [(state_dn_conv[l], state_dn_rec[l], state_ssm_conv[l], state_ssm_rec[l]) for l in range(depth)]
    ts = nbs * lpad
    cfg_s = dict(tm=ts, tm_merge=min(ts, 512), tm_moe=min(ts, 512), bt=bt_s, r=lpad, c=lpad, lc=ls, ng=1, cpi_dn=1, cpi_ssd=1)
    grp_s, st_s = _mixer(xs, mods_s[0], fins_s, st_in[0], layers[0], cfg_s)

    y_p, y_s = _moe([grp_p, grp_s], layers[0]["wg"], layers[0]["wu"], layers[0]["wd"], final_norm_w)
    y_s = y_s.reshape(nbs, lpad, d)[:, :ls]
    return (y_p, y_s) + tuple(st[None] for st in st_p) + tuple(st[None] for st in st_s)
```

```python
import functools
import math

import jax
import jax.numpy as jnp
from jax import lax
from jax.experimental import pallas as pl
from jax.experimental.pallas import tpu as pltpu

F32 = jnp.float32
BF16 = jnp.bfloat16

D_MODEL = 1024
DN_HEADS = 4
DN_DK = 128
DN_DV = 128
DN_KEY = DN_HEADS * DN_DK
DN_VAL = DN_HEADS * DN_DV
CONV_K = 4
DN_CONV_DIM = 2 * DN_KEY + DN_VAL
SSM_INNER = D_MODEL
SSM_HEADDIM = 64
SSM_HEADS = SSM_INNER // SSM_HEADDIM
SSM_GROUPS = 2
SSM_STATE = 128
SSM_CONV_DIM = SSM_INNER + 2 * SSM_GROUPS * SSM_STATE
MOE_GROUPS = 4
MOE_PER_GROUP = 8
MOE_EXPERTS = MOE_GROUPS * MOE_PER_GROUP
MOE_FF = D_MODEL // 4
EPS = 1e-6
SCAN_CHUNK = 64

LANES = 128
SUBLANES = 8
CARRY_ROW = SUBLANES - (CONV_K - 1)

COL_DN_QKV = 0
COL_SSM_XBC = COL_DN_QKV + DN_CONV_DIM
COL_SSM_Z = COL_SSM_XBC + SSM_CONV_DIM
COL_GATES = COL_SSM_Z + SSM_INNER
COL_DN_Z = COL_GATES + 2 * D_MODEL
COL_SMALL = COL_DN_Z + DN_VAL
PROJ_TN = 1792
PROJ_N = 4 * PROJ_TN
SM_A, SM_B, SM_DT = 0, DN_HEADS, 2 * DN_HEADS
RT_GROUP, RT_EXPERT = 0, MOE_GROUPS

VMEM_LIMIT = 56 * 1024 * 1024


def _cparams(*sem):
    return pltpu.CompilerParams(dimension_semantics=sem, vmem_limit_bytes=VMEM_LIMIT)


def _dot(a, b):
    return jnp.dot(a.astype(BF16), b.astype(BF16), preferred_element_type=F32)


def _dot_nt(a, b):
    return lax.dot_general(a.astype(BF16), b.astype(BF16), (((1,), (1,)), ((), ())),
                           preferred_element_type=F32)


def _dot_tn(a, b):
    return lax.dot_general(a.astype(BF16), b.astype(BF16), (((0,), (0,)), ((), ())),
                           preferred_element_type=F32)


def _split3(x):
    hi = x.astype(BF16)
    r = x - hi.astype(F32)
    mid = r.astype(BF16)
    lo = (r - mid.astype(F32)).astype(BF16)
    return hi, mid, lo


def _dot_exact_lhs(a_bf, b):
    hi, mid, lo = _split3(b)
    d = functools.partial(jnp.dot, preferred_element_type=F32)
    return d(a_bf, hi) + (d(a_bf, mid) + d(a_bf, lo))


def _dot_x3(a, b):
    a_hi = a.astype(BF16)
    a_lo = (a - a_hi.astype(F32)).astype(BF16)
    b_hi = b.astype(BF16)
    b_lo = (b - b_hi.astype(F32)).astype(BF16)
    d = functools.partial(jnp.dot, preferred_element_type=F32)
    n = b.shape[1]
    hi_both = d(a_hi, jnp.concatenate([b_hi, b_lo], axis=1))
    return hi_both[:, :n] + (hi_both[:, n:] + d(a_lo, b_hi))


def _sigmoid(x):
    return 1.0 / (1.0 + jnp.exp(-x))


def _silu(x):
    return x * _sigmoid(x)


def _softplus(x):
    return jnp.maximum(x, 0.0) + jnp.log1p(jnp.exp(-jnp.abs(x)))


def _rms(x, w):
    return x * lax.rsqrt(jnp.mean(x * x, axis=-1, keepdims=True) + EPS) * w


def _iota(shape, dim):
    return lax.broadcasted_iota(jnp.int32, shape, dim)


def _ada_kernel(c_ref, w_ref, b_ref, o_ref):
    o_ref[...] = _dot(_silu(c_ref[...]), w_ref[...]) + b_ref[...]


def _ada(c, w, b, tn=512):
    m, d = c.shape
    n = w.shape[1]
    return pl.pallas_call(
        _ada_kernel,
        grid=(n // tn,),
        in_specs=[pl.BlockSpec((m, d), lambda j: (0, 0)),
                  pl.BlockSpec((d, tn), lambda j: (0, j)),
                  pl.BlockSpec((1, tn), lambda j: (0, j))],
        out_specs=pl.BlockSpec((m, tn), lambda j: (0, j)),
        out_shape=jax.ShapeDtypeStruct((m, n), F32),
        compiler_params=_cparams("arbitrary"),
        name="ada_mod",
    )(c, w, b.reshape(1, n))


def _mod_spec(mod, tm):
    if mod.shape[1] == 1:
        return pl.BlockSpec((None, 1, D_MODEL), lambda b, i, *_: (b, 0, 0))
    return pl.BlockSpec((None, tm, D_MODEL), lambda b, i, *_: (b, i, 0))


def _inproj_kernel(x_ref, sc_ref, sh_ref, nw_ref, w_ref, o_ref, sm_ref, h_ref, *, tm, sub):
    @pl.when(pl.program_id(2) == 0)
    def _():
        per_token = sc_ref.shape[0] != 1

        def body(r, carry):
            rows = pl.ds(pl.multiple_of(r * sub, sub), sub)
            sc = sc_ref[rows, :] if per_token else sc_ref[...]
            sh = sh_ref[rows, :] if per_token else sh_ref[...]
            h = _rms(x_ref[rows, :], nw_ref[...]) * (1.0 + sc) + sh
            h_ref[rows, :] = h.astype(BF16)
            return carry

        lax.fori_loop(0, tm // sub, body, 0)

    acc = jnp.dot(h_ref[...], w_ref[...], preferred_element_type=F32)
    o_ref[...] = acc.astype(BF16)

    @pl.when(pl.program_id(2) == COL_SMALL // PROJ_TN)
    def _():
        sm_ref[...] = acc[:, COL_SMALL % PROJ_TN:COL_SMALL % PROJ_TN + LANES]


def _inproj(x3, sc, sh, nw, w_cat, tm):
    bx, lx, d = x3.shape
    n = w_cat.shape[1]
    kern = functools.partial(_inproj_kernel, tm=tm, sub=min(tm, 256))
    return pl.pallas_call(
        kern,
        grid=(bx, lx // tm, n // PROJ_TN),
        in_specs=[pl.BlockSpec((None, tm, d), lambda b, i, j: (b, i, 0)),
                  _mod_spec(sc, tm), _mod_spec(sh, tm),
                  pl.BlockSpec((1, d), lambda b, i, j: (0, 0)),
                  pl.BlockSpec((d, PROJ_TN), lambda b, i, j: (0, j))],
        out_specs=[pl.BlockSpec((None, tm, PROJ_TN), lambda b, i, j: (b, i, j)),
                   pl.BlockSpec((None, tm, LANES), lambda b, i, j: (b, i, 0))],
        out_shape=[jax.ShapeDtypeStruct((bx, lx, n), BF16), jax.ShapeDtypeStruct((bx, lx, LANES), F32)],
        scratch_shapes=[pltpu.VMEM((tm, d), BF16)],
        compiler_params=_cparams("arbitrary", "arbitrary", "arbitrary"),
        name="norm_inproj",
    )(x3, sc, sh, nw.reshape(1, d), w_cat)


def _conv_silu(u, ext_ref, bb, cw_ref, c, lc, bias=None):
    ext_ref[bb, SUBLANES:SUBLANES + c, :] = u
    y = ext_ref[bb, CARRY_ROW:CARRY_ROW + c, :] * cw_ref[0:1, :]
    for i in range(1, CONV_K):
        y = y + ext_ref[bb, CARRY_ROW + i:CARRY_ROW + i + c, :] * cw_ref[i:i + 1, :]
    ext_ref[bb, CARRY_ROW:SUBLANES, :] = ext_ref[bb, CARRY_ROW + lc:SUBLANES + lc, :]
    if bias is not None:
        y = y + bias
    return _silu(y)


def _conv_rounds(ref, rounds, ext_ref, cw_ref, r, c, lc, bias=None):
    read = _chunk_reader(ref, r, c)
    return [_conv_silu(read(bb, r0), ext_ref, bb, cw_ref, c, lc, bias) for rnd in rounds for bb, r0 in rnd]


def _chunk_reader(ref, r, c):
    if r == c:
        whole = ref[...].astype(F32)
        return lambda bb, r0: whole[bb * c:(bb + 1) * c]
    return lambda bb, r0: ref[pl.ds(bb * r + r0, c), :].astype(F32)


def _chunk_writer(ref, bt, r, c):
    if r != c:
        return (lambda bb, r0, val: ref.__setitem__((pl.ds(bb * r + r0, c), slice(None)), val.astype(ref.dtype)),
                lambda: None)
    parts = {}

    def flush():
        ref[...] = jnp.concatenate([parts[bb] for bb in range(bt)], axis=0).astype(ref.dtype)
    return (lambda bb, r0, val: parts.__setitem__(bb, val)), flush


def _run_rounds(process, bt, r, c, cpi):
    g = r // c
    if g == 1:
        process([[(bb, 0) for bb in range(bt)]])
    else:
        def body(ci, carry):
            process([[(bb, pl.multiple_of((ci * cpi + t) * c, c)) for bb in range(bt)] for t in range(cpi)])
            return carry
        lax.fori_loop(0, g // cpi, body, 0)


INV_BASE = SUBLANES


def _inverse_masks(row, col, c):
    sh = lambda x, s: jnp.right_shift(x, int(math.log2(s)))
    diag = sh(row, INV_BASE) == sh(col, INV_BASE)
    merges = []
    s = INV_BASE
    while s < c:
        merges.append((sh(row, 2 * s) == sh(col, 2 * s))
                      & (jnp.bitwise_and(sh(row, s), 1) == 1) & (jnp.bitwise_and(sh(col, s), 1) == 0))
        s *= 2
    return diag, merges


def _unit_lower_inverses(lmats, eye_f, masks):
    diag, merges = masks
    npows = [-jnp.where(diag, l, 0.0) for l in lmats]
    ps = [eye_f + n for n in npows]
    for _ in range(int(math.log2(INV_BASE)) - 1):
        npows = [_dot(n, n) for n in npows]
        ps = [p + _dot(p, n) for p, n in zip(ps, npows)]
    for m in merges:
        ts = [_dot(p, jnp.where(m, l, 0.0)) for p, l in zip(ps, lmats)]
        ps = [p - _dot(t, p) for t, p in zip(ts, ps)]
    return ps


def _dn_kernel(qkv_ref, z_ref, sm_ref, cw_ref, alog_ref, bias_ref, nw_ref, cin_ref, sin_ref,
               o_ref, cout_ref, sout_ref, ext_ref, *, bt, r, c, lc, ng, cpi):
    j = pl.program_id(1)

    @pl.when(j == 0)
    def _():
        ext_ref[:, CARRY_ROW:SUBLANES, :] = cin_ref[...]
        sout_ref[...] = sin_ref[...]

    row = _iota((c, c), 0)
    col = _iota((c, c), 1)
    tri_incl = row >= col
    tri_strict = row > col
    eye_f = (row == col).astype(F32)
    tril_bf = tri_incl.astype(BF16)
    valid = _iota((c, 1), 0) < lc
    nexp_a = -jnp.exp(alog_ref[...])
    bias = bias_ref[...]
    inv_masks = _inverse_masks(row, col, c)
    heads = range(DN_HEADS)
    read_z, read_sm = (_chunk_reader(ref, r, c) for ref in (z_ref, sm_ref))
    put_o, flush_o = _chunk_writer(o_ref, bt, r, c)

    def prep(bb, r0, y):
        sm = read_sm(bb, r0)
        g_all = nexp_a * _softplus(sm + bias)
        beta_all = _sigmoid(sm)
        if lc < c:
            g_all = jnp.where(valid, g_all, 0.0)
            beta_all = jnp.where(valid, beta_all, 0.0)
        cs_all = _dot_exact_lhs(tril_bf, g_all)
        cs_t = cs_all.T
        ecs_all = jnp.exp(cs_all)
        per_head = []
        for h in heads:
            q = y[:, h * DN_DK:(h + 1) * DN_DK]
            k = y[:, DN_KEY + h * DN_DK:DN_KEY + (h + 1) * DN_DK]
            v = y[:, 2 * DN_KEY + h * DN_DV:2 * DN_KEY + (h + 1) * DN_DV]
            q = q * lax.rsqrt(jnp.sum(q * q, axis=-1, keepdims=True) + EPS) * (DN_DK ** -0.5)
            k = k * lax.rsqrt(jnp.sum(k * k, axis=-1, keepdims=True) + EPS)
            if lc < c:
                k = jnp.where(valid, k, 0.0)
            cs = cs_all[:, SM_A + h:SM_A + h + 1]
            ecs = ecs_all[:, SM_A + h:SM_A + h + 1]
            beta = beta_all[:, SM_B + h:SM_B + h + 1]
            last = cs_all[c - 1:c, SM_A + h:SM_A + h + 1]
            decay = jnp.exp(jnp.where(tri_incl, cs - cs_t[SM_A + h:SM_A + h + 1, :], -jnp.inf))
            per_head.append(dict(q_ecs=(q * ecs).astype(BF16), q=q.astype(BF16), k=k.astype(BF16), beta=beta,
                                 decay=decay, elast=jnp.exp(last),
                                 k_tail=(k * jnp.exp(last - cs)).astype(BF16),
                                 rhs=jnp.concatenate([v * beta, k * (beta * ecs)], axis=1).astype(BF16)))
        return per_head

    def process(rounds):
        items = [it for rnd in rounds for it in rnd]
        conv = _conv_rounds(qkv_ref, rounds, ext_ref, cw_ref, r, c, lc)
        pre = [prep(bb, r0, y) for (bb, r0), y in zip(items, conv)]
        chains = [(i, h) for i in range(len(items)) for h in heads]
        a = {ch: pre[ch[0]][ch[1]] for ch in chains}
        kk = {ch: _dot_nt(a[ch]["k"], a[ch]["k"]) for ch in chains}
        qk = {ch: _dot_nt(a[ch]["q"], a[ch]["k"]) * a[ch]["decay"] for ch in chains}
        lmats = [jnp.where(tri_strict, a[ch]["beta"] * kk[ch] * a[ch]["decay"], 0.0) for ch in chains]
        pinv = _unit_lower_inverses(lmats, eye_f, inv_masks)
        sol = {ch: _dot(p, a[ch]["rhs"]) for ch, p in zip(chains, pinv)}
        first = 0
        for rnd in rounds:
            idx = range(first, first + len(rnd))
            first += len(rnd)
            rch = [(i, h) for i in idx for h in heads]
            s = {ch: sout_ref[items[ch[0]][0], ch[1]] for ch in rch}
            ws_qs = {ch: _dot(jnp.concatenate([sol[ch][:, DN_DV:].astype(BF16), a[ch]["q_ecs"]], axis=0), s[ch])
                     for ch in rch}
            v_new = {ch: (sol[ch][:, :DN_DV] - ws_qs[ch][:c]).astype(BF16) for ch in rch}
            o = {ch: ws_qs[ch][c:] + _dot(qk[ch], v_new[ch]) for ch in rch}
            for ch in rch:
                sout_ref[items[ch[0]][0], ch[1]] = a[ch]["elast"] * s[ch] + _dot_tn(a[ch]["k_tail"], v_new[ch])
            for i in idx:
                bb, r0 = items[i]
                z = read_z(bb, r0)
                put_o(bb, r0, jnp.concatenate(
                    [_rms(o[(i, h)], nw_ref[...]) * _silu(z[:, h * DN_DV:(h + 1) * DN_DV]) for h in heads], axis=1))

    _run_rounds(process, bt, r, c, cpi)
    flush_o()

    @pl.when(j == ng - 1)
    def _():
        cout_ref[...] = ext_ref[:, CARRY_ROW:SUBLANES, :]


def _small_row(vals, offset, fill=0.0):
    row = jnp.full((1, LANES), fill, F32)
    return lax.dynamic_update_slice(row, vals.astype(F32).reshape(1, -1), (0, offset))


def _dn_branch(proj, small, conv_w, a_log, dt_bias, norm_w, conv_in, rec_in, *, bt, r, c, lc, ng, cpi):
    nb = conv_in.shape[0] // bt
    rows = bt * r
    t = proj.shape[0]
    kern = functools.partial(_dn_kernel, bt=bt, r=r, c=c, lc=lc, ng=ng, cpi=cpi)
    rowmap = lambda col: (lambda i, j: (i * ng + j, col))
    const = lambda i, j: (0, 0)
    return pl.pallas_call(
        kern,
        grid=(nb, ng),
        in_specs=[pl.BlockSpec((rows, DN_CONV_DIM), rowmap(COL_DN_QKV // DN_CONV_DIM)),
                  pl.BlockSpec((rows, DN_VAL), rowmap(COL_DN_Z // DN_VAL)),
                  pl.BlockSpec((rows, LANES), rowmap(0)),
                  pl.BlockSpec((CONV_K, DN_CONV_DIM), const),
                  pl.BlockSpec((1, LANES), const),
                  pl.BlockSpec((1, LANES), const),
                  pl.BlockSpec((1, DN_DV), const),
                  pl.BlockSpec((bt, CONV_K - 1, DN_CONV_DIM), lambda i, j: (i, 0, 0)),
                  pl.BlockSpec((bt, DN_HEADS, DN_DK, DN_DV), lambda i, j: (i, 0, 0, 0))],
        out_specs=[pl.BlockSpec((rows, DN_VAL), lambda i, j: (i * ng + j, 0)),
                   pl.BlockSpec((bt, CONV_K - 1, DN_CONV_DIM), lambda i, j: (i, 0, 0)),
                   pl.BlockSpec((bt, DN_HEADS, DN_DK, DN_DV), lambda i, j: (i, 0, 0, 0))],
        out_shape=[jax.ShapeDtypeStruct((t, DN_VAL), BF16),
                   jax.ShapeDtypeStruct(conv_in.shape, F32),
                   jax.ShapeDtypeStruct(rec_in.shape, F32)],
        scratch_shapes=[pltpu.VMEM((bt, c + SUBLANES, DN_CONV_DIM), F32)],
        compiler_params=_cparams("arbitrary", "arbitrary"),
        name="gated_delta",
    )(proj, proj, small, conv_w, _small_row(a_log, SM_A), _small_row(dt_bias, SM_A),
      norm_w.reshape(1, DN_DV), conv_in, rec_in)


def _ssd_kernel(xbc_ref, z_ref, sm_ref, cw_ref, cb_ref, alog_ref, bias_ref, dvec_ref, nw_ref, cin_ref,
                hin_ref, y_ref, cout_ref, hout_ref, ext_ref, *, bt, r, c, lc, ng, cpi):
    j = pl.program_id(1)

    @pl.when(j == 0)
    def _():
        ext_ref[:, CARRY_ROW:SUBLANES, :] = cin_ref[...]
        hout_ref[...] = hin_ref[...]

    tril_bf = (_iota((c, c), 0) >= _iota((c, c), 1)).astype(BF16)
    row2 = _iota((c, 2 * c), 0)
    lane2 = _iota((c, 2 * c), 1)
    left2 = lane2 < c
    tri2 = row2 >= jnp.where(left2, lane2, lane2 - c)
    left_x = _iota((c, LANES), 1) < SSM_HEADDIM
    top_h = _iota((2 * SSM_HEADDIM, 1), 0) < SSM_HEADDIM
    valid = _iota((c, 1), 0) < lc
    nexp_a = -jnp.exp(alog_ref[...])
    bias = bias_ref[...]
    gw = SSM_INNER // SSM_GROUPS
    pairs_per_group = SSM_HEADS // SSM_GROUPS // 2

    pairs = range(SSM_HEADS // 2)
    pair_rows = lambda p: slice(p * LANES, (p + 1) * LANES)
    read_z, read_sm = (_chunk_reader(ref, r, c) for ref in (z_ref, sm_ref))
    put_y, flush_y = _chunk_writer(y_ref, bt, r, c)

    def prep(bb, r0, y):
        dt_all = _softplus(read_sm(bb, r0) + bias)
        if lc < c:
            dt_all = jnp.where(valid, dt_all, 0.0)
        cs_all = _dot_exact_lhs(tril_bf, dt_all * nexp_a)
        cs_t2 = jnp.concatenate([cs_all, cs_all], axis=0).T
        last = cs_all[c - 1:c, :]
        bm = [y[:, SSM_INNER + g * SSM_STATE:SSM_INNER + (g + 1) * SSM_STATE].astype(BF16)
              for g in range(SSM_GROUPS)]
        cm = [y[:, SSM_INNER + (SSM_GROUPS + g) * SSM_STATE:SSM_INNER + (SSM_GROUPS + g + 1) * SSM_STATE].astype(BF16)
              for g in range(SSM_GROUPS)]
        return dict(y=y, dt=dt_all, cs=cs_all, cs_t2=cs_t2, ecs=jnp.exp(cs_all),
                    tail=jnp.exp(last - cs_all), elast=jnp.exp(last), bm=bm, cm=cm)

    def process(rounds):
        items = [it for rnd in rounds for it in rnd]
        conv = _conv_rounds(xbc_ref, rounds, ext_ref, cw_ref, r, c, lc, bias=cb_ref[...])
        pre = [prep(bb, r0, y) for (bb, r0), y in zip(items, conv)]
        ids = range(len(items))
        cb2 = {(i, g): _dot_nt(pre[i]["cm"][g], jnp.concatenate([pre[i]["bm"][g]] * 2, axis=0))
               for i in ids for g in range(SSM_GROUPS)}
        y_intra, upd, xs_of = {}, {}, {}
        for i in ids:
            a = pre[i]
            for p in pairs:
                g = p // pairs_per_group
                la, lb = SM_DT + 2 * p, SM_DT + 2 * p + 1
                both = lambda v, mask=left_x: jnp.where(mask, v[:, la:la + 1], v[:, lb:lb + 1])
                xs = a["y"][:, pair_rows(p)]
                xdt = xs * both(a["dt"])
                diff = both(a["cs"], left2) - jnp.where(left2[0:1], a["cs_t2"][la:la + 1, :], a["cs_t2"][lb:lb + 1, :])
                decay2 = jnp.exp(jnp.where(tri2, diff, -jnp.inf))
                rhs = jnp.concatenate([jnp.where(left_x, xdt, 0.0), jnp.where(left_x, 0.0, xdt)], axis=0)
                y_intra[i, p] = _dot(cb2[i, g] * decay2, rhs)
                upd[i, p] = _dot_tn(xdt * both(a["tail"]), a["bm"][g])
                xs_of[i, p] = xs
        first = 0
        for rnd in rounds:
            idx = range(first, first + len(rnd))
            first += len(rnd)
            for i in idx:
                a = pre[i]
                bb, r0 = items[i]
                outs = []
                for p in pairs:
                    g = p // pairs_per_group
                    la, lb = SM_DT + 2 * p, SM_DT + 2 * p + 1
                    hs = hout_ref[bb, pair_rows(p), :]
                    y_inter = _dot_nt(a["cm"][g], hs) * jnp.where(left_x, a["ecs"][:, la:la + 1], a["ecs"][:, lb:lb + 1])
                    hout_ref[bb, pair_rows(p), :] = (
                        jnp.where(top_h, a["elast"][:, la:la + 1], a["elast"][:, lb:lb + 1]) * hs + upd[i, p])
                    outs.append(y_intra[i, p] + y_inter + dvec_ref[:, pair_rows(p)] * xs_of[i, p])
                yz = jnp.concatenate(outs, axis=1) * _silu(read_z(bb, r0))
                put_y(bb, r0, jnp.concatenate(
                    [_rms(yz[:, g * gw:(g + 1) * gw], nw_ref[:, g * gw:(g + 1) * gw]) for g in range(SSM_GROUPS)],
                    axis=1))

    _run_rounds(process, bt, r, c, cpi)
    flush_y()

    @pl.when(j == ng - 1)
    def _():
        cout_ref[...] = ext_ref[:, CARRY_ROW:SUBLANES, :]


def _ssd_branch(proj, small, conv_w, conv_b, a_log, dt_bias, d_skip, norm_w, conv_in, rec_in, *, bt, r, c, lc, ng, cpi):
    nb = conv_in.shape[0] // bt
    rows = bt * r
    t = proj.shape[0]
    hrows = SSM_HEADS * SSM_HEADDIM
    rec2 = rec_in.reshape(rec_in.shape[0], hrows, SSM_STATE)
    kern = functools.partial(_ssd_kernel, bt=bt, r=r, c=c, lc=lc, ng=ng, cpi=cpi)
    rowmap = lambda col: (lambda i, j: (i * ng + j, col))
    const = lambda i, j: (0, 0)
    y, conv_out, rec_out = pl.pallas_call(
        kern,
        grid=(nb, ng),
        in_specs=[pl.BlockSpec((rows, SSM_CONV_DIM), rowmap(COL_SSM_XBC // SSM_CONV_DIM)),
                  pl.BlockSpec((rows, SSM_INNER), rowmap(COL_SSM_Z // SSM_INNER)),
                  pl.BlockSpec((rows, LANES), rowmap(0)),
                  pl.BlockSpec((CONV_K, SSM_CONV_DIM), const),
                  pl.BlockSpec((1, SSM_CONV_DIM), const),
                  pl.BlockSpec((1, LANES), const),
                  pl.BlockSpec((1, LANES), const),
                  pl.BlockSpec((1, SSM_INNER), const),
                  pl.BlockSpec((1, SSM_INNER), const),
                  pl.BlockSpec((bt, CONV_K - 1, SSM_CONV_DIM), lambda i, j: (i, 0, 0)),
                  pl.BlockSpec((bt, hrows, SSM_STATE), lambda i, j: (i, 0, 0))],
        out_specs=[pl.BlockSpec((rows, SSM_INNER), lambda i, j: (i * ng + j, 0)),
                   pl.BlockSpec((bt, CONV_K - 1, SSM_CONV_DIM), lambda i, j: (i, 0, 0)),
                   pl.BlockSpec((bt, hrows, SSM_STATE), lambda i, j: (i, 0, 0))],
        out_shape=[jax.ShapeDtypeStruct((t, SSM_INNER), BF16),
                   jax.ShapeDtypeStruct(conv_in.shape, F32),
                   jax.ShapeDtypeStruct(rec2.shape, F32)],
        scratch_shapes=[pltpu.VMEM((bt, c + SUBLANES, SSM_CONV_DIM), F32)],
        compiler_params=_cparams("arbitrary", "arbitrary"),
        name="ssd_scan",
    )(proj, proj, small, conv_w, conv_b.reshape(1, -1), _small_row(a_log, SM_DT), _small_row(dt_bias, SM_DT),
      jnp.repeat(d_skip.astype(F32), SSM_HEADDIM).reshape(1, SSM_INNER), norm_w.reshape(1, SSM_INNER),
      conv_in, rec2)
    return y, conv_out, rec_out.reshape(rec_in.shape)


PLAN_E0, PLAN_E1, PLAN_C0, PLAN_C1 = 0, 1, 2, 3


def _route(logits):
    lane = _iota(logits.shape, 1).astype(F32)
    big = float(LANES)
    is_group = (lane >= RT_GROUP) & (lane < RT_GROUP + MOE_GROUPS)
    gl = jnp.where(is_group, logits, -jnp.inf)
    gmax = jnp.max(gl, axis=-1, keepdims=True)
    g_sel = jnp.min(jnp.where(gl == gmax, lane, big), axis=-1, keepdims=True) - RT_GROUP
    p_group = 1.0 / jnp.sum(jnp.exp(gl - gmax), axis=-1, keepdims=True)
    e_lo = RT_EXPERT + MOE_PER_GROUP * g_sel
    in_grp = (lane >= e_lo) & (lane < e_lo + MOE_PER_GROUP)
    el = jnp.where(in_grp, logits, -jnp.inf)
    ee = jnp.exp(el - jnp.max(el, axis=-1, keepdims=True))
    pe = jnp.where(in_grp, ee / jnp.sum(ee, axis=-1, keepdims=True), -1.0)
    p1 = jnp.max(pe, axis=-1, keepdims=True)
    i1 = jnp.min(jnp.where(pe == p1, lane, big), axis=-1, keepdims=True)
    pe2 = jnp.where(lane == i1, -1.0, pe)
    p2 = jnp.max(pe2, axis=-1, keepdims=True)
    i2 = jnp.min(jnp.where(pe2 == p2, lane, big), axis=-1, keepdims=True)
    tot = p1 + p2
    plan = jnp.where(lane == PLAN_E0, i1 - RT_EXPERT, 0.0) + jnp.where(lane == PLAN_E1, i2 - RT_EXPERT, 0.0)
    return plan + jnp.where(lane == PLAN_C0, p_group * p1 / tot, 0.0) + jnp.where(lane == PLAN_C1, p_group * p2 / tot, 0.0)


def _tile_ranks(plan):
    tp = plan.shape[0]
    lane = _iota(plan.shape, 1).astype(F32)
    sel0 = lane == plan[:, PLAN_E0:PLAN_E0 + 1]
    sel1 = lane == plan[:, PLAN_E1:PLAN_E1 + 1]
    sel = (sel0 | sel1).astype(BF16)
    before = (_iota((tp, tp), 0) > _iota((tp, tp), 1)).astype(BF16)
    excl = jnp.dot(before, sel, preferred_element_type=F32)
    r0 = jnp.sum(jnp.where(sel0, excl, 0.0), axis=-1, keepdims=True)
    r1 = jnp.sum(jnp.where(sel1, excl, 0.0), axis=-1, keepdims=True)
    rank = jnp.where(lane == PLAN_E0, r0, 0.0) + jnp.where(lane == PLAN_E1, r1, 0.0)
    return rank, jnp.sum(sel.astype(F32), axis=0, keepdims=True)


def _merge_kernel(x_ref, og_ref, ys_ref, gates_ref, gt1_ref, sc2_ref, sh2_ref, nw2_ref, wdn_ref, wssm_ref,
                  wout_ref, wr_ref, br_ref, x1_ref, h2_ref, plan_ref, rank_ref, cnt_ref):
    y_dn = _dot(og_ref[...], wdn_ref[...])
    y_ssm = _dot(ys_ref[...], wssm_ref[...])
    merged = (_sigmoid(gates_ref[:, :D_MODEL].astype(F32)) * y_dn
              + _sigmoid(gates_ref[:, D_MODEL:].astype(F32)) * y_ssm)
    x1 = x_ref[...] + gt1_ref[...] * _dot(merged, wout_ref[...])
    x1_ref[...] = x1
    h2 = _rms(x1, nw2_ref[...]) * (1.0 + sc2_ref[...]) + sh2_ref[...]
    h2_ref[...] = h2.astype(BF16)
    plan = _route(_dot_x3(h2, wr_ref[...]) + br_ref[...])
    plan_ref[...] = plan
    rank_ref[...], cnt_ref[...] = _tile_ranks(plan)


def _merge(x3, og, ys, proj, gt1, sc2, sh2, nw2, wdn, wssm, wout, wr, br, tm):
    bx, lx, d = x3.shape
    assert tm == TOK_TILE
    nl = lx // tm
    rowmap = lambda col: (lambda b, i: (b * nl + i, col))
    const = lambda b, i: (0, 0)
    tok = lambda w, dt: jax.ShapeDtypeStruct((bx, lx, w), dt)
    return pl.pallas_call(
        _merge_kernel,
        grid=(bx, nl),
        in_specs=[pl.BlockSpec((None, tm, d), lambda b, i: (b, i, 0)),
                  pl.BlockSpec((tm, DN_VAL), rowmap(0)),
                  pl.BlockSpec((tm, SSM_INNER), rowmap(0)),
                  pl.BlockSpec((tm, 2 * d), rowmap(COL_GATES // (2 * d))),
                  _mod_spec(gt1, tm), _mod_spec(sc2, tm), _mod_spec(sh2, tm),
                  pl.BlockSpec((1, d), const),
                  pl.BlockSpec(wdn.shape, const), pl.BlockSpec(wssm.shape, const), pl.BlockSpec(wout.shape, const),
                  pl.BlockSpec(wr.shape, const), pl.BlockSpec((1, LANES), const)],
        out_specs=[pl.BlockSpec((None, tm, d), lambda b, i: (b, i, 0)),
                   pl.BlockSpec((None, tm, d), lambda b, i: (b, i, 0)),
                   pl.BlockSpec((None, tm, LANES), lambda b, i: (b, i, 0)),
                   pl.BlockSpec((None, tm, LANES), lambda b, i: (b, i, 0)),
                   pl.BlockSpec((None, 1, LANES), lambda b, i: (b * nl + i, 0, 0))],
        out_shape=[tok(d, F32), tok(d, BF16), tok(LANES, F32), tok(LANES, F32),
                   jax.ShapeDtypeStruct((bx * nl, 1, LANES), F32)],
        compiler_params=_cparams("arbitrary", "arbitrary"),
        name="merge_route",
    )(x3, og, ys, proj, gt1, sc2, sh2, nw2.reshape(1, d), wdn, wssm, wout, wr, br)


FFN_ROWS = 512
TOK_TILE = 512
RUN_ALIGN = 2 * SUBLANES
RUN_BIG = 4 * RUN_ALIGN
RUN_SMALL = (2 * RUN_ALIGN, RUN_ALIGN)
LOCAL_ROWS = 2 * TOK_TILE + MOE_EXPERTS * RUN_ALIGN


def _local_slots(plan, rank, off_row):
    lane = _iota(plan.shape, 1).astype(F32)
    slots = []
    for k in (PLAN_E0, PLAN_E1):
        off = jnp.sum(jnp.where(lane == plan[:, k:k + 1], off_row, 0.0), axis=-1, keepdims=True)
        slots.append(off + rank[:, k:k + 1])
    return slots


def _run_blocks(p8_ref, loff_ref, base_ref, act):
    for e in range(MOE_EXPERTS):
        n = p8_ref[0, e]
        lo = loff_ref[0, e]
        go = base_ref[0, e]

        def big(k, carry, lo=lo, go=go):
            off = k * RUN_BIG
            act(pl.multiple_of(lo + off, RUN_ALIGN), pl.multiple_of(go + off, RUN_ALIGN), RUN_BIG)
            return carry

        lax.fori_loop(0, lax.shift_right_logical(n, RUN_BIG.bit_length() - 1), big, 0)
        for size in RUN_SMALL:
            @pl.when(jnp.bitwise_and(n, size) != 0)
            def _(size=size, n=n, lo=lo, go=go):
                done = jnp.bitwise_and(n, ~(2 * size - 1))
                act(pl.multiple_of(lo + done, RUN_ALIGN), pl.multiple_of(go + done, RUN_ALIGN), size)


def _dispatch_kernel(tv_ref, p8_ref, loff_ref, base_ref, plan_ref, rank_ref, off_ref, h2_ref, *rest, first):
    xs_ref, buf_ref, zero_ref, sem, zsem = rest[-5:]

    @pl.when((pl.program_id(0) == 0) & first)
    def _():
        zero_ref[...] = jnp.zeros_like(zero_ref)

        def zero_copy(i):
            rows = pl.ds(pl.multiple_of(i * FFN_ROWS, FFN_ROWS), FFN_ROWS)
            return pltpu.make_async_copy(zero_ref, xs_ref.at[rows], zsem)

        def zero_tiles(act):
            def body(i, carry):
                @pl.when(tv_ref[i] < FFN_ROWS)
                def _():
                    act(zero_copy(i))
                return carry
            lax.fori_loop(0, xs_ref.shape[0] // FFN_ROWS, body, 0)

        zero_tiles(lambda cp: cp.start())
        zero_tiles(lambda cp: cp.wait())

    s0, s1 = _local_slots(plan_ref[...], rank_ref[...], off_ref[...])
    row = _iota((1, LOCAL_ROWS), 1).astype(F32)
    onehot_t = ((row == s0) | (row == s1)).astype(BF16)
    buf_ref[...] = lax.dot_general(onehot_t, h2_ref[...], (((0,), (0,)), ((), ())),
                                   preferred_element_type=F32).astype(BF16)

    def copy(lo, go, size):
        return pltpu.make_async_copy(buf_ref.at[pl.ds(lo, size)], xs_ref.at[pl.ds(go, size)], sem)

    _run_blocks(p8_ref, loff_ref, base_ref, lambda lo, go, size: copy(lo, go, size).start())
    _run_blocks(p8_ref, loff_ref, base_ref, lambda lo, go, size: copy(lo, go, size).wait())


def _tile_scalars():
    return pl.BlockSpec((None, 1, LANES), lambda i, *_: (i, 0, 0), memory_space=pltpu.SMEM)


def _dispatch(tile_valid, p8, loff, base, plan2, rank, off_f, h2, rows, xs_prev=None):
    t, d = h2.shape
    tt = TOK_TILE
    tok = lambda w: pl.BlockSpec((tt, w), lambda i, tv: (i, 0))
    operands = (tile_valid, p8, loff, base, plan2, rank, off_f, h2) + (() if xs_prev is None else (xs_prev,))
    return pl.pallas_call(
        functools.partial(_dispatch_kernel, first=xs_prev is None),
        grid_spec=pltpu.PrefetchScalarGridSpec(
            num_scalar_prefetch=1,
            grid=(t // tt,),
            in_specs=[_tile_scalars(), _tile_scalars(), _tile_scalars(),
                      tok(LANES), tok(LANES),
                      pl.BlockSpec((None, 1, LANES), lambda i, tv: (i, 0, 0)),
                      tok(d)] + ([] if xs_prev is None else [pl.BlockSpec(memory_space=pl.ANY)]),
            out_specs=pl.BlockSpec(memory_space=pl.ANY),
            scratch_shapes=[pltpu.VMEM((LOCAL_ROWS, d), BF16), pltpu.VMEM((FFN_ROWS, d), BF16),
                            pltpu.SemaphoreType.DMA(()), pltpu.SemaphoreType.DMA(())]),
        out_shape=jax.ShapeDtypeStruct((rows, d), BF16),
        input_output_aliases={} if xs_prev is None else {len(operands) - 1: 0},
        compiler_params=_cparams("arbitrary"),
        name="moe_dispatch",
    )(*operands)


def _ffn_kernel(te_ref, tv_ref, xs_ref, wg_ref, wu_ref, wd_ref, o_ref):
    nv = tv_ref[pl.program_id(0)]

    half = FFN_ROWS // 2

    def swiglu(rows):
        x = xs_ref[rows, :]
        o_ref[rows, :] = _dot(_silu(_dot(x, wg_ref[...])) * _dot(x, wu_ref[...]), wd_ref[...]).astype(BF16)

    @pl.when(nv > half)
    def _():
        swiglu(slice(None))

    @pl.when((nv > 0) & (nv <= half))
    def _():
        swiglu(slice(0, half))
        o_ref[half:, :] = jnp.zeros((half, o_ref.shape[1]), o_ref.dtype)

    @pl.when(nv <= 0)
    def _():
        o_ref[...] = jnp.zeros_like(o_ref)


def _ffn(tile_expert, tile_valid, xs, wg, wu, wd):
    rows, d = xs.shape
    wmap = lambda i, te, tv: (te[i], 0, 0)
    return pl.pallas_call(
        _ffn_kernel,
        grid_spec=pltpu.PrefetchScalarGridSpec(
            num_scalar_prefetch=2,
            grid=(rows // FFN_ROWS,),
            in_specs=[pl.BlockSpec((FFN_ROWS, d), lambda i, te, tv: (i, 0)),
                      pl.BlockSpec((None, d, MOE_FF), wmap),
                      pl.BlockSpec((None, d, MOE_FF), wmap),
                      pl.BlockSpec((None, MOE_FF, d), wmap)],
            out_specs=pl.BlockSpec((FFN_ROWS, d), lambda i, te, tv: (i, 0))),
        out_shape=jax.ShapeDtypeStruct((rows, d), BF16),
        compiler_params=_cparams("arbitrary"),
        name="moe_ffn",
    )(tile_expert, tile_valid, xs, wg, wu, wd)


def _combine_kernel(p8_ref, loff_ref, base_ref, ys_ref, plan_ref, rank_ref, off_ref, x1_ref, gt2_ref, fsc_ref,
                    fsh_ref, fnw_ref, y_ref, buf_ref, sem):
    @pl.when((pl.program_id(0) == 0) & (pl.program_id(1) == 0))
    def _():
        buf_ref[...] = jnp.zeros_like(buf_ref)

    def copy(lo, go, size):
        return pltpu.make_async_copy(ys_ref.at[pl.ds(go, size)], buf_ref.at[pl.ds(lo, size)], sem)

    _run_blocks(p8_ref, loff_ref, base_ref, lambda lo, go, size: copy(lo, go, size).start())
    plan = plan_ref[...]
    s0, s1 = _local_slots(plan, rank_ref[...], off_ref[...])
    row = _iota((1, LOCAL_ROWS), 1).astype(F32)
    weights = (jnp.where(row == s0, plan[:, PLAN_C0:PLAN_C0 + 1], 0.0)
               + jnp.where(row == s1, plan[:, PLAN_C1:PLAN_C1 + 1], 0.0))
    _run_blocks(p8_ref, loff_ref, base_ref, lambda lo, go, size: copy(lo, go, size).wait())
    total = loff_ref[0, MOE_EXPERTS - 1] + p8_ref[0, MOE_EXPERTS - 1]
    filled = _iota((LOCAL_ROWS, 1), 0) < total
    moe = _dot(weights, jnp.where(filled, buf_ref[...], jnp.zeros((), BF16)))
    x2 = x1_ref[...] + gt2_ref[...] * moe
    y_ref[...] = _rms(x2, fnw_ref[...]) * (1.0 + fsc_ref[...]) + fsh_ref[...]


def _combine(p8, loff, base, ys, plan, rank3, off_f, x1, gt2, fsc, fsh, fnw):
    bx, lx, d = x1.shape
    tt = TOK_TILE
    nl = lx // tt
    tile = lambda b, i: b * nl + i
    scal = pl.BlockSpec((None, 1, LANES), lambda b, i: (tile(b, i), 0, 0), memory_space=pltpu.SMEM)
    tokspec = lambda w: pl.BlockSpec((None, tt, w), lambda b, i: (b, i, 0))
    return pl.pallas_call(
        _combine_kernel,
        grid=(bx, nl),
        in_specs=[scal, scal, scal,
                  pl.BlockSpec(memory_space=pl.ANY),
                  tokspec(LANES), tokspec(LANES),
                  pl.BlockSpec((None, 1, LANES), lambda b, i: (tile(b, i), 0, 0)),
                  tokspec(d), _mod_spec(gt2, tt), _mod_spec(fsc, tt), _mod_spec(fsh, tt),
                  pl.BlockSpec((1, d), lambda b, i: (0, 0))],
        out_specs=tokspec(d),
        out_shape=jax.ShapeDtypeStruct((bx, lx, d), F32),
        scratch_shapes=[pltpu.VMEM((LOCAL_ROWS, d), BF16), pltpu.SemaphoreType.DMA(())],
        compiler_params=_cparams("arbitrary", "arbitrary"),
        name="moe_combine_final",
    )(p8, loff, base, ys, plan, rank3, off_f, x1, gt2, fsc, fsh, fnw.reshape(1, d))


def _moe(groups, wg, wu, wd, fnw):
    d = groups[0]["x1"].shape[-1]
    ntiles = [g["x1"].shape[0] * g["x1"].shape[1] // TOK_TILE for g in groups]
    t = sum(ntiles) * TOK_TILE
    n = jnp.concatenate([g["cnt"][:, 0, :MOE_EXPERTS] for g in groups], axis=0).astype(jnp.int32)
    p8 = (n + RUN_ALIGN - 1) // RUN_ALIGN * RUN_ALIGN
    loff = jnp.cumsum(p8, axis=1) - p8
    erows = jnp.sum(p8, axis=0)
    epad = (erows + FFN_ROWS - 1) // FFN_ROWS * FFN_ROWS
    ends = jnp.cumsum(epad)
    starts = ends - epad
    base = starts[None, :] + jnp.cumsum(p8, axis=0) - p8
    lanes = lambda a: jnp.pad(a, ((0, 0), (0, LANES - MOE_EXPERTS)))[:, None, :]
    rows = (-(-(2 * t + (t // TOK_TILE) * MOE_EXPERTS * (RUN_ALIGN - 1)) // FFN_ROWS) + MOE_EXPERTS) * FFN_ROWS
    tile_start = jnp.arange(rows // FFN_ROWS, dtype=jnp.int32) * FFN_ROWS
    tile_expert = jnp.minimum(jnp.sum(tile_start[:, None] >= ends[None, :], axis=1), MOE_EXPERTS - 1).astype(jnp.int32)
    hot = tile_expert[:, None] == jnp.arange(MOE_EXPERTS)[None, :]
    valid_rows = lambda er: jnp.clip(jnp.sum(jnp.where(hot, er - (tile_start[:, None] - starts), 0), axis=1),
                                     0, FFN_ROWS).astype(jnp.int32)
    tile_valid = valid_rows(erows)
    p8l, loffl, basel = lanes(p8), lanes(loff), lanes(base)
    off_f = loffl.astype(F32)
    spans, first = [], 0
    for nt in ntiles:
        spans.append(slice(first, first + nt))
        first += nt
    first_valid = valid_rows(jnp.sum(p8[spans[0]], axis=0))
    xs = None
    for g, sp in zip(groups, spans):
        tg = g["h2"].shape[0] * g["h2"].shape[1]
        xs = _dispatch(first_valid, p8l[sp], loffl[sp], basel[sp], g["plan"].reshape(tg, LANES),
                       g["rank"].reshape(tg, LANES), off_f[sp], g["h2"].reshape(tg, d), rows, xs_prev=xs)
    ys = _ffn(tile_expert, tile_valid, xs, wg, wu, wd)
    return [_combine(p8l[sp], loffl[sp], basel[sp], ys, g["plan"], g["rank"], off_f[sp], g["x1"], g["gt2"],
                     g["fsc"], g["fsh"], fnw) for g, sp in zip(groups, spans)]


def _prep_layer(lp):
    w_in = lp["w_in"]
    offs = [0]
    for s in (DN_CONV_DIM, DN_VAL, DN_HEADS, DN_HEADS, SSM_CONV_DIM, SSM_INNER, SSM_HEADS, D_MODEL, D_MODEL):
        offs.append(offs[-1] + s)
    seg = lambda i: w_in[:, offs[i]:offs[i + 1]]
    small = jnp.concatenate([seg(2), seg(3), seg(6)], axis=1)
    pad = jnp.zeros((D_MODEL, PROJ_N - COL_SMALL - small.shape[1]), F32)
    w_cat = jnp.concatenate([seg(0), seg(4), seg(5), seg(7), seg(8), seg(1), small, pad], axis=1).astype(BF16)
    wr = jnp.concatenate([lp["w_group_router"], lp["w_expert_router"],
                          jnp.zeros((D_MODEL, LANES - MOE_GROUPS - MOE_EXPERTS), F32)], axis=1)
    br = jnp.concatenate([lp["b_group_router"], lp["b_expert_router"],
                          jnp.zeros((LANES - MOE_GROUPS - MOE_EXPERTS,), F32)]).reshape(1, LANES)
    return dict(lp, w_cat=w_cat, wr=wr, br=br,
                wdn=lp["w_dn_out"].astype(BF16), wssm=lp["w_ssm_out"].astype(BF16), wout=lp["w_out"].astype(BF16),
                wg=lp["w_exp_gate"], wu=lp["w_exp_up"], wd=lp["w_exp_down"])


def _mixer(x3, mods, fins, states, lp, cfg):
    bx, lx, d = x3.shape
    scan = dict(bt=cfg["bt"], r=cfg["r"], c=cfg["c"], lc=cfg["lc"], ng=cfg["ng"])
    sh1, sc1, gt1, sh2, sc2, gt2 = mods
    dn_conv, dn_rec, ssm_conv, ssm_rec = states
    proj, small = _inproj(x3, sc1, sh1, lp["norm_mix_w"], lp["w_cat"], cfg["tm"])
    proj, small = proj.reshape(bx * lx, PROJ_N), small.reshape(bx * lx, LANES)
    og, dn_conv_new, dn_rec_new = _dn_branch(proj, small, lp["dn_conv_w"], lp["dn_A_log"], lp["dn_dt_bias"],
                                             lp["dn_norm_w"], dn_conv, dn_rec, cpi=cfg["cpi_dn"], **scan)
    ys, ssm_conv_new, ssm_rec_new = _ssd_branch(proj, small, lp["ssm_conv_w"], lp["ssm_conv_b"], lp["ssm_A_log"],
                                                lp["ssm_dt_bias"], lp["ssm_D"], lp["ssm_norm_w"],
                                                ssm_conv, ssm_rec, cpi=cfg["cpi_ssd"], **scan)
    x1, h2, plan, rank, cnt = _merge(x3, og, ys, proj, gt1, sc2, sh2, lp["norm_ffn_w"], lp["wdn"], lp["wssm"],
                                     lp["wout"], lp["wr"], lp["br"], cfg["tm_merge"])
    group = dict(x1=x1, h2=h2, plan=plan, rank=rank, cnt=cnt, gt2=gt2, fsh=fins[0], fsc=fins[1])
    return group, (dn_conv_new, dn_rec_new, ssm_conv_new, ssm_rec_new)


def _per_seq(m):
    return m[:, None, :]


def kernel(x_prompt, x_sample, c_prompt, c_sample, state_dn_conv, state_dn_rec, state_ssm_conv, state_ssm_rec, w_ada, b_ada, norm_mix_w, w_in, dn_conv_w, dn_A_log, dn_dt_bias, dn_norm_w, w_dn_out, ssm_conv_w, ssm_conv_b, ssm_A_log, ssm_dt_bias, ssm_D, ssm_norm_w, w_ssm_out, w_out, norm_ffn_w, w_group_router, b_group_router, w_expert_router, b_expert_router, w_exp_gate, w_exp_up, w_exp_down, w_ada_final, b_ada_final, final_norm_w):
    depth = w_ada.shape[0]
    assert depth == 1
    per_layer = dict(w_ada=w_ada, b_ada=b_ada, norm_mix_w=norm_mix_w, w_in=w_in, dn_conv_w=dn_conv_w,
                     dn_A_log=dn_A_log, dn_dt_bias=dn_dt_bias, dn_norm_w=dn_norm_w, w_dn_out=w_dn_out,
                     ssm_conv_w=ssm_conv_w, ssm_conv_b=ssm_conv_b, ssm_A_log=ssm_A_log, ssm_dt_bias=ssm_dt_bias,
                     ssm_D=ssm_D, ssm_norm_w=ssm_norm_w, w_ssm_out=w_ssm_out, w_out=w_out, norm_ffn_w=norm_ffn_w,
                     w_group_router=w_group_router, b_group_router=b_group_router,
                     w_expert_router=w_expert_router, b_expert_router=b_expert_router,
                     w_exp_gate=w_exp_gate, w_exp_up=w_exp_up, w_exp_down=w_exp_down)
    layers = [_prep_layer({k: v[l] for k, v in per_layer.items()}) for l in range(depth)]

    nbp, lp_, d = x_prompt.shape
    nbs, ls, _ = x_sample.shape
    c_all = jnp.concatenate([c_prompt, c_sample], axis=0)
    mod_all = [_ada(c_all, lyr["w_ada"], lyr["b_ada"]) for lyr in layers]
    fin_all = _ada(c_all, w_ada_final, b_ada_final)

    mods_p = [[_per_seq(m) for m in jnp.split(ma[:nbp], 6, axis=-1)] for ma in mod_all]
    fins_p = [_per_seq(m) for m in jnp.split(fin_all[:nbp], 2, axis=-1)]
    zeros_p = [(jnp.zeros((nbp, CONV_K - 1, DN_CONV_DIM), F32), jnp.zeros((nbp, DN_HEADS, DN_DK, DN_DV), F32),
                jnp.zeros((nbp, CONV_K - 1, SSM_CONV_DIM), F32),
                jnp.zeros((nbp, SSM_HEADS, SSM_HEADDIM, SSM_STATE), F32)) for _ in range(depth)]
    c_p = min(SCAN_CHUNK, lp_)
    r_p = min(lp_, 8 * c_p)
    cfg_p = dict(tm=min(lp_, 2048), tm_merge=min(lp_, 512), tm_moe=min(lp_, 1024), bt=1, r=r_p, c=c_p, lc=c_p, ng=lp_ // r_p,
                 cpi_dn=8 if (r_p // c_p) % 8 == 0 else 1, cpi_ssd=4 if (r_p // c_p) % 4 == 0 else 1)
    grp_p, st_p = _mixer(x_prompt, mods_p[0], fins_p, zeros_p[0], layers[0], cfg_p)

    lpad = -(-ls // SUBLANES) * SUBLANES
    bt_s = 16
    xs = jnp.pad(x_sample, ((0, 0), (0, lpad - ls), (0, 0))).reshape(1, nbs * lpad, d)
    per_tok = lambda m: jnp.repeat(m, lpad, axis=0)[None]
    mods_s = [[per_tok(m) for m in jnp.split(ma[nbp:], 6, axis=-1)] for ma in mod_all]
    fins_s = [per_tok(m) for m in jnp.split(fin_all[nbp:], 2, axis=-1)]
    st_in = [(state_dn_conv[l], state_dn_rec[l], state_ssm_conv[l], state_ssm_rec[l]) for l in range(depth)]
    ts = nbs * lpad
    cfg_s = dict(tm=ts, tm_merge=min(ts, 512), tm_moe=min(ts, 512), bt=bt_s, r=lpad, c=lpad, lc=ls, ng=1, cpi_dn=1, cpi_ssd=1)
    grp_s, st_s = _mixer(xs, mods_s[0], fins_s, st_in[0], layers[0], cfg_s)

    y_p, y_s = _moe([grp_p, grp_s], layers[0]["wg"], layers[0]["wu"], layers[0]["wd"], final_norm_w)
    y_s = y_s.reshape(nbs, lpad, d)[:, :ls]
    return (y_p, y_s) + tuple(st[None] for st in st_p) + tuple(st[None] for st in st_s)
```

```python
import functools
import math

import jax
import jax.numpy as jnp
from jax import lax
from jax.experimental import pallas as pl
from jax.experimental.pallas import tpu as pltpu

F32 = jnp.float32
BF16 = jnp.bfloat16

D_MODEL = 1024
DN_HEADS = 4
DN_DK = 128
DN_DV = 128
DN_KEY = DN_HEADS * DN_DK
DN_VAL = DN_HEADS * DN_DV
CONV_K = 4
DN_CONV_DIM = 2 * DN_KEY + DN_VAL
SSM_INNER = D_MODEL
SSM_HEADDIM = 64
SSM_HEADS = SSM_INNER // SSM_HEADDIM
SSM_GROUPS = 2
SSM_STATE = 128
SSM_CONV_DIM = SSM_INNER + 2 * SSM_GROUPS * SSM_STATE
MOE_GROUPS = 4
MOE_PER_GROUP = 8
MOE_EXPERTS = MOE_GROUPS * MOE_PER_GROUP
MOE_FF = D_MODEL // 4
EPS = 1e-6
SCAN_CHUNK = 64

LANES = 128
SUBLANES = 8
CARRY_ROW = SUBLANES - (CONV_K - 1)

COL_DN_QKV = 0
COL_SSM_XBC = COL_DN_QKV + DN_CONV_DIM
COL_SSM_Z = COL_SSM_XBC + SSM_CONV_DIM
COL_GATES = COL_SSM_Z + SSM_INNER
COL_DN_Z = COL_GATES + 2 * D_MODEL
COL_SMALL = COL_DN_Z + DN_VAL
PROJ_TN = 1792
PROJ_N = 4 * PROJ_TN
SM_A, SM_B, SM_DT = 0, DN_HEADS, 2 * DN_HEADS
RT_GROUP, RT_EXPERT = 0, MOE_GROUPS

VMEM_LIMIT = 56 * 1024 * 1024


def _cparams(*sem):
    return pltpu.CompilerParams(dimension_semantics=sem, vmem_limit_bytes=VMEM_LIMIT)


def _dot(a, b):
    return jnp.dot(a.astype(BF16), b.astype(BF16), preferred_element_type=F32)


def _dot_nt(a, b):
    return lax.dot_general(a.astype(BF16), b.astype(BF16), (((1,), (1,)), ((), ())),
                           preferred_element_type=F32)


def _dot_tn(a, b):
    return lax.dot_general(a.astype(BF16), b.astype(BF16), (((0,), (0,)), ((), ())),
                           preferred_element_type=F32)


def _split3(x):
    hi = x.astype(BF16)
    r = x - hi.astype(F32)
    mid = r.astype(BF16)
    lo = (r - mid.astype(F32)).astype(BF16)
    return hi, mid, lo


def _dot_exact_lhs(a_bf, b):
    hi, mid, lo = _split3(b)
    d = functools.partial(jnp.dot, preferred_element_type=F32)
    return d(a_bf, hi) + (d(a_bf, mid) + d(a_bf, lo))


def _dot_x3(a, b):
    a_hi = a.astype(BF16)
    a_lo = (a - a_hi.astype(F32)).astype(BF16)
    b_hi = b.astype(BF16)
    b_lo = (b - b_hi.astype(F32)).astype(BF16)
    d = functools.partial(jnp.dot, preferred_element_type=F32)
    n = b.shape[1]
    hi_both = d(a_hi, jnp.concatenate([b_hi, b_lo], axis=1))
    return hi_both[:, :n] + (hi_both[:, n:] + d(a_lo, b_hi))


def _sigmoid(x):
    return 1.0 / (1.0 + jnp.exp(-x))


def _silu(x):
    return x * _sigmoid(x)


def _softplus(x):
    return jnp.maximum(x, 0.0) + jnp.log1p(jnp.exp(-jnp.abs(x)))


def _rms(x, w):
    return x * lax.rsqrt(jnp.mean(x * x, axis=-1, keepdims=True) + EPS) * w


def _iota(shape, dim):
    return lax.broadcasted_iota(jnp.int32, shape, dim)


def _ada_kernel(c_ref, w_ref, b_ref, o_ref):
    o_ref[...] = _dot(_silu(c_ref[...]), w_ref[...]) + b_ref[...]


def _ada(c, w, b, tn=512):
    m, d = c.shape
    n = w.shape[1]
    return pl.pallas_call(
        _ada_kernel,
        grid=(n // tn,),
        in_specs=[pl.BlockSpec((m, d), lambda j: (0, 0)),
                  pl.BlockSpec((d, tn), lambda j: (0, j)),
                  pl.BlockSpec((1, tn), lambda j: (0, j))],
        out_specs=pl.BlockSpec((m, tn), lambda j: (0, j)),
        out_shape=jax.ShapeDtypeStruct((m, n), F32),
        compiler_params=_cparams("arbitrary"),
        name="ada_mod",
    )(c, w, b.reshape(1, n))


def _mod_spec(mod, tm):
    if mod.shape[1] == 1:
        return pl.BlockSpec((None, 1, D_MODEL), lambda b, i, *_: (b, 0, 0))
    return pl.BlockSpec((None, tm, D_MODEL), lambda b, i, *_: (b, i, 0))


def _inproj_kernel(x_ref, sc_ref, sh_ref, nw_ref, w_ref, o_ref, sm_ref, h_ref, *, tm, sub):
    @pl.when(pl.program_id(2) == 0)
    def _():
        per_token = sc_ref.shape[0] != 1

        def body(r, carry):
            rows = pl.ds(pl.multiple_of(r * sub, sub), sub)
            sc = sc_ref[rows, :] if per_token else sc_ref[...]
            sh = sh_ref[rows, :] if per_token else sh_ref[...]
            h = _rms(x_ref[rows, :], nw_ref[...]) * (1.0 + sc) + sh
            h_ref[rows, :] = h.astype(BF16)
            return carry

        lax.fori_loop(0, tm // sub, body, 0)

    acc = jnp.dot(h_ref[...], w_ref[...], preferred_element_type=F32)
    o_ref[...] = acc.astype(BF16)

    @pl.when(pl.program_id(2) == COL_SMALL // PROJ_TN)
    def _():
        sm_ref[...] = acc[:, COL_SMALL % PROJ_TN:COL_SMALL % PROJ_TN + LANES]


def _inproj(x3, sc, sh, nw, w_cat, tm):
    bx, lx, d = x3.shape
    n = w_cat.shape[1]
    kern = functools.partial(_inproj_kernel, tm=tm, sub=min(tm, 256))
    return pl.pallas_call(
        kern,
        grid=(bx, lx // tm, n // PROJ_TN),
        in_specs=[pl.BlockSpec((None, tm, d), lambda b, i, j: (b, i, 0)),
                  _mod_spec(sc, tm), _mod_spec(sh, tm),
                  pl.BlockSpec((1, d), lambda b, i, j: (0, 0)),
                  pl.BlockSpec((d, PROJ_TN), lambda b, i, j: (0, j))],
        out_specs=[pl.BlockSpec((None, tm, PROJ_TN), lambda b, i, j: (b, i, j)),
                   pl.BlockSpec((None, tm, LANES), lambda b, i, j: (b, i, 0))],
        out_shape=[jax.ShapeDtypeStruct((bx, lx, n), BF16), jax.ShapeDtypeStruct((bx, lx, LANES), F32)],
        scratch_shapes=[pltpu.VMEM((tm, d), BF16)],
        compiler_params=_cparams("arbitrary", "arbitrary", "arbitrary"),
        name="norm_inproj",
    )(x3, sc, sh, nw.reshape(1, d), w_cat)


def _conv_silu(u, ext_ref, bb, cw_ref, c, lc, bias=None):
    ext_ref[bb, SUBLANES:SUBLANES + c, :] = u
    y = ext_ref[bb, CARRY_ROW:CARRY_ROW + c, :] * cw_ref[0:1, :]
    for i in range(1, CONV_K):
        y = y + ext_ref[bb, CARRY_ROW + i:CARRY_ROW + i + c, :] * cw_ref[i:i + 1, :]
    ext_ref[bb, CARRY_ROW:SUBLANES, :] = ext_ref[bb, CARRY_ROW + lc:SUBLANES + lc, :]
    if bias is not None:
        y = y + bias
    return _silu(y)


def _conv_rounds(ref, rounds, ext_ref, cw_ref, r, c, lc, bias=None):
    read = _chunk_reader(ref, r, c)
    return [_conv_silu(read(bb, r0), ext_ref, bb, cw_ref, c, lc, bias) for rnd in rounds for bb, r0 in rnd]


def _chunk_reader(ref, r, c):
    if r == c:
        whole = ref[...].astype(F32)
        return lambda bb, r0: whole[bb * c:(bb + 1) * c]
    return lambda bb, r0: ref[pl.ds(bb * r + r0, c), :].astype(F32)


def _chunk_writer(ref, bt, r, c):
    if r != c:
        return (lambda bb, r0, val: ref.__setitem__((pl.ds(bb * r + r0, c), slice(None)), val.astype(ref.dtype)),
                lambda: None)
    parts = {}

    def flush():
        ref[...] = jnp.concatenate([parts[bb] for bb in range(bt)], axis=0).astype(ref.dtype)
    return (lambda bb, r0, val: parts.__setitem__(bb, val)), flush


def _run_rounds(process, bt, r, c, cpi):
    g = r // c
    if g == 1:
        process([[(bb, 0) for bb in range(bt)]])
    else:
        def body(ci, carry):
            process([[(bb, pl.multiple_of((ci * cpi + t) * c, c)) for bb in range(bt)] for t in range(cpi)])
            return carry
        lax.fori_loop(0, g // cpi, body, 0)


INV_BASE = SUBLANES


def _inverse_masks(row, col, c):
    sh = lambda x, s: jnp.right_shift(x, int(math.log2(s)))
    diag = sh(row, INV_BASE) == sh(col, INV_BASE)
    merges = []
    s = INV_BASE
    while s < c:
        merges.append((sh(row, 2 * s) == sh(col, 2 * s))
                      & (jnp.bitwise_and(sh(row, s), 1) == 1) & (jnp.bitwise_and(sh(col, s), 1) == 0))
        s *= 2
    return diag, merges


def _unit_lower_inverses(lmats, eye_f, masks):
    diag, merges = masks
    npows = [-jnp.where(diag, l, 0.0) for l in lmats]
    ps = [eye_f + n for n in npows]
    for _ in range(int(math.log2(INV_BASE)) - 1):
        npows = [_dot(n, n) for n in npows]
        ps = [p + _dot(p, n) for p, n in zip(ps, npows)]
    for m in merges:
        ts = [_dot(p, jnp.where(m, l, 0.0)) for p, l in zip(ps, lmats)]
        ps = [p - _dot(t, p) for t, p in zip(ts, ps)]
    return ps


def _dn_kernel(qkv_ref, z_ref, sm_ref, cw_ref, alog_ref, bias_ref, nw_ref, cin_ref, sin_ref,
               o_ref, cout_ref, sout_ref, ext_ref, *, bt, r, c, lc, ng, cpi):
    j = pl.program_id(1)

    @pl.when(j == 0)
    def _():
        ext_ref[:, CARRY_ROW:SUBLANES, :] = cin_ref[...]
        sout_ref[...] = sin_ref[...]

    row = _iota((c, c), 0)
    col = _iota((c, c), 1)
    tri_incl = row >= col
    tri_strict = row > col
    eye_f = (row == col).astype(F32)
    tril_bf = tri_incl.astype(BF16)
    valid = _iota((c, 1), 0) < lc
    nexp_a = -jnp.exp(alog_ref[...])
    bias = bias_ref[...]
    inv_masks = _inverse_masks(row, col, c)
    heads = range(DN_HEADS)
    read_z, read_sm = (_chunk_reader(ref, r, c) for ref in (z_ref, sm_ref))
    put_o, flush_o = _chunk_writer(o_ref, bt, r, c)

    def prep(bb, r0, y):
        sm = read_sm(bb, r0)
        g_all = nexp_a * _softplus(sm + bias)
        beta_all = _sigmoid(sm)
        if lc < c:
            g_all = jnp.where(valid, g_all, 0.0)
            beta_all = jnp.where(valid, beta_all, 0.0)
        cs_all = _dot_exact_lhs(tril_bf, g_all)
        cs_t = cs_all.T
        ecs_all = jnp.exp(cs_all)
        per_head = []
        for h in heads:
            q = y[:, h * DN_DK:(h + 1) * DN_DK]
            k = y[:, DN_KEY + h * DN_DK:DN_KEY + (h + 1) * DN_DK]
            v = y[:, 2 * DN_KEY + h * DN_DV:2 * DN_KEY + (h + 1) * DN_DV]
            q = q * lax.rsqrt(jnp.sum(q * q, axis=-1, keepdims=True) + EPS) * (DN_DK ** -0.5)
            k = k * lax.rsqrt(jnp.sum(k * k, axis=-1, keepdims=True) + EPS)
            if lc < c:
                k = jnp.where(valid, k, 0.0)
            cs = cs_all[:, SM_A + h:SM_A + h + 1]
            ecs = ecs_all[:, SM_A + h:SM_A + h + 1]
            beta = beta_all[:, SM_B + h:SM_B + h + 1]
            last = cs_all[c - 1:c, SM_A + h:SM_A + h + 1]
            decay = jnp.exp(jnp.where(tri_incl, cs - cs_t[SM_A + h:SM_A + h + 1, :], -jnp.inf))
            per_head.append(dict(q_ecs=(q * ecs).astype(BF16), q=q.astype(BF16), k=k.astype(BF16), beta=beta,
                                 decay=decay, elast=jnp.exp(last),
                                 k_tail=(k * jnp.exp(last - cs)).astype(BF16),
                                 rhs=jnp.concatenate([v * beta, k * (beta * ecs)], axis=1).astype(BF16)))
        return per_head

    def process(rounds):
        items = [it for rnd in rounds for it in rnd]
        conv = _conv_rounds(qkv_ref, rounds, ext_ref, cw_ref, r, c, lc)
        pre = [prep(bb, r0, y) for (bb, r0), y in zip(items, conv)]
        chains = [(i, h) for i in range(len(items)) for h in heads]
        a = {ch: pre[ch[0]][ch[1]] for ch in chains}
        kk = {ch: _dot_nt(a[ch]["k"], a[ch]["k"]) for ch in chains}
        qk = {ch: _dot_nt(a[ch]["q"], a[ch]["k"]) * a[ch]["decay"] for ch in chains}
        lmats = [jnp.where(tri_strict, a[ch]["beta"] * kk[ch] * a[ch]["decay"], 0.0) for ch in chains]
        pinv = _unit_lower_inverses(lmats, eye_f, inv_masks)
        sol = {ch: _dot(p, a[ch]["rhs"]) for ch, p in zip(chains, pinv)}
        first = 0
        for rnd in rounds:
            idx = range(first, first + len(rnd))
            first += len(rnd)
            rch = [(i, h) for i in idx for h in heads]
            s = {ch: sout_ref[items[ch[0]][0], ch[1]] for ch in rch}
            ws_qs = {ch: _dot(jnp.concatenate([sol[ch][:, DN_DV:].astype(BF16), a[ch]["q_ecs"]], axis=0), s[ch])
                     for ch in rch}
            v_new = {ch: (sol[ch][:, :DN_DV] - ws_qs[ch][:c]).astype(BF16) for ch in rch}
            o = {ch: ws_qs[ch][c:] + _dot(qk[ch], v_new[ch]) for ch in rch}
            for ch in rch:
                sout_ref[items[ch[0]][0], ch[1]] = a[ch]["elast"] * s[ch] + _dot_tn(a[ch]["k_tail"], v_new[ch])
            for i in idx:
                bb, r0 = items[i]
                z = read_z(bb, r0)
                put_o(bb, r0, jnp.concatenate(
                    [_rms(o[(i, h)], nw_ref[...]) * _silu(z[:, h * DN_DV:(h + 1) * DN_DV]) for h in heads], axis=1))

    _run_rounds(process, bt, r, c, cpi)
    flush_o()

    @pl.when(j == ng - 1)
    def _():
        cout_ref[...] = ext_ref[:, CARRY_ROW:SUBLANES, :]


def _small_row(vals, offset, fill=0.0):
    row = jnp.full((1, LANES), fill, F32)
    return lax.dynamic_update_slice(row, vals.astype(F32).reshape(1, -1), (0, offset))


def _dn_branch(proj, small, conv_w, a_log, dt_bias, norm_w, conv_in, rec_in, *, bt, r, c, lc, ng, cpi):
    nb = conv_in.shape[0] // bt
    rows = bt * r
    t = proj.shape[0]
    kern = functools.partial(_dn_kernel, bt=bt, r=r, c=c, lc=lc, ng=ng, cpi=cpi)
    rowmap = lambda col: (lambda i, j: (i * ng + j, col))
    const = lambda i, j: (0, 0)
    return pl.pallas_call(
        kern,
        grid=(nb, ng),
        in_specs=[pl.BlockSpec((rows, DN_CONV_DIM), rowmap(COL_DN_QKV // DN_CONV_DIM)),
                  pl.BlockSpec((rows, DN_VAL), rowmap(COL_DN_Z // DN_VAL)),
                  pl.BlockSpec((rows, LANES), rowmap(0)),
                  pl.BlockSpec((CONV_K, DN_CONV_DIM), const),
                  pl.BlockSpec((1, LANES), const),
                  pl.BlockSpec((1, LANES), const),
                  pl.BlockSpec((1, DN_DV), const),
                  pl.BlockSpec((bt, CONV_K - 1, DN_CONV_DIM), lambda i, j: (i, 0, 0)),
                  pl.BlockSpec((bt, DN_HEADS, DN_DK, DN_DV), lambda i, j: (i, 0, 0, 0))],
        out_specs=[pl.BlockSpec((rows, DN_VAL), lambda i, j: (i * ng + j, 0)),
                   pl.BlockSpec((bt, CONV_K - 1, DN_CONV_DIM), lambda i, j: (i, 0, 0)),
                   pl.BlockSpec((bt, DN_HEADS, DN_DK, DN_DV), lambda i, j: (i, 0, 0, 0))],
        out_shape=[jax.ShapeDtypeStruct((t, DN_VAL), BF16),
                   jax.ShapeDtypeStruct(conv_in.shape, F32),
                   jax.ShapeDtypeStruct(rec_in.shape, F32)],
        scratch_shapes=[pltpu.VMEM((bt, c + SUBLANES, DN_CONV_DIM), F32)],
        compiler_params=_cparams("arbitrary", "arbitrary"),
        name="gated_delta",
    )(proj, proj, small, conv_w, _small_row(a_log, SM_A), _small_row(dt_bias, SM_A),
      norm_w.reshape(1, DN_DV), conv_in, rec_in)


def _ssd_kernel(xbc_ref, z_ref, sm_ref, cw_ref, cb_ref, alog_ref, bias_ref, dvec_ref, nw_ref, cin_ref,
                hin_ref, y_ref, cout_ref, hout_ref, ext_ref, *, bt, r, c, lc, ng, cpi):
    j = pl.program_id(1)

    @pl.when(j == 0)
    def _():
        ext_ref[:, CARRY_ROW:SUBLANES, :] = cin_ref[...]
        hout_ref[...] = hin_ref[...]

    tril_bf = (_iota((c, c), 0) >= _iota((c, c), 1)).astype(BF16)
    row2 = _iota((c, 2 * c), 0)
    lane2 = _iota((c, 2 * c), 1)
    left2 = lane2 < c
    tri2 = row2 >= jnp.where(left2, lane2, lane2 - c)
    left_x = _iota((c, LANES), 1) < SSM_HEADDIM
    top_h = _iota((2 * SSM_HEADDIM, 1), 0) < SSM_HEADDIM
    valid = _iota((c, 1), 0) < lc
    nexp_a = -jnp.exp(alog_ref[...])
    bias = bias_ref[...]
    gw = SSM_INNER // SSM_GROUPS
    pairs_per_group = SSM_HEADS // SSM_GROUPS // 2

    pairs = range(SSM_HEADS // 2)
    pair_rows = lambda p: slice(p * LANES, (p + 1) * LANES)
    read_z, read_sm = (_chunk_reader(ref, r, c) for ref in (z_ref, sm_ref))
    put_y, flush_y = _chunk_writer(y_ref, bt, r, c)

    def prep(bb, r0, y):
        dt_all = _softplus(read_sm(bb, r0) + bias)
        if lc < c:
            dt_all = jnp.where(valid, dt_all, 0.0)
        cs_all = _dot_exact_lhs(tril_bf, dt_all * nexp_a)
        cs_t2 = jnp.concatenate([cs_all, cs_all], axis=0).T
        last = cs_all[c - 1:c, :]
        bm = [y[:, SSM_INNER + g * SSM_STATE:SSM_INNER + (g + 1) * SSM_STATE].astype(BF16)
              for g in range(SSM_GROUPS)]
        cm = [y[:, SSM_INNER + (SSM_GROUPS + g) * SSM_STATE:SSM_INNER + (SSM_GROUPS + g + 1) * SSM_STATE].astype(BF16)
              for g in range(SSM_GROUPS)]
        return dict(y=y, dt=dt_all, cs=cs_all, cs_t2=cs_t2, ecs=jnp.exp(cs_all),
                    tail=jnp.exp(last - cs_all), elast=jnp.exp(last), bm=bm, cm=cm)

    def process(rounds):
        items = [it for rnd in rounds for it in rnd]
        conv = _conv_rounds(xbc_ref, rounds, ext_ref, cw_ref, r, c, lc, bias=cb_ref[...])
        pre = [prep(bb, r0, y) for (bb, r0), y in zip(items, conv)]
        ids = range(len(items))
        cb2 = {(i, g): _dot_nt(pre[i]["cm"][g], jnp.concatenate([pre[i]["bm"][g]] * 2, axis=0))
               for i in ids for g in range(SSM_GROUPS)}
        y_intra, upd, xs_of = {}, {}, {}
        for i in ids:
            a = pre[i]
            for p in pairs:
                g = p // pairs_per_group
                la, lb = SM_DT + 2 * p, SM_DT + 2 * p + 1
                both = lambda v, mask=left_x: jnp.where(mask, v[:, la:la + 1], v[:, lb:lb + 1])
                xs = a["y"][:, pair_rows(p)]
                xdt = xs * both(a["dt"])
                diff = both(a["cs"], left2) - jnp.where(left2[0:1], a["cs_t2"][la:la + 1, :], a["cs_t2"][lb:lb + 1, :])
                decay2 = jnp.exp(jnp.where(tri2, diff, -jnp.inf))
                rhs = jnp.concatenate([jnp.where(left_x, xdt, 0.0), jnp.where(left_x, 0.0, xdt)], axis=0)
                y_intra[i, p] = _dot(cb2[i, g] * decay2, rhs)
                upd[i, p] = _dot_tn(xdt * both(a["tail"]), a["bm"][g])
                xs_of[i, p] = xs
        first = 0
        for rnd in rounds:
            idx = range(first, first + len(rnd))
            first += len(rnd)
            for i in idx:
                a = pre[i]
                bb, r0 = items[i]
                outs = []
                for p in pairs:
                    g = p // pairs_per_group
                    la, lb = SM_DT + 2 * p, SM_DT + 2 * p + 1
                    hs = hout_ref[bb, pair_rows(p), :]
                    y_inter = _dot_nt(a["cm"][g], hs) * jnp.where(left_x, a["ecs"][:, la:la + 1], a["ecs"][:, lb:lb + 1])
                    hout_ref[bb, pair_rows(p), :] = (
                        jnp.where(top_h, a["elast"][:, la:la + 1], a["elast"][:, lb:lb + 1]) * hs + upd[i, p])
                    outs.append(y_intra[i, p] + y_inter + dvec_ref[:, pair_rows(p)] * xs_of[i, p])
                yz = jnp.concatenate(outs, axis=1) * _silu(read_z(bb, r0))
                put_y(bb, r0, jnp.concatenate(
                    [_rms(yz[:, g * gw:(g + 1) * gw], nw_ref[:, g * gw:(g + 1) * gw]) for g in range(SSM_GROUPS)],
                    axis=1))

    _run_rounds(process, bt, r, c, cpi)
    flush_y()

    @pl.when(j == ng - 1)
    def _():
        cout_ref[...] = ext_ref[:, CARRY_ROW:SUBLANES, :]


def _ssd_branch(proj, small, conv_w, conv_b, a_log, dt_bias, d_skip, norm_w, conv_in, rec_in, *, bt, r, c, lc, ng, cpi):
    nb = conv_in.shape[0] // bt
    rows = bt * r
    t = proj.shape[0]
    hrows = SSM_HEADS * SSM_HEADDIM
    rec2 = rec_in.reshape(rec_in.shape[0], hrows, SSM_STATE)
    kern = functools.partial(_ssd_kernel, bt=bt, r=r, c=c, lc=lc, ng=ng, cpi=cpi)
    rowmap = lambda col: (lambda i, j: (i * ng + j, col))
    const = lambda i, j: (0, 0)
    y, conv_out, rec_out = pl.pallas_call(
        kern,
        grid=(nb, ng),
        in_specs=[pl.BlockSpec((rows, SSM_CONV_DIM), rowmap(COL_SSM_XBC // SSM_CONV_DIM)),
                  pl.BlockSpec((rows, SSM_INNER), rowmap(COL_SSM_Z // SSM_INNER)),
                  pl.BlockSpec((rows, LANES), rowmap(0)),
                  pl.BlockSpec((CONV_K, SSM_CONV_DIM), const),
                  pl.BlockSpec((1, SSM_CONV_DIM), const),
                  pl.BlockSpec((1, LANES), const),
                  pl.BlockSpec((1, LANES), const),
                  pl.BlockSpec((1, SSM_INNER), const),
                  pl.BlockSpec((1, SSM_INNER), const),
                  pl.BlockSpec((bt, CONV_K - 1, SSM_CONV_DIM), lambda i, j: (i, 0, 0)),
                  pl.BlockSpec((bt, hrows, SSM_STATE), lambda i, j: (i, 0, 0))],
        out_specs=[pl.BlockSpec((rows, SSM_INNER), lambda i, j: (i * ng + j, 0)),
                   pl.BlockSpec((bt, CONV_K - 1, SSM_CONV_DIM), lambda i, j: (i, 0, 0)),
                   pl.BlockSpec((bt, hrows, SSM_STATE), lambda i, j: (i, 0, 0))],
        out_shape=[jax.ShapeDtypeStruct((t, SSM_INNER), BF16),
                   jax.ShapeDtypeStruct(conv_in.shape, F32),
                   jax.ShapeDtypeStruct(rec2.shape, F32)],
        scratch_shapes=[pltpu.VMEM((bt, c + SUBLANES, SSM_CONV_DIM), F32)],
        compiler_params=_cparams("arbitrary", "arbitrary"),
        name="ssd_scan",
    )(proj, proj, small, conv_w, conv_b.reshape(1, -1), _small_row(a_log, SM_DT), _small_row(dt_bias, SM_DT),
      jnp.repeat(d_skip.astype(F32), SSM_HEADDIM).reshape(1, SSM_INNER), norm_w.reshape(1, SSM_INNER),
      conv_in, rec2)
    return y, conv_out, rec_out.reshape(rec_in.shape)


PLAN_E0, PLAN_E1, PLAN_C0, PLAN_C1 = 0, 1, 2, 3


def _route(logits):
    lane = _iota(logits.shape, 1).astype(F32)
    big = float(LANES)
    is_group = (lane >= RT_GROUP) & (lane < RT_GROUP + MOE_GROUPS)
    gl = jnp.where(is_group, logits, -jnp.inf)
    gmax = jnp.max(gl, axis=-1, keepdims=True)
    g_sel = jnp.min(jnp.where(gl == gmax, lane, big), axis=-1, keepdims=True) - RT_GROUP
    p_group = 1.0 / jnp.sum(jnp.exp(gl - gmax), axis=-1, keepdims=True)
    e_lo = RT_EXPERT + MOE_PER_GROUP * g_sel
    in_grp = (lane >= e_lo) & (lane < e_lo + MOE_PER_GROUP)
    el = jnp.where(in_grp, logits, -jnp.inf)
    ee = jnp.exp(el - jnp.max(el, axis=-1, keepdims=True))
    pe = jnp.where(in_grp, ee / jnp.sum(ee, axis=-1, keepdims=True), -1.0)
    p1 = jnp.max(pe, axis=-1, keepdims=True)
    i1 = jnp.min(jnp.where(pe == p1, lane, big), axis=-1, keepdims=True)
    pe2 = jnp.where(lane == i1, -1.0, pe)
    p2 = jnp.max(pe2, axis=-1, keepdims=True)
    i2 = jnp.min(jnp.where(pe2 == p2, lane, big), axis=-1, keepdims=True)
    tot = p1 + p2
    plan = jnp.where(lane == PLAN_E0, i1 - RT_EXPERT, 0.0) + jnp.where(lane == PLAN_E1, i2 - RT_EXPERT, 0.0)
    return plan + jnp.where(lane == PLAN_C0, p_group * p1 / tot, 0.0) + jnp.where(lane == PLAN_C1, p_group * p2 / tot, 0.0)


def _tile_ranks(plan):
    tp = plan.shape[0]
    lane = _iota(plan.shape, 1).astype(F32)
    sel0 = lane == plan[:, PLAN_E0:PLAN_E0 + 1]
    sel1 = lane == plan[:, PLAN_E1:PLAN_E1 + 1]
    sel = (sel0 | sel1).astype(BF16)
    before = (_iota((tp, tp), 0) > _iota((tp, tp), 1)).astype(BF16)
    excl = jnp.dot(before, sel, preferred_element_type=F32)
    r0 = jnp.sum(jnp.where(sel0, excl, 0.0), axis=-1, keepdims=True)
    r1 = jnp.sum(jnp.where(sel1, excl, 0.0), axis=-1, keepdims=True)
    rank = jnp.where(lane == PLAN_E0, r0, 0.0) + jnp.where(lane == PLAN_E1, r1, 0.0)
    return rank, jnp.sum(sel.astype(F32), axis=0, keepdims=True)


def _merge_kernel(x_ref, og_ref, ys_ref, gates_ref, gt1_ref, sc2_ref, sh2_ref, nw2_ref, wdn_ref, wssm_ref,
                  wout_ref, wr_ref, br_ref, x1_ref, h2_ref, plan_ref, rank_ref, cnt_ref):
    y_dn = _dot(og_ref[...], wdn_ref[...])
    y_ssm = _dot(ys_ref[...], wssm_ref[...])
    merged = (_sigmoid(gates_ref[:, :D_MODEL].astype(F32)) * y_dn
              + _sigmoid(gates_ref[:, D_MODEL:].astype(F32)) * y_ssm)
    x1 = x_ref[...] + gt1_ref[...] * _dot(merged, wout_ref[...])
    x1_ref[...] = x1
    h2 = _rms(x1, nw2_ref[...]) * (1.0 + sc2_ref[...]) + sh2_ref[...]
    h2_ref[...] = h2.astype(BF16)
    plan = _route(_dot_x3(h2, wr_ref[...]) + br_ref[...])
    plan_ref[...] = plan
    rank_ref[...], cnt_ref[...] = _tile_ranks(plan)


def _merge(x3, og, ys, proj, gt1, sc2, sh2, nw2, wdn, wssm, wout, wr, br, tm):
    bx, lx, d = x3.shape
    assert tm == TOK_TILE
    nl = lx // tm
    rowmap = lambda col: (lambda b, i: (b * nl + i, col))
    const = lambda b, i: (0, 0)
    tok = lambda w, dt: jax.ShapeDtypeStruct((bx, lx, w), dt)
    return pl.pallas_call(
        _merge_kernel,
        grid=(bx, nl),
        in_specs=[pl.BlockSpec((None, tm, d), lambda b, i: (b, i, 0)),
                  pl.BlockSpec((tm, DN_VAL), rowmap(0)),
                  pl.BlockSpec((tm, SSM_INNER), rowmap(0)),
                  pl.BlockSpec((tm, 2 * d), rowmap(COL_GATES // (2 * d))),
                  _mod_spec(gt1, tm), _mod_spec(sc2, tm), _mod_spec(sh2, tm),
                  pl.BlockSpec((1, d), const),
                  pl.BlockSpec(wdn.shape, const), pl.BlockSpec(wssm.shape, const), pl.BlockSpec(wout.shape, const),
                  pl.BlockSpec(wr.shape, const), pl.BlockSpec((1, LANES), const)],
        out_specs=[pl.BlockSpec((None, tm, d), lambda b, i: (b, i, 0)),
                   pl.BlockSpec((None, tm, d), lambda b, i: (b, i, 0)),
                   pl.BlockSpec((None, tm, LANES), lambda b, i: (b, i, 0)),
                   pl.BlockSpec((None, tm, LANES), lambda b, i: (b, i, 0)),
                   pl.BlockSpec((None, 1, LANES), lambda b, i: (b * nl + i, 0, 0))],
        out_shape=[tok(d, F32), tok(d, BF16), tok(LANES, F32), tok(LANES, F32),
                   jax.ShapeDtypeStruct((bx * nl, 1, LANES), F32)],
        compiler_params=_cparams("arbitrary", "arbitrary"),
        name="merge_route",
    )(x3, og, ys, proj, gt1, sc2, sh2, nw2.reshape(1, d), wdn, wssm, wout, wr, br)


FFN_ROWS = 512
TOK_TILE = 512
RUN_ALIGN = 2 * SUBLANES
RUN_BIG = 4 * RUN_ALIGN
RUN_SMALL = (2 * RUN_ALIGN, RUN_ALIGN)
LOCAL_ROWS = 2 * TOK_TILE + MOE_EXPERTS * RUN_ALIGN


def _local_slots(plan, rank, off_row):
    lane = _iota(plan.shape, 1).astype(F32)
    slots = []
    for k in (PLAN_E0, PLAN_E1):
        off = jnp.sum(jnp.where(lane == plan[:, k:k + 1], off_row, 0.0), axis=-1, keepdims=True)
        slots.append(off + rank[:, k:k + 1])
    return slots


def _run_blocks(p8_ref, loff_ref, base_ref, act):
    for e in range(MOE_EXPERTS):
        n = p8_ref[0, e]
        lo = loff_ref[0, e]
        go = base_ref[0, e]

        def big(k, carry, lo=lo, go=go):
            off = k * RUN_BIG
            act(pl.multiple_of(lo + off, RUN_ALIGN), pl.multiple_of(go + off, RUN_ALIGN), RUN_BIG)
            return carry

        lax.fori_loop(0, lax.shift_right_logical(n, RUN_BIG.bit_length() - 1), big, 0)
        for size in RUN_SMALL:
            @pl.when(jnp.bitwise_and(n, size) != 0)
            def _(size=size, n=n, lo=lo, go=go):
                done = jnp.bitwise_and(n, ~(2 * size - 1))
                act(pl.multiple_of(lo + done, RUN_ALIGN), pl.multiple_of(go + done, RUN_ALIGN), size)


def _dispatch_kernel(tv_ref, p8_ref, loff_ref, base_ref, plan_ref, rank_ref, off_ref, h2_ref, *rest, first):
    xs_ref, buf_ref, zero_ref, sem, zsem = rest[-5:]

    @pl.when((pl.program_id(0) == 0) & first)
    def _():
        zero_ref[...] = jnp.zeros_like(zero_ref)

        def zero_copy(i):
            rows = pl.ds(pl.multiple_of(i * FFN_ROWS, FFN_ROWS), FFN_ROWS)
            return pltpu.make_async_copy(zero_ref, xs_ref.at[rows], zsem)

        def zero_tiles(act):
            def body(i, carry):
                @pl.when(tv_ref[i] < FFN_ROWS)
                def _():
                    act(zero_copy(i))
                return carry
            lax.fori_loop(0, xs_ref.shape[0] // FFN_ROWS, body, 0)

        zero_tiles(lambda cp: cp.start())
        zero_tiles(lambda cp: cp.wait())

    s0, s1 = _local_slots(plan_ref[...], rank_ref[...], off_ref[...])
    row = _iota((1, LOCAL_ROWS), 1).astype(F32)
    onehot_t = ((row == s0) | (row == s1)).astype(BF16)
    buf_ref[...] = lax.dot_general(onehot_t, h2_ref[...], (((0,), (0,)), ((), ())),
                                   preferred_element_type=F32).astype(BF16)

    def copy(lo, go, size):
        return pltpu.make_async_copy(buf_ref.at[pl.ds(lo, size)], xs_ref.at[pl.ds(go, size)], sem)

    _run_blocks(p8_ref, loff_ref, base_ref, lambda lo, go, size: copy(lo, go, size).start())
    _run_blocks(p8_ref, loff_ref, base_ref, lambda lo, go, size: copy(lo, go, size).wait())


def _tile_scalars():
    return pl.BlockSpec((None, 1, LANES), lambda i, *_: (i, 0, 0), memory_space=pltpu.SMEM)


def _dispatch(tile_valid, p8, loff, base, plan2, rank, off_f, h2, rows, xs_prev=None):
    t, d = h2.shape
    tt = TOK_TILE
    tok = lambda w: pl.BlockSpec((tt, w), lambda i, tv: (i, 0))
    operands = (tile_valid, p8, loff, base, plan2, rank, off_f, h2) + (() if xs_prev is None else (xs_prev,))
    return pl.pallas_call(
        functools.partial(_dispatch_kernel, first=xs_prev is None),
        grid_spec=pltpu.PrefetchScalarGridSpec(
            num_scalar_prefetch=1,
            grid=(t // tt,),
            in_specs=[_tile_scalars(), _tile_scalars(), _tile_scalars(),
                      tok(LANES), tok(LANES),
                      pl.BlockSpec((None, 1, LANES), lambda i, tv: (i, 0, 0)),
                      tok(d)] + ([] if xs_prev is None else [pl.BlockSpec(memory_space=pl.ANY)]),
            out_specs=pl.BlockSpec(memory_space=pl.ANY),
            scratch_shapes=[pltpu.VMEM((LOCAL_ROWS, d), BF16), pltpu.VMEM((FFN_ROWS, d), BF16),
                            pltpu.SemaphoreType.DMA(()), pltpu.SemaphoreType.DMA(())]),
        out_shape=jax.ShapeDtypeStruct((rows, d), BF16),
        input_output_aliases={} if xs_prev is None else {len(operands) - 1: 0},
        compiler_params=_cparams("arbitrary"),
        name="moe_dispatch",
    )(*operands)


def _ffn_kernel(te_ref, tv_ref, src_ref, xs_ref, wg_ref, wu_ref, wd_ref, o_ref):
    nv = tv_ref[pl.program_id(0)]

    half = FFN_ROWS // 2

    def swiglu(rows):
        x = xs_ref[rows, :]
        o_ref[rows, :] = _dot(_silu(_dot(x, wg_ref[...])) * _dot(x, wu_ref[...]), wd_ref[...]).astype(BF16)

    @pl.when(nv > half)
    def _():
        swiglu(slice(None))

    @pl.when((nv > 0) & (nv <= half))
    def _():
        swiglu(slice(0, half))
        o_ref[half:, :] = jnp.zeros((half, o_ref.shape[1]), o_ref.dtype)

    @pl.when(nv <= 0)
    def _():
        o_ref[...] = jnp.zeros_like(o_ref)


def _ffn(tile_expert, tile_valid, tile_src, xs, wg, wu, wd):
    rows, d = xs.shape
    wmap = lambda i, te, tv, src: (te[i], 0, 0)
    return pl.pallas_call(
        _ffn_kernel,
        grid_spec=pltpu.PrefetchScalarGridSpec(
            num_scalar_prefetch=3,
            grid=(rows // FFN_ROWS,),
            in_specs=[pl.BlockSpec((FFN_ROWS, d), lambda i, te, tv, src: (src[i], 0)),
                      pl.BlockSpec((None, d, MOE_FF), wmap),
                      pl.BlockSpec((None, d, MOE_FF), wmap),
                      pl.BlockSpec((None, MOE_FF, d), wmap)],
            out_specs=pl.BlockSpec((FFN_ROWS, d), lambda i, te, tv, src: (i, 0))),
        out_shape=jax.ShapeDtypeStruct((rows, d), BF16),
        compiler_params=_cparams("arbitrary"),
        name="moe_ffn",
    )(tile_expert, tile_valid, tile_src, xs, wg, wu, wd)


def _combine_kernel(p8_ref, loff_ref, base_ref, ys_ref, plan_ref, rank_ref, off_ref, x1_ref, gt2_ref, fsc_ref,
                    fsh_ref, fnw_ref, y_ref, buf_ref, sem):
    @pl.when((pl.program_id(0) == 0) & (pl.program_id(1) == 0))
    def _():
        buf_ref[...] = jnp.zeros_like(buf_ref)

    def copy(lo, go, size):
        return pltpu.make_async_copy(ys_ref.at[pl.ds(go, size)], buf_ref.at[pl.ds(lo, size)], sem)

    _run_blocks(p8_ref, loff_ref, base_ref, lambda lo, go, size: copy(lo, go, size).start())
    plan = plan_ref[...]
    s0, s1 = _local_slots(plan, rank_ref[...], off_ref[...])
    row = _iota((1, LOCAL_ROWS), 1).astype(F32)
    weights = (jnp.where(row == s0, plan[:, PLAN_C0:PLAN_C0 + 1], 0.0)
               + jnp.where(row == s1, plan[:, PLAN_C1:PLAN_C1 + 1], 0.0))
    _run_blocks(p8_ref, loff_ref, base_ref, lambda lo, go, size: copy(lo, go, size).wait())
    total = loff_ref[0, MOE_EXPERTS - 1] + p8_ref[0, MOE_EXPERTS - 1]
    filled = _iota((LOCAL_ROWS, 1), 0) < total
    moe = _dot(weights, jnp.where(filled, buf_ref[...], jnp.zeros((), BF16)))
    x2 = x1_ref[...] + gt2_ref[...] * moe
    y_ref[...] = _rms(x2, fnw_ref[...]) * (1.0 + fsc_ref[...]) + fsh_ref[...]


def _combine(p8, loff, base, ys, plan, rank3, off_f, x1, gt2, fsc, fsh, fnw):
    bx, lx, d = x1.shape
    tt = TOK_TILE
    nl = lx // tt
    tile = lambda b, i: b * nl + i
    scal = pl.BlockSpec((None, 1, LANES), lambda b, i: (tile(b, i), 0, 0), memory_space=pltpu.SMEM)
    tokspec = lambda w: pl.BlockSpec((None, tt, w), lambda b, i: (b, i, 0))
    return pl.pallas_call(
        _combine_kernel,
        grid=(bx, nl),
        in_specs=[scal, scal, scal,
                  pl.BlockSpec(memory_space=pl.ANY),
                  tokspec(LANES), tokspec(LANES),
                  pl.BlockSpec((None, 1, LANES), lambda b, i: (tile(b, i), 0, 0)),
                  tokspec(d), _mod_spec(gt2, tt), _mod_spec(fsc, tt), _mod_spec(fsh, tt),
                  pl.BlockSpec((1, d), lambda b, i: (0, 0))],
        out_specs=tokspec(d),
        out_shape=jax.ShapeDtypeStruct((bx, lx, d), F32),
        scratch_shapes=[pltpu.VMEM((LOCAL_ROWS, d), BF16), pltpu.SemaphoreType.DMA(())],
        compiler_params=_cparams("arbitrary", "arbitrary"),
        name="moe_combine_final",
    )(p8, loff, base, ys, plan, rank3, off_f, x1, gt2, fsc, fsh, fnw.reshape(1, d))


def _moe(groups, wg, wu, wd, fnw):
    d = groups[0]["x1"].shape[-1]
    ntiles = [g["x1"].shape[0] * g["x1"].shape[1] // TOK_TILE for g in groups]
    t = sum(ntiles) * TOK_TILE
    n = jnp.concatenate([g["cnt"][:, 0, :MOE_EXPERTS] for g in groups], axis=0).astype(jnp.int32)
    p8 = (n + RUN_ALIGN - 1) // RUN_ALIGN * RUN_ALIGN
    loff = jnp.cumsum(p8, axis=1) - p8
    erows = jnp.sum(p8, axis=0)
    epad = (erows + FFN_ROWS - 1) // FFN_ROWS * FFN_ROWS
    ends = jnp.cumsum(epad)
    starts = ends - epad
    base = starts[None, :] + jnp.cumsum(p8, axis=0) - p8
    lanes = lambda a: jnp.pad(a, ((0, 0), (0, LANES - MOE_EXPERTS)))[:, None, :]
    rows = (-(-(2 * t + (t // TOK_TILE) * MOE_EXPERTS * (RUN_ALIGN - 1)) // FFN_ROWS) + MOE_EXPERTS) * FFN_ROWS
    tile_start = jnp.arange(rows // FFN_ROWS, dtype=jnp.int32) * FFN_ROWS
    tile_expert = jnp.minimum(jnp.sum(tile_start[:, None] >= ends[None, :], axis=1), MOE_EXPERTS - 1).astype(jnp.int32)
    hot = tile_expert[:, None] == jnp.arange(MOE_EXPERTS)[None, :]
    valid_rows = lambda er: jnp.clip(jnp.sum(jnp.where(hot, er - (tile_start[:, None] - starts), 0), axis=1),
                                     0, FFN_ROWS).astype(jnp.int32)
    tile_valid = valid_rows(erows)
    p8l, loffl, basel = lanes(p8), lanes(loff), lanes(base)
    off_f = loffl.astype(F32)
    spans, first = [], 0
    for nt in ntiles:
        spans.append(slice(first, first + nt))
        first += nt
    first_valid = valid_rows(jnp.sum(p8[spans[0]], axis=0))
    xs = None
    for g, sp in zip(groups, spans):
        tg = g["h2"].shape[0] * g["h2"].shape[1]
        xs = _dispatch(first_valid, p8l[sp], loffl[sp], basel[sp], g["plan"].reshape(tg, LANES),
                       g["rank"].reshape(tg, LANES), off_f[sp], g["h2"].reshape(tg, d), rows, xs_prev=xs)
    tile_id = tile_start // FFN_ROWS
    tile_src = jnp.where(tile_valid > 0, tile_id, jnp.max(jnp.where(tile_valid > 0, tile_id, 0))).astype(jnp.int32)
    ys = _ffn(tile_expert, tile_valid, tile_src, xs, wg, wu, wd)
    return [_combine(p8l[sp], loffl[sp], basel[sp], ys, g["plan"], g["rank"], off_f[sp], g["x1"], g["gt2"],
                     g["fsc"], g["fsh"], fnw) for g, sp in zip(groups, spans)]


def _prep_layer(lp):
    w_in = lp["w_in"]
    offs = [0]
    for s in (DN_CONV_DIM, DN_VAL, DN_HEADS, DN_HEADS, SSM_CONV_DIM, SSM_INNER, SSM_HEADS, D_MODEL, D_MODEL):
        offs.append(offs[-1] + s)
    seg = lambda i: w_in[:, offs[i]:offs[i + 1]]
    small = jnp.concatenate([seg(2), seg(3), seg(6)], axis=1)
    pad = jnp.zeros((D_MODEL, PROJ_N - COL_SMALL - small.shape[1]), F32)
    w_cat = jnp.concatenate([seg(0), seg(4), seg(5), seg(7), seg(8), seg(1), small, pad], axis=1).astype(BF16)
    wr = jnp.concatenate([lp["w_group_router"], lp["w_expert_router"],
                          jnp.zeros((D_MODEL, LANES - MOE_GROUPS - MOE_EXPERTS), F32)], axis=1)
    br = jnp.concatenate([lp["b_group_router"], lp["b_expert_router"],
                          jnp.zeros((LANES - MOE_GROUPS - MOE_EXPERTS,), F32)]).reshape(1, LANES)
    return dict(lp, w_cat=w_cat, wr=wr, br=br,
                wdn=lp["w_dn_out"].astype(BF16), wssm=lp["w_ssm_out"].astype(BF16), wout=lp["w_out"].astype(BF16),
                wg=lp["w_exp_gate"], wu=lp["w_exp_up"], wd=lp["w_exp_down"])


def _mixer(x3, mods, fins, states, lp, cfg):
    bx, lx, d = x3.shape
    scan = dict(bt=cfg["bt"], r=cfg["r"], c=cfg["c"], lc=cfg["lc"], ng=cfg["ng"])
    sh1, sc1, gt1, sh2, sc2, gt2 = mods
    dn_conv, dn_rec, ssm_conv, ssm_rec = states
    proj, small = _inproj(x3, sc1, sh1, lp["norm_mix_w"], lp["w_cat"], cfg["tm"])
    proj, small = proj.reshape(bx * lx, PROJ_N), small.reshape(bx * lx, LANES)
    og, dn_conv_new, dn_rec_new = _dn_branch(proj, small, lp["dn_conv_w"], lp["dn_A_log"], lp["dn_dt_bias"],
                                             lp["dn_norm_w"], dn_conv, dn_rec, cpi=cfg["cpi_dn"], **scan)
    ys, ssm_conv_new, ssm_rec_new = _ssd_branch(proj, small, lp["ssm_conv_w"], lp["ssm_conv_b"], lp["ssm_A_log"],
                                                lp["ssm_dt_bias"], lp["ssm_D"], lp["ssm_norm_w"],
                                                ssm_conv, ssm_rec, cpi=cfg["cpi_ssd"], **scan)
    x1, h2, plan, rank, cnt = _merge(x3, og, ys, proj, gt1, sc2, sh2, lp["norm_ffn_w"], lp["wdn"], lp["wssm"],
                                     lp["wout"], lp["wr"], lp["br"], cfg["tm_merge"])
    group = dict(x1=x1, h2=h2, plan=plan, rank=rank, cnt=cnt, gt2=gt2, fsh=fins[0], fsc=fins[1])
    return group, (dn_conv_new, dn_rec_new, ssm_conv_new, ssm_rec_new)


def _per_seq(m):
    return m[:, None, :]


def kernel(x_prompt, x_sample, c_prompt, c_sample, state_dn_conv, state_dn_rec, state_ssm_conv, state_ssm_rec, w_ada, b_ada, norm_mix_w, w_in, dn_conv_w, dn_A_log, dn_dt_bias, dn_norm_w, w_dn_out, ssm_conv_w, ssm_conv_b, ssm_A_log, ssm_dt_bias, ssm_D, ssm_norm_w, w_ssm_out, w_out, norm_ffn_w, w_group_router, b_group_router, w_expert_router, b_expert_router, w_exp_gate, w_exp_up, w_exp_down, w_ada_final, b_ada_final, final_norm_w):
    depth = w_ada.shape[0]
    assert depth == 1
    per_layer = dict(w_ada=w_ada, b_ada=b_ada, norm_mix_w=norm_mix_w, w_in=w_in, dn_conv_w=dn_conv_w,
                     dn_A_log=dn_A_log, dn_dt_bias=dn_dt_bias, dn_norm_w=dn_norm_w, w_dn_out=w_dn_out,
                     ssm_conv_w=ssm_conv_w, ssm_conv_b=ssm_conv_b, ssm_A_log=ssm_A_log, ssm_dt_bias=ssm_dt_bias,
                     ssm_D=ssm_D, ssm_norm_w=ssm_norm_w, w_ssm_out=w_ssm_out, w_out=w_out, norm_ffn_w=norm_ffn_w,
                     w_group_router=w_group_router, b_group_router=b_group_router,
                     w_expert_router=w_expert_router, b_expert_router=b_expert_router,
                     w_exp_gate=w_exp_gate, w_exp_up=w_exp_up, w_exp_down=w_exp_down)
    layers = [_prep_layer({k: v[l] for k, v in per_layer.items()}) for l in range(depth)]

    nbp, lp_, d = x_prompt.shape
    nbs, ls, _ = x_sample.shape
    c_all = jnp.concatenate([c_prompt, c_sample], axis=0)
    mod_all = [_ada(c_all, lyr["w_ada"], lyr["b_ada"]) for lyr in layers]
    fin_all = _ada(c_all, w_ada_final, b_ada_final)

    mods_p = [[_per_seq(m) for m in jnp.split(ma[:nbp], 6, axis=-1)] for ma in mod_all]
    fins_p = [_per_seq(m) for m in jnp.split(fin_all[:nbp], 2, axis=-1)]
    zeros_p = [(jnp.zeros((nbp, CONV_K - 1, DN_CONV_DIM), F32), jnp.zeros((nbp, DN_HEADS, DN_DK, DN_DV), F32),
                jnp.zeros((nbp, CONV_K - 1, SSM_CONV_DIM), F32),
                jnp.zeros((nbp, SSM_HEADS, SSM_HEADDIM, SSM_STATE), F32)) for _ in range(depth)]
    c_p = min(SCAN_CHUNK, lp_)
    r_p = min(lp_, 8 * c_p)
    cfg_p = dict(tm=min(lp_, 2048), tm_merge=min(lp_, TOK_TILE), bt=1, r=r_p, c=c_p, lc=c_p, ng=lp_ // r_p,
                 cpi_dn=8 if (r_p // c_p) % 8 == 0 else 1, cpi_ssd=4 if (r_p // c_p) % 4 == 0 else 1)
    grp_p, st_p = _mixer(x_prompt, mods_p[0], fins_p, zeros_p[0], layers[0], cfg_p)

    lpad = -(-ls // SUBLANES) * SUBLANES
    bt_s = 16
    xs = jnp.pad(x_sample, ((0, 0), (0, lpad - ls), (0, 0))).reshape(1, nbs * lpad, d)
    per_tok = lambda m: jnp.repeat(m, lpad, axis=0)[None]
    mods_s = [[per_tok(m) for m in jnp.split(ma[nbp:], 6, axis=-1)] for ma in mod_all]
    fins_s = [per_tok(m) for m in jnp.split(fin_all[nbp:], 2, axis=-1)]
    st_in = [(state_dn_conv[l], state_dn_rec[l], state_ssm_conv[l], state_ssm_rec[l]) for l in range(depth)]
    ts = nbs * lpad
    cfg_s = dict(tm=ts, tm_merge=min(ts, TOK_TILE), bt=bt_s, r=lpad, c=lpad, lc=ls, ng=1, cpi_dn=1, cpi_ssd=1)
    grp_s, st_s = _mixer(xs, mods_s[0], fins_s, st_in[0], layers[0], cfg_s)

    y_p, y_s = _moe([grp_p, grp_s], layers[0]["wg"], layers[0]["wu"], layers[0]["wd"], final_norm_w)
    y_s = y_s.reshape(nbs, lpad, d)[:, :ls]
    return (y_p, y_s) + tuple(st[None] for st in st_p) + tuple(st[None] for st in st_s)
```

```python
import functools
import math

import jax
import jax.numpy as jnp
from jax import lax
from jax.experimental import pallas as pl
from jax.experimental.pallas import tpu as pltpu

F32 = jnp.float32
BF16 = jnp.bfloat16

D_MODEL = 1024
DN_HEADS = 4
DN_DK = 128
DN_DV = 128
DN_KEY = DN_HEADS * DN_DK
DN_VAL = DN_HEADS * DN_DV
CONV_K = 4
DN_CONV_DIM = 2 * DN_KEY + DN_VAL
SSM_INNER = D_MODEL
SSM_HEADDIM = 64
SSM_HEADS = SSM_INNER // SSM_HEADDIM
SSM_GROUPS = 2
SSM_STATE = 128
SSM_CONV_DIM = SSM_INNER + 2 * SSM_GROUPS * SSM_STATE
MOE_GROUPS = 4
MOE_PER_GROUP = 8
MOE_EXPERTS = MOE_GROUPS * MOE_PER_GROUP
MOE_FF = D_MODEL // 4
EPS = 1e-6
SCAN_CHUNK = 64

LANES = 128
SUBLANES = 8
CARRY_ROW = SUBLANES - (CONV_K - 1)

COL_DN_QKV = 0
COL_SSM_XBC = COL_DN_QKV + DN_CONV_DIM
COL_SSM_Z = COL_SSM_XBC + SSM_CONV_DIM
COL_GATES = COL_SSM_Z + SSM_INNER
COL_DN_Z = COL_GATES + 2 * D_MODEL
COL_SMALL = COL_DN_Z + DN_VAL
PROJ_TN = 1792
PROJ_N = 4 * PROJ_TN
SM_A, SM_B, SM_DT = 0, DN_HEADS, 2 * DN_HEADS
RT_GROUP, RT_EXPERT = 0, MOE_GROUPS

VMEM_LIMIT = 56 * 1024 * 1024


def _cparams(*sem):
    return pltpu.CompilerParams(dimension_semantics=sem, vmem_limit_bytes=VMEM_LIMIT)


def _dot(a, b):
    return jnp.dot(a.astype(BF16), b.astype(BF16), preferred_element_type=F32)


def _dot_nt(a, b):
    return lax.dot_general(a.astype(BF16), b.astype(BF16), (((1,), (1,)), ((), ())),
                           preferred_element_type=F32)


def _dot_tn(a, b):
    return lax.dot_general(a.astype(BF16), b.astype(BF16), (((0,), (0,)), ((), ())),
                           preferred_element_type=F32)


def _split3(x):
    hi = x.astype(BF16)
    r = x - hi.astype(F32)
    mid = r.astype(BF16)
    lo = (r - mid.astype(F32)).astype(BF16)
    return hi, mid, lo


def _dot_exact_lhs(a_bf, b):
    hi, mid, lo = _split3(b)
    d = functools.partial(jnp.dot, preferred_element_type=F32)
    return d(a_bf, hi) + (d(a_bf, mid) + d(a_bf, lo))


def _dot_x3(a, b):
    a_hi = a.astype(BF16)
    a_lo = (a - a_hi.astype(F32)).astype(BF16)
    b_hi = b.astype(BF16)
    b_lo = (b - b_hi.astype(F32)).astype(BF16)
    d = functools.partial(jnp.dot, preferred_element_type=F32)
    n = b.shape[1]
    hi_both = d(a_hi, jnp.concatenate([b_hi, b_lo], axis=1))
    return hi_both[:, :n] + (hi_both[:, n:] + d(a_lo, b_hi))


def _sigmoid(x):
    return 1.0 / (1.0 + jnp.exp(-x))


def _silu(x):
    return x * _sigmoid(x)


def _softplus(x):
    return jnp.maximum(x, 0.0) + jnp.log1p(jnp.exp(-jnp.abs(x)))


def _rms(x, w):
    return x * lax.rsqrt(jnp.mean(x * x, axis=-1, keepdims=True) + EPS) * w


def _iota(shape, dim):
    return lax.broadcasted_iota(jnp.int32, shape, dim)


def _ada_kernel(c_ref, w_ref, b_ref, o_ref):
    o_ref[...] = _dot(_silu(c_ref[...]), w_ref[...]) + b_ref[...]


def _ada(c, w, b, tn=512):
    m, d = c.shape
    n = w.shape[1]
    return pl.pallas_call(
        _ada_kernel,
        grid=(n // tn,),
        in_specs=[pl.BlockSpec((m, d), lambda j: (0, 0)),
                  pl.BlockSpec((d, tn), lambda j: (0, j)),
                  pl.BlockSpec((1, tn), lambda j: (0, j))],
        out_specs=pl.BlockSpec((m, tn), lambda j: (0, j)),
        out_shape=jax.ShapeDtypeStruct((m, n), F32),
        compiler_params=_cparams("arbitrary"),
        name="ada_mod",
    )(c, w, b.reshape(1, n))


def _mod_spec(mod, tm):
    if mod.shape[1] == 1:
        return pl.BlockSpec((None, 1, D_MODEL), lambda b, i, *_: (b, 0, 0))
    return pl.BlockSpec((None, tm, D_MODEL), lambda b, i, *_: (b, i, 0))


def _inproj_kernel(x_ref, sc_ref, sh_ref, nw_ref, w_ref, o_ref, sm_ref, h_ref, *, tm, sub):
    @pl.when(pl.program_id(2) == 0)
    def _():
        per_token = sc_ref.shape[0] != 1

        def body(r, carry):
            rows = pl.ds(pl.multiple_of(r * sub, sub), sub)
            sc = sc_ref[rows, :] if per_token else sc_ref[...]
            sh = sh_ref[rows, :] if per_token else sh_ref[...]
            h = _rms(x_ref[rows, :], nw_ref[...]) * (1.0 + sc) + sh
            h_ref[rows, :] = h.astype(BF16)
            return carry

        lax.fori_loop(0, tm // sub, body, 0)

    acc = jnp.dot(h_ref[...], w_ref[...], preferred_element_type=F32)
    o_ref[...] = acc.astype(BF16)

    @pl.when(pl.program_id(2) == COL_SMALL // PROJ_TN)
    def _():
        sm_ref[...] = acc[:, COL_SMALL % PROJ_TN:COL_SMALL % PROJ_TN + LANES]


def _inproj(x3, sc, sh, nw, w_cat, tm):
    bx, lx, d = x3.shape
    n = w_cat.shape[1]
    kern = functools.partial(_inproj_kernel, tm=tm, sub=min(tm, 256))
    return pl.pallas_call(
        kern,
        grid=(bx, lx // tm, n // PROJ_TN),
        in_specs=[pl.BlockSpec((None, tm, d), lambda b, i, j: (b, i, 0)),
                  _mod_spec(sc, tm), _mod_spec(sh, tm),
                  pl.BlockSpec((1, d), lambda b, i, j: (0, 0)),
                  pl.BlockSpec((d, PROJ_TN), lambda b, i, j: (0, j))],
        out_specs=[pl.BlockSpec((None, tm, PROJ_TN), lambda b, i, j: (b, i, j)),
                   pl.BlockSpec((None, tm, LANES), lambda b, i, j: (b, i, 0))],
        out_shape=[jax.ShapeDtypeStruct((bx, lx, n), BF16), jax.ShapeDtypeStruct((bx, lx, LANES), F32)],
        scratch_shapes=[pltpu.VMEM((tm, d), BF16)],
        compiler_params=_cparams("arbitrary", "arbitrary", "arbitrary"),
        name="norm_inproj",
    )(x3, sc, sh, nw.reshape(1, d), w_cat)


def _conv_silu(u, ext_ref, bb, cw_ref, c, lc, bias=None):
    ext_ref[bb, SUBLANES:SUBLANES + c, :] = u
    y = ext_ref[bb, CARRY_ROW:CARRY_ROW + c, :] * cw_ref[0:1, :]
    for i in range(1, CONV_K):
        y = y + ext_ref[bb, CARRY_ROW + i:CARRY_ROW + i + c, :] * cw_ref[i:i + 1, :]
    ext_ref[bb, CARRY_ROW:SUBLANES, :] = ext_ref[bb, CARRY_ROW + lc:SUBLANES + lc, :]
    if bias is not None:
        y = y + bias
    return _silu(y)


def _conv_rounds(ref, rounds, ext_ref, cw_ref, r, c, lc, bias=None):
    read = _chunk_reader(ref, r, c)
    return [_conv_silu(read(bb, r0), ext_ref, bb, cw_ref, c, lc, bias) for rnd in rounds for bb, r0 in rnd]


def _chunk_reader(ref, r, c):
    if r == c:
        whole = ref[...].astype(F32)
        return lambda bb, r0: whole[bb * c:(bb + 1) * c]
    return lambda bb, r0: ref[pl.ds(bb * r + r0, c), :].astype(F32)


def _chunk_writer(ref, bt, r, c):
    if r != c:
        return (lambda bb, r0, val: ref.__setitem__((pl.ds(bb * r + r0, c), slice(None)), val.astype(ref.dtype)),
                lambda: None)
    parts = {}

    def flush():
        ref[...] = jnp.concatenate([parts[bb] for bb in range(bt)], axis=0).astype(ref.dtype)
    return (lambda bb, r0, val: parts.__setitem__(bb, val)), flush


def _run_rounds(process, bt, r, c, cpi):
    g = r // c
    if g == 1:
        process([[(bb, 0) for bb in range(bt)]])
    else:
        def body(ci, carry):
            process([[(bb, pl.multiple_of((ci * cpi + t) * c, c)) for bb in range(bt)] for t in range(cpi)])
            return carry
        lax.fori_loop(0, g // cpi, body, 0)


INV_BASE = SUBLANES


def _inverse_masks(row, col, c):
    sh = lambda x, s: jnp.right_shift(x, int(math.log2(s)))
    diag = sh(row, INV_BASE) == sh(col, INV_BASE)
    merges = []
    s = INV_BASE
    while s < c:
        merges.append((sh(row, 2 * s) == sh(col, 2 * s))
                      & (jnp.bitwise_and(sh(row, s), 1) == 1) & (jnp.bitwise_and(sh(col, s), 1) == 0))
        s *= 2
    return diag, merges


def _unit_lower_inverses(lmats, eye_f, masks):
    diag, merges = masks
    npows = [-jnp.where(diag, l, 0.0) for l in lmats]
    ps = [eye_f + n for n in npows]
    for _ in range(int(math.log2(INV_BASE)) - 1):
        npows = [_dot(n, n) for n in npows]
        ps = [p + _dot(p, n) for p, n in zip(ps, npows)]
    for m in merges:
        ts = [_dot(p, jnp.where(m, l, 0.0)) for p, l in zip(ps, lmats)]
        ps = [p - _dot(t, p) for t, p in zip(ts, ps)]
    return ps


def _dn_kernel(qkv_ref, z_ref, sm_ref, cw_ref, alog_ref, bias_ref, nw_ref, cin_ref, sin_ref,
               o_ref, cout_ref, sout_ref, ext_ref, *, bt, r, c, lc, ng, cpi):
    j = pl.program_id(1)

    @pl.when(j == 0)
    def _():
        ext_ref[:, CARRY_ROW:SUBLANES, :] = cin_ref[...]
        sout_ref[...] = sin_ref[...]

    row = _iota((c, c), 0)
    col = _iota((c, c), 1)
    tri_incl = row >= col
    tri_strict = row > col
    eye_f = (row == col).astype(F32)
    tril_bf = tri_incl.astype(BF16)
    valid = _iota((c, 1), 0) < lc
    nexp_a = -jnp.exp(alog_ref[...])
    bias = bias_ref[...]
    inv_masks = _inverse_masks(row, col, c)
    heads = range(DN_HEADS)
    read_z, read_sm = (_chunk_reader(ref, r, c) for ref in (z_ref, sm_ref))
    put_o, flush_o = _chunk_writer(o_ref, bt, r, c)

    def prep(bb, r0, y):
        sm = read_sm(bb, r0)
        g_all = nexp_a * _softplus(sm + bias)
        beta_all = _sigmoid(sm)
        if lc < c:
            g_all = jnp.where(valid, g_all, 0.0)
            beta_all = jnp.where(valid, beta_all, 0.0)
        cs_all = _dot_exact_lhs(tril_bf, g_all)
        cs_t = cs_all.T
        ecs_all = jnp.exp(cs_all)
        per_head = []
        for h in heads:
            q = y[:, h * DN_DK:(h + 1) * DN_DK]
            k = y[:, DN_KEY + h * DN_DK:DN_KEY + (h + 1) * DN_DK]
            v = y[:, 2 * DN_KEY + h * DN_DV:2 * DN_KEY + (h + 1) * DN_DV]
            q = q * lax.rsqrt(jnp.sum(q * q, axis=-1, keepdims=True) + EPS) * (DN_DK ** -0.5)
            k = k * lax.rsqrt(jnp.sum(k * k, axis=-1, keepdims=True) + EPS)
            if lc < c:
                k = jnp.where(valid, k, 0.0)
            cs = cs_all[:, SM_A + h:SM_A + h + 1]
            ecs = ecs_all[:, SM_A + h:SM_A + h + 1]
            beta = beta_all[:, SM_B + h:SM_B + h + 1]
            last = cs_all[c - 1:c, SM_A + h:SM_A + h + 1]
            decay = jnp.exp(jnp.where(tri_incl, cs - cs_t[SM_A + h:SM_A + h + 1, :], -jnp.inf))
            per_head.append(dict(q_ecs=(q * ecs).astype(BF16), q=q.astype(BF16), k=k.astype(BF16), beta=beta,
                                 decay=decay, elast=jnp.exp(last),
                                 k_tail=(k * jnp.exp(last - cs)).astype(BF16),
                                 rhs=jnp.concatenate([v * beta, k * (beta * ecs)], axis=1).astype(BF16)))
        return per_head

    def process(rounds):
        items = [it for rnd in rounds for it in rnd]
        conv = _conv_rounds(qkv_ref, rounds, ext_ref, cw_ref, r, c, lc)
        pre = [prep(bb, r0, y) for (bb, r0), y in zip(items, conv)]
        chains = [(i, h) for i in range(len(items)) for h in heads]
        a = {ch: pre[ch[0]][ch[1]] for ch in chains}
        kk = {ch: _dot_nt(a[ch]["k"], a[ch]["k"]) for ch in chains}
        qk = {ch: _dot_nt(a[ch]["q"], a[ch]["k"]) * a[ch]["decay"] for ch in chains}
        lmats = [jnp.where(tri_strict, a[ch]["beta"] * kk[ch] * a[ch]["decay"], 0.0) for ch in chains]
        pinv = _unit_lower_inverses(lmats, eye_f, inv_masks)
        sol = {ch: _dot(p, a[ch]["rhs"]) for ch, p in zip(chains, pinv)}
        first = 0
        for rnd in rounds:
            idx = range(first, first + len(rnd))
            first += len(rnd)
            rch = [(i, h) for i in idx for h in heads]
            s = {ch: sout_ref[items[ch[0]][0], ch[1]] for ch in rch}
            ws_qs = {ch: _dot(jnp.concatenate([sol[ch][:, DN_DV:].astype(BF16), a[ch]["q_ecs"]], axis=0), s[ch])
                     for ch in rch}
            v_new = {ch: (sol[ch][:, :DN_DV] - ws_qs[ch][:c]).astype(BF16) for ch in rch}
            o = {ch: ws_qs[ch][c:] + _dot(qk[ch], v_new[ch]) for ch in rch}
            for ch in rch:
                sout_ref[items[ch[0]][0], ch[1]] = a[ch]["elast"] * s[ch] + _dot_tn(a[ch]["k_tail"], v_new[ch])
            for i in idx:
                bb, r0 = items[i]
                z = read_z(bb, r0)
                put_o(bb, r0, jnp.concatenate(
                    [_rms(o[(i, h)], nw_ref[...]) * _silu(z[:, h * DN_DV:(h + 1) * DN_DV]) for h in heads], axis=1))

    _run_rounds(process, bt, r, c, cpi)
    flush_o()

    @pl.when(j == ng - 1)
    def _():
        cout_ref[...] = ext_ref[:, CARRY_ROW:SUBLANES, :]


def _small_row(vals, offset, fill=0.0):
    row = jnp.full((1, LANES), fill, F32)
    return lax.dynamic_update_slice(row, vals.astype(F32).reshape(1, -1), (0, offset))


def _dn_branch(proj, small, conv_w, a_log, dt_bias, norm_w, conv_in, rec_in, *, bt, r, c, lc, ng, cpi):
    nb = conv_in.shape[0] // bt
    rows = bt * r
    t = proj.shape[0]
    kern = functools.partial(_dn_kernel, bt=bt, r=r, c=c, lc=lc, ng=ng, cpi=cpi)
    rowmap = lambda col: (lambda i, j: (i * ng + j, col))
    const = lambda i, j: (0, 0)
    return pl.pallas_call(
        kern,
        grid=(nb, ng),
        in_specs=[pl.BlockSpec((rows, DN_CONV_DIM), rowmap(COL_DN_QKV // DN_CONV_DIM)),
                  pl.BlockSpec((rows, DN_VAL), rowmap(COL_DN_Z // DN_VAL)),
                  pl.BlockSpec((rows, LANES), rowmap(0)),
                  pl.BlockSpec((CONV_K, DN_CONV_DIM), const),
                  pl.BlockSpec((1, LANES), const),
                  pl.BlockSpec((1, LANES), const),
                  pl.BlockSpec((1, DN_DV), const),
                  pl.BlockSpec((bt, CONV_K - 1, DN_CONV_DIM), lambda i, j: (i, 0, 0)),
                  pl.BlockSpec((bt, DN_HEADS, DN_DK, DN_DV), lambda i, j: (i, 0, 0, 0))],
        out_specs=[pl.BlockSpec((rows, DN_VAL), lambda i, j: (i * ng + j, 0)),
                   pl.BlockSpec((bt, CONV_K - 1, DN_CONV_DIM), lambda i, j: (i, 0, 0)),
                   pl.BlockSpec((bt, DN_HEADS, DN_DK, DN_DV), lambda i, j: (i, 0, 0, 0))],
        out_shape=[jax.ShapeDtypeStruct((t, DN_VAL), BF16),
                   jax.ShapeDtypeStruct(conv_in.shape, F32),
                   jax.ShapeDtypeStruct(rec_in.shape, F32)],
        scratch_shapes=[pltpu.VMEM((bt, c + SUBLANES, DN_CONV_DIM), F32)],
        compiler_params=_cparams("arbitrary", "arbitrary"),
        name="gated_delta",
    )(proj, proj, small, conv_w, _small_row(a_log, SM_A), _small_row(dt_bias, SM_A),
      norm_w.reshape(1, DN_DV), conv_in, rec_in)


def _ssd_kernel(xbc_ref, z_ref, sm_ref, cw_ref, cb_ref, alog_ref, bias_ref, dvec_ref, nw_ref, cin_ref,
                hin_ref, y_ref, cout_ref, hout_ref, ext_ref, *, bt, r, c, lc, ng, cpi):
    j = pl.program_id(1)

    @pl.when(j == 0)
    def _():
        ext_ref[:, CARRY_ROW:SUBLANES, :] = cin_ref[...]
        hout_ref[...] = hin_ref[...]

    tril_bf = (_iota((c, c), 0) >= _iota((c, c), 1)).astype(BF16)
    row2 = _iota((c, 2 * c), 0)
    lane2 = _iota((c, 2 * c), 1)
    left2 = lane2 < c
    tri2 = row2 >= jnp.where(left2, lane2, lane2 - c)
    left_x = _iota((c, LANES), 1) < SSM_HEADDIM
    top_h = _iota((2 * SSM_HEADDIM, 1), 0) < SSM_HEADDIM
    valid = _iota((c, 1), 0) < lc
    nexp_a = -jnp.exp(alog_ref[...])
    bias = bias_ref[...]
    gw = SSM_INNER // SSM_GROUPS
    pairs_per_group = SSM_HEADS // SSM_GROUPS // 2

    pairs = range(SSM_HEADS // 2)
    pair_rows = lambda p: slice(p * LANES, (p + 1) * LANES)
    read_z, read_sm = (_chunk_reader(ref, r, c) for ref in (z_ref, sm_ref))
    put_y, flush_y = _chunk_writer(y_ref, bt, r, c)

    def prep(bb, r0, y):
        dt_all = _softplus(read_sm(bb, r0) + bias)
        if lc < c:
            dt_all = jnp.where(valid, dt_all, 0.0)
        cs_all = _dot_exact_lhs(tril_bf, dt_all * nexp_a)
        cs_t2 = jnp.concatenate([cs_all, cs_all], axis=0).T
        last = cs_all[c - 1:c, :]
        bm = [y[:, SSM_INNER + g * SSM_STATE:SSM_INNER + (g + 1) * SSM_STATE].astype(BF16)
              for g in range(SSM_GROUPS)]
        cm = [y[:, SSM_INNER + (SSM_GROUPS + g) * SSM_STATE:SSM_INNER + (SSM_GROUPS + g + 1) * SSM_STATE].astype(BF16)
              for g in range(SSM_GROUPS)]
        return dict(y=y, dt=dt_all, cs=cs_all, cs_t2=cs_t2, ecs=jnp.exp(cs_all),
                    tail=jnp.exp(last - cs_all), elast=jnp.exp(last), bm=bm, cm=cm)

    def process(rounds):
        items = [it for rnd in rounds for it in rnd]
        conv = _conv_rounds(xbc_ref, rounds, ext_ref, cw_ref, r, c, lc, bias=cb_ref[...])
        pre = [prep(bb, r0, y) for (bb, r0), y in zip(items, conv)]
        ids = range(len(items))
        cb2 = {(i, g): _dot_nt(pre[i]["cm"][g], jnp.concatenate([pre[i]["bm"][g]] * 2, axis=0))
               for i in ids for g in range(SSM_GROUPS)}
        y_intra, upd, xs_of = {}, {}, {}
        for i in ids:
            a = pre[i]
            for p in pairs:
                g = p // pairs_per_group
                la, lb = SM_DT + 2 * p, SM_DT + 2 * p + 1
                both = lambda v, mask=left_x: jnp.where(mask, v[:, la:la + 1], v[:, lb:lb + 1])
                xs = a["y"][:, pair_rows(p)]
                xdt = xs * both(a["dt"])
                diff = both(a["cs"], left2) - jnp.where(left2[0:1], a["cs_t2"][la:la + 1, :], a["cs_t2"][lb:lb + 1, :])
                decay2 = jnp.exp(jnp.where(tri2, diff, -jnp.inf))
                rhs = jnp.concatenate([jnp.where(left_x, xdt, 0.0), jnp.where(left_x, 0.0, xdt)], axis=0)
                y_intra[i, p] = _dot(cb2[i, g] * decay2, rhs)
                upd[i, p] = _dot_tn(xdt * both(a["tail"]), a["bm"][g])
                xs_of[i, p] = xs
        first = 0
        for rnd in rounds:
            idx = range(first, first + len(rnd))
            first += len(rnd)
            for i in idx:
                a = pre[i]
                bb, r0 = items[i]
                outs = []
                for p in pairs:
                    g = p // pairs_per_group
                    la, lb = SM_DT + 2 * p, SM_DT + 2 * p + 1
                    hs = hout_ref[bb, pair_rows(p), :]
                    y_inter = _dot_nt(a["cm"][g], hs) * jnp.where(left_x, a["ecs"][:, la:la + 1], a["ecs"][:, lb:lb + 1])
                    hout_ref[bb, pair_rows(p), :] = (
                        jnp.where(top_h, a["elast"][:, la:la + 1], a["elast"][:, lb:lb + 1]) * hs + upd[i, p])
                    outs.append(y_intra[i, p] + y_inter + dvec_ref[:, pair_rows(p)] * xs_of[i, p])
                yz = jnp.concatenate(outs, axis=1) * _silu(read_z(bb, r0))
                put_y(bb, r0, jnp.concatenate(
                    [_rms(yz[:, g * gw:(g + 1) * gw], nw_ref[:, g * gw:(g + 1) * gw]) for g in range(SSM_GROUPS)],
                    axis=1))

    _run_rounds(process, bt, r, c, cpi)
    flush_y()

    @pl.when(j == ng - 1)
    def _():
        cout_ref[...] = ext_ref[:, CARRY_ROW:SUBLANES, :]


def _ssd_branch(proj, small, conv_w, conv_b, a_log, dt_bias, d_skip, norm_w, conv_in, rec_in, *, bt, r, c, lc, ng, cpi):
    nb = conv_in.shape[0] // bt
    rows = bt * r
    t = proj.shape[0]
    hrows = SSM_HEADS * SSM_HEADDIM
    rec2 = rec_in.reshape(rec_in.shape[0], hrows, SSM_STATE)
    kern = functools.partial(_ssd_kernel, bt=bt, r=r, c=c, lc=lc, ng=ng, cpi=cpi)
    rowmap = lambda col: (lambda i, j: (i * ng + j, col))
    const = lambda i, j: (0, 0)
    y, conv_out, rec_out = pl.pallas_call(
        kern,
        grid=(nb, ng),
        in_specs=[pl.BlockSpec((rows, SSM_CONV_DIM), rowmap(COL_SSM_XBC // SSM_CONV_DIM)),
                  pl.BlockSpec((rows, SSM_INNER), rowmap(COL_SSM_Z // SSM_INNER)),
                  pl.BlockSpec((rows, LANES), rowmap(0)),
                  pl.BlockSpec((CONV_K, SSM_CONV_DIM), const),
                  pl.BlockSpec((1, SSM_CONV_DIM), const),
                  pl.BlockSpec((1, LANES), const),
                  pl.BlockSpec((1, LANES), const),
                  pl.BlockSpec((1, SSM_INNER), const),
                  pl.BlockSpec((1, SSM_INNER), const),
                  pl.BlockSpec((bt, CONV_K - 1, SSM_CONV_DIM), lambda i, j: (i, 0, 0)),
                  pl.BlockSpec((bt, hrows, SSM_STATE), lambda i, j: (i, 0, 0))],
        out_specs=[pl.BlockSpec((rows, SSM_INNER), lambda i, j: (i * ng + j, 0)),
                   pl.BlockSpec((bt, CONV_K - 1, SSM_CONV_DIM), lambda i, j: (i, 0, 0)),
                   pl.BlockSpec((bt, hrows, SSM_STATE), lambda i, j: (i, 0, 0))],
        out_shape=[jax.ShapeDtypeStruct((t, SSM_INNER), BF16),
                   jax.ShapeDtypeStruct(conv_in.shape, F32),
                   jax.ShapeDtypeStruct(rec2.shape, F32)],
        scratch_shapes=[pltpu.VMEM((bt, c + SUBLANES, SSM_CONV_DIM), F32)],
        compiler_params=_cparams("arbitrary", "arbitrary"),
        name="ssd_scan",
    )(proj, proj, small, conv_w, conv_b.reshape(1, -1), _small_row(a_log, SM_DT), _small_row(dt_bias, SM_DT),
      jnp.repeat(d_skip.astype(F32), SSM_HEADDIM).reshape(1, SSM_INNER), norm_w.reshape(1, SSM_INNER),
      conv_in, rec2)
    return y, conv_out, rec_out.reshape(rec_in.shape)


PLAN_E0, PLAN_E1, PLAN_C0, PLAN_C1 = 0, 1, 2, 3


def _route(logits):
    lane = _iota(logits.shape, 1).astype(F32)
    big = float(LANES)
    is_group = (lane >= RT_GROUP) & (lane < RT_GROUP + MOE_GROUPS)
    gl = jnp.where(is_group, logits, -jnp.inf)
    gmax = jnp.max(gl, axis=-1, keepdims=True)
    g_sel = jnp.min(jnp.where(gl == gmax, lane, big), axis=-1, keepdims=True) - RT_GROUP
    p_group = 1.0 / jnp.sum(jnp.exp(gl - gmax), axis=-1, keepdims=True)
    e_lo = RT_EXPERT + MOE_PER_GROUP * g_sel
    in_grp = (lane >= e_lo) & (lane < e_lo + MOE_PER_GROUP)
    el = jnp.where(in_grp, logits, -jnp.inf)
    ee = jnp.exp(el - jnp.max(el, axis=-1, keepdims=True))
    pe = jnp.where(in_grp, ee / jnp.sum(ee, axis=-1, keepdims=True), -1.0)
    p1 = jnp.max(pe, axis=-1, keepdims=True)
    i1 = jnp.min(jnp.where(pe == p1, lane, big), axis=-1, keepdims=True)
    pe2 = jnp.where(lane == i1, -1.0, pe)
    p2 = jnp.max(pe2, axis=-1, keepdims=True)
    i2 = jnp.min(jnp.where(pe2 == p2, lane, big), axis=-1, keepdims=True)
    tot = p1 + p2
    plan = jnp.where(lane == PLAN_E0, i1 - RT_EXPERT, 0.0) + jnp.where(lane == PLAN_E1, i2 - RT_EXPERT, 0.0)
    return plan + jnp.where(lane == PLAN_C0, p_group * p1 / tot, 0.0) + jnp.where(lane == PLAN_C1, p_group * p2 / tot, 0.0)


def _tile_ranks(plan):
    tp = plan.shape[0]
    lane = _iota(plan.shape, 1).astype(F32)
    sel0 = lane == plan[:, PLAN_E0:PLAN_E0 + 1]
    sel1 = lane == plan[:, PLAN_E1:PLAN_E1 + 1]
    sel = (sel0 | sel1).astype(BF16)
    before = (_iota((tp, tp), 0) > _iota((tp, tp), 1)).astype(BF16)
    excl = jnp.dot(before, sel, preferred_element_type=F32)
    r0 = jnp.sum(jnp.where(sel0, excl, 0.0), axis=-1, keepdims=True)
    r1 = jnp.sum(jnp.where(sel1, excl, 0.0), axis=-1, keepdims=True)
    rank = jnp.where(lane == PLAN_E0, r0, 0.0) + jnp.where(lane == PLAN_E1, r1, 0.0)
    return rank, jnp.sum(sel.astype(F32), axis=0, keepdims=True)


def _merge_kernel(x_ref, og_ref, ys_ref, gates_ref, gt1_ref, sc2_ref, sh2_ref, nw2_ref, wdn_ref, wssm_ref,
                  wout_ref, wr_ref, br_ref, x1_ref, h2_ref, plan_ref, rank_ref, cnt_ref):
    y_dn = _dot(og_ref[...], wdn_ref[...])
    y_ssm = _dot(ys_ref[...], wssm_ref[...])
    merged = (_sigmoid(gates_ref[:, :D_MODEL].astype(F32)) * y_dn
              + _sigmoid(gates_ref[:, D_MODEL:].astype(F32)) * y_ssm)
    x1 = x_ref[...] + gt1_ref[...] * _dot(merged, wout_ref[...])
    x1_ref[...] = x1
    h2 = _rms(x1, nw2_ref[...]) * (1.0 + sc2_ref[...]) + sh2_ref[...]
    h2_ref[...] = h2.astype(BF16)
    plan = _route(_dot_x3(h2, wr_ref[...]) + br_ref[...])
    plan_ref[...] = plan
    rank_ref[...], cnt_ref[...] = _tile_ranks(plan)


def _merge(x3, og, ys, proj, gt1, sc2, sh2, nw2, wdn, wssm, wout, wr, br, tm):
    bx, lx, d = x3.shape
    assert tm == TOK_TILE
    nl = lx // tm
    rowmap = lambda col: (lambda b, i: (b * nl + i, col))
    const = lambda b, i: (0, 0)
    tok = lambda w, dt: jax.ShapeDtypeStruct((bx, lx, w), dt)
    return pl.pallas_call(
        _merge_kernel,
        grid=(bx, nl),
        in_specs=[pl.BlockSpec((None, tm, d), lambda b, i: (b, i, 0)),
                  pl.BlockSpec((tm, DN_VAL), rowmap(0)),
                  pl.BlockSpec((tm, SSM_INNER), rowmap(0)),
                  pl.BlockSpec((tm, 2 * d), rowmap(COL_GATES // (2 * d))),
                  _mod_spec(gt1, tm), _mod_spec(sc2, tm), _mod_spec(sh2, tm),
                  pl.BlockSpec((1, d), const),
                  pl.BlockSpec(wdn.shape, const), pl.BlockSpec(wssm.shape, const), pl.BlockSpec(wout.shape, const),
                  pl.BlockSpec(wr.shape, const), pl.BlockSpec((1, LANES), const)],
        out_specs=[pl.BlockSpec((None, tm, d), lambda b, i: (b, i, 0)),
                   pl.BlockSpec((None, tm, d), lambda b, i: (b, i, 0)),
                   pl.BlockSpec((None, tm, LANES), lambda b, i: (b, i, 0)),
                   pl.BlockSpec((None, tm, LANES), lambda b, i: (b, i, 0)),
                   pl.BlockSpec((None, 1, LANES), lambda b, i: (b * nl + i, 0, 0))],
        out_shape=[tok(d, F32), tok(d, BF16), tok(LANES, F32), tok(LANES, F32),
                   jax.ShapeDtypeStruct((bx * nl, 1, LANES), F32)],
        compiler_params=_cparams("arbitrary", "arbitrary"),
        name="merge_route",
    )(x3, og, ys, proj, gt1, sc2, sh2, nw2.reshape(1, d), wdn, wssm, wout, wr, br)


FFN_ROWS = 512
TOK_TILE = 512
RUN_ALIGN = 2 * SUBLANES
RUN_BIG = 4 * RUN_ALIGN
RUN_SMALL = (2 * RUN_ALIGN, RUN_ALIGN)
LOCAL_ROWS = 2 * TOK_TILE + MOE_EXPERTS * RUN_ALIGN


def _local_slots(plan, rank, off_row):
    lane = _iota(plan.shape, 1).astype(F32)
    slots = []
    for k in (PLAN_E0, PLAN_E1):
        off = jnp.sum(jnp.where(lane == plan[:, k:k + 1], off_row, 0.0), axis=-1, keepdims=True)
        slots.append(off + rank[:, k:k + 1])
    return slots


def _run_blocks(p8_ref, loff_ref, base_ref, act):
    for e in range(MOE_EXPERTS):
        n = p8_ref[0, e]
        lo = loff_ref[0, e]
        go = base_ref[0, e]

        def big(k, carry, lo=lo, go=go):
            off = k * RUN_BIG
            act(pl.multiple_of(lo + off, RUN_ALIGN), pl.multiple_of(go + off, RUN_ALIGN), RUN_BIG)
            return carry

        lax.fori_loop(0, lax.shift_right_logical(n, RUN_BIG.bit_length() - 1), big, 0)
        for size in RUN_SMALL:
            @pl.when(jnp.bitwise_and(n, size) != 0)
            def _(size=size, n=n, lo=lo, go=go):
                done = jnp.bitwise_and(n, ~(2 * size - 1))
                act(pl.multiple_of(lo + done, RUN_ALIGN), pl.multiple_of(go + done, RUN_ALIGN), size)


def _dispatch_kernel(tv_ref, p8_ref, loff_ref, base_ref, plan_ref, rank_ref, off_ref, h2_ref, *rest, first):
    xs_ref, buf_ref, zero_ref, sem, zsem = rest[-5:]

    @pl.when((pl.program_id(0) == 0) & first)
    def _():
        zero_ref[...] = jnp.zeros_like(zero_ref)

        def zero_copy(i):
            rows = pl.ds(pl.multiple_of(i * FFN_ROWS, FFN_ROWS), FFN_ROWS)
            return pltpu.make_async_copy(zero_ref, xs_ref.at[rows], zsem)

        def zero_tiles(act):
            def body(i, carry):
                @pl.when(tv_ref[i] < FFN_ROWS)
                def _():
                    act(zero_copy(i))
                return carry
            lax.fori_loop(0, xs_ref.shape[0] // FFN_ROWS, body, 0)

        zero_tiles(lambda cp: cp.start())
        zero_tiles(lambda cp: cp.wait())

    s0, s1 = _local_slots(plan_ref[...], rank_ref[...], off_ref[...])
    row = _iota((1, LOCAL_ROWS), 1).astype(F32)
    onehot_t = ((row == s0) | (row == s1)).astype(BF16)
    buf_ref[...] = lax.dot_general(onehot_t, h2_ref[...], (((0,), (0,)), ((), ())),
                                   preferred_element_type=F32).astype(BF16)

    def copy(lo, go, size):
        return pltpu.make_async_copy(buf_ref.at[pl.ds(lo, size)], xs_ref.at[pl.ds(go, size)], sem)

    _run_blocks(p8_ref, loff_ref, base_ref, lambda lo, go, size: copy(lo, go, size).start())
    _run_blocks(p8_ref, loff_ref, base_ref, lambda lo, go, size: copy(lo, go, size).wait())


def _tile_scalars():
    return pl.BlockSpec((None, 1, LANES), lambda i, *_: (i, 0, 0), memory_space=pltpu.SMEM)


def _dispatch(tile_valid, p8, loff, base, plan2, rank, off_f, h2, rows, xs_prev=None):
    t, d = h2.shape
    tt = TOK_TILE
    tok = lambda w: pl.BlockSpec((tt, w), lambda i, tv: (i, 0))
    operands = (tile_valid, p8, loff, base, plan2, rank, off_f, h2) + (() if xs_prev is None else (xs_prev,))
    return pl.pallas_call(
        functools.partial(_dispatch_kernel, first=xs_prev is None),
        grid_spec=pltpu.PrefetchScalarGridSpec(
            num_scalar_prefetch=1,
            grid=(t // tt,),
            in_specs=[_tile_scalars(), _tile_scalars(), _tile_scalars(),
                      tok(LANES), tok(LANES),
                      pl.BlockSpec((None, 1, LANES), lambda i, tv: (i, 0, 0)),
                      tok(d)] + ([] if xs_prev is None else [pl.BlockSpec(memory_space=pl.ANY)]),
            out_specs=pl.BlockSpec(memory_space=pl.ANY),
            scratch_shapes=[pltpu.VMEM((LOCAL_ROWS, d), BF16), pltpu.VMEM((FFN_ROWS, d), BF16),
                            pltpu.SemaphoreType.DMA(()), pltpu.SemaphoreType.DMA(())]),
        out_shape=jax.ShapeDtypeStruct((rows, d), BF16),
        input_output_aliases={} if xs_prev is None else {len(operands) - 1: 0},
        compiler_params=_cparams("arbitrary"),
        name="moe_dispatch",
    )(*operands)


def _ffn_kernel(te_ref, tv_ref, src_ref, xs_ref, wg_ref, wu_ref, wd_ref, o_ref):
    nv = tv_ref[pl.program_id(0)]

    half = FFN_ROWS // 2

    def swiglu(rows):
        x = xs_ref[rows, :]
        o_ref[rows, :] = _dot(_silu(_dot(x, wg_ref[...])) * _dot(x, wu_ref[...]), wd_ref[...]).astype(BF16)

    @pl.when(nv > half)
    def _():
        swiglu(slice(None))

    @pl.when((nv > 0) & (nv <= half))
    def _():
        swiglu(slice(0, half))
        o_ref[half:, :] = jnp.zeros((half, o_ref.shape[1]), o_ref.dtype)

    @pl.when(nv <= 0)
    def _():
        o_ref[...] = jnp.zeros_like(o_ref)


def _ffn(tile_expert, tile_valid, tile_src, xs, wg, wu, wd):
    rows, d = xs.shape
    wmap = lambda i, te, tv, src: (te[i], 0, 0)
    return pl.pallas_call(
        _ffn_kernel,
        grid_spec=pltpu.PrefetchScalarGridSpec(
            num_scalar_prefetch=3,
            grid=(rows // FFN_ROWS,),
            in_specs=[pl.BlockSpec((FFN_ROWS, d), lambda i, te, tv, src: (src[i], 0)),
                      pl.BlockSpec((None, d, MOE_FF), wmap),
                      pl.BlockSpec((None, d, MOE_FF), wmap),
                      pl.BlockSpec((None, MOE_FF, d), wmap)],
            out_specs=pl.BlockSpec((FFN_ROWS, d), lambda i, te, tv, src: (i, 0))),
        out_shape=jax.ShapeDtypeStruct((rows, d), BF16),
        compiler_params=_cparams("arbitrary"),
        name="moe_ffn",
    )(tile_expert, tile_valid, tile_src, xs, wg, wu, wd)


def _combine_kernel(p8_ref, loff_ref, base_ref, ys_ref, plan_ref, rank_ref, off_ref, x1_ref, gt2_ref, fsc_ref,
                    fsh_ref, fnw_ref, y_ref, buf_ref, sem):
    @pl.when((pl.program_id(0) == 0) & (pl.program_id(1) == 0))
    def _():
        buf_ref[...] = jnp.zeros_like(buf_ref)

    def copy(lo, go, size):
        return pltpu.make_async_copy(ys_ref.at[pl.ds(go, size)], buf_ref.at[pl.ds(lo, size)], sem)

    _run_blocks(p8_ref, loff_ref, base_ref, lambda lo, go, size: copy(lo, go, size).start())
    plan = plan_ref[...]
    s0, s1 = _local_slots(plan, rank_ref[...], off_ref[...])
    row = _iota((1, LOCAL_ROWS), 1).astype(F32)
    weights = (jnp.where(row == s0, plan[:, PLAN_C0:PLAN_C0 + 1], 0.0)
               + jnp.where(row == s1, plan[:, PLAN_C1:PLAN_C1 + 1], 0.0))
    _run_blocks(p8_ref, loff_ref, base_ref, lambda lo, go, size: copy(lo, go, size).wait())
    total = loff_ref[0, MOE_EXPERTS - 1] + p8_ref[0, MOE_EXPERTS - 1]
    filled = _iota((LOCAL_ROWS, 1), 0) < total
    moe = _dot(weights, jnp.where(filled, buf_ref[...], jnp.zeros((), BF16)))
    x2 = x1_ref[...] + gt2_ref[...] * moe
    y_ref[...] = _rms(x2, fnw_ref[...]) * (1.0 + fsc_ref[...]) + fsh_ref[...]


def _combine(p8, loff, base, ys, plan, rank3, off_f, x1, gt2, fsc, fsh, fnw):
    bx, lx, d = x1.shape
    tt = TOK_TILE
    nl = lx // tt
    tile = lambda b, i: b * nl + i
    scal = pl.BlockSpec((None, 1, LANES), lambda b, i: (tile(b, i), 0, 0), memory_space=pltpu.SMEM)
    tokspec = lambda w: pl.BlockSpec((None, tt, w), lambda b, i: (b, i, 0))
    return pl.pallas_call(
        _combine_kernel,
        grid=(bx, nl),
        in_specs=[scal, scal, scal,
                  pl.BlockSpec(memory_space=pl.ANY),
                  tokspec(LANES), tokspec(LANES),
                  pl.BlockSpec((None, 1, LANES), lambda b, i: (tile(b, i), 0, 0)),
                  tokspec(d), _mod_spec(gt2, tt), _mod_spec(fsc, tt), _mod_spec(fsh, tt),
                  pl.BlockSpec((1, d), lambda b, i: (0, 0))],
        out_specs=tokspec(d),
        out_shape=jax.ShapeDtypeStruct((bx, lx, d), F32),
        scratch_shapes=[pltpu.VMEM((LOCAL_ROWS, d), BF16), pltpu.SemaphoreType.DMA(())],
        compiler_params=_cparams("arbitrary", "arbitrary"),
        name="moe_combine_final",
    )(p8, loff, base, ys, plan, rank3, off_f, x1, gt2, fsc, fsh, fnw.reshape(1, d))


def _moe(groups, wg, wu, wd, fnw):
    d = groups[0]["x1"].shape[-1]
    ntiles = [g["x1"].shape[0] * g["x1"].shape[1] // TOK_TILE for g in groups]
    t = sum(ntiles) * TOK_TILE
    n = jnp.concatenate([g["cnt"][:, 0, :MOE_EXPERTS] for g in groups], axis=0).astype(jnp.int32)
    p8 = (n + RUN_ALIGN - 1) // RUN_ALIGN * RUN_ALIGN
    loff = jnp.cumsum(p8, axis=1) - p8
    erows = jnp.sum(p8, axis=0)
    epad = (erows + FFN_ROWS - 1) // FFN_ROWS * FFN_ROWS
    ends = jnp.cumsum(epad)
    starts = ends - epad
    base = starts[None, :] + jnp.cumsum(p8, axis=0) - p8
    lanes = lambda a: jnp.pad(a, ((0, 0), (0, LANES - MOE_EXPERTS)))[:, None, :]
    rows = (-(-(2 * t + (t // TOK_TILE) * MOE_EXPERTS * (RUN_ALIGN - 1)) // FFN_ROWS) + MOE_EXPERTS) * FFN_ROWS
    tile_start = jnp.arange(rows // FFN_ROWS, dtype=jnp.int32) * FFN_ROWS
    tile_expert = jnp.minimum(jnp.sum(tile_start[:, None] >= ends[None, :], axis=1), MOE_EXPERTS - 1).astype(jnp.int32)
    hot = tile_expert[:, None] == jnp.arange(MOE_EXPERTS)[None, :]
    valid_rows = lambda er: jnp.clip(jnp.sum(jnp.where(hot, er - (tile_start[:, None] - starts), 0), axis=1),
                                     0, FFN_ROWS).astype(jnp.int32)
    tile_valid = valid_rows(erows)
    p8l, loffl, basel = lanes(p8), lanes(loff), lanes(base)
    off_f = loffl.astype(F32)
    spans, first = [], 0
    for nt in ntiles:
        spans.append(slice(first, first + nt))
        first += nt
    first_valid = valid_rows(jnp.sum(p8[spans[0]], axis=0))
    xs = None
    for g, sp in zip(groups, spans):
        tg = g["h2"].shape[0] * g["h2"].shape[1]
        xs = _dispatch(first_valid, p8l[sp], loffl[sp], basel[sp], g["plan"].reshape(tg, LANES),
                       g["rank"].reshape(tg, LANES), off_f[sp], g["h2"].reshape(tg, d), rows, xs_prev=xs)
    tile_id = tile_start // FFN_ROWS
    tile_src = jnp.where(tile_valid > 0, tile_id, jnp.max(jnp.where(tile_valid > 0, tile_id, 0))).astype(jnp.int32)
    ys = _ffn(tile_expert, tile_valid, tile_src, xs, wg, wu, wd)
    return [_combine(p8l[sp], loffl[sp], basel[sp], ys, g["plan"], g["rank"], off_f[sp], g["x1"], g["gt2"],
                     g["fsc"], g["fsh"], fnw) for g, sp in zip(groups, spans)]


def _prep_layer(lp):
    w_in = lp["w_in"]
    offs = [0]
    for s in (DN_CONV_DIM, DN_VAL, DN_HEADS, DN_HEADS, SSM_CONV_DIM, SSM_INNER, SSM_HEADS, D_MODEL, D_MODEL):
        offs.append(offs[-1] + s)
    seg = lambda i: w_in[:, offs[i]:offs[i + 1]]
    small = jnp.concatenate([seg(2), seg(3), seg(6)], axis=1)
    pad = jnp.zeros((D_MODEL, PROJ_N - COL_SMALL - small.shape[1]), F32)
    w_cat = jnp.concatenate([seg(0), seg(4), seg(5), seg(7), seg(8), seg(1), small, pad], axis=1).astype(BF16)
    wr = jnp.concatenate([lp["w_group_router"], lp["w_expert_router"],
                          jnp.zeros((D_MODEL, LANES - MOE_GROUPS - MOE_EXPERTS), F32)], axis=1)
    br = jnp.concatenate([lp["b_group_router"], lp["b_expert_router"],
                          jnp.zeros((LANES - MOE_GROUPS - MOE_EXPERTS,), F32)]).reshape(1, LANES)
    return dict(lp, w_cat=w_cat, wr=wr, br=br,
                wdn=lp["w_dn_out"].astype(BF16), wssm=lp["w_ssm_out"].astype(BF16), wout=lp["w_out"].astype(BF16),
                wg=lp["w_exp_gate"], wu=lp["w_exp_up"], wd=lp["w_exp_down"])


def _mixer(x3, mods, fins, states, lp, cfg):
    bx, lx, d = x3.shape
    scan = dict(bt=cfg["bt"], r=cfg["r"], c=cfg["c"], lc=cfg["lc"], ng=cfg["ng"])
    sh1, sc1, gt1, sh2, sc2, gt2 = mods
    dn_conv, dn_rec, ssm_conv, ssm_rec = states
    proj, small = _inproj(x3, sc1, sh1, lp["norm_mix_w"], lp["w_cat"], cfg["tm"])
    proj, small = proj.reshape(bx * lx, PROJ_N), small.reshape(bx * lx, LANES)
    og, dn_conv_new, dn_rec_new = _dn_branch(proj, small, lp["dn_conv_w"], lp["dn_A_log"], lp["dn_dt_bias"],
                                             lp["dn_norm_w"], dn_conv, dn_rec, cpi=cfg["cpi_dn"], **scan)
    ys, ssm_conv_new, ssm_rec_new = _ssd_branch(proj, small, lp["ssm_conv_w"], lp["ssm_conv_b"], lp["ssm_A_log"],
                                                lp["ssm_dt_bias"], lp["ssm_D"], lp["ssm_norm_w"],
                                                ssm_conv, ssm_rec, cpi=cfg["cpi_ssd"], **scan)
    x1, h2, plan, rank, cnt = _merge(x3, og, ys, proj, gt1, sc2, sh2, lp["norm_ffn_w"], lp["wdn"], lp["wssm"],
                                     lp["wout"], lp["wr"], lp["br"], cfg["tm_merge"])
    group = dict(x1=x1, h2=h2, plan=plan, rank=rank, cnt=cnt, gt2=gt2, fsh=fins[0], fsc=fins[1])
    return group, (dn_conv_new, dn_rec_new, ssm_conv_new, ssm_rec_new)


def _per_seq(m):
    return m[:, None, :]


def kernel(x_prompt, x_sample, c_prompt, c_sample, state_dn_conv, state_dn_rec, state_ssm_conv, state_ssm_rec, w_ada, b_ada, norm_mix_w, w_in, dn_conv_w, dn_A_log, dn_dt_bias, dn_norm_w, w_dn_out, ssm_conv_w, ssm_conv_b, ssm_A_log, ssm_dt_bias, ssm_D, ssm_norm_w, w_ssm_out, w_out, norm_ffn_w, w_group_router, b_group_router, w_expert_router, b_expert_router, w_exp_gate, w_exp_up, w_exp_down, w_ada_final, b_ada_final, final_norm_w):
    depth = w_ada.shape[0]
    assert depth == 1
    per_layer = dict(w_ada=w_ada, b_ada=b_ada, norm_mix_w=norm_mix_w, w_in=w_in, dn_conv_w=dn_conv_w,
                     dn_A_log=dn_A_log, dn_dt_bias=dn_dt_bias, dn_norm_w=dn_norm_w, w_dn_out=w_dn_out,
                     ssm_conv_w=ssm_conv_w, ssm_conv_b=ssm_conv_b, ssm_A_log=ssm_A_log, ssm_dt_bias=ssm_dt_bias,
                     ssm_D=ssm_D, ssm_norm_w=ssm_norm_w, w_ssm_out=w_ssm_out, w_out=w_out, norm_ffn_w=norm_ffn_w,
                     w_group_router=w_group_router, b_group_router=b_group_router,
                     w_expert_router=w_expert_router, b_expert_router=b_expert_router,
                     w_exp_gate=w_exp_gate, w_exp_up=w_exp_up, w_exp_down=w_exp_down)
    layers = [_prep_layer({k: v[l] for k, v in per_layer.items()}) for l in range(depth)]

    nbp, lp_, d = x_prompt.shape
    nbs, ls, _ = x_sample.shape
    c_all = jnp.concatenate([c_prompt, c_sample], axis=0)
    mod_all = [_ada(c_all, lyr["w_ada"], lyr["b_ada"]) for lyr in layers]
    fin_all = _ada(c_all, w_ada_final, b_ada_final)

    mods_p = [[_per_seq(m) for m in jnp.split(ma[:nbp], 6, axis=-1)] for ma in mod_all]
    fins_p = [_per_seq(m) for m in jnp.split(fin_all[:nbp], 2, axis=-1)]
    zeros_p = [(jnp.zeros((nbp, CONV_K - 1, DN_CONV_DIM), F32), jnp.zeros((nbp, DN_HEADS, DN_DK, DN_DV), F32),
                jnp.zeros((nbp, CONV_K - 1, SSM_CONV_DIM), F32),
                jnp.zeros((nbp, SSM_HEADS, SSM_HEADDIM, SSM_STATE), F32)) for _ in range(depth)]
    c_p = min(SCAN_CHUNK, lp_)
    r_p = min(lp_, 16 * c_p)
    cfg_p = dict(tm=min(lp_, 2048), tm_merge=min(lp_, TOK_TILE), bt=1, r=r_p, c=c_p, lc=c_p, ng=lp_ // r_p,
                 cpi_dn=8 if (r_p // c_p) % 8 == 0 else 1, cpi_ssd=4 if (r_p // c_p) % 4 == 0 else 1)
    grp_p, st_p = _mixer(x_prompt, mods_p[0], fins_p, zeros_p[0], layers[0], cfg_p)

    lpad = -(-ls // SUBLANES) * SUBLANES
    bt_s = 16
    xs = jnp.pad(x_sample, ((0, 0), (0, lpad - ls), (0, 0))).reshape(1, nbs * lpad, d)
    per_tok = lambda m: jnp.repeat(m, lpad, axis=0)[None]
    mods_s = [[per_tok(m) for m in jnp.split(ma[nbp:], 6, axis=-1)] for ma in mod_all]
    fins_s = [per_tok(m) for m in jnp.split(fin_all[nbp:], 2, axis=-1)]
    st_in = [(state_dn_conv[l], state_dn_rec[l], state_ssm_conv[l], state_ssm_rec[l]) for l in range(depth)]
    ts = nbs * lpad
    cfg_s = dict(tm=ts, tm_merge=min(ts, TOK_TILE), bt=bt_s, r=lpad, c=lpad, lc=ls, ng=1, cpi_dn=1, cpi_ssd=1)
    grp_s, st_s = _mixer(xs, mods_s[0], fins_s, st_in[0], layers[0], cfg_s)

    y_p, y_s = _moe([grp_p, grp_s], layers[0]["wg"], layers[0]["wu"], layers[0]["wd"], final_norm_w)
    y_s = y_s.reshape(nbs, lpad, d)[:, :ls]
    return (y_p, y_s) + tuple(st[None] for st in st_p) + tuple(st[None] for st in st_s)
```

```python
import functools
import math

import jax
import jax.numpy as jnp
from jax import lax
from jax.experimental import pallas as pl
from jax.experimental.pallas import tpu as pltpu

F32 = jnp.float32
BF16 = jnp.bfloat16

D_MODEL = 1024
DN_HEADS = 4
DN_DK = 128
DN_DV = 128
DN_KEY = DN_HEADS * DN_DK
DN_VAL = DN_HEADS * DN_DV
CONV_K = 4
DN_CONV_DIM = 2 * DN_KEY + DN_VAL
SSM_INNER = D_MODEL
SSM_HEADDIM = 64
SSM_HEADS = SSM_INNER // SSM_HEADDIM
SSM_GROUPS = 2
SSM_STATE = 128
SSM_CONV_DIM = SSM_INNER + 2 * SSM_GROUPS * SSM_STATE
MOE_GROUPS = 4
MOE_PER_GROUP = 8
MOE_EXPERTS = MOE_GROUPS * MOE_PER_GROUP
MOE_FF = D_MODEL // 4
EPS = 1e-6
SCAN_CHUNK = 64

LANES = 128
SUBLANES = 8
CARRY_ROW = SUBLANES - (CONV_K - 1)

COL_DN_QKV = 0
COL_SSM_XBC = COL_DN_QKV + DN_CONV_DIM
COL_SSM_Z = COL_SSM_XBC + SSM_CONV_DIM
COL_GATES = COL_SSM_Z + SSM_INNER
COL_DN_Z = COL_GATES + 2 * D_MODEL
COL_SMALL = COL_DN_Z + DN_VAL
PROJ_TN = 1792
PROJ_N = 4 * PROJ_TN
SM_A, SM_B, SM_DT = 0, DN_HEADS, 2 * DN_HEADS
RT_GROUP, RT_EXPERT = 0, MOE_GROUPS

VMEM_LIMIT = 56 * 1024 * 1024


def _cparams(*sem):
    return pltpu.CompilerParams(dimension_semantics=sem, vmem_limit_bytes=VMEM_LIMIT)


def _dot(a, b):
    return jnp.dot(a.astype(BF16), b.astype(BF16), preferred_element_type=F32)


def _dot_nt(a, b):
    return lax.dot_general(a.astype(BF16), b.astype(BF16), (((1,), (1,)), ((), ())),
                           preferred_element_type=F32)


def _dot_tn(a, b):
    return lax.dot_general(a.astype(BF16), b.astype(BF16), (((0,), (0,)), ((), ())),
                           preferred_element_type=F32)


def _split3(x):
    hi = x.astype(BF16)
    r = x - hi.astype(F32)
    mid = r.astype(BF16)
    lo = (r - mid.astype(F32)).astype(BF16)
    return hi, mid, lo


def _dot_exact_lhs(a_bf, b):
    hi, mid, lo = _split3(b)
    d = functools.partial(jnp.dot, preferred_element_type=F32)
    return d(a_bf, hi) + (d(a_bf, mid) + d(a_bf, lo))


def _dot_x3(a, b):
    a_hi = a.astype(BF16)
    a_lo = (a - a_hi.astype(F32)).astype(BF16)
    b_hi = b.astype(BF16)
    b_lo = (b - b_hi.astype(F32)).astype(BF16)
    d = functools.partial(jnp.dot, preferred_element_type=F32)
    n = b.shape[1]
    hi_both = d(a_hi, jnp.concatenate([b_hi, b_lo], axis=1))
    return hi_both[:, :n] + (hi_both[:, n:] + d(a_lo, b_hi))


def _sigmoid(x):
    return 1.0 / (1.0 + jnp.exp(-x))


def _silu(x):
    return x * _sigmoid(x)


def _softplus(x):
    return jnp.maximum(x, 0.0) + jnp.log1p(jnp.exp(-jnp.abs(x)))


def _rms(x, w):
    return x * lax.rsqrt(jnp.mean(x * x, axis=-1, keepdims=True) + EPS) * w


def _iota(shape, dim):
    return lax.broadcasted_iota(jnp.int32, shape, dim)


def _ada_kernel(c_ref, w_ref, b_ref, o_ref):
    o_ref[...] = _dot(_silu(c_ref[...]), w_ref[...]) + b_ref[...]


def _ada(c, w, b, tn=512):
    m, d = c.shape
    n = w.shape[1]
    return pl.pallas_call(
        _ada_kernel,
        grid=(n // tn,),
        in_specs=[pl.BlockSpec((m, d), lambda j: (0, 0)),
                  pl.BlockSpec((d, tn), lambda j: (0, j)),
                  pl.BlockSpec((1, tn), lambda j: (0, j))],
        out_specs=pl.BlockSpec((m, tn), lambda j: (0, j)),
        out_shape=jax.ShapeDtypeStruct((m, n), F32),
        compiler_params=_cparams("arbitrary"),
        name="ada_mod",
    )(c, w, b.reshape(1, n))


def _mod_spec(mod, tm):
    if mod.shape[1] == 1:
        return pl.BlockSpec((None, 1, D_MODEL), lambda b, i, *_: (b, 0, 0))
    return pl.BlockSpec((None, tm, D_MODEL), lambda b, i, *_: (b, i, 0))


def _inproj_kernel(x_ref, sc_ref, sh_ref, nw_ref, w_ref, o_ref, sm_ref, h_ref, *, tm, sub):
    @pl.when(pl.program_id(2) == 0)
    def _():
        per_token = sc_ref.shape[0] != 1

        def body(r, carry):
            rows = pl.ds(pl.multiple_of(r * sub, sub), sub)
            sc = sc_ref[rows, :] if per_token else sc_ref[...]
            sh = sh_ref[rows, :] if per_token else sh_ref[...]
            h = _rms(x_ref[rows, :], nw_ref[...]) * (1.0 + sc) + sh
            h_ref[rows, :] = h.astype(BF16)
            return carry

        lax.fori_loop(0, tm // sub, body, 0)

    acc = jnp.dot(h_ref[...], w_ref[...], preferred_element_type=F32)
    o_ref[...] = acc.astype(BF16)

    @pl.when(pl.program_id(2) == COL_SMALL // PROJ_TN)
    def _():
        sm_ref[...] = acc[:, COL_SMALL % PROJ_TN:COL_SMALL % PROJ_TN + LANES]


def _inproj(x3, sc, sh, nw, w_cat, tm):
    bx, lx, d = x3.shape
    n = w_cat.shape[1]
    kern = functools.partial(_inproj_kernel, tm=tm, sub=min(tm, 256))
    return pl.pallas_call(
        kern,
        grid=(bx, lx // tm, n // PROJ_TN),
        in_specs=[pl.BlockSpec((None, tm, d), lambda b, i, j: (b, i, 0)),
                  _mod_spec(sc, tm), _mod_spec(sh, tm),
                  pl.BlockSpec((1, d), lambda b, i, j: (0, 0)),
                  pl.BlockSpec((d, PROJ_TN), lambda b, i, j: (0, j))],
        out_specs=[pl.BlockSpec((None, tm, PROJ_TN), lambda b, i, j: (b, i, j)),
                   pl.BlockSpec((None, tm, LANES), lambda b, i, j: (b, i, 0))],
        out_shape=[jax.ShapeDtypeStruct((bx, lx, n), BF16), jax.ShapeDtypeStruct((bx, lx, LANES), F32)],
        scratch_shapes=[pltpu.VMEM((tm, d), BF16)],
        compiler_params=_cparams("arbitrary", "arbitrary", "arbitrary"),
        name="norm_inproj",
    )(x3, sc, sh, nw.reshape(1, d), w_cat)


def _conv_silu(u, ext_ref, bb, cw_ref, c, lc, bias=None):
    ext_ref[bb, SUBLANES:SUBLANES + c, :] = u
    y = ext_ref[bb, CARRY_ROW:CARRY_ROW + c, :] * cw_ref[0:1, :]
    for i in range(1, CONV_K):
        y = y + ext_ref[bb, CARRY_ROW + i:CARRY_ROW + i + c, :] * cw_ref[i:i + 1, :]
    ext_ref[bb, CARRY_ROW:SUBLANES, :] = ext_ref[bb, CARRY_ROW + lc:SUBLANES + lc, :]
    if bias is not None:
        y = y + bias
    return _silu(y)


def _conv_rounds(ref, rounds, ext_ref, cw_ref, r, c, lc, bias=None):
    read = _chunk_reader(ref, r, c)
    return [_conv_silu(read(bb, r0), ext_ref, bb, cw_ref, c, lc, bias) for rnd in rounds for bb, r0 in rnd]


def _chunk_reader(ref, r, c):
    if r == c:
        whole = ref[...].astype(F32)
        return lambda bb, r0: whole[bb * c:(bb + 1) * c]
    return lambda bb, r0: ref[pl.ds(bb * r + r0, c), :].astype(F32)


def _chunk_writer(ref, bt, r, c):
    if r != c:
        return (lambda bb, r0, val: ref.__setitem__((pl.ds(bb * r + r0, c), slice(None)), val.astype(ref.dtype)),
                lambda: None)
    parts = {}

    def flush():
        ref[...] = jnp.concatenate([parts[bb] for bb in range(bt)], axis=0).astype(ref.dtype)
    return (lambda bb, r0, val: parts.__setitem__(bb, val)), flush


def _run_rounds(process, bt, r, c, cpi):
    g = r // c
    if g == 1:
        process([[(bb, 0) for bb in range(bt)]])
    else:
        def body(ci, carry):
            process([[(bb, pl.multiple_of((ci * cpi + t) * c, c)) for bb in range(bt)] for t in range(cpi)])
            return carry
        lax.fori_loop(0, g // cpi, body, 0)


INV_BASE = SUBLANES


def _inverse_masks(row, col, c):
    sh = lambda x, s: jnp.right_shift(x, int(math.log2(s)))
    diag = sh(row, INV_BASE) == sh(col, INV_BASE)
    merges = []
    s = INV_BASE
    while s < c:
        merges.append((sh(row, 2 * s) == sh(col, 2 * s))
                      & (jnp.bitwise_and(sh(row, s), 1) == 1) & (jnp.bitwise_and(sh(col, s), 1) == 0))
        s *= 2
    return diag, merges


def _unit_lower_inverses(lmats, eye_f, masks):
    diag, merges = masks
    npows = [-jnp.where(diag, l, 0.0) for l in lmats]
    ps = [eye_f + n for n in npows]
    for _ in range(int(math.log2(INV_BASE)) - 1):
        npows = [_dot(n, n) for n in npows]
        ps = [p + _dot(p, n) for p, n in zip(ps, npows)]
    for m in merges:
        ts = [_dot(p, jnp.where(m, l, 0.0)) for p, l in zip(ps, lmats)]
        ps = [p - _dot(t, p) for t, p in zip(ts, ps)]
    return ps


def _dn_kernel(qkv_ref, z_ref, sm_ref, cw_ref, alog_ref, bias_ref, nw_ref, cin_ref, sin_ref,
               o_ref, cout_ref, sout_ref, ext_ref, *, bt, r, c, lc, ng, cpi):
    j = pl.program_id(1)

    @pl.when(j == 0)
    def _():
        ext_ref[:, CARRY_ROW:SUBLANES, :] = cin_ref[...]
        sout_ref[...] = sin_ref[...]

    row = _iota((c, c), 0)
    col = _iota((c, c), 1)
    tri_incl = row >= col
    tri_strict = row > col
    eye_f = (row == col).astype(F32)
    tril_bf = tri_incl.astype(BF16)
    valid = _iota((c, 1), 0) < lc
    nexp_a = -jnp.exp(alog_ref[...])
    bias = bias_ref[...]
    inv_masks = _inverse_masks(row, col, c)
    heads = range(DN_HEADS)
    read_z, read_sm = (_chunk_reader(ref, r, c) for ref in (z_ref, sm_ref))
    put_o, flush_o = _chunk_writer(o_ref, bt, r, c)

    def prep(bb, r0, y):
        sm = read_sm(bb, r0)
        g_all = nexp_a * _softplus(sm + bias)
        beta_all = _sigmoid(sm)
        if lc < c:
            g_all = jnp.where(valid, g_all, 0.0)
            beta_all = jnp.where(valid, beta_all, 0.0)
        cs_all = _dot_exact_lhs(tril_bf, g_all)
        cs_t = cs_all.T
        ecs_all = jnp.exp(cs_all)
        per_head = []
        for h in heads:
            q = y[:, h * DN_DK:(h + 1) * DN_DK]
            k = y[:, DN_KEY + h * DN_DK:DN_KEY + (h + 1) * DN_DK]
            v = y[:, 2 * DN_KEY + h * DN_DV:2 * DN_KEY + (h + 1) * DN_DV]
            q = q * lax.rsqrt(jnp.sum(q * q, axis=-1, keepdims=True) + EPS) * (DN_DK ** -0.5)
            k = k * lax.rsqrt(jnp.sum(k * k, axis=-1, keepdims=True) + EPS)
            if lc < c:
                k = jnp.where(valid, k, 0.0)
            cs = cs_all[:, SM_A + h:SM_A + h + 1]
            ecs = ecs_all[:, SM_A + h:SM_A + h + 1]
            beta = beta_all[:, SM_B + h:SM_B + h + 1]
            last = cs_all[c - 1:c, SM_A + h:SM_A + h + 1]
            decay = jnp.exp(jnp.where(tri_incl, cs - cs_t[SM_A + h:SM_A + h + 1, :], -jnp.inf))
            per_head.append(dict(q_ecs=(q * ecs).astype(BF16), q=q.astype(BF16), k=k.astype(BF16), beta=beta,
                                 decay=decay, elast=jnp.exp(last),
                                 k_tail=(k * jnp.exp(last - cs)).astype(BF16),
                                 rhs=jnp.concatenate([v * beta, k * (beta * ecs)], axis=1).astype(BF16)))
        return per_head

    def process(rounds):
        items = [it for rnd in rounds for it in rnd]
        conv = _conv_rounds(qkv_ref, rounds, ext_ref, cw_ref, r, c, lc)
        pre = [prep(bb, r0, y) for (bb, r0), y in zip(items, conv)]
        chains = [(i, h) for i in range(len(items)) for h in heads]
        a = {ch: pre[ch[0]][ch[1]] for ch in chains}
        kk = {ch: _dot_nt(a[ch]["k"], a[ch]["k"]) for ch in chains}
        qk = {ch: _dot_nt(a[ch]["q"], a[ch]["k"]) * a[ch]["decay"] for ch in chains}
        lmats = [jnp.where(tri_strict, a[ch]["beta"] * kk[ch] * a[ch]["decay"], 0.0) for ch in chains]
        pinv = _unit_lower_inverses(lmats, eye_f, inv_masks)
        sol = {ch: _dot(p, a[ch]["rhs"]) for ch, p in zip(chains, pinv)}
        first = 0
        for rnd in rounds:
            idx = range(first, first + len(rnd))
            first += len(rnd)
            rch = [(i, h) for i in idx for h in heads]
            s = {ch: sout_ref[items[ch[0]][0], ch[1]] for ch in rch}
            ws_qs = {ch: _dot(jnp.concatenate([sol[ch][:, DN_DV:].astype(BF16), a[ch]["q_ecs"]], axis=0), s[ch])
                     for ch in rch}
            v_new = {ch: (sol[ch][:, :DN_DV] - ws_qs[ch][:c]).astype(BF16) for ch in rch}
            o = {ch: ws_qs[ch][c:] + _dot(qk[ch], v_new[ch]) for ch in rch}
            for ch in rch:
                sout_ref[items[ch[0]][0], ch[1]] = a[ch]["elast"] * s[ch] + _dot_tn(a[ch]["k_tail"], v_new[ch])
            for i in idx:
                bb, r0 = items[i]
                z = read_z(bb, r0)
                put_o(bb, r0, jnp.concatenate(
                    [_rms(o[(i, h)], nw_ref[...]) * _silu(z[:, h * DN_DV:(h + 1) * DN_DV]) for h in heads], axis=1))

    _run_rounds(process, bt, r, c, cpi)
    flush_o()

    @pl.when(j == ng - 1)
    def _():
        cout_ref[...] = ext_ref[:, CARRY_ROW:SUBLANES, :]


def _small_row(vals, offset, fill=0.0):
    row = jnp.full((1, LANES), fill, F32)
    return lax.dynamic_update_slice(row, vals.astype(F32).reshape(1, -1), (0, offset))


def _dn_branch(proj, small, conv_w, a_log, dt_bias, norm_w, conv_in, rec_in, *, bt, r, c, lc, ng, cpi):
    nb = conv_in.shape[0] // bt
    rows = bt * r
    t = proj.shape[0]
    kern = functools.partial(_dn_kernel, bt=bt, r=r, c=c, lc=lc, ng=ng, cpi=cpi)
    rowmap = lambda col: (lambda i, j: (i * ng + j, col))
    const = lambda i, j: (0, 0)
    return pl.pallas_call(
        kern,
        grid=(nb, ng),
        in_specs=[pl.BlockSpec((rows, DN_CONV_DIM), rowmap(COL_DN_QKV // DN_CONV_DIM)),
                  pl.BlockSpec((rows, DN_VAL), rowmap(COL_DN_Z // DN_VAL)),
                  pl.BlockSpec((rows, LANES), rowmap(0)),
                  pl.BlockSpec((CONV_K, DN_CONV_DIM), const),
                  pl.BlockSpec((1, LANES), const),
                  pl.BlockSpec((1, LANES), const),
                  pl.BlockSpec((1, DN_DV), const),
                  pl.BlockSpec((bt, CONV_K - 1, DN_CONV_DIM), lambda i, j: (i, 0, 0)),
                  pl.BlockSpec((bt, DN_HEADS, DN_DK, DN_DV), lambda i, j: (i, 0, 0, 0))],
        out_specs=[pl.BlockSpec((rows, DN_VAL), lambda i, j: (i * ng + j, 0)),
                   pl.BlockSpec((bt, CONV_K - 1, DN_CONV_DIM), lambda i, j: (i, 0, 0)),
                   pl.BlockSpec((bt, DN_HEADS, DN_DK, DN_DV), lambda i, j: (i, 0, 0, 0))],
        out_shape=[jax.ShapeDtypeStruct((t, DN_VAL), BF16),
                   jax.ShapeDtypeStruct(conv_in.shape, F32),
                   jax.ShapeDtypeStruct(rec_in.shape, F32)],
        scratch_shapes=[pltpu.VMEM((bt, c + SUBLANES, DN_CONV_DIM), F32)],
        compiler_params=_cparams("arbitrary", "arbitrary"),
        name="gated_delta",
    )(proj, proj, small, conv_w, _small_row(a_log, SM_A), _small_row(dt_bias, SM_A),
      norm_w.reshape(1, DN_DV), conv_in, rec_in)


def _ssd_kernel(xbc_ref, z_ref, sm_ref, cw_ref, cb_ref, alog_ref, bias_ref, dvec_ref, nw_ref, cin_ref,
                hin_ref, y_ref, cout_ref, hout_ref, ext_ref, *, bt, r, c, lc, ng, cpi):
    j = pl.program_id(1)

    @pl.when(j == 0)
    def _():
        ext_ref[:, CARRY_ROW:SUBLANES, :] = cin_ref[...]
        hout_ref[...] = hin_ref[...]

    tril_bf = (_iota((c, c), 0) >= _iota((c, c), 1)).astype(BF16)
    row2 = _iota((c, 2 * c), 0)
    lane2 = _iota((c, 2 * c), 1)
    left2 = lane2 < c
    tri2 = row2 >= jnp.where(left2, lane2, lane2 - c)
    left_x = _iota((c, LANES), 1) < SSM_HEADDIM
    top_h = _iota((2 * SSM_HEADDIM, 1), 0) < SSM_HEADDIM
    valid = _iota((c, 1), 0) < lc
    nexp_a = -jnp.exp(alog_ref[...])
    bias = bias_ref[...]
    gw = SSM_INNER // SSM_GROUPS
    pairs_per_group = SSM_HEADS // SSM_GROUPS // 2

    pairs = range(SSM_HEADS // 2)
    pair_rows = lambda p: slice(p * LANES, (p + 1) * LANES)
    read_z, read_sm = (_chunk_reader(ref, r, c) for ref in (z_ref, sm_ref))
    put_y, flush_y = _chunk_writer(y_ref, bt, r, c)

    def prep(bb, r0, y):
        dt_all = _softplus(read_sm(bb, r0) + bias)
        if lc < c:
            dt_all = jnp.where(valid, dt_all, 0.0)
        cs_all = _dot_exact_lhs(tril_bf, dt_all * nexp_a)
        cs_t2 = jnp.concatenate([cs_all, cs_all], axis=0).T
        last = cs_all[c - 1:c, :]
        bm = [y[:, SSM_INNER + g * SSM_STATE:SSM_INNER + (g + 1) * SSM_STATE].astype(BF16)
              for g in range(SSM_GROUPS)]
        cm = [y[:, SSM_INNER + (SSM_GROUPS + g) * SSM_STATE:SSM_INNER + (SSM_GROUPS + g + 1) * SSM_STATE].astype(BF16)
              for g in range(SSM_GROUPS)]
        return dict(y=y, dt=dt_all, cs=cs_all, cs_t2=cs_t2, ecs=jnp.exp(cs_all),
                    tail=jnp.exp(last - cs_all), elast=jnp.exp(last), bm=bm, cm=cm)

    def process(rounds):
        items = [it for rnd in rounds for it in rnd]
        conv = _conv_rounds(xbc_ref, rounds, ext_ref, cw_ref, r, c, lc, bias=cb_ref[...])
        pre = [prep(bb, r0, y) for (bb, r0), y in zip(items, conv)]
        ids = range(len(items))
        cb2 = {(i, g): _dot_nt(pre[i]["cm"][g], jnp.concatenate([pre[i]["bm"][g]] * 2, axis=0))
               for i in ids for g in range(SSM_GROUPS)}
        y_intra, upd, xs_of = {}, {}, {}
        for i in ids:
            a = pre[i]
            for p in pairs:
                g = p // pairs_per_group
                la, lb = SM_DT + 2 * p, SM_DT + 2 * p + 1
                both = lambda v, mask=left_x: jnp.where(mask, v[:, la:la + 1], v[:, lb:lb + 1])
                xs = a["y"][:, pair_rows(p)]
                xdt = xs * both(a["dt"])
                diff = both(a["cs"], left2) - jnp.where(left2[0:1], a["cs_t2"][la:la + 1, :], a["cs_t2"][lb:lb + 1, :])
                decay2 = jnp.exp(jnp.where(tri2, diff, -jnp.inf))
                rhs = jnp.concatenate([jnp.where(left_x, xdt, 0.0), jnp.where(left_x, 0.0, xdt)], axis=0)
                y_intra[i, p] = _dot(cb2[i, g] * decay2, rhs)
                upd[i, p] = _dot_tn(xdt * both(a["tail"]), a["bm"][g])
                xs_of[i, p] = xs
        first = 0
        for rnd in rounds:
            idx = range(first, first + len(rnd))
            first += len(rnd)
            for i in idx:
                a = pre[i]
                bb, r0 = items[i]
                outs = []
                for p in pairs:
                    g = p // pairs_per_group
                    la, lb = SM_DT + 2 * p, SM_DT + 2 * p + 1
                    hs = hout_ref[bb, pair_rows(p), :]
                    y_inter = _dot_nt(a["cm"][g], hs) * jnp.where(left_x, a["ecs"][:, la:la + 1], a["ecs"][:, lb:lb + 1])
                    hout_ref[bb, pair_rows(p), :] = (
                        jnp.where(top_h, a["elast"][:, la:la + 1], a["elast"][:, lb:lb + 1]) * hs + upd[i, p])
                    outs.append(y_intra[i, p] + y_inter + dvec_ref[:, pair_rows(p)] * xs_of[i, p])
                yz = jnp.concatenate(outs, axis=1) * _silu(read_z(bb, r0))
                put_y(bb, r0, jnp.concatenate(
                    [_rms(yz[:, g * gw:(g + 1) * gw], nw_ref[:, g * gw:(g + 1) * gw]) for g in range(SSM_GROUPS)],
                    axis=1))

    _run_rounds(process, bt, r, c, cpi)
    flush_y()

    @pl.when(j == ng - 1)
    def _():
        cout_ref[...] = ext_ref[:, CARRY_ROW:SUBLANES, :]


def _ssd_branch(proj, small, conv_w, conv_b, a_log, dt_bias, d_skip, norm_w, conv_in, rec_in, *, bt, r, c, lc, ng, cpi):
    nb = conv_in.shape[0] // bt
    rows = bt * r
    t = proj.shape[0]
    hrows = SSM_HEADS * SSM_HEADDIM
    rec2 = rec_in.reshape(rec_in.shape[0], hrows, SSM_STATE)
    kern = functools.partial(_ssd_kernel, bt=bt, r=r, c=c, lc=lc, ng=ng, cpi=cpi)
    rowmap = lambda col: (lambda i, j: (i * ng + j, col))
    const = lambda i, j: (0, 0)
    y, conv_out, rec_out = pl.pallas_call(
        kern,
        grid=(nb, ng),
        in_specs=[pl.BlockSpec((rows, SSM_CONV_DIM), rowmap(COL_SSM_XBC // SSM_CONV_DIM)),
                  pl.BlockSpec((rows, SSM_INNER), rowmap(COL_SSM_Z // SSM_INNER)),
                  pl.BlockSpec((rows, LANES), rowmap(0)),
                  pl.BlockSpec((CONV_K, SSM_CONV_DIM), const),
                  pl.BlockSpec((1, SSM_CONV_DIM), const),
                  pl.BlockSpec((1, LANES), const),
                  pl.BlockSpec((1, LANES), const),
                  pl.BlockSpec((1, SSM_INNER), const),
                  pl.BlockSpec((1, SSM_INNER), const),
                  pl.BlockSpec((bt, CONV_K - 1, SSM_CONV_DIM), lambda i, j: (i, 0, 0)),
                  pl.BlockSpec((bt, hrows, SSM_STATE), lambda i, j: (i, 0, 0))],
        out_specs=[pl.BlockSpec((rows, SSM_INNER), lambda i, j: (i * ng + j, 0)),
                   pl.BlockSpec((bt, CONV_K - 1, SSM_CONV_DIM), lambda i, j: (i, 0, 0)),
                   pl.BlockSpec((bt, hrows, SSM_STATE), lambda i, j: (i, 0, 0))],
        out_shape=[jax.ShapeDtypeStruct((t, SSM_INNER), BF16),
                   jax.ShapeDtypeStruct(conv_in.shape, F32),
                   jax.ShapeDtypeStruct(rec2.shape, F32)],
        scratch_shapes=[pltpu.VMEM((bt, c + SUBLANES, SSM_CONV_DIM), F32)],
        compiler_params=_cparams("arbitrary", "arbitrary"),
        name="ssd_scan",
    )(proj, proj, small, conv_w, conv_b.reshape(1, -1), _small_row(a_log, SM_DT), _small_row(dt_bias, SM_DT),
      jnp.repeat(d_skip.astype(F32), SSM_HEADDIM).reshape(1, SSM_INNER), norm_w.reshape(1, SSM_INNER),
      conv_in, rec2)
    return y, conv_out, rec_out.reshape(rec_in.shape)


PLAN_E0, PLAN_E1, PLAN_C0, PLAN_C1 = 0, 1, 2, 3


def _route(logits):
    lane = _iota(logits.shape, 1).astype(F32)
    big = float(LANES)
    is_group = (lane >= RT_GROUP) & (lane < RT_GROUP + MOE_GROUPS)
    gl = jnp.where(is_group, logits, -jnp.inf)
    gmax = jnp.max(gl, axis=-1, keepdims=True)
    g_sel = jnp.min(jnp.where(gl == gmax, lane, big), axis=-1, keepdims=True) - RT_GROUP
    p_group = 1.0 / jnp.sum(jnp.exp(gl - gmax), axis=-1, keepdims=True)
    e_lo = RT_EXPERT + MOE_PER_GROUP * g_sel
    in_grp = (lane >= e_lo) & (lane < e_lo + MOE_PER_GROUP)
    el = jnp.where(in_grp, logits, -jnp.inf)
    ee = jnp.exp(el - jnp.max(el, axis=-1, keepdims=True))
    pe = jnp.where(in_grp, ee / jnp.sum(ee, axis=-1, keepdims=True), -1.0)
    p1 = jnp.max(pe, axis=-1, keepdims=True)
    i1 = jnp.min(jnp.where(pe == p1, lane, big), axis=-1, keepdims=True)
    pe2 = jnp.where(lane == i1, -1.0, pe)
    p2 = jnp.max(pe2, axis=-1, keepdims=True)
    i2 = jnp.min(jnp.where(pe2 == p2, lane, big), axis=-1, keepdims=True)
    tot = p1 + p2
    plan = jnp.where(lane == PLAN_E0, i1 - RT_EXPERT, 0.0) + jnp.where(lane == PLAN_E1, i2 - RT_EXPERT, 0.0)
    return plan + jnp.where(lane == PLAN_C0, p_group * p1 / tot, 0.0) + jnp.where(lane == PLAN_C1, p_group * p2 / tot, 0.0)


def _tile_ranks(plan):
    tp = plan.shape[0]
    lane = _iota(plan.shape, 1).astype(F32)
    sel0 = lane == plan[:, PLAN_E0:PLAN_E0 + 1]
    sel1 = lane == plan[:, PLAN_E1:PLAN_E1 + 1]
    sel = (sel0 | sel1).astype(BF16)
    before = (_iota((tp, tp), 0) > _iota((tp, tp), 1)).astype(BF16)
    excl = jnp.dot(before, sel, preferred_element_type=F32)
    r0 = jnp.sum(jnp.where(sel0, excl, 0.0), axis=-1, keepdims=True)
    r1 = jnp.sum(jnp.where(sel1, excl, 0.0), axis=-1, keepdims=True)
    rank = jnp.where(lane == PLAN_E0, r0, 0.0) + jnp.where(lane == PLAN_E1, r1, 0.0)
    return rank, jnp.sum(sel.astype(F32), axis=0, keepdims=True)


def _merge_kernel(x_ref, og_ref, ys_ref, gates_ref, gt1_ref, sc2_ref, sh2_ref, nw2_ref, wdn_ref, wssm_ref,
                  wout_ref, wr_ref, br_ref, x1_ref, h2_ref, plan_ref, rank_ref, cnt_ref):
    y_dn = _dot(og_ref[...], wdn_ref[...])
    y_ssm = _dot(ys_ref[...], wssm_ref[...])
    merged = (_sigmoid(gates_ref[:, :D_MODEL].astype(F32)) * y_dn
              + _sigmoid(gates_ref[:, D_MODEL:].astype(F32)) * y_ssm)
    x1 = x_ref[...] + gt1_ref[...] * _dot(merged, wout_ref[...])
    x1_ref[...] = x1
    h2 = _rms(x1, nw2_ref[...]) * (1.0 + sc2_ref[...]) + sh2_ref[...]
    h2_ref[...] = h2.astype(BF16)
    plan = _route(_dot_x3(h2, wr_ref[...]) + br_ref[...])
    plan_ref[...] = plan
    rank_ref[...], cnt_ref[...] = _tile_ranks(plan)


def _merge(x3, og, ys, proj, gt1, sc2, sh2, nw2, wdn, wssm, wout, wr, br, tm):
    bx, lx, d = x3.shape
    assert tm == TOK_TILE
    nl = lx // tm
    rowmap = lambda col: (lambda b, i: (b * nl + i, col))
    const = lambda b, i: (0, 0)
    tok = lambda w, dt: jax.ShapeDtypeStruct((bx, lx, w), dt)
    return pl.pallas_call(
        _merge_kernel,
        grid=(bx, nl),
        in_specs=[pl.BlockSpec((None, tm, d), lambda b, i: (b, i, 0)),
                  pl.BlockSpec((tm, DN_VAL), rowmap(0)),
                  pl.BlockSpec((tm, SSM_INNER), rowmap(0)),
                  pl.BlockSpec((tm, 2 * d), rowmap(COL_GATES // (2 * d))),
                  _mod_spec(gt1, tm), _mod_spec(sc2, tm), _mod_spec(sh2, tm),
                  pl.BlockSpec((1, d), const),
                  pl.BlockSpec(wdn.shape, const), pl.BlockSpec(wssm.shape, const), pl.BlockSpec(wout.shape, const),
                  pl.BlockSpec(wr.shape, const), pl.BlockSpec((1, LANES), const)],
        out_specs=[pl.BlockSpec((None, tm, d), lambda b, i: (b, i, 0)),
                   pl.BlockSpec((None, tm, d), lambda b, i: (b, i, 0)),
                   pl.BlockSpec((None, tm, LANES), lambda b, i: (b, i, 0)),
                   pl.BlockSpec((None, tm, LANES), lambda b, i: (b, i, 0)),
                   pl.BlockSpec((None, 1, LANES), lambda b, i: (b * nl + i, 0, 0))],
        out_shape=[tok(d, F32), tok(d, BF16), tok(LANES, F32), tok(LANES, F32),
                   jax.ShapeDtypeStruct((bx * nl, 1, LANES), F32)],
        compiler_params=_cparams("arbitrary", "arbitrary"),
        name="merge_route",
    )(x3, og, ys, proj, gt1, sc2, sh2, nw2.reshape(1, d), wdn, wssm, wout, wr, br)


FFN_ROWS = 512
TOK_TILE = 512
RUN_ALIGN = 2 * SUBLANES
RUN_BIG = 4 * RUN_ALIGN
RUN_SMALL = (2 * RUN_ALIGN, RUN_ALIGN)
LOCAL_ROWS = 2 * TOK_TILE + MOE_EXPERTS * RUN_ALIGN


def _local_slots(plan, rank, off_row):
    lane = _iota(plan.shape, 1).astype(F32)
    slots = []
    for k in (PLAN_E0, PLAN_E1):
        off = jnp.sum(jnp.where(lane == plan[:, k:k + 1], off_row, 0.0), axis=-1, keepdims=True)
        slots.append(off + rank[:, k:k + 1])
    return slots


def _run_blocks(p8_ref, loff_ref, base_ref, act):
    for e in range(MOE_EXPERTS):
        n = p8_ref[0, e]
        lo = loff_ref[0, e]
        go = base_ref[0, e]

        def big(k, carry, lo=lo, go=go):
            off = k * RUN_BIG
            act(pl.multiple_of(lo + off, RUN_ALIGN), pl.multiple_of(go + off, RUN_ALIGN), RUN_BIG)
            return carry

        lax.fori_loop(0, lax.shift_right_logical(n, RUN_BIG.bit_length() - 1), big, 0)
        for size in RUN_SMALL:
            @pl.when(jnp.bitwise_and(n, size) != 0)
            def _(size=size, n=n, lo=lo, go=go):
                done = jnp.bitwise_and(n, ~(2 * size - 1))
                act(pl.multiple_of(lo + done, RUN_ALIGN), pl.multiple_of(go + done, RUN_ALIGN), size)


def _dispatch_kernel(tv_ref, p8_ref, loff_ref, base_ref, plan_ref, rank_ref, off_ref, h2_ref, *rest, first):
    xs_ref, buf_ref, zero_ref, sem, zsem = rest[-5:]

    @pl.when((pl.program_id(0) == 0) & first)
    def _():
        zero_ref[...] = jnp.zeros_like(zero_ref)

        def zero_copy(i):
            rows = pl.ds(pl.multiple_of(i * FFN_ROWS, FFN_ROWS), FFN_ROWS)
            return pltpu.make_async_copy(zero_ref, xs_ref.at[rows], zsem)

        def zero_tiles(act):
            def body(i, carry):
                @pl.when(tv_ref[i] < FFN_ROWS)
                def _():
                    act(zero_copy(i))
                return carry
            lax.fori_loop(0, xs_ref.shape[0] // FFN_ROWS, body, 0)

        zero_tiles(lambda cp: cp.start())
        zero_tiles(lambda cp: cp.wait())

    s0, s1 = _local_slots(plan_ref[...], rank_ref[...], off_ref[...])
    row = _iota((1, LOCAL_ROWS), 1).astype(F32)
    onehot_t = ((row == s0) | (row == s1)).astype(BF16)
    buf_ref[...] = lax.dot_general(onehot_t, h2_ref[...], (((0,), (0,)), ((), ())),
                                   preferred_element_type=F32).astype(BF16)

    def copy(lo, go, size):
        return pltpu.make_async_copy(buf_ref.at[pl.ds(lo, size)], xs_ref.at[pl.ds(go, size)], sem)

    _run_blocks(p8_ref, loff_ref, base_ref, lambda lo, go, size: copy(lo, go, size).start())
    _run_blocks(p8_ref, loff_ref, base_ref, lambda lo, go, size: copy(lo, go, size).wait())


def _tile_scalars():
    return pl.BlockSpec((None, 1, LANES), lambda i, *_: (i, 0, 0), memory_space=pltpu.SMEM)


def _dispatch(tile_valid, p8, loff, base, plan2, rank, off_f, h2, rows, xs_prev=None):
    t, d = h2.shape
    tt = TOK_TILE
    tok = lambda w: pl.BlockSpec((tt, w), lambda i, tv: (i, 0))
    operands = (tile_valid, p8, loff, base, plan2, rank, off_f, h2) + (() if xs_prev is None else (xs_prev,))
    return pl.pallas_call(
        functools.partial(_dispatch_kernel, first=xs_prev is None),
        grid_spec=pltpu.PrefetchScalarGridSpec(
            num_scalar_prefetch=1,
            grid=(t // tt,),
            in_specs=[_tile_scalars(), _tile_scalars(), _tile_scalars(),
                      tok(LANES), tok(LANES),
                      pl.BlockSpec((None, 1, LANES), lambda i, tv: (i, 0, 0)),
                      tok(d)] + ([] if xs_prev is None else [pl.BlockSpec(memory_space=pl.ANY)]),
            out_specs=pl.BlockSpec(memory_space=pl.ANY),
            scratch_shapes=[pltpu.VMEM((LOCAL_ROWS, d), BF16), pltpu.VMEM((FFN_ROWS, d), BF16),
                            pltpu.SemaphoreType.DMA(()), pltpu.SemaphoreType.DMA(())]),
        out_shape=jax.ShapeDtypeStruct((rows, d), BF16),
        input_output_aliases={} if xs_prev is None else {len(operands) - 1: 0},
        compiler_params=_cparams("arbitrary"),
        name="moe_dispatch",
    )(*operands)


def _ffn_kernel(te_ref, tv_ref, src_ref, xs_ref, wg_ref, wu_ref, wd_ref, o_ref):
    nv = tv_ref[pl.program_id(0)]

    half = FFN_ROWS // 2

    def swiglu(rows):
        x = xs_ref[rows, :]
        o_ref[rows, :] = _dot(_silu(_dot(x, wg_ref[...])) * _dot(x, wu_ref[...]), wd_ref[...]).astype(BF16)

    @pl.when(nv > half)
    def _():
        swiglu(slice(None))

    @pl.when((nv > 0) & (nv <= half))
    def _():
        swiglu(slice(0, half))
        o_ref[half:, :] = jnp.zeros((half, o_ref.shape[1]), o_ref.dtype)

    @pl.when(nv <= 0)
    def _():
        o_ref[...] = jnp.zeros_like(o_ref)


def _ffn(tile_expert, tile_valid, tile_src, xs, wg, wu, wd):
    rows, d = xs.shape
    wmap = lambda i, te, tv, src: (te[i], 0, 0)
    return pl.pallas_call(
        _ffn_kernel,
        grid_spec=pltpu.PrefetchScalarGridSpec(
            num_scalar_prefetch=3,
            grid=(rows // FFN_ROWS,),
            in_specs=[pl.BlockSpec((FFN_ROWS, d), lambda i, te, tv, src: (src[i], 0)),
                      pl.BlockSpec((None, d, MOE_FF), wmap),
                      pl.BlockSpec((None, d, MOE_FF), wmap),
                      pl.BlockSpec((None, MOE_FF, d), wmap)],
            out_specs=pl.BlockSpec((FFN_ROWS, d), lambda i, te, tv, src: (i, 0))),
        out_shape=jax.ShapeDtypeStruct((rows, d), BF16),
        compiler_params=_cparams("arbitrary"),
        name="moe_ffn",
    )(tile_expert, tile_valid, tile_src, xs, wg, wu, wd)


def _combine_kernel(p8_ref, loff_ref, base_ref, p8n_ref, loffn_ref, basen_ref, ys_ref, plan_ref, rank_ref, off_ref,
                    x1_ref, gt2_ref, fsc_ref, fsh_ref, fnw_ref, y_ref, buf_ref, sem):
    step = pl.program_id(0) * pl.num_programs(1) + pl.program_id(1)
    last = pl.num_programs(0) * pl.num_programs(1) - 1
    slot = jnp.bitwise_and(step, 1)

    def copy(k, lo, go, size):
        return pltpu.make_async_copy(ys_ref.at[pl.ds(go, size)], buf_ref.at[k, pl.ds(lo, size)], sem.at[k])

    @pl.when(step == 0)
    def _():
        buf_ref[...] = jnp.zeros_like(buf_ref)
        _run_blocks(p8_ref, loff_ref, base_ref, lambda lo, go, size: copy(0, lo, go, size).start())

    @pl.when(step < last)
    def _():
        _run_blocks(p8n_ref, loffn_ref, basen_ref, lambda lo, go, size: copy(1 - slot, lo, go, size).start())

    plan = plan_ref[...]
    s0, s1 = _local_slots(plan, rank_ref[...], off_ref[...])
    row = _iota((1, LOCAL_ROWS), 1).astype(F32)
    weights = (jnp.where(row == s0, plan[:, PLAN_C0:PLAN_C0 + 1], 0.0)
               + jnp.where(row == s1, plan[:, PLAN_C1:PLAN_C1 + 1], 0.0))
    _run_blocks(p8_ref, loff_ref, base_ref, lambda lo, go, size: copy(slot, lo, go, size).wait())
    total = loff_ref[0, MOE_EXPERTS - 1] + p8_ref[0, MOE_EXPERTS - 1]
    filled = _iota((LOCAL_ROWS, 1), 0) < total
    moe = _dot(weights, jnp.where(filled, buf_ref[slot], jnp.zeros((), BF16)))
    x2 = x1_ref[...] + gt2_ref[...] * moe
    y_ref[...] = _rms(x2, fnw_ref[...]) * (1.0 + fsc_ref[...]) + fsh_ref[...]


def _combine(p8, loff, base, ys, plan, rank3, off_f, x1, gt2, fsc, fsh, fnw):
    bx, lx, d = x1.shape
    tt = TOK_TILE
    nl = lx // tt
    tile = lambda b, i: b * nl + i
    scal = pl.BlockSpec((None, 1, LANES), lambda b, i: (tile(b, i), 0, 0), memory_space=pltpu.SMEM)
    scal_next = pl.BlockSpec((None, 1, LANES), lambda b, i: (jnp.minimum(tile(b, i) + 1, bx * nl - 1), 0, 0),
                             memory_space=pltpu.SMEM)
    tokspec = lambda w: pl.BlockSpec((None, tt, w), lambda b, i: (b, i, 0))
    return pl.pallas_call(
        _combine_kernel,
        grid=(bx, nl),
        in_specs=[scal, scal, scal, scal_next, scal_next, scal_next,
                  pl.BlockSpec(memory_space=pl.ANY),
                  tokspec(LANES), tokspec(LANES),
                  pl.BlockSpec((None, 1, LANES), lambda b, i: (tile(b, i), 0, 0)),
                  tokspec(d), _mod_spec(gt2, tt), _mod_spec(fsc, tt), _mod_spec(fsh, tt),
                  pl.BlockSpec((1, d), lambda b, i: (0, 0))],
        out_specs=tokspec(d),
        out_shape=jax.ShapeDtypeStruct((bx, lx, d), F32),
        scratch_shapes=[pltpu.VMEM((2, LOCAL_ROWS, d), BF16), pltpu.SemaphoreType.DMA((2,))],
        compiler_params=_cparams("arbitrary", "arbitrary"),
        name="moe_combine_final",
    )(p8, loff, base, p8, loff, base, ys, plan, rank3, off_f, x1, gt2, fsc, fsh, fnw.reshape(1, d))


def _moe(groups, wg, wu, wd, fnw):
    d = groups[0]["x1"].shape[-1]
    ntiles = [g["x1"].shape[0] * g["x1"].shape[1] // TOK_TILE for g in groups]
    t = sum(ntiles) * TOK_TILE
    n = jnp.concatenate([g["cnt"][:, 0, :MOE_EXPERTS] for g in groups], axis=0).astype(jnp.int32)
    p8 = (n + RUN_ALIGN - 1) // RUN_ALIGN * RUN_ALIGN
    loff = jnp.cumsum(p8, axis=1) - p8
    erows = jnp.sum(p8, axis=0)
    epad = (erows + FFN_ROWS - 1) // FFN_ROWS * FFN_ROWS
    ends = jnp.cumsum(epad)
    starts = ends - epad
    base = starts[None, :] + jnp.cumsum(p8, axis=0) - p8
    lanes = lambda a: jnp.pad(a, ((0, 0), (0, LANES - MOE_EXPERTS)))[:, None, :]
    rows = (-(-(2 * t + (t // TOK_TILE) * MOE_EXPERTS * (RUN_ALIGN - 1)) // FFN_ROWS) + MOE_EXPERTS) * FFN_ROWS
    tile_start = jnp.arange(rows // FFN_ROWS, dtype=jnp.int32) * FFN_ROWS
    tile_expert = jnp.minimum(jnp.sum(tile_start[:, None] >= ends[None, :], axis=1), MOE_EXPERTS - 1).astype(jnp.int32)
    hot = tile_expert[:, None] == jnp.arange(MOE_EXPERTS)[None, :]
    valid_rows = lambda er: jnp.clip(jnp.sum(jnp.where(hot, er - (tile_start[:, None] - starts), 0), axis=1),
                                     0, FFN_ROWS).astype(jnp.int32)
    tile_valid = valid_rows(erows)
    p8l, loffl, basel = lanes(p8), lanes(loff), lanes(base)
    off_f = loffl.astype(F32)
    spans, first = [], 0
    for nt in ntiles:
        spans.append(slice(first, first + nt))
        first += nt
    first_valid = valid_rows(jnp.sum(p8[spans[0]], axis=0))
    xs = None
    for g, sp in zip(groups, spans):
        tg = g["h2"].shape[0] * g["h2"].shape[1]
        xs = _dispatch(first_valid, p8l[sp], loffl[sp], basel[sp], g["plan"].reshape(tg, LANES),
                       g["rank"].reshape(tg, LANES), off_f[sp], g["h2"].reshape(tg, d), rows, xs_prev=xs)
    tile_id = tile_start // FFN_ROWS
    tile_src = jnp.where(tile_valid > 0, tile_id, jnp.max(jnp.where(tile_valid > 0, tile_id, 0))).astype(jnp.int32)
    ys = _ffn(tile_expert, tile_valid, tile_src, xs, wg, wu, wd)
    return [_combine(p8l[sp], loffl[sp], basel[sp], ys, g["plan"], g["rank"], off_f[sp], g["x1"], g["gt2"],
                     g["fsc"], g["fsh"], fnw) for g, sp in zip(groups, spans)]


def _prep_layer(lp):
    w_in = lp["w_in"]
    offs = [0]
    for s in (DN_CONV_DIM, DN_VAL, DN_HEADS, DN_HEADS, SSM_CONV_DIM, SSM_INNER, SSM_HEADS, D_MODEL, D_MODEL):
        offs.append(offs[-1] + s)
    seg = lambda i: w_in[:, offs[i]:offs[i + 1]]
    small = jnp.concatenate([seg(2), seg(3), seg(6)], axis=1)
    pad = jnp.zeros((D_MODEL, PROJ_N - COL_SMALL - small.shape[1]), F32)
    w_cat = jnp.concatenate([seg(0), seg(4), seg(5), seg(7), seg(8), seg(1), small, pad], axis=1).astype(BF16)
    wr = jnp.concatenate([lp["w_group_router"], lp["w_expert_router"],
                          jnp.zeros((D_MODEL, LANES - MOE_GROUPS - MOE_EXPERTS), F32)], axis=1)
    br = jnp.concatenate([lp["b_group_router"], lp["b_expert_router"],
                          jnp.zeros((LANES - MOE_GROUPS - MOE_EXPERTS,), F32)]).reshape(1, LANES)
    return dict(lp, w_cat=w_cat, wr=wr, br=br,
                wdn=lp["w_dn_out"].astype(BF16), wssm=lp["w_ssm_out"].astype(BF16), wout=lp["w_out"].astype(BF16),
                wg=lp["w_exp_gate"], wu=lp["w_exp_up"], wd=lp["w_exp_down"])


def _mixer(x3, mods, fins, states, lp, cfg):
    bx, lx, d = x3.shape
    scan = dict(bt=cfg["bt"], r=cfg["r"], c=cfg["c"], lc=cfg["lc"], ng=cfg["ng"])
    sh1, sc1, gt1, sh2, sc2, gt2 = mods
    dn_conv, dn_rec, ssm_conv, ssm_rec = states
    proj, small = _inproj(x3, sc1, sh1, lp["norm_mix_w"], lp["w_cat"], cfg["tm"])
    proj, small = proj.reshape(bx * lx, PROJ_N), small.reshape(bx * lx, LANES)
    og, dn_conv_new, dn_rec_new = _dn_branch(proj, small, lp["dn_conv_w"], lp["dn_A_log"], lp["dn_dt_bias"],
                                             lp["dn_norm_w"], dn_conv, dn_rec, cpi=cfg["cpi_dn"], **scan)
    ys, ssm_conv_new, ssm_rec_new = _ssd_branch(proj, small, lp["ssm_conv_w"], lp["ssm_conv_b"], lp["ssm_A_log"],
                                                lp["ssm_dt_bias"], lp["ssm_D"], lp["ssm_norm_w"],
                                                ssm_conv, ssm_rec, cpi=cfg["cpi_ssd"], **scan)
    x1, h2, plan, rank, cnt = _merge(x3, og, ys, proj, gt1, sc2, sh2, lp["norm_ffn_w"], lp["wdn"], lp["wssm"],
                                     lp["wout"], lp["wr"], lp["br"], cfg["tm_merge"])
    group = dict(x1=x1, h2=h2, plan=plan, rank=rank, cnt=cnt, gt2=gt2, fsh=fins[0], fsc=fins[1])
    return group, (dn_conv_new, dn_rec_new, ssm_conv_new, ssm_rec_new)


def _per_seq(m):
    return m[:, None, :]


def kernel(x_prompt, x_sample, c_prompt, c_sample, state_dn_conv, state_dn_rec, state_ssm_conv, state_ssm_rec, w_ada, b_ada, norm_mix_w, w_in, dn_conv_w, dn_A_log, dn_dt_bias, dn_norm_w, w_dn_out, ssm_conv_w, ssm_conv_b, ssm_A_log, ssm_dt_bias, ssm_D, ssm_norm_w, w_ssm_out, w_out, norm_ffn_w, w_group_router, b_group_router, w_expert_router, b_expert_router, w_exp_gate, w_exp_up, w_exp_down, w_ada_final, b_ada_final, final_norm_w):
    depth = w_ada.shape[0]
    assert depth == 1
    per_layer = dict(w_ada=w_ada, b_ada=b_ada, norm_mix_w=norm_mix_w, w_in=w_in, dn_conv_w=dn_conv_w,
                     dn_A_log=dn_A_log, dn_dt_bias=dn_dt_bias, dn_norm_w=dn_norm_w, w_dn_out=w_dn_out,
                     ssm_conv_w=ssm_conv_w, ssm_conv_b=ssm_conv_b, ssm_A_log=ssm_A_log, ssm_dt_bias=ssm_dt_bias,
                     ssm_D=ssm_D, ssm_norm_w=ssm_norm_w, w_ssm_out=w_ssm_out, w_out=w_out, norm_ffn_w=norm_ffn_w,
                     w_group_router=w_group_router, b_group_router=b_group_router,
                     w_expert_router=w_expert_router, b_expert_router=b_expert_router,
                     w_exp_gate=w_exp_gate, w_exp_up=w_exp_up, w_exp_down=w_exp_down)
    layers = [_prep_layer({k: v[l] for k, v in per_layer.items()}) for l in range(depth)]

    nbp, lp_, d = x_prompt.shape
    nbs, ls, _ = x_sample.shape
    c_all = jnp.concatenate([c_prompt, c_sample], axis=0)
    mod_all = [_ada(c_all, lyr["w_ada"], lyr["b_ada"]) for lyr in layers]
    fin_all = _ada(c_all, w_ada_final, b_ada_final)

    mods_p = [[_per_seq(m) for m in jnp.split(ma[:nbp], 6, axis=-1)] for ma in mod_all]
    fins_p = [_per_seq(m) for m in jnp.split(fin_all[:nbp], 2, axis=-1)]
    zeros_p = [(jnp.zeros((nbp, CONV_K - 1, DN_CONV_DIM), F32), jnp.zeros((nbp, DN_HEADS, DN_DK, DN_DV), F32),
                jnp.zeros((nbp, CONV_K - 1, SSM_CONV_DIM), F32),
                jnp.zeros((nbp, SSM_HEADS, SSM_HEADDIM, SSM_STATE), F32)) for _ in range(depth)]
    c_p = min(SCAN_CHUNK, lp_)
    r_p = min(lp_, 16 * c_p)
    cfg_p = dict(tm=min(lp_, 2048), tm_merge=min(lp_, TOK_TILE), bt=1, r=r_p, c=c_p, lc=c_p, ng=lp_ // r_p,
                 cpi_dn=8 if (r_p // c_p) % 8 == 0 else 1, cpi_ssd=4 if (r_p // c_p) % 4 == 0 else 1)
    grp_p, st_p = _mixer(x_prompt, mods_p[0], fins_p, zeros_p[0], layers[0], cfg_p)

    lpad = -(-ls // SUBLANES) * SUBLANES
    bt_s = 16
    xs = jnp.pad(x_sample, ((0, 0), (0, lpad - ls), (0, 0))).reshape(1, nbs * lpad, d)
    per_tok = lambda m: jnp.repeat(m, lpad, axis=0)[None]
    mods_s = [[per_tok(m) for m in jnp.split(ma[nbp:], 6, axis=-1)] for ma in mod_all]
    fins_s = [per_tok(m) for m in jnp.split(fin_all[nbp:], 2, axis=-1)]
    st_in = [(state_dn_conv[l], state_dn_rec[l], state_ssm_conv[l], state_ssm_rec[l]) for l in range(depth)]
    ts = nbs * lpad
    cfg_s = dict(tm=ts, tm_merge=min(ts, TOK_TILE), bt=bt_s, r=lpad, c=lpad, lc=ls, ng=1, cpi_dn=1, cpi_ssd=1)
    grp_s, st_s = _mixer(xs, mods_s[0], fins_s, st_in[0], layers[0], cfg_s)

    y_p, y_s = _moe([grp_p, grp_s], layers[0]["wg"], layers[0]["wu"], layers[0]["wd"], final_norm_w)
    y_s = y_s.reshape(nbs, lpad, d)[:, :ls]
    return (y_p, y_s) + tuple(st[None] for st in st_p) + tuple(st[None] for st in st_s)
```

```python
import functools
import math

import jax
import jax.numpy as jnp
from jax import lax
from jax.experimental import pallas as pl
from jax.experimental.pallas import tpu as pltpu

F32 = jnp.float32
BF16 = jnp.bfloat16

D_MODEL = 1024
DN_HEADS = 4
DN_DK = 128
DN_DV = 128
DN_KEY = DN_HEADS * DN_DK
DN_VAL = DN_HEADS * DN_DV
CONV_K = 4
DN_CONV_DIM = 2 * DN_KEY + DN_VAL
SSM_INNER = D_MODEL
SSM_HEADDIM = 64
SSM_HEADS = SSM_INNER // SSM_HEADDIM
SSM_GROUPS = 2
SSM_STATE = 128
SSM_CONV_DIM = SSM_INNER + 2 * SSM_GROUPS * SSM_STATE
MOE_GROUPS = 4
MOE_PER_GROUP = 8
MOE_EXPERTS = MOE_GROUPS * MOE_PER_GROUP
MOE_FF = D_MODEL // 4
EPS = 1e-6
SCAN_CHUNK = 64

LANES = 128
SUBLANES = 8
CARRY_ROW = SUBLANES - (CONV_K - 1)

COL_DN_QKV = 0
COL_SSM_XBC = COL_DN_QKV + DN_CONV_DIM
COL_SSM_Z = COL_SSM_XBC + SSM_CONV_DIM
COL_GATES = COL_SSM_Z + SSM_INNER
COL_DN_Z = COL_GATES + 2 * D_MODEL
COL_SMALL = COL_DN_Z + DN_VAL
PROJ_TN = 1792
PROJ_N = 4 * PROJ_TN
SM_A, SM_B, SM_DT = 0, DN_HEADS, 2 * DN_HEADS
RT_GROUP, RT_EXPERT = 0, MOE_GROUPS

VMEM_LIMIT = 56 * 1024 * 1024


def _cparams(*sem):
    return pltpu.CompilerParams(dimension_semantics=sem, vmem_limit_bytes=VMEM_LIMIT)


def _dot(a, b):
    return jnp.dot(a.astype(BF16), b.astype(BF16), preferred_element_type=F32)


def _dot_nt(a, b):
    return lax.dot_general(a.astype(BF16), b.astype(BF16), (((1,), (1,)), ((), ())),
                           preferred_element_type=F32)


def _dot_tn(a, b):
    return lax.dot_general(a.astype(BF16), b.astype(BF16), (((0,), (0,)), ((), ())),
                           preferred_element_type=F32)


def _split3(x):
    hi = x.astype(BF16)
    r = x - hi.astype(F32)
    mid = r.astype(BF16)
    lo = (r - mid.astype(F32)).astype(BF16)
    return hi, mid, lo


def _dot_exact_lhs(a_bf, b):
    hi, mid, lo = _split3(b)
    d = functools.partial(jnp.dot, preferred_element_type=F32)
    return d(a_bf, hi) + (d(a_bf, mid) + d(a_bf, lo))


def _dot_x3(a, b):
    a_hi = a.astype(BF16)
    a_lo = (a - a_hi.astype(F32)).astype(BF16)
    b_hi = b.astype(BF16)
    b_lo = (b - b_hi.astype(F32)).astype(BF16)
    d = functools.partial(jnp.dot, preferred_element_type=F32)
    n = b.shape[1]
    hi_both = d(a_hi, jnp.concatenate([b_hi, b_lo], axis=1))
    return hi_both[:, :n] + (hi_both[:, n:] + d(a_lo, b_hi))


def _sigmoid(x):
    return 1.0 / (1.0 + jnp.exp(-x))


def _silu(x):
    return x * _sigmoid(x)


def _softplus(x):
    return jnp.maximum(x, 0.0) + jnp.log1p(jnp.exp(-jnp.abs(x)))


def _rms(x, w):
    return x * lax.rsqrt(jnp.mean(x * x, axis=-1, keepdims=True) + EPS) * w


def _iota(shape, dim):
    return lax.broadcasted_iota(jnp.int32, shape, dim)


def _ada_kernel(c_ref, w_ref, b_ref, o_ref):
    o_ref[...] = _dot(_silu(c_ref[...]), w_ref[...]) + b_ref[...]


def _ada(c, w, b, tn=512):
    m, d = c.shape
    n = w.shape[1]
    return pl.pallas_call(
        _ada_kernel,
        grid=(n // tn,),
        in_specs=[pl.BlockSpec((m, d), lambda j: (0, 0)),
                  pl.BlockSpec((d, tn), lambda j: (0, j)),
                  pl.BlockSpec((1, tn), lambda j: (0, j))],
        out_specs=pl.BlockSpec((m, tn), lambda j: (0, j)),
        out_shape=jax.ShapeDtypeStruct((m, n), F32),
        compiler_params=_cparams("arbitrary"),
        name="ada_mod",
    )(c, w, b.reshape(1, n))


def _mod_spec(mod, tm):
    if mod.shape[1] == 1:
        return pl.BlockSpec((None, 1, D_MODEL), lambda b, i, *_: (b, 0, 0))
    return pl.BlockSpec((None, tm, D_MODEL), lambda b, i, *_: (b, i, 0))


def _inproj_kernel(x_ref, sc_ref, sh_ref, nw_ref, w_ref, o_ref, sm_ref, h_ref, *, tm, sub):
    @pl.when(pl.program_id(2) == 0)
    def _():
        per_token = sc_ref.shape[0] != 1

        def body(r, carry):
            rows = pl.ds(pl.multiple_of(r * sub, sub), sub)
            sc = sc_ref[rows, :] if per_token else sc_ref[...]
            sh = sh_ref[rows, :] if per_token else sh_ref[...]
            h = _rms(x_ref[rows, :], nw_ref[...]) * (1.0 + sc) + sh
            h_ref[rows, :] = h.astype(BF16)
            return carry

        lax.fori_loop(0, tm // sub, body, 0)

    acc = jnp.dot(h_ref[...], w_ref[...], preferred_element_type=F32)
    o_ref[...] = acc.astype(BF16)

    @pl.when(pl.program_id(2) == COL_SMALL // PROJ_TN)
    def _():
        sm_ref[...] = acc[:, COL_SMALL % PROJ_TN:COL_SMALL % PROJ_TN + LANES]


def _inproj(x3, sc, sh, nw, w_cat, tm):
    bx, lx, d = x3.shape
    n = w_cat.shape[1]
    kern = functools.partial(_inproj_kernel, tm=tm, sub=min(tm, 256))
    return pl.pallas_call(
        kern,
        grid=(bx, lx // tm, n // PROJ_TN),
        in_specs=[pl.BlockSpec((None, tm, d), lambda b, i, j: (b, i, 0)),
                  _mod_spec(sc, tm), _mod_spec(sh, tm),
                  pl.BlockSpec((1, d), lambda b, i, j: (0, 0)),
                  pl.BlockSpec((d, PROJ_TN), lambda b, i, j: (0, j))],
        out_specs=[pl.BlockSpec((None, tm, PROJ_TN), lambda b, i, j: (b, i, j)),
                   pl.BlockSpec((None, tm, LANES), lambda b, i, j: (b, i, 0))],
        out_shape=[jax.ShapeDtypeStruct((bx, lx, n), BF16), jax.ShapeDtypeStruct((bx, lx, LANES), F32)],
        scratch_shapes=[pltpu.VMEM((tm, d), BF16)],
        compiler_params=_cparams("arbitrary", "arbitrary", "arbitrary"),
        name="norm_inproj",
    )(x3, sc, sh, nw.reshape(1, d), w_cat)


def _conv_silu(u, ext_ref, bb, cw_ref, c, lc, bias=None):
    ext_ref[bb, SUBLANES:SUBLANES + c, :] = u
    y = ext_ref[bb, CARRY_ROW:CARRY_ROW + c, :] * cw_ref[0:1, :]
    for i in range(1, CONV_K):
        y = y + ext_ref[bb, CARRY_ROW + i:CARRY_ROW + i + c, :] * cw_ref[i:i + 1, :]
    ext_ref[bb, CARRY_ROW:SUBLANES, :] = ext_ref[bb, CARRY_ROW + lc:SUBLANES + lc, :]
    if bias is not None:
        y = y + bias
    return _silu(y)


def _conv_rounds(ref, rounds, ext_ref, cw_ref, r, c, lc, bias=None):
    read = _chunk_reader(ref, r, c)
    return [_conv_silu(read(bb, r0), ext_ref, bb, cw_ref, c, lc, bias) for rnd in rounds for bb, r0 in rnd]


def _chunk_reader(ref, r, c):
    if r == c:
        whole = ref[...].astype(F32)
        return lambda bb, r0: whole[bb * c:(bb + 1) * c]
    return lambda bb, r0: ref[pl.ds(bb * r + r0, c), :].astype(F32)


def _chunk_writer(ref, bt, r, c):
    if r != c:
        return (lambda bb, r0, val: ref.__setitem__((pl.ds(bb * r + r0, c), slice(None)), val.astype(ref.dtype)),
                lambda: None)
    parts = {}

    def flush():
        ref[...] = jnp.concatenate([parts[bb] for bb in range(bt)], axis=0).astype(ref.dtype)
    return (lambda bb, r0, val: parts.__setitem__(bb, val)), flush


def _run_rounds(process, bt, r, c, cpi):
    g = r // c
    if g == 1:
        process([[(bb, 0) for bb in range(bt)]])
    else:
        def body(ci, carry):
            process([[(bb, pl.multiple_of((ci * cpi + t) * c, c)) for bb in range(bt)] for t in range(cpi)])
            return carry
        lax.fori_loop(0, g // cpi, body, 0)


INV_BASE = SUBLANES


def _inverse_masks(row, col, c):
    sh = lambda x, s: jnp.right_shift(x, int(math.log2(s)))
    diag = sh(row, INV_BASE) == sh(col, INV_BASE)
    merges = []
    s = INV_BASE
    while s < c:
        merges.append((sh(row, 2 * s) == sh(col, 2 * s))
                      & (jnp.bitwise_and(sh(row, s), 1) == 1) & (jnp.bitwise_and(sh(col, s), 1) == 0))
        s *= 2
    return diag, merges


def _unit_lower_inverses(lmats, eye_f, masks):
    diag, merges = masks
    npows = [-jnp.where(diag, l, 0.0) for l in lmats]
    ps = [eye_f + n for n in npows]
    for _ in range(int(math.log2(INV_BASE)) - 1):
        npows = [_dot(n, n) for n in npows]
        ps = [p + _dot(p, n) for p, n in zip(ps, npows)]
    for m in merges:
        ts = [_dot(p, jnp.where(m, l, 0.0)) for p, l in zip(ps, lmats)]
        ps = [p - _dot(t, p) for t, p in zip(ts, ps)]
    return ps


def _dn_kernel(qkv_ref, z_ref, sm_ref, cw_ref, alog_ref, bias_ref, nw_ref, cin_ref, sin_ref,
               o_ref, cout_ref, sout_ref, ext_ref, *, bt, r, c, lc, ng, cpi):
    j = pl.program_id(1)

    @pl.when(j == 0)
    def _():
        ext_ref[:, CARRY_ROW:SUBLANES, :] = cin_ref[...]
        sout_ref[...] = sin_ref[...]

    row = _iota((c, c), 0)
    col = _iota((c, c), 1)
    tri_incl = row >= col
    tri_strict = row > col
    eye_f = (row == col).astype(F32)
    tril_bf = tri_incl.astype(BF16)
    valid = _iota((c, 1), 0) < lc
    nexp_a = -jnp.exp(alog_ref[...])
    bias = bias_ref[...]
    inv_masks = _inverse_masks(row, col, c)
    heads = range(DN_HEADS)
    read_z, read_sm = (_chunk_reader(ref, r, c) for ref in (z_ref, sm_ref))
    put_o, flush_o = _chunk_writer(o_ref, bt, r, c)

    def prep(bb, r0, y):
        sm = read_sm(bb, r0)
        g_all = nexp_a * _softplus(sm + bias)
        beta_all = _sigmoid(sm)
        if lc < c:
            g_all = jnp.where(valid, g_all, 0.0)
            beta_all = jnp.where(valid, beta_all, 0.0)
        cs_all = _dot_exact_lhs(tril_bf, g_all)
        cs_t = cs_all.T
        ecs_all = jnp.exp(cs_all)
        per_head = []
        for h in heads:
            q = y[:, h * DN_DK:(h + 1) * DN_DK]
            k = y[:, DN_KEY + h * DN_DK:DN_KEY + (h + 1) * DN_DK]
            v = y[:, 2 * DN_KEY + h * DN_DV:2 * DN_KEY + (h + 1) * DN_DV]
            q = q * lax.rsqrt(jnp.sum(q * q, axis=-1, keepdims=True) + EPS) * (DN_DK ** -0.5)
            k = k * lax.rsqrt(jnp.sum(k * k, axis=-1, keepdims=True) + EPS)
            if lc < c:
                k = jnp.where(valid, k, 0.0)
            cs = cs_all[:, SM_A + h:SM_A + h + 1]
            ecs = ecs_all[:, SM_A + h:SM_A + h + 1]
            beta = beta_all[:, SM_B + h:SM_B + h + 1]
            last = cs_all[c - 1:c, SM_A + h:SM_A + h + 1]
            decay = jnp.exp(jnp.where(tri_incl, cs - cs_t[SM_A + h:SM_A + h + 1, :], -jnp.inf))
            per_head.append(dict(q_ecs=(q * ecs).astype(BF16), q=q.astype(BF16), k=k.astype(BF16), beta=beta,
                                 decay=decay, elast=jnp.exp(last),
                                 k_tail=(k * jnp.exp(last - cs)).astype(BF16),
                                 rhs=jnp.concatenate([v * beta, k * (beta * ecs)], axis=1).astype(BF16)))
        return per_head

    def process(rounds):
        items = [it for rnd in rounds for it in rnd]
        conv = _conv_rounds(qkv_ref, rounds, ext_ref, cw_ref, r, c, lc)
        pre = [prep(bb, r0, y) for (bb, r0), y in zip(items, conv)]
        chains = [(i, h) for i in range(len(items)) for h in heads]
        a = {ch: pre[ch[0]][ch[1]] for ch in chains}
        kk = {ch: _dot_nt(a[ch]["k"], a[ch]["k"]) for ch in chains}
        qk = {ch: _dot_nt(a[ch]["q"], a[ch]["k"]) * a[ch]["decay"] for ch in chains}
        lmats = [jnp.where(tri_strict, a[ch]["beta"] * kk[ch] * a[ch]["decay"], 0.0) for ch in chains]
        pinv = _unit_lower_inverses(lmats, eye_f, inv_masks)
        sol = {ch: _dot(p, a[ch]["rhs"]) for ch, p in zip(chains, pinv)}
        first = 0
        for rnd in rounds:
            idx = range(first, first + len(rnd))
            first += len(rnd)
            rch = [(i, h) for i in idx for h in heads]
            s = {ch: sout_ref[items[ch[0]][0], ch[1]] for ch in rch}
            ws_qs = {ch: _dot(jnp.concatenate([sol[ch][:, DN_DV:].astype(BF16), a[ch]["q_ecs"]], axis=0), s[ch])
                     for ch in rch}
            v_new = {ch: (sol[ch][:, :DN_DV] - ws_qs[ch][:c]).astype(BF16) for ch in rch}
            o = {ch: ws_qs[ch][c:] + _dot(qk[ch], v_new[ch]) for ch in rch}
            for ch in rch:
                sout_ref[items[ch[0]][0], ch[1]] = a[ch]["elast"] * s[ch] + _dot_tn(a[ch]["k_tail"], v_new[ch])
            for i in idx:
                bb, r0 = items[i]
                z = read_z(bb, r0)
                put_o(bb, r0, jnp.concatenate(
                    [_rms(o[(i, h)], nw_ref[...]) * _silu(z[:, h * DN_DV:(h + 1) * DN_DV]) for h in heads], axis=1))

    _run_rounds(process, bt, r, c, cpi)
    flush_o()

    @pl.when(j == ng - 1)
    def _():
        cout_ref[...] = ext_ref[:, CARRY_ROW:SUBLANES, :]


def _small_row(vals, offset, fill=0.0):
    row = jnp.full((1, LANES), fill, F32)
    return lax.dynamic_update_slice(row, vals.astype(F32).reshape(1, -1), (0, offset))


def _dn_branch(proj, small, conv_w, a_log, dt_bias, norm_w, conv_in, rec_in, *, bt, r, c, lc, ng, cpi):
    nb = conv_in.shape[0] // bt
    rows = bt * r
    t = proj.shape[0]
    kern = functools.partial(_dn_kernel, bt=bt, r=r, c=c, lc=lc, ng=ng, cpi=cpi)
    rowmap = lambda col: (lambda i, j: (i * ng + j, col))
    const = lambda i, j: (0, 0)
    return pl.pallas_call(
        kern,
        grid=(nb, ng),
        in_specs=[pl.BlockSpec((rows, DN_CONV_DIM), rowmap(COL_DN_QKV // DN_CONV_DIM)),
                  pl.BlockSpec((rows, DN_VAL), rowmap(COL_DN_Z // DN_VAL)),
                  pl.BlockSpec((rows, LANES), rowmap(0)),
                  pl.BlockSpec((CONV_K, DN_CONV_DIM), const),
                  pl.BlockSpec((1, LANES), const),
                  pl.BlockSpec((1, LANES), const),
                  pl.BlockSpec((1, DN_DV), const),
                  pl.BlockSpec((bt, CONV_K - 1, DN_CONV_DIM), lambda i, j: (i, 0, 0)),
                  pl.BlockSpec((bt, DN_HEADS, DN_DK, DN_DV), lambda i, j: (i, 0, 0, 0))],
        out_specs=[pl.BlockSpec((rows, DN_VAL), lambda i, j: (i * ng + j, 0)),
                   pl.BlockSpec((bt, CONV_K - 1, DN_CONV_DIM), lambda i, j: (i, 0, 0)),
                   pl.BlockSpec((bt, DN_HEADS, DN_DK, DN_DV), lambda i, j: (i, 0, 0, 0))],
        out_shape=[jax.ShapeDtypeStruct((t, DN_VAL), BF16),
                   jax.ShapeDtypeStruct(conv_in.shape, F32),
                   jax.ShapeDtypeStruct(rec_in.shape, F32)],
        scratch_shapes=[pltpu.VMEM((bt, c + SUBLANES, DN_CONV_DIM), F32)],
        compiler_params=_cparams("arbitrary", "arbitrary"),
        name="gated_delta",
    )(proj, proj, small, conv_w, _small_row(a_log, SM_A), _small_row(dt_bias, SM_A),
      norm_w.reshape(1, DN_DV), conv_in, rec_in)


def _ssd_kernel(xbc_ref, z_ref, sm_ref, cw_ref, cb_ref, alog_ref, bias_ref, dvec_ref, nw_ref, cin_ref,
                hin_ref, y_ref, cout_ref, hout_ref, ext_ref, *, bt, r, c, lc, ng, cpi):
    j = pl.program_id(1)

    @pl.when(j == 0)
    def _():
        ext_ref[:, CARRY_ROW:SUBLANES, :] = cin_ref[...]
        hout_ref[...] = hin_ref[...]

    tril_bf = (_iota((c, c), 0) >= _iota((c, c), 1)).astype(BF16)
    row2 = _iota((c, 2 * c), 0)
    lane2 = _iota((c, 2 * c), 1)
    left2 = lane2 < c
    tri2 = row2 >= jnp.where(left2, lane2, lane2 - c)
    left_x = _iota((c, LANES), 1) < SSM_HEADDIM
    top_h = _iota((2 * SSM_HEADDIM, 1), 0) < SSM_HEADDIM
    valid = _iota((c, 1), 0) < lc
    nexp_a = -jnp.exp(alog_ref[...])
    bias = bias_ref[...]
    gw = SSM_INNER // SSM_GROUPS
    pairs_per_group = SSM_HEADS // SSM_GROUPS // 2

    pairs = range(SSM_HEADS // 2)
    pair_rows = lambda p: slice(p * LANES, (p + 1) * LANES)
    read_z, read_sm = (_chunk_reader(ref, r, c) for ref in (z_ref, sm_ref))
    put_y, flush_y = _chunk_writer(y_ref, bt, r, c)

    def prep(bb, r0, y):
        dt_all = _softplus(read_sm(bb, r0) + bias)
        if lc < c:
            dt_all = jnp.where(valid, dt_all, 0.0)
        cs_all = _dot_exact_lhs(tril_bf, dt_all * nexp_a)
        cs_t2 = jnp.concatenate([cs_all, cs_all], axis=0).T
        last = cs_all[c - 1:c, :]
        bm = [y[:, SSM_INNER + g * SSM_STATE:SSM_INNER + (g + 1) * SSM_STATE].astype(BF16)
              for g in range(SSM_GROUPS)]
        cm = [y[:, SSM_INNER + (SSM_GROUPS + g) * SSM_STATE:SSM_INNER + (SSM_GROUPS + g + 1) * SSM_STATE].astype(BF16)
              for g in range(SSM_GROUPS)]
        return dict(y=y, dt=dt_all, cs=cs_all, cs_t2=cs_t2, ecs=jnp.exp(cs_all),
                    tail=jnp.exp(last - cs_all), elast=jnp.exp(last), bm=bm, cm=cm)

    def process(rounds):
        items = [it for rnd in rounds for it in rnd]
        conv = _conv_rounds(xbc_ref, rounds, ext_ref, cw_ref, r, c, lc, bias=cb_ref[...])
        pre = [prep(bb, r0, y) for (bb, r0), y in zip(items, conv)]
        ids = range(len(items))
        cb2 = {(i, g): _dot_nt(pre[i]["cm"][g], jnp.concatenate([pre[i]["bm"][g]] * 2, axis=0))
               for i in ids for g in range(SSM_GROUPS)}
        y_intra, upd, xs_of = {}, {}, {}
        for i in ids:
            a = pre[i]
            for p in pairs:
                g = p // pairs_per_group
                la, lb = SM_DT + 2 * p, SM_DT + 2 * p + 1
                both = lambda v, mask=left_x: jnp.where(mask, v[:, la:la + 1], v[:, lb:lb + 1])
                xs = a["y"][:, pair_rows(p)]
                xdt = xs * both(a["dt"])
                diff = both(a["cs"], left2) - jnp.where(left2[0:1], a["cs_t2"][la:la + 1, :], a["cs_t2"][lb:lb + 1, :])
                decay2 = jnp.exp(jnp.where(tri2, diff, -jnp.inf))
                rhs = jnp.concatenate([jnp.where(left_x, xdt, 0.0), jnp.where(left_x, 0.0, xdt)], axis=0)
                y_intra[i, p] = _dot(cb2[i, g] * decay2, rhs)
                upd[i, p] = _dot_tn(xdt * both(a["tail"]), a["bm"][g])
                xs_of[i, p] = xs
        first = 0
        for rnd in rounds:
            idx = range(first, first + len(rnd))
            first += len(rnd)
            for i in idx:
                a = pre[i]
                bb, r0 = items[i]
                outs = []
                for p in pairs:
                    g = p // pairs_per_group
                    la, lb = SM_DT + 2 * p, SM_DT + 2 * p + 1
                    hs = hout_ref[bb, pair_rows(p), :]
                    y_inter = _dot_nt(a["cm"][g], hs) * jnp.where(left_x, a["ecs"][:, la:la + 1], a["ecs"][:, lb:lb + 1])
                    hout_ref[bb, pair_rows(p), :] = (
                        jnp.where(top_h, a["elast"][:, la:la + 1], a["elast"][:, lb:lb + 1]) * hs + upd[i, p])
                    outs.append(y_intra[i, p] + y_inter + dvec_ref[:, pair_rows(p)] * xs_of[i, p])
                yz = jnp.concatenate(outs, axis=1) * _silu(read_z(bb, r0))
                put_y(bb, r0, jnp.concatenate(
                    [_rms(yz[:, g * gw:(g + 1) * gw], nw_ref[:, g * gw:(g + 1) * gw]) for g in range(SSM_GROUPS)],
                    axis=1))

    _run_rounds(process, bt, r, c, cpi)
    flush_y()

    @pl.when(j == ng - 1)
    def _():
        cout_ref[...] = ext_ref[:, CARRY_ROW:SUBLANES, :]


def _ssd_branch(proj, small, conv_w, conv_b, a_log, dt_bias, d_skip, norm_w, conv_in, rec_in, *, bt, r, c, lc, ng, cpi):
    nb = conv_in.shape[0] // bt
    rows = bt * r
    t = proj.shape[0]
    hrows = SSM_HEADS * SSM_HEADDIM
    rec2 = rec_in.reshape(rec_in.shape[0], hrows, SSM_STATE)
    kern = functools.partial(_ssd_kernel, bt=bt, r=r, c=c, lc=lc, ng=ng, cpi=cpi)
    rowmap = lambda col: (lambda i, j: (i * ng + j, col))
    const = lambda i, j: (0, 0)
    y, conv_out, rec_out = pl.pallas_call(
        kern,
        grid=(nb, ng),
        in_specs=[pl.BlockSpec((rows, SSM_CONV_DIM), rowmap(COL_SSM_XBC // SSM_CONV_DIM)),
                  pl.BlockSpec((rows, SSM_INNER), rowmap(COL_SSM_Z // SSM_INNER)),
                  pl.BlockSpec((rows, LANES), rowmap(0)),
                  pl.BlockSpec((CONV_K, SSM_CONV_DIM), const),
                  pl.BlockSpec((1, SSM_CONV_DIM), const),
                  pl.BlockSpec((1, LANES), const),
                  pl.BlockSpec((1, LANES), const),
                  pl.BlockSpec((1, SSM_INNER), const),
                  pl.BlockSpec((1, SSM_INNER), const),
                  pl.BlockSpec((bt, CONV_K - 1, SSM_CONV_DIM), lambda i, j: (i, 0, 0)),
                  pl.BlockSpec((bt, hrows, SSM_STATE), lambda i, j: (i, 0, 0))],
        out_specs=[pl.BlockSpec((rows, SSM_INNER), lambda i, j: (i * ng + j, 0)),
                   pl.BlockSpec((bt, CONV_K - 1, SSM_CONV_DIM), lambda i, j: (i, 0, 0)),
                   pl.BlockSpec((bt, hrows, SSM_STATE), lambda i, j: (i, 0, 0))],
        out_shape=[jax.ShapeDtypeStruct((t, SSM_INNER), BF16),
                   jax.ShapeDtypeStruct(conv_in.shape, F32),
                   jax.ShapeDtypeStruct(rec2.shape, F32)],
        scratch_shapes=[pltpu.VMEM((bt, c + SUBLANES, SSM_CONV_DIM), F32)],
        compiler_params=_cparams("arbitrary", "arbitrary"),
        name="ssd_scan",
    )(proj, proj, small, conv_w, conv_b.reshape(1, -1), _small_row(a_log, SM_DT), _small_row(dt_bias, SM_DT),
      jnp.repeat(d_skip.astype(F32), SSM_HEADDIM).reshape(1, SSM_INNER), norm_w.reshape(1, SSM_INNER),
      conv_in, rec2)
    return y, conv_out, rec_out.reshape(rec_in.shape)


PLAN_E0, PLAN_E1, PLAN_C0, PLAN_C1 = 0, 1, 2, 3


def _route(logits):
    lane = _iota(logits.shape, 1).astype(F32)
    big = float(LANES)
    is_group = (lane >= RT_GROUP) & (lane < RT_GROUP + MOE_GROUPS)
    gl = jnp.where(is_group, logits, -jnp.inf)
    gmax = jnp.max(gl, axis=-1, keepdims=True)
    g_sel = jnp.min(jnp.where(gl == gmax, lane, big), axis=-1, keepdims=True) - RT_GROUP
    p_group = 1.0 / jnp.sum(jnp.exp(gl - gmax), axis=-1, keepdims=True)
    e_lo = RT_EXPERT + MOE_PER_GROUP * g_sel
    in_grp = (lane >= e_lo) & (lane < e_lo + MOE_PER_GROUP)
    el = jnp.where(in_grp, logits, -jnp.inf)
    ee = jnp.exp(el - jnp.max(el, axis=-1, keepdims=True))
    pe = jnp.where(in_grp, ee / jnp.sum(ee, axis=-1, keepdims=True), -1.0)
    p1 = jnp.max(pe, axis=-1, keepdims=True)
    i1 = jnp.min(jnp.where(pe == p1, lane, big), axis=-1, keepdims=True)
    pe2 = jnp.where(lane == i1, -1.0, pe)
    p2 = jnp.max(pe2, axis=-1, keepdims=True)
    i2 = jnp.min(jnp.where(pe2 == p2, lane, big), axis=-1, keepdims=True)
    tot = p1 + p2
    plan = jnp.where(lane == PLAN_E0, i1 - RT_EXPERT, 0.0) + jnp.where(lane == PLAN_E1, i2 - RT_EXPERT, 0.0)
    return plan + jnp.where(lane == PLAN_C0, p_group * p1 / tot, 0.0) + jnp.where(lane == PLAN_C1, p_group * p2 / tot, 0.0)


def _tile_ranks(plan):
    tp = plan.shape[0]
    lane = _iota(plan.shape, 1).astype(F32)
    sel0 = lane == plan[:, PLAN_E0:PLAN_E0 + 1]
    sel1 = lane == plan[:, PLAN_E1:PLAN_E1 + 1]
    sel = (sel0 | sel1).astype(BF16)
    before = (_iota((tp, tp), 0) > _iota((tp, tp), 1)).astype(BF16)
    excl = jnp.dot(before, sel, preferred_element_type=F32)
    r0 = jnp.sum(jnp.where(sel0, excl, 0.0), axis=-1, keepdims=True)
    r1 = jnp.sum(jnp.where(sel1, excl, 0.0), axis=-1, keepdims=True)
    rank = jnp.where(lane == PLAN_E0, r0, 0.0) + jnp.where(lane == PLAN_E1, r1, 0.0)
    return rank, jnp.sum(sel.astype(F32), axis=0, keepdims=True)


def _merge_kernel(x_ref, og_ref, ys_ref, gates_ref, gt1_ref, sc2_ref, sh2_ref, nw2_ref, wdn_ref, wssm_ref,
                  wout_ref, wr_ref, br_ref, x1_ref, h2_ref, plan_ref, rank_ref, cnt_ref):
    y_dn = _dot(og_ref[...], wdn_ref[...])
    y_ssm = _dot(ys_ref[...], wssm_ref[...])
    merged = (_sigmoid(gates_ref[:, :D_MODEL].astype(F32)) * y_dn
              + _sigmoid(gates_ref[:, D_MODEL:].astype(F32)) * y_ssm)
    x1 = x_ref[...] + gt1_ref[...] * _dot(merged, wout_ref[...])
    x1_ref[...] = x1
    h2 = _rms(x1, nw2_ref[...]) * (1.0 + sc2_ref[...]) + sh2_ref[...]
    h2_ref[...] = h2.astype(BF16)
    plan = _route(_dot_x3(h2, wr_ref[...]) + br_ref[...])
    plan_ref[...] = plan
    rank_ref[...], cnt_ref[...] = _tile_ranks(plan)


def _merge(x3, og, ys, proj, gt1, sc2, sh2, nw2, wdn, wssm, wout, wr, br, tm):
    bx, lx, d = x3.shape
    assert tm == TOK_TILE
    nl = lx // tm
    rowmap = lambda col: (lambda b, i: (b * nl + i, col))
    const = lambda b, i: (0, 0)
    tok = lambda w, dt: jax.ShapeDtypeStruct((bx, lx, w), dt)
    return pl.pallas_call(
        _merge_kernel,
        grid=(bx, nl),
        in_specs=[pl.BlockSpec((None, tm, d), lambda b, i: (b, i, 0)),
                  pl.BlockSpec((tm, DN_VAL), rowmap(0)),
                  pl.BlockSpec((tm, SSM_INNER), rowmap(0)),
                  pl.BlockSpec((tm, 2 * d), rowmap(COL_GATES // (2 * d))),
                  _mod_spec(gt1, tm), _mod_spec(sc2, tm), _mod_spec(sh2, tm),
                  pl.BlockSpec((1, d), const),
                  pl.BlockSpec(wdn.shape, const), pl.BlockSpec(wssm.shape, const), pl.BlockSpec(wout.shape, const),
                  pl.BlockSpec(wr.shape, const), pl.BlockSpec((1, LANES), const)],
        out_specs=[pl.BlockSpec((None, tm, d), lambda b, i: (b, i, 0)),
                   pl.BlockSpec((None, tm, d), lambda b, i: (b, i, 0)),
                   pl.BlockSpec((None, tm, LANES), lambda b, i: (b, i, 0)),
                   pl.BlockSpec((None, tm, LANES), lambda b, i: (b, i, 0)),
                   pl.BlockSpec((None, 1, LANES), lambda b, i: (b * nl + i, 0, 0))],
        out_shape=[tok(d, F32), tok(d, BF16), tok(LANES, F32), tok(LANES, F32),
                   jax.ShapeDtypeStruct((bx * nl, 1, LANES), F32)],
        compiler_params=_cparams("arbitrary", "arbitrary"),
        name="merge_route",
    )(x3, og, ys, proj, gt1, sc2, sh2, nw2.reshape(1, d), wdn, wssm, wout, wr, br)


FFN_ROWS = 512
TOK_TILE = 512
RUN_ALIGN = 2 * SUBLANES
RUN_BIG = 4 * RUN_ALIGN
RUN_SMALL = (2 * RUN_ALIGN, RUN_ALIGN)
LOCAL_ROWS = 2 * TOK_TILE + MOE_EXPERTS * RUN_ALIGN


def _local_slots(plan, rank, off_row):
    lane = _iota(plan.shape, 1).astype(F32)
    slots = []
    for k in (PLAN_E0, PLAN_E1):
        off = jnp.sum(jnp.where(lane == plan[:, k:k + 1], off_row, 0.0), axis=-1, keepdims=True)
        slots.append(off + rank[:, k:k + 1])
    return slots


def _run_blocks(p8_ref, loff_ref, base_ref, act):
    for e in range(MOE_EXPERTS):
        n = p8_ref[0, e]
        lo = loff_ref[0, e]
        go = base_ref[0, e]

        def big(k, carry, lo=lo, go=go):
            off = k * RUN_BIG
            act(pl.multiple_of(lo + off, RUN_ALIGN), pl.multiple_of(go + off, RUN_ALIGN), RUN_BIG)
            return carry

        lax.fori_loop(0, lax.shift_right_logical(n, RUN_BIG.bit_length() - 1), big, 0)
        for size in RUN_SMALL:
            @pl.when(jnp.bitwise_and(n, size) != 0)
            def _(size=size, n=n, lo=lo, go=go):
                done = jnp.bitwise_and(n, ~(2 * size - 1))
                act(pl.multiple_of(lo + done, RUN_ALIGN), pl.multiple_of(go + done, RUN_ALIGN), size)


def _dispatch_kernel(tv_ref, p8_ref, loff_ref, base_ref, p8p_ref, loffp_ref, basep_ref, plan_ref, rank_ref, off_ref,
                     h2_ref, *rest, first):
    xs_ref, buf_ref, zero_ref, sem, zsem = rest[-5:]
    step = pl.program_id(0)
    last = pl.num_programs(0) - 1
    slot = jnp.bitwise_and(step, 1)

    @pl.when((pl.program_id(0) == 0) & first)
    def _():
        zero_ref[...] = jnp.zeros_like(zero_ref)

        def zero_copy(i):
            rows = pl.ds(pl.multiple_of(i * FFN_ROWS, FFN_ROWS), FFN_ROWS)
            return pltpu.make_async_copy(zero_ref, xs_ref.at[rows], zsem)

        def zero_tiles(act):
            def body(i, carry):
                @pl.when(tv_ref[i] < FFN_ROWS)
                def _():
                    act(zero_copy(i))
                return carry
            lax.fori_loop(0, xs_ref.shape[0] // FFN_ROWS, body, 0)

        zero_tiles(lambda cp: cp.start())
        zero_tiles(lambda cp: cp.wait())

    s0, s1 = _local_slots(plan_ref[...], rank_ref[...], off_ref[...])
    row = _iota((1, LOCAL_ROWS), 1).astype(F32)
    onehot_t = ((row == s0) | (row == s1)).astype(BF16)
    buf_ref[slot] = lax.dot_general(onehot_t, h2_ref[...], (((0,), (0,)), ((), ())),
                                    preferred_element_type=F32).astype(BF16)

    def copy(k, lo, go, size):
        return pltpu.make_async_copy(buf_ref.at[k, pl.ds(lo, size)], xs_ref.at[pl.ds(go, size)], sem.at[k])

    _run_blocks(p8_ref, loff_ref, base_ref, lambda lo, go, size: copy(slot, lo, go, size).start())

    @pl.when(step > 0)
    def _():
        _run_blocks(p8p_ref, loffp_ref, basep_ref, lambda lo, go, size: copy(1 - slot, lo, go, size).wait())

    @pl.when(step == last)
    def _():
        _run_blocks(p8_ref, loff_ref, base_ref, lambda lo, go, size: copy(slot, lo, go, size).wait())


def _tile_scalars():
    return pl.BlockSpec((None, 1, LANES), lambda i, *_: (i, 0, 0), memory_space=pltpu.SMEM)


def _dispatch(tile_valid, p8, loff, base, plan2, rank, off_f, h2, rows, xs_prev=None):
    t, d = h2.shape
    tt = TOK_TILE
    tok = lambda w: pl.BlockSpec((tt, w), lambda i, tv: (i, 0))
    operands = ((tile_valid, p8, loff, base, p8, loff, base, plan2, rank, off_f, h2)
                + (() if xs_prev is None else (xs_prev,)))
    prev = pl.BlockSpec((None, 1, LANES), lambda i, tv: (jnp.maximum(i - 1, 0), 0, 0), memory_space=pltpu.SMEM)
    return pl.pallas_call(
        functools.partial(_dispatch_kernel, first=xs_prev is None),
        grid_spec=pltpu.PrefetchScalarGridSpec(
            num_scalar_prefetch=1,
            grid=(t // tt,),
            in_specs=[_tile_scalars(), _tile_scalars(), _tile_scalars(), prev, prev, prev,
                      tok(LANES), tok(LANES),
                      pl.BlockSpec((None, 1, LANES), lambda i, tv: (i, 0, 0)),
                      tok(d)] + ([] if xs_prev is None else [pl.BlockSpec(memory_space=pl.ANY)]),
            out_specs=pl.BlockSpec(memory_space=pl.ANY),
            scratch_shapes=[pltpu.VMEM((2, LOCAL_ROWS, d), BF16), pltpu.VMEM((FFN_ROWS, d), BF16),
                            pltpu.SemaphoreType.DMA((2,)), pltpu.SemaphoreType.DMA(())]),
        out_shape=jax.ShapeDtypeStruct((rows, d), BF16),
        input_output_aliases={} if xs_prev is None else {len(operands) - 1: 0},
        compiler_params=_cparams("arbitrary"),
        name="moe_dispatch",
    )(*operands)


def _ffn_kernel(te_ref, tv_ref, src_ref, xs_ref, wg_ref, wu_ref, wd_ref, o_ref):
    nv = tv_ref[pl.program_id(0)]

    half = FFN_ROWS // 2

    def swiglu(rows):
        x = xs_ref[rows, :]
        o_ref[rows, :] = _dot(_silu(_dot(x, wg_ref[...])) * _dot(x, wu_ref[...]), wd_ref[...]).astype(BF16)

    @pl.when(nv > half)
    def _():
        swiglu(slice(None))

    @pl.when((nv > 0) & (nv <= half))
    def _():
        swiglu(slice(0, half))
        o_ref[half:, :] = jnp.zeros((half, o_ref.shape[1]), o_ref.dtype)

    @pl.when(nv <= 0)
    def _():
        o_ref[...] = jnp.zeros_like(o_ref)


def _ffn(tile_expert, tile_valid, tile_src, xs, wg, wu, wd):
    rows, d = xs.shape
    wmap = lambda i, te, tv, src: (te[i], 0, 0)
    return pl.pallas_call(
        _ffn_kernel,
        grid_spec=pltpu.PrefetchScalarGridSpec(
            num_scalar_prefetch=3,
            grid=(rows // FFN_ROWS,),
            in_specs=[pl.BlockSpec((FFN_ROWS, d), lambda i, te, tv, src: (src[i], 0)),
                      pl.BlockSpec((None, d, MOE_FF), wmap),
                      pl.BlockSpec((None, d, MOE_FF), wmap),
                      pl.BlockSpec((None, MOE_FF, d), wmap)],
            out_specs=pl.BlockSpec((FFN_ROWS, d), lambda i, te, tv, src: (i, 0))),
        out_shape=jax.ShapeDtypeStruct((rows, d), BF16),
        compiler_params=_cparams("arbitrary"),
        name="moe_ffn",
    )(tile_expert, tile_valid, tile_src, xs, wg, wu, wd)


def _combine_kernel(p8_ref, loff_ref, base_ref, p8n_ref, loffn_ref, basen_ref, ys_ref, plan_ref, rank_ref, off_ref,
                    x1_ref, gt2_ref, fsc_ref, fsh_ref, fnw_ref, y_ref, buf_ref, sem):
    step = pl.program_id(0) * pl.num_programs(1) + pl.program_id(1)
    last = pl.num_programs(0) * pl.num_programs(1) - 1
    slot = jnp.bitwise_and(step, 1)

    def copy(k, lo, go, size):
        return pltpu.make_async_copy(ys_ref.at[pl.ds(go, size)], buf_ref.at[k, pl.ds(lo, size)], sem.at[k])

    @pl.when(step == 0)
    def _():
        buf_ref[...] = jnp.zeros_like(buf_ref)
        _run_blocks(p8_ref, loff_ref, base_ref, lambda lo, go, size: copy(0, lo, go, size).start())

    @pl.when(step < last)
    def _():
        _run_blocks(p8n_ref, loffn_ref, basen_ref, lambda lo, go, size: copy(1 - slot, lo, go, size).start())

    plan = plan_ref[...]
    s0, s1 = _local_slots(plan, rank_ref[...], off_ref[...])
    row = _iota((1, LOCAL_ROWS), 1).astype(F32)
    weights = (jnp.where(row == s0, plan[:, PLAN_C0:PLAN_C0 + 1], 0.0)
               + jnp.where(row == s1, plan[:, PLAN_C1:PLAN_C1 + 1], 0.0))
    _run_blocks(p8_ref, loff_ref, base_ref, lambda lo, go, size: copy(slot, lo, go, size).wait())
    total = loff_ref[0, MOE_EXPERTS - 1] + p8_ref[0, MOE_EXPERTS - 1]
    filled = _iota((LOCAL_ROWS, 1), 0) < total
    moe = _dot(weights, jnp.where(filled, buf_ref[slot], jnp.zeros((), BF16)))
    x2 = x1_ref[...] + gt2_ref[...] * moe
    y_ref[...] = _rms(x2, fnw_ref[...]) * (1.0 + fsc_ref[...]) + fsh_ref[...]


def _combine(p8, loff, base, ys, plan, rank3, off_f, x1, gt2, fsc, fsh, fnw):
    bx, lx, d = x1.shape
    tt = TOK_TILE
    nl = lx // tt
    tile = lambda b, i: b * nl + i
    scal = pl.BlockSpec((None, 1, LANES), lambda b, i: (tile(b, i), 0, 0), memory_space=pltpu.SMEM)
    scal_next = pl.BlockSpec((None, 1, LANES), lambda b, i: (jnp.minimum(tile(b, i) + 1, bx * nl - 1), 0, 0),
                             memory_space=pltpu.SMEM)
    tokspec = lambda w: pl.BlockSpec((None, tt, w), lambda b, i: (b, i, 0))
    return pl.pallas_call(
        _combine_kernel,
        grid=(bx, nl),
        in_specs=[scal, scal, scal, scal_next, scal_next, scal_next,
                  pl.BlockSpec(memory_space=pl.ANY),
                  tokspec(LANES), tokspec(LANES),
                  pl.BlockSpec((None, 1, LANES), lambda b, i: (tile(b, i), 0, 0)),
                  tokspec(d), _mod_spec(gt2, tt), _mod_spec(fsc, tt), _mod_spec(fsh, tt),
                  pl.BlockSpec((1, d), lambda b, i: (0, 0))],
        out_specs=tokspec(d),
        out_shape=jax.ShapeDtypeStruct((bx, lx, d), F32),
        scratch_shapes=[pltpu.VMEM((2, LOCAL_ROWS, d), BF16), pltpu.SemaphoreType.DMA((2,))],
        compiler_params=_cparams("arbitrary", "arbitrary"),
        name="moe_combine_final",
    )(p8, loff, base, p8, loff, base, ys, plan, rank3, off_f, x1, gt2, fsc, fsh, fnw.reshape(1, d))


def _moe(groups, wg, wu, wd, fnw):
    d = groups[0]["x1"].shape[-1]
    ntiles = [g["x1"].shape[0] * g["x1"].shape[1] // TOK_TILE for g in groups]
    t = sum(ntiles) * TOK_TILE
    n = jnp.concatenate([g["cnt"][:, 0, :MOE_EXPERTS] for g in groups], axis=0).astype(jnp.int32)
    p8 = (n + RUN_ALIGN - 1) // RUN_ALIGN * RUN_ALIGN
    loff = jnp.cumsum(p8, axis=1) - p8
    erows = jnp.sum(p8, axis=0)
    epad = (erows + FFN_ROWS - 1) // FFN_ROWS * FFN_ROWS
    ends = jnp.cumsum(epad)
    starts = ends - epad
    base = starts[None, :] + jnp.cumsum(p8, axis=0) - p8
    lanes = lambda a: jnp.pad(a, ((0, 0), (0, LANES - MOE_EXPERTS)))[:, None, :]
    rows = (-(-(2 * t + (t // TOK_TILE) * MOE_EXPERTS * (RUN_ALIGN - 1)) // FFN_ROWS) + MOE_EXPERTS) * FFN_ROWS
    tile_start = jnp.arange(rows // FFN_ROWS, dtype=jnp.int32) * FFN_ROWS
    tile_expert = jnp.minimum(jnp.sum(tile_start[:, None] >= ends[None, :], axis=1), MOE_EXPERTS - 1).astype(jnp.int32)
    hot = tile_expert[:, None] == jnp.arange(MOE_EXPERTS)[None, :]
    valid_rows = lambda er: jnp.clip(jnp.sum(jnp.where(hot, er - (tile_start[:, None] - starts), 0), axis=1),
                                     0, FFN_ROWS).astype(jnp.int32)
    tile_valid = valid_rows(erows)
    p8l, loffl, basel = lanes(p8), lanes(loff), lanes(base)
    off_f = loffl.astype(F32)
    spans, first = [], 0
    for nt in ntiles:
        spans.append(slice(first, first + nt))
        first += nt
    first_valid = valid_rows(jnp.sum(p8[spans[0]], axis=0))
    xs = None
    for g, sp in zip(groups, spans):
        tg = g["h2"].shape[0] * g["h2"].shape[1]
        xs = _dispatch(first_valid, p8l[sp], loffl[sp], basel[sp], g["plan"].reshape(tg, LANES),
                       g["rank"].reshape(tg, LANES), off_f[sp], g["h2"].reshape(tg, d), rows, xs_prev=xs)
    tile_id = tile_start // FFN_ROWS
    tile_src = jnp.where(tile_valid > 0, tile_id, jnp.max(jnp.where(tile_valid > 0, tile_id, 0))).astype(jnp.int32)
    ys = _ffn(tile_expert, tile_valid, tile_src, xs, wg, wu, wd)
    return [_combine(p8l[sp], loffl[sp], basel[sp], ys, g["plan"], g["rank"], off_f[sp], g["x1"], g["gt2"],
                     g["fsc"], g["fsh"], fnw) for g, sp in zip(groups, spans)]


def _prep_layer(lp):
    w_in = lp["w_in"]
    offs = [0]
    for s in (DN_CONV_DIM, DN_VAL, DN_HEADS, DN_HEADS, SSM_CONV_DIM, SSM_INNER, SSM_HEADS, D_MODEL, D_MODEL):
        offs.append(offs[-1] + s)
    seg = lambda i: w_in[:, offs[i]:offs[i + 1]]
    small = jnp.concatenate([seg(2), seg(3), seg(6)], axis=1)
    pad = jnp.zeros((D_MODEL, PROJ_N - COL_SMALL - small.shape[1]), F32)
    w_cat = jnp.concatenate([seg(0), seg(4), seg(5), seg(7), seg(8), seg(1), small, pad], axis=1).astype(BF16)
    wr = jnp.concatenate([lp["w_group_router"], lp["w_expert_router"],
                          jnp.zeros((D_MODEL, LANES - MOE_GROUPS - MOE_EXPERTS), F32)], axis=1)
    br = jnp.concatenate([lp["b_group_router"], lp["b_expert_router"],
                          jnp.zeros((LANES - MOE_GROUPS - MOE_EXPERTS,), F32)]).reshape(1, LANES)
    return dict(lp, w_cat=w_cat, wr=wr, br=br,
                wdn=lp["w_dn_out"].astype(BF16), wssm=lp["w_ssm_out"].astype(BF16), wout=lp["w_out"].astype(BF16),
                wg=lp["w_exp_gate"], wu=lp["w_exp_up"], wd=lp["w_exp_down"])


def _mixer(x3, mods, fins, states, lp, cfg):
    bx, lx, d = x3.shape
    scan = dict(bt=cfg["bt"], r=cfg["r"], c=cfg["c"], lc=cfg["lc"], ng=cfg["ng"])
    sh1, sc1, gt1, sh2, sc2, gt2 = mods
    dn_conv, dn_rec, ssm_conv, ssm_rec = states
    proj, small = _inproj(x3, sc1, sh1, lp["norm_mix_w"], lp["w_cat"], cfg["tm"])
    proj, small = proj.reshape(bx * lx, PROJ_N), small.reshape(bx * lx, LANES)
    og, dn_conv_new, dn_rec_new = _dn_branch(proj, small, lp["dn_conv_w"], lp["dn_A_log"], lp["dn_dt_bias"],
                                             lp["dn_norm_w"], dn_conv, dn_rec, cpi=cfg["cpi_dn"], **scan)
    ys, ssm_conv_new, ssm_rec_new = _ssd_branch(proj, small, lp["ssm_conv_w"], lp["ssm_conv_b"], lp["ssm_A_log"],
                                                lp["ssm_dt_bias"], lp["ssm_D"], lp["ssm_norm_w"],
                                                ssm_conv, ssm_rec, cpi=cfg["cpi_ssd"], **scan)
    x1, h2, plan, rank, cnt = _merge(x3, og, ys, proj, gt1, sc2, sh2, lp["norm_ffn_w"], lp["wdn"], lp["wssm"],
                                     lp["wout"], lp["wr"], lp["br"], cfg["tm_merge"])
    group = dict(x1=x1, h2=h2, plan=plan, rank=rank, cnt=cnt, gt2=gt2, fsh=fins[0], fsc=fins[1])
    return group, (dn_conv_new, dn_rec_new, ssm_conv_new, ssm_rec_new)


def _per_seq(m):
    return m[:, None, :]


def kernel(x_prompt, x_sample, c_prompt, c_sample, state_dn_conv, state_dn_rec, state_ssm_conv, state_ssm_rec, w_ada, b_ada, norm_mix_w, w_in, dn_conv_w, dn_A_log, dn_dt_bias, dn_norm_w, w_dn_out, ssm_conv_w, ssm_conv_b, ssm_A_log, ssm_dt_bias, ssm_D, ssm_norm_w, w_ssm_out, w_out, norm_ffn_w, w_group_router, b_group_router, w_expert_router, b_expert_router, w_exp_gate, w_exp_up, w_exp_down, w_ada_final, b_ada_final, final_norm_w):
    depth = w_ada.shape[0]
    assert depth == 1
    per_layer = dict(w_ada=w_ada, b_ada=b_ada, norm_mix_w=norm_mix_w, w_in=w_in, dn_conv_w=dn_conv_w,
                     dn_A_log=dn_A_log, dn_dt_bias=dn_dt_bias, dn_norm_w=dn_norm_w, w_dn_out=w_dn_out,
                     ssm_conv_w=ssm_conv_w, ssm_conv_b=ssm_conv_b, ssm_A_log=ssm_A_log, ssm_dt_bias=ssm_dt_bias,
                     ssm_D=ssm_D, ssm_norm_w=ssm_norm_w, w_ssm_out=w_ssm_out, w_out=w_out, norm_ffn_w=norm_ffn_w,
                     w_group_router=w_group_router, b_group_router=b_group_router,
                     w_expert_router=w_expert_router, b_expert_router=b_expert_router,
                     w_exp_gate=w_exp_gate, w_exp_up=w_exp_up, w_exp_down=w_exp_down)
    layers = [_prep_layer({k: v[l] for k, v in per_layer.items()}) for l in range(depth)]

    nbp, lp_, d = x_prompt.shape
    nbs, ls, _ = x_sample.shape
    c_all = jnp.concatenate([c_prompt, c_sample], axis=0)
    mod_all = [_ada(c_all, lyr["w_ada"], lyr["b_ada"]) for lyr in layers]
    fin_all = _ada(c_all, w_ada_final, b_ada_final)

    mods_p = [[_per_seq(m) for m in jnp.split(ma[:nbp], 6, axis=-1)] for ma in mod_all]
    fins_p = [_per_seq(m) for m in jnp.split(fin_all[:nbp], 2, axis=-1)]
    zeros_p = [(jnp.zeros((nbp, CONV_K - 1, DN_CONV_DIM), F32), jnp.zeros((nbp, DN_HEADS, DN_DK, DN_DV), F32),
                jnp.zeros((nbp, CONV_K - 1, SSM_CONV_DIM), F32),
                jnp.zeros((nbp, SSM_HEADS, SSM_HEADDIM, SSM_STATE), F32)) for _ in range(depth)]
    c_p = min(SCAN_CHUNK, lp_)
    r_p = min(lp_, 16 * c_p)
    cfg_p = dict(tm=min(lp_, 2048), tm_merge=min(lp_, TOK_TILE), bt=1, r=r_p, c=c_p, lc=c_p, ng=lp_ // r_p,
                 cpi_dn=8 if (r_p // c_p) % 8 == 0 else 1, cpi_ssd=4 if (r_p // c_p) % 4 == 0 else 1)
    grp_p, st_p = _mixer(x_prompt, mods_p[0], fins_p, zeros_p[0], layers[0], cfg_p)

    lpad = -(-ls // SUBLANES) * SUBLANES
    bt_s = 16
    xs = jnp.pad(x_sample, ((0, 0), (0, lpad - ls), (0, 0))).reshape(1, nbs * lpad, d)
    per_tok = lambda m: jnp.repeat(m, lpad, axis=0)[None]
    mods_s = [[per_tok(m) for m in jnp.split(ma[nbp:], 6, axis=-1)] for ma in mod_all]
    fins_s = [per_tok(m) for m in jnp.split(fin_all[nbp:], 2, axis=-1)]
    st_in = [(state_dn_conv[l], state_dn_rec[l], state_ssm_conv[l], state_ssm_rec[l]) for l in range(depth)]
    ts = nbs * lpad
    cfg_s = dict(tm=ts, tm_merge=min(ts, TOK_TILE), bt=bt_s, r=lpad, c=lpad, lc=ls, ng=1, cpi_dn=1, cpi_ssd=1)
    grp_s, st_s = _mixer(xs, mods_s[0], fins_s, st_in[0], layers[0], cfg_s)

    y_p, y_s = _moe([grp_p, grp_s], layers[0]["wg"], layers[0]["wu"], layers[0]["wd"], final_norm_w)
    y_s = y_s.reshape(nbs, lpad, d)[:, :ls]
    return (y_p, y_s) + tuple(st[None] for st in st_p) + tuple(st[None] for st in st_s)
```

```python
import functools
import math

import jax
import jax.numpy as jnp
from jax import lax
from jax.experimental import pallas as pl
from jax.experimental.pallas import tpu as pltpu

F32 = jnp.float32
BF16 = jnp.bfloat16

D_MODEL = 1024
DN_HEADS = 4
DN_DK = 128
DN_DV = 128
DN_KEY = DN_HEADS * DN_DK
DN_VAL = DN_HEADS * DN_DV
CONV_K = 4
DN_CONV_DIM = 2 * DN_KEY + DN_VAL
SSM_INNER = D_MODEL
SSM_HEADDIM = 64
SSM_HEADS = SSM_INNER // SSM_HEADDIM
SSM_GROUPS = 2
SSM_STATE = 128
SSM_CONV_DIM = SSM_INNER + 2 * SSM_GROUPS * SSM_STATE
MOE_GROUPS = 4
MOE_PER_GROUP = 8
MOE_EXPERTS = MOE_GROUPS * MOE_PER_GROUP
MOE_FF = D_MODEL // 4
EPS = 1e-6
SCAN_CHUNK = 64

LANES = 128
SUBLANES = 8
CARRY_ROW = SUBLANES - (CONV_K - 1)

COL_DN_QKV = 0
COL_SSM_XBC = COL_DN_QKV + DN_CONV_DIM
COL_SSM_Z = COL_SSM_XBC + SSM_CONV_DIM
COL_GATES = COL_SSM_Z + SSM_INNER
COL_DN_Z = COL_GATES + 2 * D_MODEL
COL_SMALL = COL_DN_Z + DN_VAL
PROJ_TN = 1792
PROJ_N = 4 * PROJ_TN
SM_A, SM_B, SM_DT = 0, DN_HEADS, 2 * DN_HEADS
RT_GROUP, RT_EXPERT = 0, MOE_GROUPS

VMEM_LIMIT = 56 * 1024 * 1024


def _cparams(*sem):
    return pltpu.CompilerParams(dimension_semantics=sem, vmem_limit_bytes=VMEM_LIMIT)


def _dot(a, b):
    return jnp.dot(a.astype(BF16), b.astype(BF16), preferred_element_type=F32)


def _dot_nt(a, b):
    return lax.dot_general(a.astype(BF16), b.astype(BF16), (((1,), (1,)), ((), ())),
                           preferred_element_type=F32)


def _dot_tn(a, b):
    return lax.dot_general(a.astype(BF16), b.astype(BF16), (((0,), (0,)), ((), ())),
                           preferred_element_type=F32)


def _split3(x):
    hi = x.astype(BF16)
    r = x - hi.astype(F32)
    mid = r.astype(BF16)
    lo = (r - mid.astype(F32)).astype(BF16)
    return hi, mid, lo


def _dot_exact_lhs(a_bf, b):
    hi, mid, lo = _split3(b)
    d = functools.partial(jnp.dot, preferred_element_type=F32)
    return d(a_bf, hi) + (d(a_bf, mid) + d(a_bf, lo))


def _dot_x3(a, b):
    a_hi = a.astype(BF16)
    a_lo = (a - a_hi.astype(F32)).astype(BF16)
    b_hi = b.astype(BF16)
    b_lo = (b - b_hi.astype(F32)).astype(BF16)
    d = functools.partial(jnp.dot, preferred_element_type=F32)
    n = b.shape[1]
    hi_both = d(a_hi, jnp.concatenate([b_hi, b_lo], axis=1))
    return hi_both[:, :n] + (hi_both[:, n:] + d(a_lo, b_hi))


def _sigmoid(x):
    return 1.0 / (1.0 + jnp.exp(-x))


def _silu(x):
    return x * _sigmoid(x)


def _softplus(x):
    return jnp.maximum(x, 0.0) + jnp.log1p(jnp.exp(-jnp.abs(x)))


def _rms(x, w):
    return x * lax.rsqrt(jnp.mean(x * x, axis=-1, keepdims=True) + EPS) * w


def _iota(shape, dim):
    return lax.broadcasted_iota(jnp.int32, shape, dim)


def _ada_kernel(c_ref, w_ref, b_ref, o_ref):
    o_ref[...] = _dot(_silu(c_ref[...]), w_ref[...]) + b_ref[...]


def _ada(c, w, b, tn=512):
    m, d = c.shape
    n = w.shape[1]
    return pl.pallas_call(
        _ada_kernel,
        grid=(n // tn,),
        in_specs=[pl.BlockSpec((m, d), lambda j: (0, 0)),
                  pl.BlockSpec((d, tn), lambda j: (0, j)),
                  pl.BlockSpec((1, tn), lambda j: (0, j))],
        out_specs=pl.BlockSpec((m, tn), lambda j: (0, j)),
        out_shape=jax.ShapeDtypeStruct((m, n), F32),
        compiler_params=_cparams("arbitrary"),
        name="ada_mod",
    )(c, w, b.reshape(1, n))


def _mod_spec(mod, tm):
    if mod.shape[1] == 1:
        return pl.BlockSpec((None, 1, D_MODEL), lambda b, i, *_: (b, 0, 0))
    return pl.BlockSpec((None, tm, D_MODEL), lambda b, i, *_: (b, i, 0))


def _inproj_kernel(x_ref, sc_ref, sh_ref, nw_ref, w_ref, o_ref, sm_ref, h_ref, *, tm, sub):
    @pl.when(pl.program_id(2) == 0)
    def _():
        per_token = sc_ref.shape[0] != 1

        def body(r, carry):
            rows = pl.ds(pl.multiple_of(r * sub, sub), sub)
            sc = sc_ref[rows, :] if per_token else sc_ref[...]
            sh = sh_ref[rows, :] if per_token else sh_ref[...]
            h = _rms(x_ref[rows, :], nw_ref[...]) * (1.0 + sc) + sh
            h_ref[rows, :] = h.astype(BF16)
            return carry

        lax.fori_loop(0, tm // sub, body, 0)

    acc = jnp.dot(h_ref[...], w_ref[...], preferred_element_type=F32)
    o_ref[...] = acc.astype(BF16)

    @pl.when(pl.program_id(2) == COL_SMALL // PROJ_TN)
    def _():
        sm_ref[...] = acc[:, COL_SMALL % PROJ_TN:COL_SMALL % PROJ_TN + LANES]


def _inproj(x3, sc, sh, nw, w_cat, tm):
    bx, lx, d = x3.shape
    n = w_cat.shape[1]
    kern = functools.partial(_inproj_kernel, tm=tm, sub=min(tm, 256))
    return pl.pallas_call(
        kern,
        grid=(bx, lx // tm, n // PROJ_TN),
        in_specs=[pl.BlockSpec((None, tm, d), lambda b, i, j: (b, i, 0)),
                  _mod_spec(sc, tm), _mod_spec(sh, tm),
                  pl.BlockSpec((1, d), lambda b, i, j: (0, 0)),
                  pl.BlockSpec((d, PROJ_TN), lambda b, i, j: (0, j))],
        out_specs=[pl.BlockSpec((None, tm, PROJ_TN), lambda b, i, j: (b, i, j)),
                   pl.BlockSpec((None, tm, LANES), lambda b, i, j: (b, i, 0))],
        out_shape=[jax.ShapeDtypeStruct((bx, lx, n), BF16), jax.ShapeDtypeStruct((bx, lx, LANES), F32)],
        scratch_shapes=[pltpu.VMEM((tm, d), BF16)],
        compiler_params=_cparams("arbitrary", "arbitrary", "arbitrary"),
        name="norm_inproj",
    )(x3, sc, sh, nw.reshape(1, d), w_cat)


def _conv_silu(u, ext_ref, bb, cw_ref, c, lc, bias=None):
    ext_ref[bb, SUBLANES:SUBLANES + c, :] = u
    y = ext_ref[bb, CARRY_ROW:CARRY_ROW + c, :] * cw_ref[0:1, :]
    for i in range(1, CONV_K):
        y = y + ext_ref[bb, CARRY_ROW + i:CARRY_ROW + i + c, :] * cw_ref[i:i + 1, :]
    ext_ref[bb, CARRY_ROW:SUBLANES, :] = ext_ref[bb, CARRY_ROW + lc:SUBLANES + lc, :]
    if bias is not None:
        y = y + bias
    return _silu(y)


def _conv_rounds(ref, rounds, ext_ref, cw_ref, r, c, lc, bias=None):
    read = _chunk_reader(ref, r, c)
    return [_conv_silu(read(bb, r0), ext_ref, bb, cw_ref, c, lc, bias) for rnd in rounds for bb, r0 in rnd]


def _chunk_reader(ref, r, c):
    if r == c:
        whole = ref[...].astype(F32)
        return lambda bb, r0: whole[bb * c:(bb + 1) * c]
    return lambda bb, r0: ref[pl.ds(bb * r + r0, c), :].astype(F32)


def _chunk_writer(ref, bt, r, c):
    if r != c:
        return (lambda bb, r0, val: ref.__setitem__((pl.ds(bb * r + r0, c), slice(None)), val.astype(ref.dtype)),
                lambda: None)
    parts = {}

    def flush():
        ref[...] = jnp.concatenate([parts[bb] for bb in range(bt)], axis=0).astype(ref.dtype)
    return (lambda bb, r0, val: parts.__setitem__(bb, val)), flush


def _run_rounds(process, bt, r, c, cpi):
    g = r // c
    if g == 1:
        process([[(bb, 0) for bb in range(bt)]])
    else:
        def body(ci, carry):
            process([[(bb, pl.multiple_of((ci * cpi + t) * c, c)) for bb in range(bt)] for t in range(cpi)])
            return carry
        lax.fori_loop(0, g // cpi, body, 0)


INV_BASE = SUBLANES


def _inverse_masks(row, col, c):
    sh = lambda x, s: jnp.right_shift(x, int(math.log2(s)))
    diag = sh(row, INV_BASE) == sh(col, INV_BASE)
    merges = []
    s = INV_BASE
    while s < c:
        merges.append((sh(row, 2 * s) == sh(col, 2 * s))
                      & (jnp.bitwise_and(sh(row, s), 1) == 1) & (jnp.bitwise_and(sh(col, s), 1) == 0))
        s *= 2
    return diag, merges


def _unit_lower_inverses(lmats, eye_f, masks):
    diag, merges = masks
    npows = [-jnp.where(diag, l, 0.0) for l in lmats]
    ps = [eye_f + n for n in npows]
    for _ in range(int(math.log2(INV_BASE)) - 1):
        npows = [_dot(n, n) for n in npows]
        ps = [p + _dot(p, n) for p, n in zip(ps, npows)]
    for m in merges:
        ts = [_dot(p, jnp.where(m, l, 0.0)) for p, l in zip(ps, lmats)]
        ps = [p - _dot(t, p) for t, p in zip(ts, ps)]
    return ps


def _dn_kernel(qkv_ref, z_ref, sm_ref, cw_ref, alog_ref, bias_ref, nw_ref, cin_ref, sin_ref,
               o_ref, cout_ref, sout_ref, ext_ref, *, bt, r, c, lc, ng, cpi):
    j = pl.program_id(1)

    @pl.when(j == 0)
    def _():
        ext_ref[:, CARRY_ROW:SUBLANES, :] = cin_ref[...]
        sout_ref[...] = sin_ref[...]

    row = _iota((c, c), 0)
    col = _iota((c, c), 1)
    tri_incl = row >= col
    tri_strict = row > col
    eye_f = (row == col).astype(F32)
    tril_bf = tri_incl.astype(BF16)
    valid = _iota((c, 1), 0) < lc
    nexp_a = -jnp.exp(alog_ref[...])
    bias = bias_ref[...]
    inv_masks = _inverse_masks(row, col, c)
    heads = range(DN_HEADS)
    read_z, read_sm = (_chunk_reader(ref, r, c) for ref in (z_ref, sm_ref))
    put_o, flush_o = _chunk_writer(o_ref, bt, r, c)

    def prep(bb, r0, y):
        sm = read_sm(bb, r0)
        g_all = nexp_a * _softplus(sm + bias)
        beta_all = _sigmoid(sm)
        if lc < c:
            g_all = jnp.where(valid, g_all, 0.0)
            beta_all = jnp.where(valid, beta_all, 0.0)
        cs_all = _dot_exact_lhs(tril_bf, g_all)
        cs_t = cs_all.T
        ecs_all = jnp.exp(cs_all)
        per_head = []
        for h in heads:
            q = y[:, h * DN_DK:(h + 1) * DN_DK]
            k = y[:, DN_KEY + h * DN_DK:DN_KEY + (h + 1) * DN_DK]
            v = y[:, 2 * DN_KEY + h * DN_DV:2 * DN_KEY + (h + 1) * DN_DV]
            q = q * lax.rsqrt(jnp.sum(q * q, axis=-1, keepdims=True) + EPS) * (DN_DK ** -0.5)
            k = k * lax.rsqrt(jnp.sum(k * k, axis=-1, keepdims=True) + EPS)
            if lc < c:
                k = jnp.where(valid, k, 0.0)
            cs = cs_all[:, SM_A + h:SM_A + h + 1]
            ecs = ecs_all[:, SM_A + h:SM_A + h + 1]
            beta = beta_all[:, SM_B + h:SM_B + h + 1]
            last = cs_all[c - 1:c, SM_A + h:SM_A + h + 1]
            decay = jnp.exp(jnp.where(tri_incl, cs - cs_t[SM_A + h:SM_A + h + 1, :], -jnp.inf))
            per_head.append(dict(q_ecs=(q * ecs).astype(BF16), q=q.astype(BF16), k=k.astype(BF16), beta=beta,
                                 decay=decay, elast=jnp.exp(last),
                                 k_tail=(k * jnp.exp(last - cs)).astype(BF16),
                                 rhs=jnp.concatenate([v * beta, k * (beta * ecs)], axis=1).astype(BF16)))
        return per_head

    def process(rounds):
        items = [it for rnd in rounds for it in rnd]
        conv = _conv_rounds(qkv_ref, rounds, ext_ref, cw_ref, r, c, lc)
        pre = [prep(bb, r0, y) for (bb, r0), y in zip(items, conv)]
        chains = [(i, h) for i in range(len(items)) for h in heads]
        a = {ch: pre[ch[0]][ch[1]] for ch in chains}
        kk = {ch: _dot_nt(a[ch]["k"], a[ch]["k"]) for ch in chains}
        qk = {ch: _dot_nt(a[ch]["q"], a[ch]["k"]) * a[ch]["decay"] for ch in chains}
        lmats = [jnp.where(tri_strict, a[ch]["beta"] * kk[ch] * a[ch]["decay"], 0.0) for ch in chains]
        pinv = _unit_lower_inverses(lmats, eye_f, inv_masks)
        sol = {ch: _dot(p, a[ch]["rhs"]) for ch, p in zip(chains, pinv)}
        first = 0
        for rnd in rounds:
            idx = range(first, first + len(rnd))
            first += len(rnd)
            rch = [(i, h) for i in idx for h in heads]
            s = {ch: sout_ref[items[ch[0]][0], ch[1]] for ch in rch}
            ws_qs = {ch: _dot(jnp.concatenate([sol[ch][:, DN_DV:].astype(BF16), a[ch]["q_ecs"]], axis=0), s[ch])
                     for ch in rch}
            v_new = {ch: (sol[ch][:, :DN_DV] - ws_qs[ch][:c]).astype(BF16) for ch in rch}
            o = {ch: ws_qs[ch][c:] + _dot(qk[ch], v_new[ch]) for ch in rch}
            for ch in rch:
                sout_ref[items[ch[0]][0], ch[1]] = a[ch]["elast"] * s[ch] + _dot_tn(a[ch]["k_tail"], v_new[ch])
            for i in idx:
                bb, r0 = items[i]
                z = read_z(bb, r0)
                put_o(bb, r0, jnp.concatenate(
                    [_rms(o[(i, h)], nw_ref[...]) * _silu(z[:, h * DN_DV:(h + 1) * DN_DV]) for h in heads], axis=1))

    _run_rounds(process, bt, r, c, cpi)
    flush_o()

    @pl.when(j == ng - 1)
    def _():
        cout_ref[...] = ext_ref[:, CARRY_ROW:SUBLANES, :]


def _small_row(vals, offset, fill=0.0):
    row = jnp.full((1, LANES), fill, F32)
    return lax.dynamic_update_slice(row, vals.astype(F32).reshape(1, -1), (0, offset))


def _dn_branch(proj, small, conv_w, a_log, dt_bias, norm_w, conv_in, rec_in, *, bt, r, c, lc, ng, cpi):
    nb = conv_in.shape[0] // bt
    rows = bt * r
    t = proj.shape[0]
    kern = functools.partial(_dn_kernel, bt=bt, r=r, c=c, lc=lc, ng=ng, cpi=cpi)
    rowmap = lambda col: (lambda i, j: (i * ng + j, col))
    const = lambda i, j: (0, 0)
    return pl.pallas_call(
        kern,
        grid=(nb, ng),
        in_specs=[pl.BlockSpec((rows, DN_CONV_DIM), rowmap(COL_DN_QKV // DN_CONV_DIM)),
                  pl.BlockSpec((rows, DN_VAL), rowmap(COL_DN_Z // DN_VAL)),
                  pl.BlockSpec((rows, LANES), rowmap(0)),
                  pl.BlockSpec((CONV_K, DN_CONV_DIM), const),
                  pl.BlockSpec((1, LANES), const),
                  pl.BlockSpec((1, LANES), const),
                  pl.BlockSpec((1, DN_DV), const),
                  pl.BlockSpec((bt, CONV_K - 1, DN_CONV_DIM), lambda i, j: (i, 0, 0)),
                  pl.BlockSpec((bt, DN_HEADS, DN_DK, DN_DV), lambda i, j: (i, 0, 0, 0))],
        out_specs=[pl.BlockSpec((rows, DN_VAL), lambda i, j: (i * ng + j, 0)),
                   pl.BlockSpec((bt, CONV_K - 1, DN_CONV_DIM), lambda i, j: (i, 0, 0)),
                   pl.BlockSpec((bt, DN_HEADS, DN_DK, DN_DV), lambda i, j: (i, 0, 0, 0))],
        out_shape=[jax.ShapeDtypeStruct((t, DN_VAL), BF16),
                   jax.ShapeDtypeStruct(conv_in.shape, F32),
                   jax.ShapeDtypeStruct(rec_in.shape, F32)],
        scratch_shapes=[pltpu.VMEM((bt, c + SUBLANES, DN_CONV_DIM), F32)],
        compiler_params=_cparams("arbitrary", "arbitrary"),
        name="gated_delta",
    )(proj, proj, small, conv_w, _small_row(a_log, SM_A), _small_row(dt_bias, SM_A),
      norm_w.reshape(1, DN_DV), conv_in, rec_in)


def _ssd_kernel(xbc_ref, z_ref, sm_ref, cw_ref, cb_ref, alog_ref, bias_ref, dvec_ref, nw_ref, cin_ref,
                hin_ref, y_ref, cout_ref, hout_ref, ext_ref, *, bt, r, c, lc, ng, cpi):
    j = pl.program_id(1)

    @pl.when(j == 0)
    def _():
        ext_ref[:, CARRY_ROW:SUBLANES, :] = cin_ref[...]
        hout_ref[...] = hin_ref[...]

    tril_bf = (_iota((c, c), 0) >= _iota((c, c), 1)).astype(BF16)
    row2 = _iota((c, 2 * c), 0)
    lane2 = _iota((c, 2 * c), 1)
    left2 = lane2 < c
    tri2 = row2 >= jnp.where(left2, lane2, lane2 - c)
    left_x = _iota((c, LANES), 1) < SSM_HEADDIM
    top_h = _iota((2 * SSM_HEADDIM, 1), 0) < SSM_HEADDIM
    valid = _iota((c, 1), 0) < lc
    nexp_a = -jnp.exp(alog_ref[...])
    bias = bias_ref[...]
    gw = SSM_INNER // SSM_GROUPS
    pairs_per_group = SSM_HEADS // SSM_GROUPS // 2

    pairs = range(SSM_HEADS // 2)
    pair_rows = lambda p: slice(p * LANES, (p + 1) * LANES)
    read_z, read_sm = (_chunk_reader(ref, r, c) for ref in (z_ref, sm_ref))
    put_y, flush_y = _chunk_writer(y_ref, bt, r, c)

    def prep(bb, r0, y):
        dt_all = _softplus(read_sm(bb, r0) + bias)
        if lc < c:
            dt_all = jnp.where(valid, dt_all, 0.0)
        cs_all = _dot_exact_lhs(tril_bf, dt_all * nexp_a)
        cs_t2 = jnp.concatenate([cs_all, cs_all], axis=0).T
        last = cs_all[c - 1:c, :]
        bm = [y[:, SSM_INNER + g * SSM_STATE:SSM_INNER + (g + 1) * SSM_STATE].astype(BF16)
              for g in range(SSM_GROUPS)]
        cm = [y[:, SSM_INNER + (SSM_GROUPS + g) * SSM_STATE:SSM_INNER + (SSM_GROUPS + g + 1) * SSM_STATE].astype(BF16)
              for g in range(SSM_GROUPS)]
        return dict(y=y, dt=dt_all, cs=cs_all, cs_t2=cs_t2, ecs=jnp.exp(cs_all),
                    tail=jnp.exp(last - cs_all), elast=jnp.exp(last), bm=bm, cm=cm)

    def process(rounds):
        items = [it for rnd in rounds for it in rnd]
        conv = _conv_rounds(xbc_ref, rounds, ext_ref, cw_ref, r, c, lc, bias=cb_ref[...])
        pre = [prep(bb, r0, y) for (bb, r0), y in zip(items, conv)]
        ids = range(len(items))
        cb2 = {(i, g): _dot_nt(pre[i]["cm"][g], jnp.concatenate([pre[i]["bm"][g]] * 2, axis=0))
               for i in ids for g in range(SSM_GROUPS)}
        y_intra, upd, xs_of = {}, {}, {}
        for i in ids:
            a = pre[i]
            for p in pairs:
                g = p // pairs_per_group
                la, lb = SM_DT + 2 * p, SM_DT + 2 * p + 1
                both = lambda v, mask=left_x: jnp.where(mask, v[:, la:la + 1], v[:, lb:lb + 1])
                xs = a["y"][:, pair_rows(p)]
                xdt = xs * both(a["dt"])
                diff = both(a["cs"], left2) - jnp.where(left2[0:1], a["cs_t2"][la:la + 1, :], a["cs_t2"][lb:lb + 1, :])
                decay2 = jnp.exp(jnp.where(tri2, diff, -jnp.inf))
                rhs = jnp.concatenate([jnp.where(left_x, xdt, 0.0), jnp.where(left_x, 0.0, xdt)], axis=0)
                y_intra[i, p] = _dot(cb2[i, g] * decay2, rhs)
                upd[i, p] = _dot_tn(xdt * both(a["tail"]), a["bm"][g])
                xs_of[i, p] = xs
        first = 0
        for rnd in rounds:
            idx = range(first, first + len(rnd))
            first += len(rnd)
            for i in idx:
                a = pre[i]
                bb, r0 = items[i]
                outs = []
                for p in pairs:
                    g = p // pairs_per_group
                    la, lb = SM_DT + 2 * p, SM_DT + 2 * p + 1
                    hs = hout_ref[bb, pair_rows(p), :]
                    y_inter = _dot_nt(a["cm"][g], hs) * jnp.where(left_x, a["ecs"][:, la:la + 1], a["ecs"][:, lb:lb + 1])
                    hout_ref[bb, pair_rows(p), :] = (
                        jnp.where(top_h, a["elast"][:, la:la + 1], a["elast"][:, lb:lb + 1]) * hs + upd[i, p])
                    outs.append(y_intra[i, p] + y_inter + dvec_ref[:, pair_rows(p)] * xs_of[i, p])
                yz = jnp.concatenate(outs, axis=1) * _silu(read_z(bb, r0))
                put_y(bb, r0, jnp.concatenate(
                    [_rms(yz[:, g * gw:(g + 1) * gw], nw_ref[:, g * gw:(g + 1) * gw]) for g in range(SSM_GROUPS)],
                    axis=1))

    _run_rounds(process, bt, r, c, cpi)
    flush_y()

    @pl.when(j == ng - 1)
    def _():
        cout_ref[...] = ext_ref[:, CARRY_ROW:SUBLANES, :]


def _ssd_branch(proj, small, conv_w, conv_b, a_log, dt_bias, d_skip, norm_w, conv_in, rec_in, *, bt, r, c, lc, ng, cpi):
    nb = conv_in.shape[0] // bt
    rows = bt * r
    t = proj.shape[0]
    hrows = SSM_HEADS * SSM_HEADDIM
    rec2 = rec_in.reshape(rec_in.shape[0], hrows, SSM_STATE)
    kern = functools.partial(_ssd_kernel, bt=bt, r=r, c=c, lc=lc, ng=ng, cpi=cpi)
    rowmap = lambda col: (lambda i, j: (i * ng + j, col))
    const = lambda i, j: (0, 0)
    y, conv_out, rec_out = pl.pallas_call(
        kern,
        grid=(nb, ng),
        in_specs=[pl.BlockSpec((rows, SSM_CONV_DIM), rowmap(COL_SSM_XBC // SSM_CONV_DIM)),
                  pl.BlockSpec((rows, SSM_INNER), rowmap(COL_SSM_Z // SSM_INNER)),
                  pl.BlockSpec((rows, LANES), rowmap(0)),
                  pl.BlockSpec((CONV_K, SSM_CONV_DIM), const),
                  pl.BlockSpec((1, SSM_CONV_DIM), const),
                  pl.BlockSpec((1, LANES), const),
                  pl.BlockSpec((1, LANES), const),
                  pl.BlockSpec((1, SSM_INNER), const),
                  pl.BlockSpec((1, SSM_INNER), const),
                  pl.BlockSpec((bt, CONV_K - 1, SSM_CONV_DIM), lambda i, j: (i, 0, 0)),
                  pl.BlockSpec((bt, hrows, SSM_STATE), lambda i, j: (i, 0, 0))],
        out_specs=[pl.BlockSpec((rows, SSM_INNER), lambda i, j: (i * ng + j, 0)),
                   pl.BlockSpec((bt, CONV_K - 1, SSM_CONV_DIM), lambda i, j: (i, 0, 0)),
                   pl.BlockSpec((bt, hrows, SSM_STATE), lambda i, j: (i, 0, 0))],
        out_shape=[jax.ShapeDtypeStruct((t, SSM_INNER), BF16),
                   jax.ShapeDtypeStruct(conv_in.shape, F32),
                   jax.ShapeDtypeStruct(rec2.shape, F32)],
        scratch_shapes=[pltpu.VMEM((bt, c + SUBLANES, SSM_CONV_DIM), F32)],
        compiler_params=_cparams("arbitrary", "arbitrary"),
        name="ssd_scan",
    )(proj, proj, small, conv_w, conv_b.reshape(1, -1), _small_row(a_log, SM_DT), _small_row(dt_bias, SM_DT),
      jnp.repeat(d_skip.astype(F32), SSM_HEADDIM).reshape(1, SSM_INNER), norm_w.reshape(1, SSM_INNER),
      conv_in, rec2)
    return y, conv_out, rec_out.reshape(rec_in.shape)


PLAN_E0, PLAN_E1, PLAN_C0, PLAN_C1 = 0, 1, 2, 3


def _route(logits):
    lane = _iota(logits.shape, 1).astype(F32)
    big = float(LANES)
    is_group = (lane >= RT_GROUP) & (lane < RT_GROUP + MOE_GROUPS)
    gl = jnp.where(is_group, logits, -jnp.inf)
    gmax = jnp.max(gl, axis=-1, keepdims=True)
    g_sel = jnp.min(jnp.where(gl == gmax, lane, big), axis=-1, keepdims=True) - RT_GROUP
    p_group = 1.0 / jnp.sum(jnp.exp(gl - gmax), axis=-1, keepdims=True)
    e_lo = RT_EXPERT + MOE_PER_GROUP * g_sel
    in_grp = (lane >= e_lo) & (lane < e_lo + MOE_PER_GROUP)
    el = jnp.where(in_grp, logits, -jnp.inf)
    ee = jnp.exp(el - jnp.max(el, axis=-1, keepdims=True))
    pe = jnp.where(in_grp, ee / jnp.sum(ee, axis=-1, keepdims=True), -1.0)
    p1 = jnp.max(pe, axis=-1, keepdims=True)
    i1 = jnp.min(jnp.where(pe == p1, lane, big), axis=-1, keepdims=True)
    pe2 = jnp.where(lane == i1, -1.0, pe)
    p2 = jnp.max(pe2, axis=-1, keepdims=True)
    i2 = jnp.min(jnp.where(pe2 == p2, lane, big), axis=-1, keepdims=True)
    tot = p1 + p2
    plan = jnp.where(lane == PLAN_E0, i1 - RT_EXPERT, 0.0) + jnp.where(lane == PLAN_E1, i2 - RT_EXPERT, 0.0)
    return plan + jnp.where(lane == PLAN_C0, p_group * p1 / tot, 0.0) + jnp.where(lane == PLAN_C1, p_group * p2 / tot, 0.0)


def _tile_ranks(plan):
    tp = plan.shape[0]
    lane = _iota(plan.shape, 1).astype(F32)
    sel0 = lane == plan[:, PLAN_E0:PLAN_E0 + 1]
    sel1 = lane == plan[:, PLAN_E1:PLAN_E1 + 1]
    sel = (sel0 | sel1).astype(BF16)
    before = (_iota((tp, tp), 0) > _iota((tp, tp), 1)).astype(BF16)
    excl = jnp.dot(before, sel, preferred_element_type=F32)
    r0 = jnp.sum(jnp.where(sel0, excl, 0.0), axis=-1, keepdims=True)
    r1 = jnp.sum(jnp.where(sel1, excl, 0.0), axis=-1, keepdims=True)
    rank = jnp.where(lane == PLAN_E0, r0, 0.0) + jnp.where(lane == PLAN_E1, r1, 0.0)
    return rank, jnp.sum(sel.astype(F32), axis=0, keepdims=True)


def _merge_kernel(x_ref, og_ref, ys_ref, gates_ref, gt1_ref, sc2_ref, sh2_ref, nw2_ref, wdn_ref, wssm_ref,
                  wout_ref, wr_ref, br_ref, x1_ref, h2_ref, plan_ref, rank_ref, cnt_ref):
    y_dn = _dot(og_ref[...], wdn_ref[...])
    y_ssm = _dot(ys_ref[...], wssm_ref[...])
    merged = (_sigmoid(gates_ref[:, :D_MODEL].astype(F32)) * y_dn
              + _sigmoid(gates_ref[:, D_MODEL:].astype(F32)) * y_ssm)
    x1 = x_ref[...] + gt1_ref[...] * _dot(merged, wout_ref[...])
    x1_ref[...] = x1
    h2 = _rms(x1, nw2_ref[...]) * (1.0 + sc2_ref[...]) + sh2_ref[...]
    h2_ref[...] = h2.astype(BF16)
    plan = _route(_dot_x3(h2, wr_ref[...]) + br_ref[...])
    plan_ref[...] = plan
    rank_ref[...], cnt_ref[...] = _tile_ranks(plan)


def _merge(x3, og, ys, proj, gt1, sc2, sh2, nw2, wdn, wssm, wout, wr, br, tm):
    bx, lx, d = x3.shape
    assert tm == TOK_TILE
    nl = lx // tm
    rowmap = lambda col: (lambda b, i: (b * nl + i, col))
    const = lambda b, i: (0, 0)
    tok = lambda w, dt: jax.ShapeDtypeStruct((bx, lx, w), dt)
    return pl.pallas_call(
        _merge_kernel,
        grid=(bx, nl),
        in_specs=[pl.BlockSpec((None, tm, d), lambda b, i: (b, i, 0)),
                  pl.BlockSpec((tm, DN_VAL), rowmap(0)),
                  pl.BlockSpec((tm, SSM_INNER), rowmap(0)),
                  pl.BlockSpec((tm, 2 * d), rowmap(COL_GATES // (2 * d))),
                  _mod_spec(gt1, tm), _mod_spec(sc2, tm), _mod_spec(sh2, tm),
                  pl.BlockSpec((1, d), const),
                  pl.BlockSpec(wdn.shape, const), pl.BlockSpec(wssm.shape, const), pl.BlockSpec(wout.shape, const),
                  pl.BlockSpec(wr.shape, const), pl.BlockSpec((1, LANES), const)],
        out_specs=[pl.BlockSpec((None, tm, d), lambda b, i: (b, i, 0)),
                   pl.BlockSpec((None, tm, d), lambda b, i: (b, i, 0)),
                   pl.BlockSpec((None, tm, LANES), lambda b, i: (b, i, 0)),
                   pl.BlockSpec((None, tm, LANES), lambda b, i: (b, i, 0)),
                   pl.BlockSpec((None, 1, LANES), lambda b, i: (b * nl + i, 0, 0))],
        out_shape=[tok(d, F32), tok(d, BF16), tok(LANES, F32), tok(LANES, F32),
                   jax.ShapeDtypeStruct((bx * nl, 1, LANES), F32)],
        compiler_params=_cparams("arbitrary", "arbitrary"),
        name="merge_route",
    )(x3, og, ys, proj, gt1, sc2, sh2, nw2.reshape(1, d), wdn, wssm, wout, wr, br)


FFN_ROWS = 512
TOK_TILE = 512
RUN_ALIGN = 2 * SUBLANES
RUN_BIG = 4 * RUN_ALIGN
RUN_SMALL = (2 * RUN_ALIGN, RUN_ALIGN)
LOCAL_ROWS = 2 * TOK_TILE + MOE_EXPERTS * RUN_ALIGN


def _local_slots(plan, rank, off_row):
    lane = _iota(plan.shape, 1).astype(F32)
    slots = []
    for k in (PLAN_E0, PLAN_E1):
        off = jnp.sum(jnp.where(lane == plan[:, k:k + 1], off_row, 0.0), axis=-1, keepdims=True)
        slots.append(off + rank[:, k:k + 1])
    return slots


def _run_blocks(p8_ref, loff_ref, base_ref, act):
    for e in range(MOE_EXPERTS):
        n = p8_ref[0, e]
        lo = loff_ref[0, e]
        go = base_ref[0, e]

        def big(k, carry, lo=lo, go=go):
            off = k * RUN_BIG
            act(pl.multiple_of(lo + off, RUN_ALIGN), pl.multiple_of(go + off, RUN_ALIGN), RUN_BIG)
            return carry

        lax.fori_loop(0, lax.shift_right_logical(n, RUN_BIG.bit_length() - 1), big, 0)
        for size in RUN_SMALL:
            @pl.when(jnp.bitwise_and(n, size) != 0)
            def _(size=size, n=n, lo=lo, go=go):
                done = jnp.bitwise_and(n, ~(2 * size - 1))
                act(pl.multiple_of(lo + done, RUN_ALIGN), pl.multiple_of(go + done, RUN_ALIGN), size)


def _dispatch_kernel(tv_ref, p8_ref, loff_ref, base_ref, p8p_ref, loffp_ref, basep_ref, plan_ref, rank_ref, off_ref,
                     h2_ref, *rest, first):
    xs_ref, buf_ref, zero_ref, sem, zsem = rest[-5:]
    step = pl.program_id(0)
    last = pl.num_programs(0) - 1
    slot = jnp.bitwise_and(step, 1)

    def zero_blocks(act):
        def body(i, carry):
            fv = tv_ref[i]
            n = FFN_ROWS - fv
            size = FFN_ROWS
            while size >= RUN_ALIGN:
                @pl.when(jnp.bitwise_and(n, size) != 0)
                def _(size=size):
                    at = pl.multiple_of(i * FFN_ROWS + fv + jnp.bitwise_and(n, ~(2 * size - 1)), RUN_ALIGN)
                    act(pltpu.make_async_copy(zero_ref.at[pl.ds(0, size)], xs_ref.at[pl.ds(at, size)], zsem))
                size //= 2
            return carry
        lax.fori_loop(0, xs_ref.shape[0] // FFN_ROWS, body, 0)

    @pl.when((step == 0) & first)
    def _():
        zero_ref[...] = jnp.zeros_like(zero_ref)
        zero_blocks(lambda cp: cp.start())

    s0, s1 = _local_slots(plan_ref[...], rank_ref[...], off_ref[...])
    row = _iota((1, LOCAL_ROWS), 1).astype(F32)
    onehot_t = ((row == s0) | (row == s1)).astype(BF16)
    buf_ref[slot] = lax.dot_general(onehot_t, h2_ref[...], (((0,), (0,)), ((), ())),
                                    preferred_element_type=F32).astype(BF16)

    def copy(k, lo, go, size):
        return pltpu.make_async_copy(buf_ref.at[k, pl.ds(lo, size)], xs_ref.at[pl.ds(go, size)], sem.at[k])

    _run_blocks(p8_ref, loff_ref, base_ref, lambda lo, go, size: copy(slot, lo, go, size).start())

    @pl.when(step > 0)
    def _():
        _run_blocks(p8p_ref, loffp_ref, basep_ref, lambda lo, go, size: copy(1 - slot, lo, go, size).wait())

    @pl.when(step == last)
    def _():
        _run_blocks(p8_ref, loff_ref, base_ref, lambda lo, go, size: copy(slot, lo, go, size).wait())

    @pl.when((step == last) & first)
    def _():
        zero_blocks(lambda cp: cp.wait())


def _tile_scalars():
    return pl.BlockSpec((None, 1, LANES), lambda i, *_: (i, 0, 0), memory_space=pltpu.SMEM)


def _dispatch(tile_valid, p8, loff, base, plan2, rank, off_f, h2, rows, xs_prev=None):
    t, d = h2.shape
    tt = TOK_TILE
    tok = lambda w: pl.BlockSpec((tt, w), lambda i, tv: (i, 0))
    operands = ((tile_valid, p8, loff, base, p8, loff, base, plan2, rank, off_f, h2)
                + (() if xs_prev is None else (xs_prev,)))
    prev = pl.BlockSpec((None, 1, LANES), lambda i, tv: (jnp.maximum(i - 1, 0), 0, 0), memory_space=pltpu.SMEM)
    return pl.pallas_call(
        functools.partial(_dispatch_kernel, first=xs_prev is None),
        grid_spec=pltpu.PrefetchScalarGridSpec(
            num_scalar_prefetch=1,
            grid=(t // tt,),
            in_specs=[_tile_scalars(), _tile_scalars(), _tile_scalars(), prev, prev, prev,
                      tok(LANES), tok(LANES),
                      pl.BlockSpec((None, 1, LANES), lambda i, tv: (i, 0, 0)),
                      tok(d)] + ([] if xs_prev is None else [pl.BlockSpec(memory_space=pl.ANY)]),
            out_specs=pl.BlockSpec(memory_space=pl.ANY),
            scratch_shapes=[pltpu.VMEM((2, LOCAL_ROWS, d), BF16), pltpu.VMEM((FFN_ROWS, d), BF16),
                            pltpu.SemaphoreType.DMA((2,)), pltpu.SemaphoreType.DMA(())]),
        out_shape=jax.ShapeDtypeStruct((rows, d), BF16),
        input_output_aliases={} if xs_prev is None else {len(operands) - 1: 0},
        compiler_params=_cparams("arbitrary"),
        name="moe_dispatch",
    )(*operands)


def _ffn_kernel(te_ref, tv_ref, src_ref, xs_ref, wg_ref, wu_ref, wd_ref, o_ref):
    nv = tv_ref[pl.program_id(0)]

    half = FFN_ROWS // 2

    def swiglu(rows):
        x = xs_ref[rows, :]
        o_ref[rows, :] = _dot(_silu(_dot(x, wg_ref[...])) * _dot(x, wu_ref[...]), wd_ref[...]).astype(BF16)

    @pl.when(nv > half)
    def _():
        swiglu(slice(None))

    @pl.when((nv > 0) & (nv <= half))
    def _():
        swiglu(slice(0, half))
        o_ref[half:, :] = jnp.zeros((half, o_ref.shape[1]), o_ref.dtype)

    @pl.when(nv <= 0)
    def _():
        o_ref[...] = jnp.zeros_like(o_ref)


def _ffn(tile_expert, tile_valid, tile_src, xs, wg, wu, wd):
    rows, d = xs.shape
    wmap = lambda i, te, tv, src: (te[i], 0, 0)
    return pl.pallas_call(
        _ffn_kernel,
        grid_spec=pltpu.PrefetchScalarGridSpec(
            num_scalar_prefetch=3,
            grid=(rows // FFN_ROWS,),
            in_specs=[pl.BlockSpec((FFN_ROWS, d), lambda i, te, tv, src: (src[i], 0)),
                      pl.BlockSpec((None, d, MOE_FF), wmap),
                      pl.BlockSpec((None, d, MOE_FF), wmap),
                      pl.BlockSpec((None, MOE_FF, d), wmap)],
            out_specs=pl.BlockSpec((FFN_ROWS, d), lambda i, te, tv, src: (i, 0))),
        out_shape=jax.ShapeDtypeStruct((rows, d), BF16),
        compiler_params=_cparams("arbitrary"),
        name="moe_ffn",
    )(tile_expert, tile_valid, tile_src, xs, wg, wu, wd)


def _combine_kernel(p8_ref, loff_ref, base_ref, p8n_ref, loffn_ref, basen_ref, ys_ref, plan_ref, rank_ref, off_ref,
                    x1_ref, gt2_ref, fsc_ref, fsh_ref, fnw_ref, y_ref, buf_ref, sem):
    step = pl.program_id(0) * pl.num_programs(1) + pl.program_id(1)
    last = pl.num_programs(0) * pl.num_programs(1) - 1
    slot = jnp.bitwise_and(step, 1)

    def copy(k, lo, go, size):
        return pltpu.make_async_copy(ys_ref.at[pl.ds(go, size)], buf_ref.at[k, pl.ds(lo, size)], sem.at[k])

    @pl.when(step == 0)
    def _():
        buf_ref[...] = jnp.zeros_like(buf_ref)
        _run_blocks(p8_ref, loff_ref, base_ref, lambda lo, go, size: copy(0, lo, go, size).start())

    @pl.when(step < last)
    def _():
        _run_blocks(p8n_ref, loffn_ref, basen_ref, lambda lo, go, size: copy(1 - slot, lo, go, size).start())

    plan = plan_ref[...]
    s0, s1 = _local_slots(plan, rank_ref[...], off_ref[...])
    row = _iota((1, LOCAL_ROWS), 1).astype(F32)
    weights = (jnp.where(row == s0, plan[:, PLAN_C0:PLAN_C0 + 1], 0.0)
               + jnp.where(row == s1, plan[:, PLAN_C1:PLAN_C1 + 1], 0.0))
    _run_blocks(p8_ref, loff_ref, base_ref, lambda lo, go, size: copy(slot, lo, go, size).wait())
    total = loff_ref[0, MOE_EXPERTS - 1] + p8_ref[0, MOE_EXPERTS - 1]
    filled = _iota((LOCAL_ROWS, 1), 0) < total
    moe = _dot(weights, jnp.where(filled, buf_ref[slot], jnp.zeros((), BF16)))
    x2 = x1_ref[...] + gt2_ref[...] * moe
    y_ref[...] = _rms(x2, fnw_ref[...]) * (1.0 + fsc_ref[...]) + fsh_ref[...]


def _combine(p8, loff, base, ys, plan, rank3, off_f, x1, gt2, fsc, fsh, fnw):
    bx, lx, d = x1.shape
    tt = TOK_TILE
    nl = lx // tt
    tile = lambda b, i: b * nl + i
    scal = pl.BlockSpec((None, 1, LANES), lambda b, i: (tile(b, i), 0, 0), memory_space=pltpu.SMEM)
    scal_next = pl.BlockSpec((None, 1, LANES), lambda b, i: (jnp.minimum(tile(b, i) + 1, bx * nl - 1), 0, 0),
                             memory_space=pltpu.SMEM)
    tokspec = lambda w: pl.BlockSpec((None, tt, w), lambda b, i: (b, i, 0))
    return pl.pallas_call(
        _combine_kernel,
        grid=(bx, nl),
        in_specs=[scal, scal, scal, scal_next, scal_next, scal_next,
                  pl.BlockSpec(memory_space=pl.ANY),
                  tokspec(LANES), tokspec(LANES),
                  pl.BlockSpec((None, 1, LANES), lambda b, i: (tile(b, i), 0, 0)),
                  tokspec(d), _mod_spec(gt2, tt), _mod_spec(fsc, tt), _mod_spec(fsh, tt),
                  pl.BlockSpec((1, d), lambda b, i: (0, 0))],
        out_specs=tokspec(d),
        out_shape=jax.ShapeDtypeStruct((bx, lx, d), F32),
        scratch_shapes=[pltpu.VMEM((2, LOCAL_ROWS, d), BF16), pltpu.SemaphoreType.DMA((2,))],
        compiler_params=_cparams("arbitrary", "arbitrary"),
        name="moe_combine_final",
    )(p8, loff, base, p8, loff, base, ys, plan, rank3, off_f, x1, gt2, fsc, fsh, fnw.reshape(1, d))


def _moe(groups, wg, wu, wd, fnw):
    d = groups[0]["x1"].shape[-1]
    ntiles = [g["x1"].shape[0] * g["x1"].shape[1] // TOK_TILE for g in groups]
    t = sum(ntiles) * TOK_TILE
    n = jnp.concatenate([g["cnt"][:, 0, :MOE_EXPERTS] for g in groups], axis=0).astype(jnp.int32)
    p8 = (n + RUN_ALIGN - 1) // RUN_ALIGN * RUN_ALIGN
    loff = jnp.cumsum(p8, axis=1) - p8
    erows = jnp.sum(p8, axis=0)
    epad = (erows + FFN_ROWS - 1) // FFN_ROWS * FFN_ROWS
    ends = jnp.cumsum(epad)
    starts = ends - epad
    base = starts[None, :] + jnp.cumsum(p8, axis=0) - p8
    lanes = lambda a: jnp.pad(a, ((0, 0), (0, LANES - MOE_EXPERTS)))[:, None, :]
    rows = (-(-(2 * t + (t // TOK_TILE) * MOE_EXPERTS * (RUN_ALIGN - 1)) // FFN_ROWS) + MOE_EXPERTS) * FFN_ROWS
    tile_start = jnp.arange(rows // FFN_ROWS, dtype=jnp.int32) * FFN_ROWS
    tile_expert = jnp.minimum(jnp.sum(tile_start[:, None] >= ends[None, :], axis=1), MOE_EXPERTS - 1).astype(jnp.int32)
    hot = tile_expert[:, None] == jnp.arange(MOE_EXPERTS)[None, :]
    valid_rows = lambda er: jnp.clip(jnp.sum(jnp.where(hot, er - (tile_start[:, None] - starts), 0), axis=1),
                                     0, FFN_ROWS).astype(jnp.int32)
    tile_valid = valid_rows(erows)
    p8l, loffl, basel = lanes(p8), lanes(loff), lanes(base)
    off_f = loffl.astype(F32)
    spans, first = [], 0
    for nt in ntiles:
        spans.append(slice(first, first + nt))
        first += nt
    first_valid = valid_rows(jnp.sum(p8[spans[0]], axis=0))
    xs = None
    for g, sp in zip(groups, spans):
        tg = g["h2"].shape[0] * g["h2"].shape[1]
        xs = _dispatch(first_valid, p8l[sp], loffl[sp], basel[sp], g["plan"].reshape(tg, LANES),
                       g["rank"].reshape(tg, LANES), off_f[sp], g["h2"].reshape(tg, d), rows, xs_prev=xs)
    tile_id = tile_start // FFN_ROWS
    tile_src = jnp.where(tile_valid > 0, tile_id, jnp.max(jnp.where(tile_valid > 0, tile_id, 0))).astype(jnp.int32)
    ys = _ffn(tile_expert, tile_valid, tile_src, xs, wg, wu, wd)
    return [_combine(p8l[sp], loffl[sp], basel[sp], ys, g["plan"], g["rank"], off_f[sp], g["x1"], g["gt2"],
                     g["fsc"], g["fsh"], fnw) for g, sp in zip(groups, spans)]


def _prep_layer(lp):
    w_in = lp["w_in"]
    offs = [0]
    for s in (DN_CONV_DIM, DN_VAL, DN_HEADS, DN_HEADS, SSM_CONV_DIM, SSM_INNER, SSM_HEADS, D_MODEL, D_MODEL):
        offs.append(offs[-1] + s)
    seg = lambda i: w_in[:, offs[i]:offs[i + 1]]
    small = jnp.concatenate([seg(2), seg(3), seg(6)], axis=1)
    pad = jnp.zeros((D_MODEL, PROJ_N - COL_SMALL - small.shape[1]), F32)
    w_cat = jnp.concatenate([seg(0), seg(4), seg(5), seg(7), seg(8), seg(1), small, pad], axis=1).astype(BF16)
    wr = jnp.concatenate([lp["w_group_router"], lp["w_expert_router"],
                          jnp.zeros((D_MODEL, LANES - MOE_GROUPS - MOE_EXPERTS), F32)], axis=1)
    br = jnp.concatenate([lp["b_group_router"], lp["b_expert_router"],
                          jnp.zeros((LANES - MOE_GROUPS - MOE_EXPERTS,), F32)]).reshape(1, LANES)
    return dict(lp, w_cat=w_cat, wr=wr, br=br,
                wdn=lp["w_dn_out"].astype(BF16), wssm=lp["w_ssm_out"].astype(BF16), wout=lp["w_out"].astype(BF16),
                wg=lp["w_exp_gate"], wu=lp["w_exp_up"], wd=lp["w_exp_down"])


def _mixer(x3, mods, fins, states, lp, cfg):
    bx, lx, d = x3.shape
    scan = dict(bt=cfg["bt"], r=cfg["r"], c=cfg["c"], lc=cfg["lc"], ng=cfg["ng"])
    sh1, sc1, gt1, sh2, sc2, gt2 = mods
    dn_conv, dn_rec, ssm_conv, ssm_rec = states
    proj, small = _inproj(x3, sc1, sh1, lp["norm_mix_w"], lp["w_cat"], cfg["tm"])
    proj, small = proj.reshape(bx * lx, PROJ_N), small.reshape(bx * lx, LANES)
    og, dn_conv_new, dn_rec_new = _dn_branch(proj, small, lp["dn_conv_w"], lp["dn_A_log"], lp["dn_dt_bias"],
                                             lp["dn_norm_w"], dn_conv, dn_rec, cpi=cfg["cpi_dn"], **scan)
    ys, ssm_conv_new, ssm_rec_new = _ssd_branch(proj, small, lp["ssm_conv_w"], lp["ssm_conv_b"], lp["ssm_A_log"],
                                                lp["ssm_dt_bias"], lp["ssm_D"], lp["ssm_norm_w"],
                                                ssm_conv, ssm_rec, cpi=cfg["cpi_ssd"], **scan)
    x1, h2, plan, rank, cnt = _merge(x3, og, ys, proj, gt1, sc2, sh2, lp["norm_ffn_w"], lp["wdn"], lp["wssm"],
                                     lp["wout"], lp["wr"], lp["br"], cfg["tm_merge"])
    group = dict(x1=x1, h2=h2, plan=plan, rank=rank, cnt=cnt, gt2=gt2, fsh=fins[0], fsc=fins[1])
    return group, (dn_conv_new, dn_rec_new, ssm_conv_new, ssm_rec_new)


def _per_seq(m):
    return m[:, None, :]


def kernel(x_prompt, x_sample, c_prompt, c_sample, state_dn_conv, state_dn_rec, state_ssm_conv, state_ssm_rec, w_ada, b_ada, norm_mix_w, w_in, dn_conv_w, dn_A_log, dn_dt_bias, dn_norm_w, w_dn_out, ssm_conv_w, ssm_conv_b, ssm_A_log, ssm_dt_bias, ssm_D, ssm_norm_w, w_ssm_out, w_out, norm_ffn_w, w_group_router, b_group_router, w_expert_router, b_expert_router, w_exp_gate, w_exp_up, w_exp_down, w_ada_final, b_ada_final, final_norm_w):
    depth = w_ada.shape[0]
    assert depth == 1
    per_layer = dict(w_ada=w_ada, b_ada=b_ada, norm_mix_w=norm_mix_w, w_in=w_in, dn_conv_w=dn_conv_w,
                     dn_A_log=dn_A_log, dn_dt_bias=dn_dt_bias, dn_norm_w=dn_norm_w, w_dn_out=w_dn_out,
                     ssm_conv_w=ssm_conv_w, ssm_conv_b=ssm_conv_b, ssm_A_log=ssm_A_log, ssm_dt_bias=ssm_dt_bias,
                     ssm_D=ssm_D, ssm_norm_w=ssm_norm_w, w_ssm_out=w_ssm_out, w_out=w_out, norm_ffn_w=norm_ffn_w,
                     w_group_router=w_group_router, b_group_router=b_group_router,
                     w_expert_router=w_expert_router, b_expert_router=b_expert_router,
                     w_exp_gate=w_exp_gate, w_exp_up=w_exp_up, w_exp_down=w_exp_down)
    layers = [_prep_layer({k: v[l] for k, v in per_layer.items()}) for l in range(depth)]

    nbp, lp_, d = x_prompt.shape
    nbs, ls, _ = x_sample.shape
    c_all = jnp.concatenate([c_prompt, c_sample], axis=0)
    mod_all = [_ada(c_all, lyr["w_ada"], lyr["b_ada"]) for lyr in layers]
    fin_all = _ada(c_all, w_ada_final, b_ada_final)

    mods_p = [[_per_seq(m) for m in jnp.split(ma[:nbp], 6, axis=-1)] for ma in mod_all]
    fins_p = [_per_seq(m) for m in jnp.split(fin_all[:nbp], 2, axis=-1)]
    zeros_p = [(jnp.zeros((nbp, CONV_K - 1, DN_CONV_DIM), F32), jnp.zeros((nbp, DN_HEADS, DN_DK, DN_DV), F32),
                jnp.zeros((nbp, CONV_K - 1, SSM_CONV_DIM), F32),
                jnp.zeros((nbp, SSM_HEADS, SSM_HEADDIM, SSM_STATE), F32)) for _ in range(depth)]
    c_p = min(SCAN_CHUNK, lp_)
    r_p = min(lp_, 16 * c_p)
    cfg_p = dict(tm=min(lp_, 2048), tm_merge=min(lp_, TOK_TILE), bt=1, r=r_p, c=c_p, lc=c_p, ng=lp_ // r_p,
                 cpi_dn=8 if (r_p // c_p) % 8 == 0 else 1, cpi_ssd=4 if (r_p // c_p) % 4 == 0 else 1)
    grp_p, st_p = _mixer(x_prompt, mods_p[0], fins_p, zeros_p[0], layers[0], cfg_p)

    lpad = -(-ls // SUBLANES) * SUBLANES
    bt_s = 16
    xs = jnp.pad(x_sample, ((0, 0), (0, lpad - ls), (0, 0))).reshape(1, nbs * lpad, d)
    per_tok = lambda m: jnp.repeat(m, lpad, axis=0)[None]
    mods_s = [[per_tok(m) for m in jnp.split(ma[nbp:], 6, axis=-1)] for ma in mod_all]
    fins_s = [per_tok(m) for m in jnp.split(fin_all[nbp:], 2, axis=-1)]
    st_in = [(state_dn_conv[l], state_dn_rec[l], state_ssm_conv[l], state_ssm_rec[l]) for l in range(depth)]
    ts = nbs * lpad
    cfg_s = dict(tm=ts, tm_merge=min(ts, TOK_TILE), bt=bt_s, r=lpad, c=lpad, lc=ls, ng=1, cpi_dn=1, cpi_ssd=1)
    grp_s, st_s = _mixer(xs, mods_s[0], fins_s, st_in[0], layers[0], cfg_s)

    y_p, y_s = _moe([grp_p, grp_s], layers[0]["wg"], layers[0]["wu"], layers[0]["wd"], final_norm_w)
    y_s = y_s.reshape(nbs, lpad, d)[:, :ls]
    return (y_p, y_s) + tuple(st[None] for st in st_p) + tuple(st[None] for st in st_s)
```
